```python
import math
import jax, jax.numpy as jnp
from jax import lax
import numpy as np

D_MODEL = 1024
BATCH = 4
SEQ = 4096
DEPTH = 1

GRID_W = 64
CTX_LEN = 256
D_MIX = D_MODEL
HG_WIDTH = D_MIX // 2
HG_HEAD_DIM = 128
HG_HEADS = HG_WIDTH // HG_HEAD_DIM
HG_EXPAND = 128
HG_KDIM = HG_HEADS * HG_EXPAND
POOL_WIDTH = D_MIX - HG_WIDTH
POOL_WINDOWS = (2, 4, 8, 16)
POOL_GROUP = POOL_WIDTH // len(POOL_WINDOWS)
IN_COLS = 2 * HG_KDIM + HG_KDIM + 2 * HG_WIDTH + POOL_WIDTH
CHUNK = 64
N_EXPERTS = 32
TOP_K = 4
D_FF = D_MODEL
SWIGLU_LIMIT = 7.0
SWIGLU_ALPHA = 1.702
MOE_BLOCK = 128
EPS = 1e-6

kernel_name = "hybrid_hgrn2_pool_moe_dit_block"


def rms_norm(x, gain):
    xf = x.astype(jnp.float32)
    y = xf * lax.rsqrt(jnp.mean(xf * xf, axis=-1, keepdims=True) + EPS)
    return (y * gain.astype(jnp.float32)).astype(x.dtype)


def gla_chunked(q, k, v, logf, s0):
    B, H, T, DK = q.shape
    DV = v.shape[-1]
    n = T // CHUNK

    def to_chunks(a):
        return a.reshape(B, H, n, CHUNK, a.shape[-1]).transpose(2, 0, 1, 3, 4)

    causal = jnp.tril(jnp.ones((CHUNK, CHUNK), dtype=bool))[:, :, None]

    def step(S, inp):
        qc, kc, vc, lc = inp
        b = jnp.cumsum(lc, axis=-2)
        o_inter = jnp.einsum('bhtk,bhkv->bhtv', qc * jnp.exp(b), S)
        diff = b[:, :, :, None, :] - b[:, :, None, :, :]
        decay = jnp.exp(jnp.where(causal, diff, -jnp.inf))
        att = jnp.einsum('bhtk,bhsk,bhtsk->bhts', qc, kc, decay)
        o = o_inter + jnp.einsum('bhts,bhsv->bhtv', att, vc)
        b_last = b[:, :, -1:, :]
        S = jnp.exp(b_last[:, :, 0, :, None]) * S + jnp.einsum(
            'bhsk,bhsv->bhkv', kc * jnp.exp(b_last - b), vc)
        return S, o

    S, o = lax.scan(step, s0, (to_chunks(q), to_chunks(k), to_chunks(v), to_chunks(logf)))
    o = o.transpose(1, 2, 0, 3, 4).reshape(B, H, T, DV)
    return o, S


def hgrn2_mixer(p, lb, norm_gain, s0_f, s0_b):
    B, T, _ = p.shape
    q, ff, fb, i, g = jnp.split(p, [HG_KDIM, 2 * HG_KDIM, 3 * HG_KDIM, 3 * HG_KDIM + HG_WIDTH], axis=-1)

    def heads(a):
        return a.reshape(B, T, HG_HEADS, -1).transpose(0, 2, 1, 3).astype(jnp.float32)

    qh = heads(jax.nn.silu(q))
    vh = heads(i)

    def direction(fpre, lbd, s0, reverse):
        fz = heads(fpre)
        lbh = lbd.astype(jnp.float32).reshape(HG_HEADS, 1, HG_EXPAND)
        logf = jnp.log(lbh + (1.0 - lbh) * jax.nn.sigmoid(fz))
        kk = (1.0 - lbh) * jax.nn.sigmoid(-fz)
        qq, vv = qh, vh
        if reverse:
            qq, kk, vv, logf = (jnp.flip(a, axis=2) for a in (qq, kk, vv, logf))
        o, s = gla_chunked(qq, kk, vv, logf, s0)
        if reverse:
            o = jnp.flip(o, axis=2)
        return o, s

    o_f, s_f = direction(ff, lb[0], s0_f, False)
    o_b, s_b = direction(fb, lb[1], s0_b, True)
    o = o_f + o_b
    o = o * lax.rsqrt(jnp.mean(o * o, axis=-1, keepdims=True) + EPS)
    o = o.transpose(0, 2, 1, 3).reshape(B, T, HG_WIDTH)
    o = o * norm_gain.astype(jnp.float32) * jax.nn.sigmoid(g.astype(jnp.float32))
    return o.astype(p.dtype), s_f, s_b


def window_mean(x, axis, w):
    n = x.shape[axis]
    cs = jnp.cumsum(x.astype(jnp.float32), axis=axis)
    zero = jnp.zeros_like(lax.slice_in_dim(cs, 0, 1, axis=axis))
    cs = jnp.concatenate([zero, cs], axis=axis)
    t = jnp.arange(n)
    lo = jnp.clip(t - w // 2, 0, n)
    hi = jnp.clip(t - w // 2 + w, 0, n)
    s = jnp.take(cs, hi, axis=axis) - jnp.take(cs, lo, axis=axis)
    shape = [1] * x.ndim
    shape[axis] = n
    cnt = (hi - lo).astype(jnp.float32).reshape(shape)
    return s / cnt


def pool_mixer(u, pool_w, pool_scale, rows):
    B, T, _ = u.shape
    outs = []
    for gi, w in enumerate(POOL_WINDOWS):
        ug = u[..., gi * POOL_GROUP:(gi + 1) * POOL_GROUP].astype(jnp.float32)
        if rows is None:
            m = window_mean(ug, 1, w)
        else:
            grid = ug.reshape(B, rows, GRID_W, POOL_GROUP)
            m = window_mean(window_mean(grid, 1, w), 2, w).reshape(B, T, POOL_GROUP)
        outs.append(jnp.einsum('btc,cd->btd', m - ug, pool_w[gi].astype(jnp.float32)))
    y = jnp.concatenate(outs, axis=-1) * pool_scale.astype(jnp.float32)
    return y.astype(u.dtype)


def moe(xt, w_router, b_router, w_gate, b_gate, w_up, b_up, w_down, b_down):
    T, D = xt.shape
    logits = (xt @ w_router + b_router).astype(jnp.float32)
    top_val, top_idx = lax.top_k(logits, TOP_K)
    top_w = jax.nn.softmax(top_val, axis=-1)
    A = T * TOP_K
    flat_e = top_idx.reshape(-1).astype(jnp.int32)
    flat_tok = jnp.arange(A, dtype=jnp.int32) // TOP_K
    flat_w = top_w.reshape(-1)
    order = jnp.argsort(flat_e, stable=True)
    sorted_e = flat_e[order]
    counts = jnp.bincount(flat_e, length=N_EXPERTS)
    padded = ((counts + MOE_BLOCK - 1) // MOE_BLOCK) * MOE_BLOCK
    start = jnp.cumsum(counts) - counts
    pend = jnp.cumsum(padded)
    pstart = pend - padded
    rank = jnp.arange(A, dtype=jnp.int32) - start[sorted_e]
    dest = pstart[sorted_e] + rank
    n_blocks = -(-A // MOE_BLOCK) + N_EXPERTS
    P = n_blocks * MOE_BLOCK
    row_tok = jnp.full((P,), T, dtype=jnp.int32).at[dest].set(flat_tok[order])
    row_w = jnp.zeros((P,), jnp.float32).at[dest].set(flat_w[order])
    block_e = jnp.clip(jnp.searchsorted(pend, jnp.arange(n_blocks) * MOE_BLOCK, side='right'), 0, N_EXPERTS - 1)
    xpad = jnp.concatenate([xt, jnp.zeros((1, D), xt.dtype)], axis=0)
    xb = xpad[row_tok].reshape(n_blocks, MOE_BLOCK, D)

    def expert_block(args):
        xblk, e = args
        gate = xblk @ w_gate[e] + b_gate[e]
        up = xblk @ w_up[e] + b_up[e]
        gate = jnp.minimum(gate, SWIGLU_LIMIT)
        up = jnp.clip(up, -SWIGLU_LIMIT, SWIGLU_LIMIT)
        h = (up + 1.0) * gate * jax.nn.sigmoid(SWIGLU_ALPHA * gate)
        return h @ w_down[e] + b_down[e]

    yb = lax.map(expert_block, (xb, block_e)).reshape(P, D)
    y = jax.ops.segment_sum(yb.astype(jnp.float32) * row_w[:, None], row_tok, num_segments=T + 1)[:T]
    return y.astype(xt.dtype)


def setup_inputs(seed: int = 0) -> dict:
    key = jax.random.key(seed)
    ks = jax.random.split(key, 26)
    L, D, E, F = DEPTH, D_MODEL, N_EXPERTS, D_FF
    nrm = lambda k, s, sc: jax.random.normal(k, s, jnp.float32) * sc
    return {
        "x": nrm(ks[0], (BATCH, SEQ, D), 1.0),
        "c": nrm(ks[1], (BATCH, D), 1.0),
        "ctx": nrm(ks[2], (BATCH, CTX_LEN, D), 1.0),
        "c_ctx": nrm(ks[3], (D,), 1.0),
        "w_ada": nrm(ks[4], (L, D, 6 * D), 0.5 / math.sqrt(D)),
        "b_ada": nrm(ks[5], (L, 6 * D), 0.02),
        "g_pre_mix": 1.0 + nrm(ks[6], (L, D), 0.02),
        "g_post_mix": 1.0 + nrm(ks[7], (L, D), 0.02),
        "g_pre_ffn": 1.0 + nrm(ks[8], (L, D), 0.02),
        "g_post_ffn": 1.0 + nrm(ks[9], (L, D), 0.02),
        "w_in": nrm(ks[10], (L, D, IN_COLS), 1.0 / math.sqrt(D)),
        "w_out": nrm(ks[11], (L, D_MIX, D), 1.0 / math.sqrt(D_MIX)),
        "hgrn_lb_logits": nrm(ks[12], (L + 1, 2, HG_KDIM), 0.1),
        "hgrn_norm": 1.0 + nrm(ks[13], (L, HG_WIDTH), 0.02),
        "pool_w": nrm(ks[14], (L, len(POOL_WINDOWS), POOL_GROUP, POOL_GROUP), 1.0 / math.sqrt(POOL_GROUP)),
        "pool_scale": 1.0 + nrm(ks[15], (L, POOL_WIDTH), 0.1),
        "w_router": nrm(ks[16], (L, D, E), 1.0 / math.sqrt(D)),
        "b_router": nrm(ks[17], (L, E), 0.01),
        "w_gate": nrm(ks[18], (L, E, D, F), 1.0 / math.sqrt(D)),
        "b_gate": nrm(ks[19], (L, E, F), 0.01),
        "w_up": nrm(ks[20], (L, E, D, F), 1.0 / math.sqrt(D)),
        "b_up": nrm(ks[21], (L, E, F), 0.01),
        "w_down": nrm(ks[22], (L, E, F, D), 1.0 / math.sqrt(F)),
        "b_down": nrm(ks[23], (L, E, D), 0.01),
    }


def reference(x, c, ctx, c_ctx, w_ada, b_ada, g_pre_mix, g_post_mix, g_pre_ffn, g_post_ffn,
              w_in, w_out, hgrn_lb_logits, hgrn_norm, pool_w, pool_scale,
              w_router, b_router, w_gate, b_gate, w_up, b_up, w_down, b_down):
    B, S, D = x.shape
    rows = S // GRID_W
    Bc, Tc, _ = ctx.shape
    lb_table = jnp.cumsum(jax.nn.softmax(hgrn_lb_logits.astype(jnp.float32), axis=0), axis=0)

    for l in range(DEPTH):
        last = l == DEPTH - 1
        mod_lat = (jax.nn.silu(c) @ w_ada[l] + b_ada[l])[:, None, :]
        mod_ctx = (jax.nn.silu(c_ctx) @ w_ada[l] + b_ada[l])[None, None, :]
        sh1, sc1, gt1, sh2, sc2, gt2 = jnp.split(mod_lat, 6, axis=-1)
        csh1, csc1, cgt1, csh2, csc2, cgt2 = jnp.split(mod_ctx, 6, axis=-1)
        lb = lb_table[l]

        hc = rms_norm(ctx, g_pre_mix[l]) * (1.0 + csc1) + csh1
        pc = hc @ w_in[l]
        zeros_state = jnp.zeros((Bc, HG_HEADS, HG_EXPAND, HG_HEAD_DIM), jnp.float32)
        o_hg_c, s_f, s_b = hgrn2_mixer(pc[..., :IN_COLS - POOL_WIDTH], lb, hgrn_norm[l], zeros_state, zeros_state)

        hx = rms_norm(x, g_pre_mix[l]) * (1.0 + sc1) + sh1
        px = hx @ w_in[l]
        o_hg_x, _, _ = hgrn2_mixer(px[..., :IN_COLS - POOL_WIDTH], lb, hgrn_norm[l], s_f, s_b)
        o_pool_x = pool_mixer(px[..., IN_COLS - POOL_WIDTH:], pool_w[l], pool_scale[l], rows)
        yx = jnp.concatenate([o_hg_x, o_pool_x], axis=-1) @ w_out[l]
        x = x + gt1 * rms_norm(yx, g_post_mix[l])

        if not last:
            o_pool_c = pool_mixer(pc[..., IN_COLS - POOL_WIDTH:], pool_w[l], pool_scale[l], None)
            yc = jnp.concatenate([o_hg_c, o_pool_c], axis=-1) @ w_out[l]
            ctx = ctx + cgt1 * rms_norm(yc, g_post_mix[l])

        hx2 = (rms_norm(x, g_pre_ffn[l]) * (1.0 + sc2) + sh2).reshape(B * S, D)
        if last:
            tokens = hx2
        else:
            hc2 = (rms_norm(ctx, g_pre_ffn[l]) * (1.0 + csc2) + csh2).reshape(Bc * Tc, D)
            tokens = jnp.concatenate([hx2, hc2], axis=0)
        y = moe(tokens, w_router[l], b_router[l], w_gate[l], b_gate[l], w_up[l], b_up[l], w_down[l], b_down[l])
        x = x + gt2 * rms_norm(y[:B * S].reshape(B, S, D), g_post_ffn[l])
        if not last:
            ctx = ctx + cgt2 * rms_norm(y[B * S:].reshape(Bc, Tc, D), g_post_ffn[l])

    return x
```

```python
import functools

import jax
import jax.numpy as jnp
from jax import lax
from jax.experimental import pallas as pl
from jax.experimental.pallas import tpu as pltpu

F32 = jnp.float32
BF16 = jnp.bfloat16
I32 = jnp.int32

GRID_W = 64
HG_HEADS = 4
HEAD_DIM = 128
HG_WIDTH = HG_HEADS * HEAD_DIM
POOL_WINDOWS = (2, 4, 8, 16)
POOL_GROUP = 128
N_EXPERTS = 32
TOP_K = 4
SWIGLU_LIMIT = 7.0
SWIGLU_ALPHA = 1.702
EPS = 1e-6

LANES = 128
SUBLANES = 8
V7X_VMEM_BYTES = 64 * 1024 * 1024

CHUNK = 64
SUB = 8
LEVELS = (32, 16, 8)
MOE_BM = 256


def _vmem_limit(nbytes):
    return int(min(nbytes * 3 // 2 + (4 << 20), V7X_VMEM_BYTES - (6 << 20)))


def _log2(n):
    assert n & (n - 1) == 0
    return n.bit_length() - 1


def _sigmoid(x):
    return 1.0 / (1.0 + jnp.exp(-x))


def _rms(x, gain):
    return x * lax.rsqrt(jnp.mean(x * x, axis=-1, keepdims=True) + EPS) * gain


def _dot(a, b):
    return jnp.dot(a, b, preferred_element_type=F32)


def _dot_nt(a, b):
    return lax.dot_general(a, b, (((1,), (1,)), ((), ())), preferred_element_type=F32)


def _ada_kernel(c_ref, w_ref, b_ref, o_ref):
    c = c_ref[...]
    s = (c * _sigmoid(c)).astype(BF16)
    o_ref[...] = _dot(s, w_ref[...].astype(BF16)) + b_ref[...]


def _ada(cc, w, b):
    rows, d = cc.shape
    n = w.shape[1]
    tn = 1536 if n % 1536 == 0 else n
    return pl.pallas_call(
        _ada_kernel,
        out_shape=jax.ShapeDtypeStruct((rows, n), F32),
        grid=(n // tn,),
        in_specs=[pl.BlockSpec((rows, d), lambda j: (0, 0)),
                  pl.BlockSpec((d, tn), lambda j: (0, j)),
                  pl.BlockSpec((1, tn), lambda j: (0, j))],
        out_specs=pl.BlockSpec((rows, tn), lambda j: (0, j)),
        compiler_params=pltpu.CompilerParams(
            dimension_semantics=("arbitrary",),
            vmem_limit_bytes=_vmem_limit(2 * d * tn * 4 + d * tn * 2)),
        name="ada",
    )(cc, w, b)


def _inproj_kernel(x_ref, mod_ref, gpre_ref, w_ref, lbl_ref, gn_ref, *outs, mod_row, full):
    d = x_ref.shape[-1]
    r = pl.program_id(0) if mod_row is None else mod_row
    sh = mod_ref[pl.ds(r, 1), pl.ds(0, d)]
    sc = mod_ref[pl.ds(r, 1), pl.ds(d, d)]
    h = (_rms(x_ref[0], gpre_ref[...]) * (1.0 + sc) + sh).astype(BF16)

    def proj(g):
        return _dot(h, w_ref[:, g * HG_WIDTH:(g + 1) * HG_WIDTH])

    def put(ref, val):
        for hh in range(HG_HEADS):
            ref[0, hh] = val[:, hh * HEAD_DIM:(hh + 1) * HEAD_DIM]

    def lower_bound(direction):
        l0 = lbl_ref[pl.ds(direction, 1), :]
        l1 = lbl_ref[pl.ds(2 + direction, 1), :]
        m = jnp.maximum(l0, l1)
        e0 = jnp.exp(l0 - m)
        return e0 / (e0 + jnp.exp(l1 - m))

    if full:
        q_o, lff_o, kf_o, lfb_o, kb_o, v_o, gate_o, u_o = outs
        q = proj(0)
        put(q_o, q * _sigmoid(q))
    else:
        lff_o, kf_o, lfb_o, kb_o, v_o = outs
    for direction, (lf_o, k_o) in enumerate(((lff_o, kf_o), (lfb_o, kb_o))):
        lb = lower_bound(direction)
        sg = _sigmoid(proj(1 + direction))
        put(lf_o, jnp.log(lb + (1.0 - lb) * sg))
        put(k_o, (1.0 - lb) * (1.0 - sg))
    put(v_o, proj(3))
    if full:
        put(gate_o, gn_ref[...] * _sigmoid(proj(4)))
        put(u_o, proj(5))


def _inproj(x, mod, gpre, w_in, lbl, gn, *, mod_row, full):
    b, t, d = x.shape
    tm = min(512, t)
    n_out = 8 if full else 5
    n_cols = w_in.shape[1]
    head = pl.BlockSpec((1, HG_HEADS, tm, HEAD_DIM), lambda i, j: (i, 0, j, 0))
    kern = functools.partial(_inproj_kernel, mod_row=mod_row, full=full)
    est = 2 * tm * d * 4 + 2 * d * n_cols * 2 + n_out * 2 * tm * HG_WIDTH * 4 + 4 * tm * HG_WIDTH * 4
    return pl.pallas_call(
        kern,
        out_shape=[jax.ShapeDtypeStruct((b, HG_HEADS, t, HEAD_DIM), F32)] * n_out,
        grid=(b, t // tm),
        in_specs=[pl.BlockSpec((1, tm, d), lambda i, j: (i, j, 0)),
                  pl.BlockSpec(mod.shape, lambda i, j: (0, 0)),
                  pl.BlockSpec((1, d), lambda i, j: (0, 0)),
                  pl.BlockSpec(w_in.shape, lambda i, j: (0, 0)),
                  pl.BlockSpec(lbl.shape, lambda i, j: (0, 0)),
                  pl.BlockSpec((1, HG_WIDTH), lambda i, j: (0, 0))],
        out_specs=[head] * n_out,
        compiler_params=pltpu.CompilerParams(
            dimension_semantics=("arbitrary", "arbitrary"),
            vmem_limit_bytes=_vmem_limit(est)),
        name="inproj_full" if full else "inproj_ctx",
    )(x, mod, gpre, w_in, lbl, gn)


def _chunk_consts(rev):
    row = lax.broadcasted_iota(I32, (CHUNK, CHUNK), 0)
    col = lax.broadcasted_iota(I32, (CHUNK, CHUNK), 1)
    tri = jnp.where((col >= row) if rev else (col <= row), 1.0, 0.0).astype(BF16)
    same = [(row >> _log2(2 * h)) == (col >> _log2(2 * h)) for h in LEVELS]
    rowl = lax.broadcasted_iota(I32, (CHUNK, LANES), 0)
    want = 0 if rev else 1
    isq = [((rowl >> _log2(h)) & 1) == want for h in LEVELS]
    r8 = lax.broadcasted_iota(I32, (SUB, LANES), 0)
    return tri, same, isq, r8


def _hgrn_chunk(q, k, v, lf, st, consts, rev, need_o):
    tri, same, isq, r8 = consts
    hi = lf.astype(BF16)
    lo = (lf - hi.astype(F32)).astype(BF16)
    b = _dot(tri, hi) + _dot(tri, lo)
    edge = b[0:1, :] if rev else b[CHUNK - 1:CHUNK, :]
    kd = k * jnp.exp(edge - b)
    st_new = st * jnp.exp(edge) + _dot(v.T.astype(BF16), kd.astype(BF16))
    if not need_o:
        return None, st_new

    o = _dot_nt((q * jnp.exp(b)).astype(BF16), st.astype(BF16))

    att = jnp.zeros((CHUNK, CHUNK), F32)
    for li, h in enumerate(LEVELS):
        pieces = []
        for m in range(CHUNK // (2 * h)):
            idx = m * 2 * h + (h if rev else h - 1)
            pieces.append(jnp.broadcast_to(b[idx:idx + 1, :], (2 * h, LANES)))
        ref_rows = pieces[0] if len(pieces) == 1 else jnp.concatenate(pieces, axis=0)
        e = jnp.exp(-jnp.abs(b - ref_rows))
        qt = jnp.where(isq[li], q * e, 0.0).astype(BF16)
        kt = jnp.where(isq[li], 0.0, k * e).astype(BF16)
        att = att + jnp.where(same[li], _dot_nt(qt, kt), 0.0)
    o = o + _dot(att.astype(BF16), v.astype(BF16))

    diag = []
    for m in range(CHUNK // SUB):
        sl = slice(m * SUB, (m + 1) * SUB)
        qm, km, bm, vm = q[sl], k[sl], b[sl], v[sl]
        om = jnp.zeros((SUB, LANES), F32)
        for s in range(SUB):
            ex = jnp.exp(jnp.minimum(bm - bm[s:s + 1, :], 0.0))
            a = jnp.sum(qm * km[s:s + 1, :] * ex, axis=1, keepdims=True)
            keep = (r8 <= s) if rev else (r8 >= s)
            om = om + jnp.where(keep, a * vm[s:s + 1, :], 0.0)
        diag.append(om)
    return o + jnp.concatenate(diag, axis=0), st_new


def _hgrn_kernel(*refs, rev, need_o, final, want_state, n_chunks):
    it = iter(refs)
    q_ref = next(it) if need_o else None
    k_ref, v_ref, lf_ref, s0_ref = next(it), next(it), next(it), next(it)
    prev_ref = next(it) if final else None
    gate_ref = next(it) if final else None
    o_ref = next(it) if need_o else None
    sout_ref = next(it) if want_state else None
    st_ref = next(it)

    j = pl.program_id(2)
    nb = pl.num_programs(2)

    @pl.when(j == 0)
    def _():
        st_ref[...] = s0_ref[0, 0]

    def body(ci, st):
        consts = _chunk_consts(rev)
        c = (n_chunks - 1 - ci) if rev else ci
        rows = pl.ds(pl.multiple_of(c * CHUNK, CHUNK), CHUNK)
        q = q_ref[0, 0, rows, :] if need_o else None
        o, st_new = _hgrn_chunk(q, k_ref[0, 0, rows, :], v_ref[0, 0, rows, :],
                                lf_ref[0, 0, rows, :], st, consts, rev, need_o)
        if need_o:
            if final:
                tot = o + prev_ref[0, 0, rows, :]
                o = tot * lax.rsqrt(jnp.mean(tot * tot, axis=-1, keepdims=True) + EPS)
                o = o * gate_ref[0, 0, rows, :]
            o_ref[0, 0, rows, :] = o
        return st_new

    st = lax.fori_loop(0, n_chunks, body, st_ref[...])
    st_ref[...] = st

    if want_state:
        @pl.when(j == nb - 1)
        def _():
            sout_ref[0, 0] = st


def _hgrn(q, k, v, lf, s0, prev=None, gate=None, *, rev, need_o, want_state, name):
    b, hh, t, _ = k.shape
    tb = min(512, t)
    nb = t // tb
    final = prev is not None

    def tmap(i, h, j):
        return (i, h, (nb - 1 - j) if rev else j, 0)

    seq = pl.BlockSpec((1, 1, tb, HEAD_DIM), tmap)
    state = pl.BlockSpec((1, 1, HEAD_DIM, HEAD_DIM), lambda i, h, j: (i, h, 0, 0))
    args, in_specs = [], []
    for a in ((q,) if need_o else ()) + (k, v, lf):
        args.append(a)
        in_specs.append(seq)
    args.append(s0)
    in_specs.append(state)
    if final:
        args += [prev, gate]
        in_specs += [seq, seq]
    out_shape, out_specs = [], []
    if need_o:
        out_shape.append(jax.ShapeDtypeStruct((b, hh, t, HEAD_DIM), F32))
        out_specs.append(seq)
    if want_state:
        out_shape.append(jax.ShapeDtypeStruct((b, hh, HEAD_DIM, HEAD_DIM), F32))
        out_specs.append(state)
    kern = functools.partial(_hgrn_kernel, rev=rev, need_o=need_o, final=final,
                             want_state=want_state, n_chunks=tb // CHUNK)
    est = 2 * (len(args) + len(out_shape)) * tb * HEAD_DIM * 4
    return pl.pallas_call(
        kern,
        out_shape=out_shape,
        grid=(b, hh, nb),
        in_specs=in_specs,
        out_specs=out_specs,
        scratch_shapes=[pltpu.VMEM((HEAD_DIM, HEAD_DIM), F32)],
        compiler_params=pltpu.CompilerParams(
            dimension_semantics=("arbitrary", "arbitrary", "arbitrary"),
            vmem_limit_bytes=_vmem_limit(est)),
        name=name,
    )(*args)


def _window_sum(x, w, stride, pos, extent):
    n = x.shape[0]

    def ahead(y, dist):
        return jnp.where(pos + dist < extent, pltpu.roll(y, (n - dist * stride) % n, 0), 0.0)

    def behind(y, dist):
        return jnp.where(pos >= dist, pltpu.roll(y, dist * stride, 0), 0.0)

    fwd, bwd, h = x, x, 1
    while h < w // 2:
        fwd = fwd + ahead(fwd, h)
        bwd = bwd + behind(bwd, h)
        h *= 2
    return fwd + behind(bwd, 1)


def _window_count(pos, w, extent):
    return (jnp.minimum(pos + w // 2, extent) - jnp.maximum(pos - w // 2, 0)).astype(F32)


def _pool_kernel(u_ref, pw_ref, ps_ref, o_ref):
    g = pl.program_id(1)
    n = u_ref.shape[2]
    tok = lax.broadcasted_iota(I32, (n, LANES), 0)
    col = tok & (GRID_W - 1)
    row = tok >> _log2(GRID_W)
    for gi, w in enumerate(POOL_WINDOWS):
        @pl.when(g == gi)
        def _(w=w):
            u = u_ref[0, 0]
            rows = n // GRID_W
            s = _window_sum(_window_sum(u, w, GRID_W, row, rows), w, 1, col, GRID_W)
            m = s / (_window_count(row, w, rows) * _window_count(col, w, GRID_W))
            o_ref[0, 0] = _dot((m - u).astype(BF16), pw_ref[0]) * ps_ref[0]


def _pool(u, pw, ps):
    b, g, t, c = u.shape
    blk = pl.BlockSpec((1, 1, t, c), lambda i, j: (i, j, 0, 0))
    return pl.pallas_call(
        _pool_kernel,
        out_shape=jax.ShapeDtypeStruct(u.shape, F32),
        grid=(b, g),
        in_specs=[blk,
                  pl.BlockSpec((1, c, c), lambda i, j: (j, 0, 0)),
                  pl.BlockSpec((1, 1, c), lambda i, j: (j, 0, 0))],
        out_specs=blk,
        compiler_params=pltpu.CompilerParams(
            dimension_semantics=("arbitrary", "arbitrary"),
            vmem_limit_bytes=_vmem_limit(12 * t * c * 4)),
        name="pool",
    )(u, pw, ps)


def _outproj_kernel(hg_ref, pool_ref, x_ref, w_ref, mod_ref, gpost_ref, gpre_ref, wr_ref, br_ref,
                    x1_ref, h2_ref, ti_ref, tw_ref):
    d = x_ref.shape[-1]
    r = pl.program_id(0)
    gt1 = mod_ref[pl.ds(r, 1), pl.ds(2 * d, d)]
    sh2 = mod_ref[pl.ds(r, 1), pl.ds(3 * d, d)]
    sc2 = mod_ref[pl.ds(r, 1), pl.ds(4 * d, d)]
    cat = jnp.concatenate([hg_ref[0, hh] for hh in range(HG_HEADS)]
                          + [pool_ref[0, g] for g in range(len(POOL_WINDOWS))], axis=1)
    y = _dot(cat.astype(BF16), w_ref[...])
    x1 = x_ref[0] + gt1 * _rms(y, gpost_ref[...])
    x1_ref[0] = x1
    h2 = _rms(x1, gpre_ref[...]) * (1.0 + sc2) + sh2
    h2_ref[0] = h2

    lane = lax.broadcasted_iota(I32, (h2.shape[0], LANES), 1).astype(F32)
    logits = jnp.where(lane < N_EXPERTS, _dot(h2.astype(BF16), wr_ref[...]) + br_ref[...], -jnp.inf)
    ti = jnp.zeros(logits.shape, F32)
    tw = jnp.zeros(logits.shape, F32)
    top = None
    for jj in range(TOP_K):
        m = jnp.max(logits, axis=1, keepdims=True)
        idx = jnp.min(jnp.where(logits == m, lane, float(LANES)), axis=1, keepdims=True)
        logits = jnp.where(lane == idx, -jnp.inf, logits)
        top = m if top is None else top
        ti = jnp.where(lane == jj, idx, ti)
        tw = jnp.where(lane == jj, jnp.exp(m - top), tw)
    ti_ref[0] = ti.astype(I32)
    tw_ref[0] = tw / jnp.sum(tw, axis=1, keepdims=True)


def _outproj(hg, pool, x, w_out, mod, gpost, gpre, wr, br):
    b, t, d = x.shape
    tm = min(512, t)
    head = pl.BlockSpec((1, HG_HEADS, tm, HEAD_DIM), lambda i, j: (i, 0, j, 0))
    tok = pl.BlockSpec((1, tm, d), lambda i, j: (i, j, 0))
    lanes = pl.BlockSpec((1, tm, LANES), lambda i, j: (i, j, 0))
    vec = pl.BlockSpec((1, d), lambda i, j: (0, 0))
    est = 2 * (2 * tm * HG_WIDTH * 4 + 3 * tm * d * 4 + 2 * tm * LANES * 4) + 2 * d * d * 2 + 6 * tm * d * 4
    return pl.pallas_call(
        _outproj_kernel,
        out_shape=[jax.ShapeDtypeStruct((b, t, d), F32), jax.ShapeDtypeStruct((b, t, d), F32),
                   jax.ShapeDtypeStruct((b, t, LANES), I32), jax.ShapeDtypeStruct((b, t, LANES), F32)],
        grid=(b, t // tm),
        in_specs=[head, head, tok,
                  pl.BlockSpec(w_out.shape, lambda i, j: (0, 0)),
                  pl.BlockSpec(mod.shape, lambda i, j: (0, 0)),
                  vec, vec,
                  pl.BlockSpec(wr.shape, lambda i, j: (0, 0)),
                  pl.BlockSpec((1, LANES), lambda i, j: (0, 0))],
        out_specs=[tok, tok, lanes, lanes],
        compiler_params=pltpu.CompilerParams(
            dimension_semantics=("arbitrary", "arbitrary"),
            vmem_limit_bytes=_vmem_limit(est)),
        name="outproj",
    )(hg, pool, x, w_out, mod, gpost, gpre, wr, br)


def _rank_kernel(ti_ref, rank_ref, cnt_ref, carry_ref):
    i = pl.program_id(0)

    @pl.when(i == 0)
    def _():
        carry_ref[...] = jnp.zeros(carry_ref.shape, F32)

    ti = ti_ref[...]
    tm = ti.shape[0]
    lane = lax.broadcasted_iota(I32, (tm, LANES), 1)
    onehot = [lane == ti[:, jj:jj + 1] for jj in range(TOP_K)]
    cnt = sum(jnp.where(oh, 1.0, 0.0) for oh in onehot)
    row = lax.broadcasted_iota(I32, (tm, tm), 0)
    col = lax.broadcasted_iota(I32, (tm, tm), 1)
    before = jnp.where(col < row, 1.0, 0.0).astype(BF16)
    prefix = _dot(before, cnt.astype(BF16)) + carry_ref[...]
    rank = jnp.zeros((tm, LANES), F32)
    for jj in range(TOP_K):
        rj = jnp.sum(jnp.where(onehot[jj], prefix, 0.0), axis=1, keepdims=True)
        rank = jnp.where(lane == jj, rj, rank)
    rank_ref[...] = rank.astype(I32)
    carry = carry_ref[...] + jnp.sum(cnt, axis=0, keepdims=True)
    carry_ref[...] = carry
    cnt_ref[...] = carry.astype(I32)


def _rank(ti):
    t = ti.shape[0]
    tm = min(512, t)
    return pl.pallas_call(
        _rank_kernel,
        out_shape=[jax.ShapeDtypeStruct((t, LANES), I32), jax.ShapeDtypeStruct((1, LANES), I32)],
        grid=(t // tm,),
        in_specs=[pl.BlockSpec((tm, LANES), lambda i: (i, 0))],
        out_specs=[pl.BlockSpec((tm, LANES), lambda i: (i, 0)),
                   pl.BlockSpec((1, LANES), lambda i: (0, 0))],
        scratch_shapes=[pltpu.VMEM((1, LANES), F32)],
        compiler_params=pltpu.CompilerParams(dimension_semantics=("arbitrary",)),
        name="rank",
    )(ti)


def _dispatch_kernel(dest_ref, h_ref, xs_ref, sem):
    tm = h_ref.shape[0]

    def row_copy(r, d):
        return pltpu.make_async_copy(h_ref.at[pl.ds(r, 1), :], xs_ref.at[pl.ds(d, 1), :], sem)

    def start(r, carry):
        for jj in range(TOP_K):
            row_copy(r, dest_ref[r * TOP_K + jj]).start()
        return carry

    def wait(r, carry):
        for jj in range(TOP_K):
            row_copy(r, dest_ref[r * TOP_K + jj]).wait()
        return carry

    lax.fori_loop(0, tm, start, 0)
    lax.fori_loop(0, tm, wait, 0)


def _dispatch(dest, h2):
    t, d = h2.shape
    tm = min(256, t)
    return pl.pallas_call(
        _dispatch_kernel,
        out_shape=jax.ShapeDtypeStruct((t * TOP_K, d), h2.dtype),
        grid=(t // tm,),
        in_specs=[pl.BlockSpec((tm * TOP_K,), lambda i: (i,), memory_space=pltpu.SMEM),
                  pl.BlockSpec((tm, d), lambda i: (i, 0))],
        out_specs=pl.BlockSpec(memory_space=pl.ANY),
        scratch_shapes=[pltpu.SemaphoreType.DMA],
        compiler_params=pltpu.CompilerParams(dimension_semantics=("arbitrary",)),
        name="dispatch",
    )(dest, h2)


def _moe_kernel(ie_ref, ib_ref, lo_ref, hi_ref, first_ref,
                x_ref, wg_ref, wu_ref, wd_ref, bg_ref, bu_ref, bd_ref, y_ref,
                wgb, wub, wdb):
    i = pl.program_id(0)
    prev = jnp.maximum(i - 1, 0)
    changed = jnp.logical_or(i == 0, ie_ref[i] != ie_ref[prev])
    lo, hi = lo_ref[i], hi_ref[i]

    @pl.when(jnp.logical_and(changed, hi > lo))
    def _():
        wgb[...] = wg_ref[...].astype(BF16)
        wub[...] = wu_ref[...].astype(BF16)
        wdb[...] = wd_ref[...].astype(BF16)

    @pl.when(hi > lo)
    def _():
        x = x_ref[...].astype(BF16)
        gate = jnp.minimum(_dot(x, wgb[...]) + bg_ref[...], SWIGLU_LIMIT)
        up = jnp.clip(_dot(x, wub[...]) + bu_ref[...], -SWIGLU_LIMIT, SWIGLU_LIMIT)
        hmid = (up + 1.0) * gate * _sigmoid(SWIGLU_ALPHA * gate)
        y = _dot(hmid.astype(BF16), wdb[...]) + bd_ref[...]
        row = lax.broadcasted_iota(I32, y.shape, 0)
        mine = jnp.logical_and(row >= lo, row < hi)

        @pl.when(first_ref[i] == 1)
        def _():
            y_ref[...] = jnp.where(mine, y, 0.0)

        @pl.when(first_ref[i] == 0)
        def _():
            y_ref[...] = jnp.where(mine, y, y_ref[...])


def _moe(items, xs, wg, wu, wd, bg, bu, bd):
    a, d = xs.shape
    f = wg.shape[-1]
    n_items = items[0].shape[0]
    rows = pl.BlockSpec((MOE_BM, d), lambda i, ie, ib, lo, hi, fi: (ib[i], 0))

    def wspec(k, n):
        return pl.BlockSpec((None, k, n), lambda i, ie, ib, lo, hi, fi: (ie[i], 0, 0))

    def bspec(n):
        return pl.BlockSpec((None, 1, n), lambda i, ie, ib, lo, hi, fi: (ie[i], 0, 0))

    est = 2 * 3 * d * f * 4 + 3 * d * f * 2 + 4 * MOE_BM * d * 4 + 4 * MOE_BM * f * 4
    return pl.pallas_call(
        _moe_kernel,
        out_shape=jax.ShapeDtypeStruct((a, d), F32),
        grid_spec=pltpu.PrefetchScalarGridSpec(
            num_scalar_prefetch=5,
            grid=(n_items,),
            in_specs=[rows, wspec(d, f), wspec(d, f), wspec(f, d), bspec(f), bspec(f), bspec(d)],
            out_specs=rows,
            scratch_shapes=[pltpu.VMEM((d, f), BF16), pltpu.VMEM((d, f), BF16),
                            pltpu.VMEM((f, d), BF16)]),
        compiler_params=pltpu.CompilerParams(
            dimension_semantics=("arbitrary",),
            vmem_limit_bytes=_vmem_limit(est)),
        name="moe",
    )(*items, xs, wg, wu, wd, bg, bu, bd)


def _moe_items(counts, n_rows):
    n_blk = n_rows // MOE_BM
    n_items = n_blk + N_EXPERTS - 1
    ends = jnp.cumsum(counts)
    starts = ends - counts
    first_blk = starts // MOE_BM
    last_blk = (ends - 1) // MOE_BM
    per_e = jnp.where(counts > 0, last_blk - first_blk + 1, 0)
    item_end = jnp.cumsum(per_e)
    item_start = item_end - per_e
    total = item_end[-1]
    i = jnp.arange(n_items, dtype=I32)
    e = jnp.minimum(jnp.searchsorted(item_end, i, side="right"), N_EXPERTS - 1).astype(I32)
    blk = first_blk[e] + (i - item_start[e])
    lo = jnp.maximum(starts[e], blk * MOE_BM) - blk * MOE_BM
    hi = jnp.minimum(ends[e], (blk + 1) * MOE_BM) - blk * MOE_BM
    live = i < total
    last = jnp.maximum(total - 1, 0)
    e = jnp.where(live, e, e[last])
    blk = jnp.where(live, blk, n_blk - 1)
    lo = jnp.where(live, lo, 0)
    hi = jnp.where(live, hi, 0)
    prev_blk = jnp.concatenate([jnp.full((1,), -1, I32), blk[:-1]])
    first = jnp.where(jnp.logical_and(live, blk != prev_blk), 1, 0)
    return tuple(z.astype(I32) for z in (e, blk, lo, hi, first))


def _combine_kernel(dest_ref, ys_ref, tw_ref, x1_ref, mod_ref, gpost_ref, o_ref, buf, sem,
                    *, tiles_per_batch):
    tm, d = x1_ref.shape

    def row_copy(r, jj):
        return pltpu.make_async_copy(ys_ref.at[pl.ds(dest_ref[r * TOP_K + jj], 1), :],
                                     buf.at[jj, pl.ds(r, 1), :], sem)

    def start(r, carry):
        for jj in range(TOP_K):
            row_copy(r, jj).start()
        return carry

    def wait(r, carry):
        for jj in range(TOP_K):
            row_copy(r, jj).wait()
        return carry

    lax.fori_loop(0, tm, start, 0)
    lax.fori_loop(0, tm, wait, 0)

    tw = tw_ref[...]
    y = tw[:, 0:1] * buf[0]
    for jj in range(1, TOP_K):
        y = y + tw[:, jj:jj + 1] * buf[jj]
    r = pl.program_id(0) // tiles_per_batch
    gt2 = mod_ref[pl.ds(r, 1), pl.ds(5 * d, d)]
    o_ref[...] = x1_ref[...] + gt2 * _rms(y, gpost_ref[...])


def _combine(dest, ys, tw, x1, mod, gpost, tokens_per_batch):
    t, d = x1.shape
    tm = min(256, tokens_per_batch)
    kern = functools.partial(_combine_kernel, tiles_per_batch=tokens_per_batch // tm)
    est = TOP_K * tm * d * 4 + 2 * (2 * tm * d * 4 + tm * LANES * 4) + 3 * tm * d * 4
    return pl.pallas_call(
        kern,
        out_shape=jax.ShapeDtypeStruct((t, d), F32),
        grid=(t // tm,),
        in_specs=[pl.BlockSpec((tm * TOP_K,), lambda i: (i,), memory_space=pltpu.SMEM),
                  pl.BlockSpec(memory_space=pl.ANY),
                  pl.BlockSpec((tm, LANES), lambda i: (i, 0)),
                  pl.BlockSpec((tm, d), lambda i: (i, 0)),
                  pl.BlockSpec(mod.shape, lambda i: (0, 0)),
                  pl.BlockSpec((1, d), lambda i: (0, 0))],
        out_specs=pl.BlockSpec((tm, d), lambda i: (i, 0)),
        scratch_shapes=[pltpu.VMEM((TOP_K, tm, d), F32), pltpu.SemaphoreType.DMA],
        compiler_params=pltpu.CompilerParams(
            dimension_semantics=("arbitrary",),
            vmem_limit_bytes=_vmem_limit(est)),
        name="combine",
    )(dest, ys, tw, x1, mod, gpost)


def kernel(x, c, ctx, c_ctx, w_ada, b_ada, g_pre_mix, g_post_mix, g_pre_ffn, g_post_ffn,
           w_in, w_out, hgrn_lb_logits, hgrn_norm, pool_w, pool_scale,
           w_router, b_router, w_gate, b_gate, w_up, b_up, w_down, b_down):
    b, s, d = x.shape
    layer = 0
    n_e = w_router.shape[-1]

    cc = jnp.concatenate([c, c_ctx[None, :]], axis=0)
    cc = jnp.pad(cc, ((0, -(b + 1) % SUBLANES), (0, 0)))
    mod = _ada(cc, w_ada[layer], b_ada[layer][None, :])

    w_in_b = w_in[layer].astype(BF16)
    lbl = hgrn_lb_logits[:2].reshape(4, HG_WIDTH)
    gn = hgrn_norm[layer][None, :]
    gpre = g_pre_mix[layer][None, :]

    lff_c, kf_c, lfb_c, kb_c, v_c = _inproj(ctx, mod, gpre, w_in_b, lbl, gn, mod_row=b, full=False)
    zeros = jnp.zeros((b, HG_HEADS, HEAD_DIM, HEAD_DIM), F32)
    (s_f,) = _hgrn(None, kf_c, v_c, lff_c, zeros, rev=False, need_o=False, want_state=True,
                   name="hgrn_ctx_fwd")
    (s_b,) = _hgrn(None, kb_c, v_c, lfb_c, zeros, rev=True, need_o=False, want_state=True,
                   name="hgrn_ctx_bwd")

    q, lff, kf, lfb, kb, v, gate, u = _inproj(x, mod, gpre, w_in_b, lbl, gn, mod_row=None, full=True)
    (o_f,) = _hgrn(q, kf, v, lff, s_f, rev=False, need_o=True, want_state=False, name="hgrn_fwd")
    (o_hg,) = _hgrn(q, kb, v, lfb, s_b, o_f, gate, rev=True, need_o=True, want_state=False,
                    name="hgrn_bwd")
    o_pool = _pool(u, pool_w[layer].astype(BF16), pool_scale[layer].reshape(len(POOL_WINDOWS), 1, POOL_GROUP))

    wr = jnp.pad(w_router[layer], ((0, 0), (0, LANES - n_e))).astype(BF16)
    br = jnp.pad(b_router[layer], (0, LANES - n_e))[None, :]
    x1, h2, ti, tw = _outproj(o_hg, o_pool, x, w_out[layer].astype(BF16), mod,
                              g_post_mix[layer][None, :], g_pre_ffn[layer][None, :], wr, br)

    t = b * s
    ti = ti.reshape(t, LANES)
    rank, cnt = _rank(ti)
    counts = cnt[0, :n_e]
    starts = jnp.cumsum(counts) - counts
    dest = (starts[ti[:, :TOP_K]] + rank[:, :TOP_K]).reshape(t * TOP_K).astype(I32)
    items = _moe_items(counts, t * TOP_K)

    xs = _dispatch(dest, h2.reshape(t, d))
    ys = _moe(items, xs, w_gate[layer], w_up[layer], w_down[layer],
              b_gate[layer][:, None, :], b_up[layer][:, None, :], b_down[layer][:, None, :])
    out = _combine(dest, ys, tw.reshape(t, LANES), x1.reshape(t, d), mod,
                   g_post_ffn[layer][None, :], s)
    return out.reshape(b, s, d)
```

```python
import functools
import math

import numpy as np
import jax
import jax.numpy as jnp
from jax import lax
from jax.experimental import pallas as pl
from jax.experimental.pallas import tpu as pltpu

F32 = jnp.float32
BF16 = jnp.bfloat16
I32 = jnp.int32

GRID_W = 64
HG_HEADS = 4
HEAD_DIM = 128
HG_WIDTH = HG_HEADS * HEAD_DIM
POOL_WINDOWS = (2, 4, 8, 16)
POOL_GROUP = 128
N_EXPERTS = 32
TOP_K = 4
SWIGLU_LIMIT = 7.0
SWIGLU_ALPHA = 1.702
EPS = 1e-6

LANES = 128
SUBLANES = 8
V7X_VMEM_BYTES = 64 * 1024 * 1024

CHUNK = 64
SUB = 8
LEVELS = (32, 16, 8)
MOE_BM = 256
LOG2_E = math.log2(math.e)
MASKED_EXPONENT = -1e30


def _vmem_limit(nbytes):
    return int(min(nbytes * 3 // 2 + (4 << 20), V7X_VMEM_BYTES - (6 << 20)))


def _log2(n):
    assert n & (n - 1) == 0
    return n.bit_length() - 1


def _sigmoid(x):
    return 1.0 / (1.0 + jnp.exp(-x))


def _rms(x, gain):
    return x * lax.rsqrt(jnp.mean(x * x, axis=-1, keepdims=True) + EPS) * gain


def _dot(a, b):
    return jnp.dot(a, b, preferred_element_type=F32)


def _dot_nt(a, b):
    return lax.dot_general(a, b, (((1,), (1,)), ((), ())), preferred_element_type=F32)


def _ada_kernel(c_ref, w_ref, b_ref, o_ref):
    c = c_ref[...]
    s = (c * _sigmoid(c)).astype(BF16)
    o_ref[...] = _dot(s, w_ref[...].astype(BF16)) + b_ref[...]


def _ada(cc, w, b):
    rows, d = cc.shape
    n = w.shape[1]
    tn = 1536 if n % 1536 == 0 else n
    return pl.pallas_call(
        _ada_kernel,
        out_shape=jax.ShapeDtypeStruct((rows, n), F32),
        grid=(n // tn,),
        in_specs=[pl.BlockSpec((rows, d), lambda j: (0, 0)),
                  pl.BlockSpec((d, tn), lambda j: (0, j)),
                  pl.BlockSpec((1, tn), lambda j: (0, j))],
        out_specs=pl.BlockSpec((rows, tn), lambda j: (0, j)),
        compiler_params=pltpu.CompilerParams(
            dimension_semantics=("arbitrary",),
            vmem_limit_bytes=_vmem_limit(2 * d * tn * 4 + d * tn * 2)),
        name="ada",
    )(cc, w, b)


def _inproj_kernel(x_ref, mod_ref, gpre_ref, w_ref, lbl_ref, gn_ref, *outs, mod_row, full):
    d = x_ref.shape[-1]
    r = pl.program_id(0) if mod_row is None else mod_row
    sh = mod_ref[pl.ds(r, 1), pl.ds(0, d)]
    sc = mod_ref[pl.ds(r, 1), pl.ds(d, d)]
    h = (_rms(x_ref[0], gpre_ref[...]) * (1.0 + sc) + sh).astype(BF16)

    def proj(g):
        return _dot(h, w_ref[:, g * HG_WIDTH:(g + 1) * HG_WIDTH])

    def put(ref, val):
        for hh in range(HG_HEADS):
            ref[0, hh] = val[:, hh * HEAD_DIM:(hh + 1) * HEAD_DIM]

    def lower_bound(direction):
        l0 = lbl_ref[pl.ds(direction, 1), :]
        l1 = lbl_ref[pl.ds(2 + direction, 1), :]
        m = jnp.maximum(l0, l1)
        e0 = jnp.exp(l0 - m)
        return e0 / (e0 + jnp.exp(l1 - m))

    if full:
        q_o, lff_o, kf_o, lfb_o, kb_o, v_o, gate_o, u_o = outs
        q = proj(0)
        put(q_o, q * _sigmoid(q))
    else:
        lff_o, kf_o, lfb_o, kb_o, v_o = outs
    for direction, (lf_o, k_o) in enumerate(((lff_o, kf_o), (lfb_o, kb_o))):
        lb = lower_bound(direction)
        sg = _sigmoid(proj(1 + direction))
        put(lf_o, jnp.log(lb + (1.0 - lb) * sg))
        put(k_o, (1.0 - lb) * (1.0 - sg))
    put(v_o, proj(3))
    if full:
        put(gate_o, gn_ref[...] * _sigmoid(proj(4)))
        put(u_o, proj(5))


def _inproj(x, mod, gpre, w_in, lbl, gn, *, mod_row, full):
    b, t, d = x.shape
    tm = min(512, t)
    n_out = 8 if full else 5
    n_cols = w_in.shape[1]
    head = pl.BlockSpec((1, HG_HEADS, tm, HEAD_DIM), lambda i, j: (i, 0, j, 0))
    kern = functools.partial(_inproj_kernel, mod_row=mod_row, full=full)
    est = 2 * tm * d * 4 + 2 * d * n_cols * 2 + n_out * 2 * tm * HG_WIDTH * 4 + 4 * tm * HG_WIDTH * 4
    return pl.pallas_call(
        kern,
        out_shape=[jax.ShapeDtypeStruct((b, HG_HEADS, t, HEAD_DIM), F32)] * n_out,
        grid=(b, t // tm),
        in_specs=[pl.BlockSpec((1, tm, d), lambda i, j: (i, j, 0)),
                  pl.BlockSpec(mod.shape, lambda i, j: (0, 0)),
                  pl.BlockSpec((1, d), lambda i, j: (0, 0)),
                  pl.BlockSpec(w_in.shape, lambda i, j: (0, 0)),
                  pl.BlockSpec(lbl.shape, lambda i, j: (0, 0)),
                  pl.BlockSpec((1, HG_WIDTH), lambda i, j: (0, 0))],
        out_specs=[head] * n_out,
        compiler_params=pltpu.CompilerParams(
            dimension_semantics=("arbitrary", "arbitrary"),
            vmem_limit_bytes=_vmem_limit(est)),
        name="inproj_full" if full else "inproj_ctx",
    )(x, mod, gpre, w_in, lbl, gn)


def _hgrn_tables(rev):
    row = np.arange(CHUNK)[:, None]
    col = np.arange(CHUNK)[None, :]
    tri = (col >= row) if rev else (col <= row)
    same = np.stack([(row // (2 * h)) == (col // (2 * h)) for h in LEVELS])
    want = 0 if rev else 1
    mq = np.stack([np.broadcast_to(((row // h) % 2) == want, (CHUNK, LANES)) for h in LEVELS])
    t8 = np.arange(SUB)[None, :, None]
    s8 = np.arange(SUB)[:, None, None]
    visible = np.broadcast_to((t8 <= s8) if rev else (t8 >= s8), (SUB, SUB, LANES))
    return (jnp.asarray(tri, BF16), jnp.asarray(same, F32), jnp.asarray(mq, F32),
            jnp.asarray(~mq, F32), jnp.asarray(np.where(visible, 0.0, MASKED_EXPONENT), F32))


def _hgrn_chunk(q, k, v, lf, st, tabs, b_scr, krow, vrow, rev, need_o):
    tri_ref, same_ref, mq_ref, mk_ref, neg_ref = tabs
    hi = lf.astype(BF16)
    lo = (lf - hi.astype(F32)).astype(BF16)
    tri = tri_ref[...]
    b = (_dot(tri, hi) + _dot(tri, lo)) * LOG2_E
    edge = b[0:1, :] if rev else b[CHUNK - 1:CHUNK, :]
    kd = k * jnp.exp2(edge - b)
    st_new = st * jnp.exp2(edge) + _dot(v.T.astype(BF16), kd.astype(BF16))
    if not need_o:
        return None, st_new

    o = _dot_nt((q * jnp.exp2(b)).astype(BF16), st.astype(BF16))

    att = jnp.zeros((CHUNK, CHUNK), F32)
    for li, h in enumerate(LEVELS):
        pieces = []
        for m in range(CHUNK // (2 * h)):
            idx = m * 2 * h + (h if rev else h - 1)
            pieces.append(jnp.broadcast_to(b[idx:idx + 1, :], (2 * h, LANES)))
        ref_rows = pieces[0] if len(pieces) == 1 else jnp.concatenate(pieces, axis=0)
        e = jnp.exp2(-jnp.abs(b - ref_rows))
        qt = (q * e * mq_ref[li]).astype(BF16)
        kt = (k * e * mk_ref[li]).astype(BF16)
        att = att + _dot_nt(qt, kt) * same_ref[li]
    o = o + _dot(att.astype(BF16), v.astype(BF16))

    b_scr[...] = b
    diag = []
    for m in range(CHUNK // SUB):
        sl = slice(m * SUB, (m + 1) * SUB)
        qm, bm = q[sl], b[sl]
        om = jnp.zeros((SUB, LANES), F32)
        for s in range(SUB):
            i = m * SUB + s
            brow = jnp.broadcast_to(b_scr[pl.ds(i, 1), :], (SUB, LANES))
            ex = jnp.exp2(bm - brow + neg_ref[s])
            a = jnp.sum(qm * krow(i) * ex, axis=1, keepdims=True)
            om = om + a * vrow(i)
        diag.append(om)
    return o + jnp.concatenate(diag, axis=0), st_new


def _hgrn_kernel(*refs, rev, need_o, final, want_state, n_chunks):
    it = iter(refs)
    tabs = tuple(next(it) for _ in range(5))
    q_ref = next(it) if need_o else None
    k_ref, v_ref, lf_ref, s0_ref = next(it), next(it), next(it), next(it)
    prev_ref = next(it) if final else None
    gate_ref = next(it) if final else None
    o_ref = next(it) if need_o else None
    sout_ref = next(it) if want_state else None
    st_ref, b_scr = next(it), next(it)

    j = pl.program_id(1)
    nb = pl.num_programs(1)

    @pl.when(j == 0)
    def _():
        st_ref[...] = s0_ref[0]

    def body(ci, carry):
        c = (n_chunks - 1 - ci) if rev else ci
        base = pl.multiple_of(c * CHUNK, CHUNK)
        rows = pl.ds(base, CHUNK)
        for hh in range(HG_HEADS):
            def krow(i, hh=hh):
                return jnp.broadcast_to(k_ref[0, hh, pl.ds(base + i, 1), :], (SUB, LANES))

            def vrow(i, hh=hh):
                return jnp.broadcast_to(v_ref[0, hh, pl.ds(base + i, 1), :], (SUB, LANES))

            q = q_ref[0, hh, rows, :] if need_o else None
            o, st_new = _hgrn_chunk(q, k_ref[0, hh, rows, :], v_ref[0, hh, rows, :],
                                    lf_ref[0, hh, rows, :], st_ref[hh], tabs, b_scr.at[hh],
                                    krow, vrow, rev, need_o)
            st_ref[hh] = st_new
            if need_o:
                if final:
                    tot = o + prev_ref[0, hh, rows, :]
                    o = tot * lax.rsqrt(jnp.mean(tot * tot, axis=-1, keepdims=True) + EPS)
                    o = o * gate_ref[0, hh, rows, :]
                o_ref[0, hh, rows, :] = o
        return carry

    lax.fori_loop(0, n_chunks, body, 0)

    if want_state:
        @pl.when(j == nb - 1)
        def _():
            sout_ref[0] = st_ref[...]


def _hgrn(q, k, v, lf, s0, prev=None, gate=None, *, rev, need_o, want_state, name):
    b, hh, t, _ = k.shape
    tb = min(512, t)
    nb = t // tb
    final = prev is not None
    tabs = _hgrn_tables(rev)

    seq = pl.BlockSpec((1, hh, tb, HEAD_DIM), lambda i, j: (i, 0, (nb - 1 - j) if rev else j, 0))
    state = pl.BlockSpec((1, hh, HEAD_DIM, HEAD_DIM), lambda i, j: (i, 0, 0, 0))
    args = list(tabs)
    in_specs = [pl.BlockSpec(a.shape, functools.partial(lambda nd, i, j: (0,) * nd, a.ndim)) for a in tabs]
    for a in ((q,) if need_o else ()) + (k, v, lf):
        args.append(a)
        in_specs.append(seq)
    args.append(s0)
    in_specs.append(state)
    if final:
        args += [prev, gate]
        in_specs += [seq, seq]
    out_shape, out_specs = [], []
    if need_o:
        out_shape.append(jax.ShapeDtypeStruct((b, hh, t, HEAD_DIM), F32))
        out_specs.append(seq)
    if want_state:
        out_shape.append(jax.ShapeDtypeStruct((b, hh, HEAD_DIM, HEAD_DIM), F32))
        out_specs.append(state)
    kern = functools.partial(_hgrn_kernel, rev=rev, need_o=need_o, final=final,
                             want_state=want_state, n_chunks=tb // CHUNK)
    est = 2 * (len(args) + len(out_shape)) * hh * tb * HEAD_DIM * 4
    return pl.pallas_call(
        kern,
        out_shape=out_shape,
        grid=(b, nb),
        in_specs=in_specs,
        out_specs=out_specs,
        scratch_shapes=[pltpu.VMEM((hh, HEAD_DIM, HEAD_DIM), F32),
                        pltpu.VMEM((hh, CHUNK, LANES), F32)],
        compiler_params=pltpu.CompilerParams(
            dimension_semantics=("arbitrary", "arbitrary"),
            vmem_limit_bytes=_vmem_limit(est)),
        name=name,
    )(*args)


def _window_sum(x, w, stride, pos, extent):
    n = x.shape[0]

    def ahead(y, dist):
        return jnp.where(pos + dist < extent, pltpu.roll(y, (n - dist * stride) % n, 0), 0.0)

    def behind(y, dist):
        return jnp.where(pos >= dist, pltpu.roll(y, dist * stride, 0), 0.0)

    fwd, bwd, h = x, x, 1
    while h < w // 2:
        fwd = fwd + ahead(fwd, h)
        bwd = bwd + behind(bwd, h)
        h *= 2
    return fwd + behind(bwd, 1)


def _window_count(pos, w, extent):
    return (jnp.minimum(pos + w // 2, extent) - jnp.maximum(pos - w // 2, 0)).astype(F32)


def _pool_kernel(u_ref, pw_ref, ps_ref, o_ref):
    g = pl.program_id(1)
    n = u_ref.shape[2]
    tok = lax.broadcasted_iota(I32, (n, LANES), 0)
    col = tok & (GRID_W - 1)
    row = tok >> _log2(GRID_W)
    for gi, w in enumerate(POOL_WINDOWS):
        @pl.when(g == gi)
        def _(w=w):
            u = u_ref[0, 0]
            rows = n // GRID_W
            s = _window_sum(_window_sum(u, w, GRID_W, row, rows), w, 1, col, GRID_W)
            m = s / (_window_count(row, w, rows) * _window_count(col, w, GRID_W))
            o_ref[0, 0] = _dot((m - u).astype(BF16), pw_ref[0]) * ps_ref[0]


def _pool(u, pw, ps):
    b, g, t, c = u.shape
    blk = pl.BlockSpec((1, 1, t, c), lambda i, j: (i, j, 0, 0))
    return pl.pallas_call(
        _pool_kernel,
        out_shape=jax.ShapeDtypeStruct(u.shape, F32),
        grid=(b, g),
        in_specs=[blk,
                  pl.BlockSpec((1, c, c), lambda i, j: (j, 0, 0)),
                  pl.BlockSpec((1, 1, c), lambda i, j: (j, 0, 0))],
        out_specs=blk,
        compiler_params=pltpu.CompilerParams(
            dimension_semantics=("arbitrary", "arbitrary"),
            vmem_limit_bytes=_vmem_limit(12 * t * c * 4)),
        name="pool",
    )(u, pw, ps)


def _outproj_kernel(hg_ref, pool_ref, x_ref, w_ref, mod_ref, gpost_ref, gpre_ref, wr_ref, br_ref,
                    x1_ref, h2_ref, ti_ref, tw_ref, cnt_ref):
    d = x_ref.shape[-1]
    r = pl.program_id(0)

    @pl.when(jnp.logical_and(r == 0, pl.program_id(1) == 0))
    def _():
        cnt_ref[...] = jnp.zeros(cnt_ref.shape, F32)

    gt1 = mod_ref[pl.ds(r, 1), pl.ds(2 * d, d)]
    sh2 = mod_ref[pl.ds(r, 1), pl.ds(3 * d, d)]
    sc2 = mod_ref[pl.ds(r, 1), pl.ds(4 * d, d)]
    cat = jnp.concatenate([hg_ref[0, hh] for hh in range(HG_HEADS)]
                          + [pool_ref[0, g] for g in range(len(POOL_WINDOWS))], axis=1)
    y = _dot(cat.astype(BF16), w_ref[...])
    x1 = x_ref[0] + gt1 * _rms(y, gpost_ref[...])
    x1_ref[0] = x1
    h2 = _rms(x1, gpre_ref[...]) * (1.0 + sc2) + sh2
    h2_ref[0] = h2

    lane = lax.broadcasted_iota(I32, (h2.shape[0], LANES), 1).astype(F32)
    logits = jnp.where(lane < N_EXPERTS, _dot(h2.astype(BF16), wr_ref[...]) + br_ref[...], -jnp.inf)
    ti = jnp.zeros(logits.shape, F32)
    tw = jnp.zeros(logits.shape, F32)
    chosen = jnp.zeros(logits.shape, F32)
    top = None
    for jj in range(TOP_K):
        m = jnp.max(logits, axis=1, keepdims=True)
        idx = jnp.min(jnp.where(logits == m, lane, float(LANES)), axis=1, keepdims=True)
        logits = jnp.where(lane == idx, -jnp.inf, logits)
        chosen = jnp.where(lane == idx, 1.0, chosen)
        top = m if top is None else top
        ti = jnp.where(lane == jj, idx, ti)
        tw = jnp.where(lane == jj, jnp.exp(m - top), tw)
    ti_ref[0] = ti.astype(I32)
    tw_ref[0] = tw / jnp.sum(tw, axis=1, keepdims=True)
    cnt_ref[...] += jnp.sum(chosen, axis=0, keepdims=True)


def _outproj(hg, pool, x, w_out, mod, gpost, gpre, wr, br):
    b, t, d = x.shape
    tm = min(512, t)
    head = pl.BlockSpec((1, HG_HEADS, tm, HEAD_DIM), lambda i, j: (i, 0, j, 0))
    tok = pl.BlockSpec((1, tm, d), lambda i, j: (i, j, 0))
    lanes = pl.BlockSpec((1, tm, LANES), lambda i, j: (i, j, 0))
    vec = pl.BlockSpec((1, d), lambda i, j: (0, 0))
    est = 2 * (2 * tm * HG_WIDTH * 4 + 3 * tm * d * 4 + 2 * tm * LANES * 4) + 2 * d * d * 2 + 6 * tm * d * 4
    return pl.pallas_call(
        _outproj_kernel,
        out_shape=[jax.ShapeDtypeStruct((b, t, d), F32), jax.ShapeDtypeStruct((b, t, d), F32),
                   jax.ShapeDtypeStruct((b, t, LANES), I32), jax.ShapeDtypeStruct((b, t, LANES), F32),
                   jax.ShapeDtypeStruct((1, LANES), F32)],
        grid=(b, t // tm),
        in_specs=[head, head, tok,
                  pl.BlockSpec(w_out.shape, lambda i, j: (0, 0)),
                  pl.BlockSpec(mod.shape, lambda i, j: (0, 0)),
                  vec, vec,
                  pl.BlockSpec(wr.shape, lambda i, j: (0, 0)),
                  pl.BlockSpec((1, LANES), lambda i, j: (0, 0))],
        out_specs=[tok, tok, lanes, lanes, pl.BlockSpec((1, LANES), lambda i, j: (0, 0))],
        compiler_params=pltpu.CompilerParams(
            dimension_semantics=("arbitrary", "arbitrary"),
            vmem_limit_bytes=_vmem_limit(est)),
        name="outproj",
    )(hg, pool, x, w_out, mod, gpost, gpre, wr, br)


def _rank_kernel(ti_ref, start_ref, dest_ref, carry_ref):
    i = pl.program_id(0)

    @pl.when(i == 0)
    def _():
        carry_ref[...] = start_ref[...]

    ti = ti_ref[...]
    tm = ti.shape[0]
    lane = lax.broadcasted_iota(I32, (tm, LANES), 1)
    onehot = [lane == ti[:, jj:jj + 1] for jj in range(TOP_K)]
    cnt = sum(jnp.where(oh, 1.0, 0.0) for oh in onehot)
    row = lax.broadcasted_iota(I32, (tm, tm), 0)
    col = lax.broadcasted_iota(I32, (tm, tm), 1)
    before = jnp.where(col < row, 1.0, 0.0).astype(BF16)
    prefix = _dot(before, cnt.astype(BF16)) + carry_ref[...]
    rank = jnp.zeros((tm, LANES), F32)
    for jj in range(TOP_K):
        rj = jnp.sum(jnp.where(onehot[jj], prefix, 0.0), axis=1, keepdims=True)
        rank = jnp.where(lane == jj, rj, rank)
    dest_ref[...] = rank.astype(I32)
    carry_ref[...] += jnp.sum(cnt, axis=0, keepdims=True)


def _rank(ti, starts):
    t = ti.shape[0]
    tm = min(512, t)
    return pl.pallas_call(
        _rank_kernel,
        out_shape=jax.ShapeDtypeStruct((t, LANES), I32),
        grid=(t // tm,),
        in_specs=[pl.BlockSpec((tm, LANES), lambda i: (i, 0)),
                  pl.BlockSpec((1, LANES), lambda i: (0, 0))],
        out_specs=pl.BlockSpec((tm, LANES), lambda i: (i, 0)),
        scratch_shapes=[pltpu.VMEM((1, LANES), F32)],
        compiler_params=pltpu.CompilerParams(dimension_semantics=("arbitrary",)),
        name="rank",
    )(ti, starts)


def _dispatch_kernel(dest_ref, h_ref, xs_ref, sem):
    tm = h_ref.shape[0]

    def row_copy(r, d):
        return pltpu.make_async_copy(h_ref.at[pl.ds(r, 1), :], xs_ref.at[pl.ds(d, 1), :], sem)

    def start(r, carry):
        for jj in range(TOP_K):
            row_copy(r, dest_ref[r * TOP_K + jj]).start()
        return carry

    def wait(r, carry):
        for jj in range(TOP_K):
            row_copy(r, dest_ref[r * TOP_K + jj]).wait()
        return carry

    lax.fori_loop(0, tm, start, 0)
    lax.fori_loop(0, tm, wait, 0)


def _dispatch(dest, h2):
    t, d = h2.shape
    tm = min(256, t)
    return pl.pallas_call(
        _dispatch_kernel,
        out_shape=jax.ShapeDtypeStruct((t * TOP_K, d), h2.dtype),
        grid=(t // tm,),
        in_specs=[pl.BlockSpec((tm * TOP_K,), lambda i: (i,), memory_space=pltpu.SMEM),
                  pl.BlockSpec((tm, d), lambda i: (i, 0))],
        out_specs=pl.BlockSpec(memory_space=pl.ANY),
        scratch_shapes=[pltpu.SemaphoreType.DMA],
        compiler_params=pltpu.CompilerParams(dimension_semantics=("arbitrary",)),
        name="dispatch",
    )(dest, h2)


def _moe_kernel(ie_ref, ib_ref, lo_ref, hi_ref, first_ref,
                x_ref, wg_ref, wu_ref, wd_ref, bg_ref, bu_ref, bd_ref, y_ref,
                wgb, wub, wdb):
    i = pl.program_id(0)
    prev = jnp.maximum(i - 1, 0)
    changed = jnp.logical_or(i == 0, ie_ref[i] != ie_ref[prev])
    lo, hi = lo_ref[i], hi_ref[i]

    @pl.when(jnp.logical_and(changed, hi > lo))
    def _():
        wgb[...] = wg_ref[...].astype(BF16)
        wub[...] = wu_ref[...].astype(BF16)
        wdb[...] = wd_ref[...].astype(BF16)

    @pl.when(hi > lo)
    def _():
        x = x_ref[...].astype(BF16)
        gate = jnp.minimum(_dot(x, wgb[...]) + bg_ref[...], SWIGLU_LIMIT)
        up = jnp.clip(_dot(x, wub[...]) + bu_ref[...], -SWIGLU_LIMIT, SWIGLU_LIMIT)
        hmid = (up + 1.0) * gate * _sigmoid(SWIGLU_ALPHA * gate)
        y = _dot(hmid.astype(BF16), wdb[...]) + bd_ref[...]
        row = lax.broadcasted_iota(I32, y.shape, 0)
        mine = jnp.logical_and(row >= lo, row < hi)

        @pl.when(first_ref[i] == 1)
        def _():
            y_ref[...] = jnp.where(mine, y, 0.0)

        @pl.when(first_ref[i] == 0)
        def _():
            y_ref[...] = jnp.where(mine, y, y_ref[...])


def _moe(items, xs, wg, wu, wd, bg, bu, bd):
    a, d = xs.shape
    f = wg.shape[-1]
    n_items = items[0].shape[0]
    rows = pl.BlockSpec((MOE_BM, d), lambda i, ie, ib, lo, hi, fi: (ib[i], 0))

    def wspec(k, n):
        return pl.BlockSpec((None, k, n), lambda i, ie, ib, lo, hi, fi: (ie[i], 0, 0))

    def bspec(n):
        return pl.BlockSpec((None, 1, n), lambda i, ie, ib, lo, hi, fi: (ie[i], 0, 0))

    est = 2 * 3 * d * f * 4 + 3 * d * f * 2 + 4 * MOE_BM * d * 4 + 4 * MOE_BM * f * 4
    return pl.pallas_call(
        _moe_kernel,
        out_shape=jax.ShapeDtypeStruct((a, d), F32),
        grid_spec=pltpu.PrefetchScalarGridSpec(
            num_scalar_prefetch=5,
            grid=(n_items,),
            in_specs=[rows, wspec(d, f), wspec(d, f), wspec(f, d), bspec(f), bspec(f), bspec(d)],
            out_specs=rows,
            scratch_shapes=[pltpu.VMEM((d, f), BF16), pltpu.VMEM((d, f), BF16),
                            pltpu.VMEM((f, d), BF16)]),
        compiler_params=pltpu.CompilerParams(
            dimension_semantics=("arbitrary",),
            vmem_limit_bytes=_vmem_limit(est)),
        name="moe",
    )(*items, xs, wg, wu, wd, bg, bu, bd)


def _moe_items(counts, n_rows):
    n_blk = n_rows // MOE_BM
    n_items = n_blk + N_EXPERTS - 1
    ends = jnp.cumsum(counts)
    starts = ends - counts
    first_blk = starts // MOE_BM
    last_blk = (ends - 1) // MOE_BM
    per_e = jnp.where(counts > 0, last_blk - first_blk + 1, 0)
    item_end = jnp.cumsum(per_e)
    item_start = item_end - per_e
    total = item_end[-1]
    i = jnp.arange(n_items, dtype=I32)
    last_e = jnp.max(jnp.where(counts > 0, jnp.arange(N_EXPERTS, dtype=I32), 0))
    e = jnp.minimum(jnp.sum(item_end[None, :] <= i[:, None], axis=1), last_e).astype(I32)
    onehot = e[:, None] == jnp.arange(N_EXPERTS, dtype=I32)[None, :]

    def pick(table):
        return jnp.sum(jnp.where(onehot, table[None, :], 0), axis=1)

    blk = pick(first_blk) + (i - pick(item_start))
    lo = jnp.maximum(pick(starts), blk * MOE_BM) - blk * MOE_BM
    hi = jnp.minimum(pick(ends), (blk + 1) * MOE_BM) - blk * MOE_BM
    live = i < total
    blk = jnp.where(live, blk, n_blk - 1)
    lo = jnp.where(live, lo, 0)
    hi = jnp.where(live, hi, 0)
    prev_blk = jnp.concatenate([jnp.full((1,), -1, I32), blk[:-1]])
    first = jnp.where(jnp.logical_and(live, blk != prev_blk), 1, 0)
    return tuple(z.astype(I32) for z in (e, blk, lo, hi, first))


def _combine_kernel(dest_ref, ys_ref, tw_ref, x1_ref, mod_ref, gpost_ref, o_ref, buf, sem,
                    *, tiles_per_batch):
    tm, d = x1_ref.shape

    def row_copy(r, jj):
        return pltpu.make_async_copy(ys_ref.at[pl.ds(dest_ref[r * TOP_K + jj], 1), :],
                                     buf.at[jj, pl.ds(r, 1), :], sem)

    def start(r, carry):
        for jj in range(TOP_K):
            row_copy(r, jj).start()
        return carry

    def wait(r, carry):
        for jj in range(TOP_K):
            row_copy(r, jj).wait()
        return carry

    lax.fori_loop(0, tm, start, 0)
    lax.fori_loop(0, tm, wait, 0)

    tw = tw_ref[...]
    y = tw[:, 0:1] * buf[0]
    for jj in range(1, TOP_K):
        y = y + tw[:, jj:jj + 1] * buf[jj]
    r = pl.program_id(0) // tiles_per_batch
    gt2 = mod_ref[pl.ds(r, 1), pl.ds(5 * d, d)]
    o_ref[...] = x1_ref[...] + gt2 * _rms(y, gpost_ref[...])


def _combine(dest, ys, tw, x1, mod, gpost, tokens_per_batch):
    t, d = x1.shape
    tm = min(256, tokens_per_batch)
    kern = functools.partial(_combine_kernel, tiles_per_batch=tokens_per_batch // tm)
    est = TOP_K * tm * d * 4 + 2 * (2 * tm * d * 4 + tm * LANES * 4) + 3 * tm * d * 4
    return pl.pallas_call(
        kern,
        out_shape=jax.ShapeDtypeStruct((t, d), F32),
        grid=(t // tm,),
        in_specs=[pl.BlockSpec((tm * TOP_K,), lambda i: (i,), memory_space=pltpu.SMEM),
                  pl.BlockSpec(memory_space=pl.ANY),
                  pl.BlockSpec((tm, LANES), lambda i: (i, 0)),
                  pl.BlockSpec((tm, d), lambda i: (i, 0)),
                  pl.BlockSpec(mod.shape, lambda i: (0, 0)),
                  pl.BlockSpec((1, d), lambda i: (0, 0))],
        out_specs=pl.BlockSpec((tm, d), lambda i: (i, 0)),
        scratch_shapes=[pltpu.VMEM((TOP_K, tm, d), F32), pltpu.SemaphoreType.DMA],
        compiler_params=pltpu.CompilerParams(
            dimension_semantics=("arbitrary",),
            vmem_limit_bytes=_vmem_limit(est)),
        name="combine",
    )(dest, ys, tw, x1, mod, gpost)


def kernel(x, c, ctx, c_ctx, w_ada, b_ada, g_pre_mix, g_post_mix, g_pre_ffn, g_post_ffn,
           w_in, w_out, hgrn_lb_logits, hgrn_norm, pool_w, pool_scale,
           w_router, b_router, w_gate, b_gate, w_up, b_up, w_down, b_down):
    b, s, d = x.shape
    layer = 0
    n_e = w_router.shape[-1]

    cc = jnp.concatenate([c, c_ctx[None, :]], axis=0)
    cc = jnp.pad(cc, ((0, -(b + 1) % SUBLANES), (0, 0)))
    mod = _ada(cc, w_ada[layer], b_ada[layer][None, :])

    w_in_b = w_in[layer].astype(BF16)
    lbl = hgrn_lb_logits[:2].reshape(4, HG_WIDTH)
    gn = hgrn_norm[layer][None, :]
    gpre = g_pre_mix[layer][None, :]

    lff_c, kf_c, lfb_c, kb_c, v_c = _inproj(ctx, mod, gpre, w_in_b, lbl, gn, mod_row=b, full=False)
    zeros = jnp.zeros((b, HG_HEADS, HEAD_DIM, HEAD_DIM), F32)
    (s_f,) = _hgrn(None, kf_c, v_c, lff_c, zeros, rev=False, need_o=False, want_state=True,
                   name="hgrn_ctx_fwd")
    (s_b,) = _hgrn(None, kb_c, v_c, lfb_c, zeros, rev=True, need_o=False, want_state=True,
                   name="hgrn_ctx_bwd")

    q, lff, kf, lfb, kb, v, gate, u = _inproj(x, mod, gpre, w_in_b, lbl, gn, mod_row=None, full=True)
    (o_f,) = _hgrn(q, kf, v, lff, s_f, rev=False, need_o=True, want_state=False, name="hgrn_fwd")
    (o_hg,) = _hgrn(q, kb, v, lfb, s_b, o_f, gate, rev=True, need_o=True, want_state=False,
                    name="hgrn_bwd")
    o_pool = _pool(u, pool_w[layer].astype(BF16), pool_scale[layer].reshape(len(POOL_WINDOWS), 1, POOL_GROUP))

    wr = jnp.pad(w_router[layer], ((0, 0), (0, LANES - n_e))).astype(BF16)
    br = jnp.pad(b_router[layer], (0, LANES - n_e))[None, :]
    x1, h2, ti, tw, cnt = _outproj(o_hg, o_pool, x, w_out[layer].astype(BF16), mod,
                                   g_post_mix[layer][None, :], g_pre_ffn[layer][None, :], wr, br)

    t = b * s
    starts = jnp.cumsum(cnt, axis=1) - cnt
    dest = _rank(ti.reshape(t, LANES), starts)[:, :TOP_K].reshape(t * TOP_K)
    items = _moe_items(cnt[0, :n_e].astype(I32), t * TOP_K)

    xs = _dispatch(dest, h2.reshape(t, d))
    ys = _moe(items, xs, w_gate[layer], w_up[layer], w_down[layer],
              b_gate[layer][:, None, :], b_up[layer][:, None, :], b_down[layer][:, None, :])
    out = _combine(dest, ys, tw.reshape(t, LANES), x1.reshape(t, d), mod,
                   g_post_ffn[layer][None, :], s)
    return out.reshape(b, s, d)
```

```python
import functools
import math

import numpy as np
import jax
import jax.numpy as jnp
from jax import lax
from jax.experimental import pallas as pl
from jax.experimental.pallas import tpu as pltpu

F32 = jnp.float32
BF16 = jnp.bfloat16
I32 = jnp.int32

GRID_W = 64
HG_HEADS = 4
HEAD_DIM = 128
HG_WIDTH = HG_HEADS * HEAD_DIM
POOL_WINDOWS = (2, 4, 8, 16)
POOL_GROUP = 128
N_EXPERTS = 32
TOP_K = 4
SWIGLU_LIMIT = 7.0
SWIGLU_ALPHA = 1.702
EPS = 1e-6

LANES = 128
SUBLANES = 8
V7X_VMEM_BYTES = 64 * 1024 * 1024

CHUNK = 64
LEVELS = (32, 16, 8, 4, 2, 1)
MOE_BM = 256
LOG2_E = math.log2(math.e)


def _vmem_limit(nbytes):
    return int(min(nbytes * 3 // 2 + (4 << 20), V7X_VMEM_BYTES - (6 << 20)))


def _log2(n):
    assert n & (n - 1) == 0
    return n.bit_length() - 1


def _sigmoid(x):
    return 1.0 / (1.0 + jnp.exp(-x))


def _rms(x, gain):
    return x * lax.rsqrt(jnp.mean(x * x, axis=-1, keepdims=True) + EPS) * gain


def _dot(a, b):
    return jnp.dot(a, b, preferred_element_type=F32)


def _dot_nt(a, b):
    return lax.dot_general(a, b, (((1,), (1,)), ((), ())), preferred_element_type=F32)


def _ada_kernel(c_ref, w_ref, b_ref, o_ref):
    c = c_ref[...]
    s = (c * _sigmoid(c)).astype(BF16)
    o_ref[...] = _dot(s, w_ref[...].astype(BF16)) + b_ref[...]


def _ada(cc, w, b):
    rows, d = cc.shape
    n = w.shape[1]
    tn = 1536 if n % 1536 == 0 else n
    return pl.pallas_call(
        _ada_kernel,
        out_shape=jax.ShapeDtypeStruct((rows, n), F32),
        grid=(n // tn,),
        in_specs=[pl.BlockSpec((rows, d), lambda j: (0, 0)),
                  pl.BlockSpec((d, tn), lambda j: (0, j)),
                  pl.BlockSpec((1, tn), lambda j: (0, j))],
        out_specs=pl.BlockSpec((rows, tn), lambda j: (0, j)),
        compiler_params=pltpu.CompilerParams(
            dimension_semantics=("arbitrary",),
            vmem_limit_bytes=_vmem_limit(2 * d * tn * 4 + d * tn * 2)),
        name="ada",
    )(cc, w, b)


def _inproj_kernel(x_ref, mod_ref, gpre_ref, w_ref, lbl_ref, gn_ref, *outs, mod_row, full):
    d = x_ref.shape[-1]
    r = pl.program_id(0) if mod_row is None else mod_row
    sh = mod_ref[pl.ds(r, 1), pl.ds(0, d)]
    sc = mod_ref[pl.ds(r, 1), pl.ds(d, d)]
    h = (_rms(x_ref[0], gpre_ref[...]) * (1.0 + sc) + sh).astype(BF16)

    def proj(g):
        return _dot(h, w_ref[:, g * HG_WIDTH:(g + 1) * HG_WIDTH])

    def put(ref, val):
        for hh in range(HG_HEADS):
            ref[0, hh] = val[:, hh * HEAD_DIM:(hh + 1) * HEAD_DIM]

    def lower_bound(direction):
        l0 = lbl_ref[pl.ds(direction, 1), :]
        l1 = lbl_ref[pl.ds(2 + direction, 1), :]
        m = jnp.maximum(l0, l1)
        e0 = jnp.exp(l0 - m)
        return e0 / (e0 + jnp.exp(l1 - m))

    if full:
        q_o, lff_o, kf_o, lfb_o, kb_o, v_o, gate_o, u_o = outs
        q = proj(0)
        put(q_o, q * _sigmoid(q))
    else:
        lff_o, kf_o, lfb_o, kb_o, v_o = outs
    for direction, (lf_o, k_o) in enumerate(((lff_o, kf_o), (lfb_o, kb_o))):
        lb = lower_bound(direction)
        sg = _sigmoid(proj(1 + direction))
        put(lf_o, jnp.log(lb + (1.0 - lb) * sg))
        put(k_o, (1.0 - lb) * (1.0 - sg))
    put(v_o, proj(3))
    if full:
        put(gate_o, gn_ref[...] * _sigmoid(proj(4)))
        put(u_o, proj(5))


def _inproj(x, mod, gpre, w_in, lbl, gn, *, mod_row, full):
    b, t, d = x.shape
    tm = min(512, t)
    n_out = 8 if full else 5
    n_cols = w_in.shape[1]
    head = pl.BlockSpec((1, HG_HEADS, tm, HEAD_DIM), lambda i, j: (i, 0, j, 0))
    kern = functools.partial(_inproj_kernel, mod_row=mod_row, full=full)
    est = 2 * tm * d * 4 + 2 * d * n_cols * 2 + n_out * 2 * tm * HG_WIDTH * 4 + 4 * tm * HG_WIDTH * 4
    return pl.pallas_call(
        kern,
        out_shape=[jax.ShapeDtypeStruct((b, HG_HEADS, t, HEAD_DIM), F32)] * n_out,
        grid=(b, t // tm),
        in_specs=[pl.BlockSpec((1, tm, d), lambda i, j: (i, j, 0)),
                  pl.BlockSpec(mod.shape, lambda i, j: (0, 0)),
                  pl.BlockSpec((1, d), lambda i, j: (0, 0)),
                  pl.BlockSpec(w_in.shape, lambda i, j: (0, 0)),
                  pl.BlockSpec(lbl.shape, lambda i, j: (0, 0)),
                  pl.BlockSpec((1, HG_WIDTH), lambda i, j: (0, 0))],
        out_specs=[head] * n_out,
        compiler_params=pltpu.CompilerParams(
            dimension_semantics=("arbitrary", "arbitrary"),
            vmem_limit_bytes=_vmem_limit(est)),
        name="inproj_full" if full else "inproj_ctx",
    )(x, mod, gpre, w_in, lbl, gn)


def _hgrn_tables(rev):
    row = np.arange(CHUNK)[:, None]
    col = np.arange(CHUNK)[None, :]
    tri = ((col >= row) if rev else (col <= row)).astype(np.float32)
    blocks, masks = [tri], []
    for h in LEVELS:
        start = (np.arange(CHUNK) // (2 * h)) * (2 * h)
        blocks.append(tri - tri[start + (h if rev else h - 1)])
        is_q = ((row // h) % 2) == (0 if rev else 1)
        key_half = ((col // h) % 2) == (1 if rev else 0)
        masks.append(((row // (2 * h)) == (col // (2 * h))) & is_q & key_half)
    pair = np.stack([np.concatenate(masks[i:i + 2], axis=1) for i in range(0, len(LEVELS), 2)])
    cums = np.concatenate(blocks, axis=0)
    return (jnp.asarray(np.concatenate([cums, cums], axis=1), BF16), jnp.asarray(pair, F32),
            jnp.asarray(np.eye(CHUNK), F32))


def _hgrn_chunk(qs, ks, vs, lfs, sts, tabs, rev, need_o):
    cums_ref, pair_ref, eye_ref = tabs
    heads = range(len(ks))
    n_pairs = len(LEVELS) // 2
    n_cum = CHUNK * (1 + len(LEVELS)) if need_o else CHUNK
    cums = cums_ref[0:n_cum, :]

    his, los = [], []
    for h in heads:
        lf2 = lfs[h] * LOG2_E
        hi = lf2.astype(BF16)
        his.append(hi)
        los.append((lf2 - hi.astype(F32)).astype(BF16))
    both = _dot(cums, jnp.concatenate([jnp.concatenate(his, axis=1), jnp.concatenate(los, axis=1)], axis=0))
    bd = [both[:, h * LANES:(h + 1) * LANES] for h in heads]

    k16 = [ks[h].astype(BF16) for h in heads]
    v16 = [vs[h].astype(BF16) for h in heads]
    b = [bd[h][0:CHUNK] for h in heads]
    edge = [b[h][0:1, :] if rev else b[h][CHUNK - 1:CHUNK, :] for h in heads]
    kd = [k16[h] * jnp.exp2(edge[h] - b[h]).astype(BF16) for h in heads]
    grow = [lax.dot_general(v16[h], kd[h], (((0,), (0,)), ((), ())), preferred_element_type=F32)
            for h in heads]
    st_new = [sts[h] * jnp.exp2(edge[h]) + grow[h] for h in heads]
    if not need_o:
        return None, st_new

    q16 = [qs[h].astype(BF16) for h in heads]
    carried = [_dot_nt(q16[h] * jnp.exp2(b[h]).astype(BF16), sts[h].astype(BF16)) for h in heads]

    zero = jnp.zeros((CHUNK, LANES), BF16)
    att = [[] for _ in heads]
    for p in range(n_pairs):
        lhs, rhs = [], []
        for h in heads:
            qe, ke = [], []
            for i in (2 * p, 2 * p + 1):
                e = jnp.exp2(-jnp.abs(bd[h][CHUNK * (1 + i):CHUNK * (2 + i)])).astype(BF16)
                qe.append(q16[h] * e)
                ke.append(k16[h] * e)
            lhs.append(jnp.concatenate(qe, axis=1))
            rhs.append(jnp.concatenate([jnp.concatenate([ke[0], zero], axis=1),
                                        jnp.concatenate([zero, ke[1]], axis=1)], axis=0))
        scores = [_dot_nt(lhs[h], rhs[h]) for h in heads]
        for h in heads:
            att[h].append((scores[h] * pair_ref[p]).astype(BF16))
    same_row = [_dot_nt(q16[h], k16[h]) for h in heads]
    o = []
    for h in heads:
        att[h].append((same_row[h] * eye_ref[...]).astype(BF16))
        vals = jnp.concatenate([v16[h]] * (2 * n_pairs + 1), axis=0)
        o.append(carried[h] + _dot(jnp.concatenate(att[h], axis=1), vals))
    return o, st_new


def _hgrn_kernel(*refs, rev, need_o, final, want_state, n_chunks):
    it = iter(refs)
    tabs = tuple(next(it) for _ in range(3))
    q_ref = next(it) if need_o else None
    k_ref, v_ref, lf_ref, s0_ref = next(it), next(it), next(it), next(it)
    prev_ref = next(it) if final else None
    gate_ref = next(it) if final else None
    o_ref = next(it) if need_o else None
    sout_ref = next(it) if want_state else None
    st_ref = next(it)

    j = pl.program_id(1)
    nb = pl.num_programs(1)

    @pl.when(j == 0)
    def _():
        st_ref[...] = s0_ref[0]

    def body(ci, carry):
        c = (n_chunks - 1 - ci) if rev else ci
        base = pl.multiple_of(c * CHUNK, CHUNK)
        rows = pl.ds(base, CHUNK)
        heads = range(HG_HEADS)
        o, st_new = _hgrn_chunk([q_ref[0, hh, rows, :] for hh in heads] if need_o else None,
                                [k_ref[0, hh, rows, :] for hh in heads],
                                [v_ref[0, hh, rows, :] for hh in heads],
                                [lf_ref[0, hh, rows, :] for hh in heads],
                                [st_ref[hh] for hh in heads], tabs, rev, need_o)
        for hh in heads:
            st_ref[hh] = st_new[hh]
            if need_o:
                out = o[hh]
                if final:
                    tot = out + prev_ref[0, hh, rows, :]
                    out = tot * lax.rsqrt(jnp.mean(tot * tot, axis=-1, keepdims=True) + EPS)
                    out = out * gate_ref[0, hh, rows, :]
                o_ref[0, hh, rows, :] = out
        return carry

    lax.fori_loop(0, n_chunks, body, 0)

    if want_state:
        @pl.when(j == nb - 1)
        def _():
            sout_ref[0] = st_ref[...]


def _hgrn(q, k, v, lf, s0, prev=None, gate=None, *, rev, need_o, want_state, name):
    b, hh, t, _ = k.shape
    tb = min(512, t)
    nb = t // tb
    final = prev is not None
    tabs = _hgrn_tables(rev)

    seq = pl.BlockSpec((1, hh, tb, HEAD_DIM), lambda i, j: (i, 0, (nb - 1 - j) if rev else j, 0))
    state = pl.BlockSpec((1, hh, HEAD_DIM, HEAD_DIM), lambda i, j: (i, 0, 0, 0))
    args = list(tabs)
    in_specs = [pl.BlockSpec(a.shape, functools.partial(lambda nd, i, j: (0,) * nd, a.ndim)) for a in tabs]
    for a in ((q,) if need_o else ()) + (k, v, lf):
        args.append(a)
        in_specs.append(seq)
    args.append(s0)
    in_specs.append(state)
    if final:
        args += [prev, gate]
        in_specs += [seq, seq]
    out_shape, out_specs = [], []
    if need_o:
        out_shape.append(jax.ShapeDtypeStruct((b, hh, t, HEAD_DIM), F32))
        out_specs.append(seq)
    if want_state:
        out_shape.append(jax.ShapeDtypeStruct((b, hh, HEAD_DIM, HEAD_DIM), F32))
        out_specs.append(state)
    kern = functools.partial(_hgrn_kernel, rev=rev, need_o=need_o, final=final,
                             want_state=want_state, n_chunks=tb // CHUNK)
    est = 2 * (len(args) + len(out_shape)) * hh * tb * HEAD_DIM * 4
    return pl.pallas_call(
        kern,
        out_shape=out_shape,
        grid=(b, nb),
        in_specs=in_specs,
        out_specs=out_specs,
        scratch_shapes=[pltpu.VMEM((hh, HEAD_DIM, HEAD_DIM), F32)],
        compiler_params=pltpu.CompilerParams(
            dimension_semantics=("arbitrary", "arbitrary"),
            vmem_limit_bytes=_vmem_limit(est)),
        name=name,
    )(*args)


def _window_sum(x, w, stride, pos, extent):
    n = x.shape[0]

    def ahead(y, dist):
        return jnp.where(pos + dist < extent, pltpu.roll(y, (n - dist * stride) % n, 0), 0.0)

    def behind(y, dist):
        return jnp.where(pos >= dist, pltpu.roll(y, dist * stride, 0), 0.0)

    fwd, bwd, h = x, x, 1
    while h < w // 2:
        fwd = fwd + ahead(fwd, h)
        bwd = bwd + behind(bwd, h)
        h *= 2
    return fwd + behind(bwd, 1)


def _window_count(pos, w, extent):
    return (jnp.minimum(pos + w // 2, extent) - jnp.maximum(pos - w // 2, 0)).astype(F32)


def _pool_kernel(u_ref, pw_ref, ps_ref, o_ref):
    g = pl.program_id(1)
    n = u_ref.shape[2]
    tok = lax.broadcasted_iota(I32, (n, LANES), 0)
    col = tok & (GRID_W - 1)
    row = tok >> _log2(GRID_W)
    for gi, w in enumerate(POOL_WINDOWS):
        @pl.when(g == gi)
        def _(w=w):
            u = u_ref[0, 0]
            rows = n // GRID_W
            s = _window_sum(_window_sum(u, w, GRID_W, row, rows), w, 1, col, GRID_W)
            m = s / (_window_count(row, w, rows) * _window_count(col, w, GRID_W))
            o_ref[0, 0] = _dot((m - u).astype(BF16), pw_ref[0]) * ps_ref[0]


def _pool(u, pw, ps):
    b, g, t, c = u.shape
    blk = pl.BlockSpec((1, 1, t, c), lambda i, j: (i, j, 0, 0))
    return pl.pallas_call(
        _pool_kernel,
        out_shape=jax.ShapeDtypeStruct(u.shape, F32),
        grid=(b, g),
        in_specs=[blk,
                  pl.BlockSpec((1, c, c), lambda i, j: (j, 0, 0)),
                  pl.BlockSpec((1, 1, c), lambda i, j: (j, 0, 0))],
        out_specs=blk,
        compiler_params=pltpu.CompilerParams(
            dimension_semantics=("arbitrary", "arbitrary"),
            vmem_limit_bytes=_vmem_limit(12 * t * c * 4)),
        name="pool",
    )(u, pw, ps)


def _outproj_kernel(hg_ref, pool_ref, x_ref, w_ref, mod_ref, gpost_ref, gpre_ref, wr_ref, br_ref,
                    x1_ref, h2_ref, ti_ref, tw_ref, cnt_ref):
    d = x_ref.shape[-1]
    r = pl.program_id(0)

    @pl.when(jnp.logical_and(r == 0, pl.program_id(1) == 0))
    def _():
        cnt_ref[...] = jnp.zeros(cnt_ref.shape, F32)

    gt1 = mod_ref[pl.ds(r, 1), pl.ds(2 * d, d)]
    sh2 = mod_ref[pl.ds(r, 1), pl.ds(3 * d, d)]
    sc2 = mod_ref[pl.ds(r, 1), pl.ds(4 * d, d)]
    cat = jnp.concatenate([hg_ref[0, hh] for hh in range(HG_HEADS)]
                          + [pool_ref[0, g] for g in range(len(POOL_WINDOWS))], axis=1)
    y = _dot(cat.astype(BF16), w_ref[...])
    x1 = x_ref[0] + gt1 * _rms(y, gpost_ref[...])
    x1_ref[0] = x1
    h2 = _rms(x1, gpre_ref[...]) * (1.0 + sc2) + sh2
    h2_ref[0] = h2

    lane = lax.broadcasted_iota(I32, (h2.shape[0], LANES), 1).astype(F32)
    logits = jnp.where(lane < N_EXPERTS, _dot(h2.astype(BF16), wr_ref[...]) + br_ref[...], -jnp.inf)
    ti = jnp.zeros(logits.shape, F32)
    tw = jnp.zeros(logits.shape, F32)
    chosen = jnp.zeros(logits.shape, F32)
    top = None
    for jj in range(TOP_K):
        m = jnp.max(logits, axis=1, keepdims=True)
        idx = jnp.min(jnp.where(logits == m, lane, float(LANES)), axis=1, keepdims=True)
        logits = jnp.where(lane == idx, -jnp.inf, logits)
        chosen = jnp.where(lane == idx, 1.0, chosen)
        top = m if top is None else top
        ti = jnp.where(lane == jj, idx, ti)
        tw = jnp.where(lane == jj, jnp.exp(m - top), tw)
    ti_ref[0] = ti.astype(I32)
    tw_ref[0] = tw / jnp.sum(tw, axis=1, keepdims=True)
    cnt_ref[...] += jnp.sum(chosen, axis=0, keepdims=True)


def _outproj(hg, pool, x, w_out, mod, gpost, gpre, wr, br):
    b, t, d = x.shape
    tm = min(512, t)
    head = pl.BlockSpec((1, HG_HEADS, tm, HEAD_DIM), lambda i, j: (i, 0, j, 0))
    tok = pl.BlockSpec((1, tm, d), lambda i, j: (i, j, 0))
    lanes = pl.BlockSpec((1, tm, LANES), lambda i, j: (i, j, 0))
    vec = pl.BlockSpec((1, d), lambda i, j: (0, 0))
    est = 2 * (2 * tm * HG_WIDTH * 4 + 3 * tm * d * 4 + 2 * tm * LANES * 4) + 2 * d * d * 2 + 6 * tm * d * 4
    return pl.pallas_call(
        _outproj_kernel,
        out_shape=[jax.ShapeDtypeStruct((b, t, d), F32), jax.ShapeDtypeStruct((b, t, d), F32),
                   jax.ShapeDtypeStruct((b, t, LANES), I32), jax.ShapeDtypeStruct((b, t, LANES), F32),
                   jax.ShapeDtypeStruct((1, LANES), F32)],
        grid=(b, t // tm),
        in_specs=[head, head, tok,
                  pl.BlockSpec(w_out.shape, lambda i, j: (0, 0)),
                  pl.BlockSpec(mod.shape, lambda i, j: (0, 0)),
                  vec, vec,
                  pl.BlockSpec(wr.shape, lambda i, j: (0, 0)),
                  pl.BlockSpec((1, LANES), lambda i, j: (0, 0))],
        out_specs=[tok, tok, lanes, lanes, pl.BlockSpec((1, LANES), lambda i, j: (0, 0))],
        compiler_params=pltpu.CompilerParams(
            dimension_semantics=("arbitrary", "arbitrary"),
            vmem_limit_bytes=_vmem_limit(est)),
        name="outproj",
    )(hg, pool, x, w_out, mod, gpost, gpre, wr, br)


def _rank_kernel(ti_ref, start_ref, dest_ref, carry_ref):
    i = pl.program_id(0)

    @pl.when(i == 0)
    def _():
        carry_ref[...] = start_ref[...]

    ti = ti_ref[...]
    tm = ti.shape[0]
    lane = lax.broadcasted_iota(I32, (tm, LANES), 1)
    onehot = [lane == ti[:, jj:jj + 1] for jj in range(TOP_K)]
    cnt = sum(jnp.where(oh, 1.0, 0.0) for oh in onehot)
    row = lax.broadcasted_iota(I32, (tm, tm), 0)
    col = lax.broadcasted_iota(I32, (tm, tm), 1)
    before = jnp.where(col < row, 1.0, 0.0).astype(BF16)
    prefix = _dot(before, cnt.astype(BF16)) + carry_ref[...]
    rank = jnp.zeros((tm, LANES), F32)
    for jj in range(TOP_K):
        rj = jnp.sum(jnp.where(onehot[jj], prefix, 0.0), axis=1, keepdims=True)
        rank = jnp.where(lane == jj, rj, rank)
    dest_ref[...] = rank.astype(I32)
    carry_ref[...] += jnp.sum(cnt, axis=0, keepdims=True)


def _rank(ti, starts):
    t = ti.shape[0]
    tm = min(512, t)
    return pl.pallas_call(
        _rank_kernel,
        out_shape=jax.ShapeDtypeStruct((t, LANES), I32),
        grid=(t // tm,),
        in_specs=[pl.BlockSpec((tm, LANES), lambda i: (i, 0)),
                  pl.BlockSpec((1, LANES), lambda i: (0, 0))],
        out_specs=pl.BlockSpec((tm, LANES), lambda i: (i, 0)),
        scratch_shapes=[pltpu.VMEM((1, LANES), F32)],
        compiler_params=pltpu.CompilerParams(dimension_semantics=("arbitrary",)),
        name="rank",
    )(ti, starts)


def _dispatch_kernel(dest_ref, h_ref, xs_ref, sem):
    tm = h_ref.shape[0]

    def start(r, carry):
        for jj in range(TOP_K):
            pltpu.make_async_copy(h_ref.at[pl.ds(r, 1), :],
                                  xs_ref.at[pl.ds(dest_ref[r * TOP_K + jj], 1), :],
                                  sem).start(priority=jj % 2)
        return carry

    lax.fori_loop(0, tm, start, 0, unroll=4)
    for jj in range(TOP_K):
        pltpu.make_async_copy(h_ref, xs_ref.at[pl.ds(0, tm), :], sem).wait()


def _dispatch(dest, h2):
    t, d = h2.shape
    tm = min(256, t)
    return pl.pallas_call(
        _dispatch_kernel,
        out_shape=jax.ShapeDtypeStruct((t * TOP_K, d), h2.dtype),
        grid=(t // tm,),
        in_specs=[pl.BlockSpec((tm * TOP_K,), lambda i: (i,), memory_space=pltpu.SMEM),
                  pl.BlockSpec((tm, d), lambda i: (i, 0))],
        out_specs=pl.BlockSpec(memory_space=pl.ANY),
        scratch_shapes=[pltpu.SemaphoreType.DMA],
        compiler_params=pltpu.CompilerParams(dimension_semantics=("arbitrary",)),
        name="dispatch",
    )(dest, h2)


def _moe_kernel(ie_ref, ib_ref, lo_ref, hi_ref, first_ref,
                x_ref, wg_ref, wu_ref, wd_ref, bg_ref, bu_ref, bd_ref, y_ref,
                wgb, wub, wdb):
    i = pl.program_id(0)
    prev = jnp.maximum(i - 1, 0)
    changed = jnp.logical_or(i == 0, ie_ref[i] != ie_ref[prev])
    lo, hi = lo_ref[i], hi_ref[i]

    @pl.when(jnp.logical_and(changed, hi > lo))
    def _():
        wgb[...] = wg_ref[...].astype(BF16)
        wub[...] = wu_ref[...].astype(BF16)
        wdb[...] = wd_ref[...].astype(BF16)

    @pl.when(hi > lo)
    def _():
        x = x_ref[...].astype(BF16)
        gate = jnp.minimum(_dot(x, wgb[...]) + bg_ref[...], SWIGLU_LIMIT)
        up = jnp.clip(_dot(x, wub[...]) + bu_ref[...], -SWIGLU_LIMIT, SWIGLU_LIMIT)
        hmid = (up + 1.0) * gate * _sigmoid(SWIGLU_ALPHA * gate)
        y = _dot(hmid.astype(BF16), wdb[...]) + bd_ref[...]
        row = lax.broadcasted_iota(I32, y.shape, 0)
        mine = jnp.logical_and(row >= lo, row < hi)

        @pl.when(first_ref[i] == 1)
        def _():
            y_ref[...] = jnp.where(mine, y, 0.0)

        @pl.when(first_ref[i] == 0)
        def _():
            y_ref[...] = jnp.where(mine, y, y_ref[...])


def _moe(items, xs, wg, wu, wd, bg, bu, bd):
    a, d = xs.shape
    f = wg.shape[-1]
    n_items = items[0].shape[0]
    rows = pl.BlockSpec((MOE_BM, d), lambda i, ie, ib, lo, hi, fi: (ib[i], 0))

    def wspec(k, n):
        return pl.BlockSpec((None, k, n), lambda i, ie, ib, lo, hi, fi: (ie[i], 0, 0))

    def bspec(n):
        return pl.BlockSpec((None, 1, n), lambda i, ie, ib, lo, hi, fi: (ie[i], 0, 0))

    est = 2 * 3 * d * f * 4 + 3 * d * f * 2 + 4 * MOE_BM * d * 4 + 4 * MOE_BM * f * 4
    return pl.pallas_call(
        _moe_kernel,
        out_shape=jax.ShapeDtypeStruct((a, d), F32),
        grid_spec=pltpu.PrefetchScalarGridSpec(
            num_scalar_prefetch=5,
            grid=(n_items,),
            in_specs=[rows, wspec(d, f), wspec(d, f), wspec(f, d), bspec(f), bspec(f), bspec(d)],
            out_specs=rows,
            scratch_shapes=[pltpu.VMEM((d, f), BF16), pltpu.VMEM((d, f), BF16),
                            pltpu.VMEM((f, d), BF16)]),
        compiler_params=pltpu.CompilerParams(
            dimension_semantics=("arbitrary",),
            vmem_limit_bytes=_vmem_limit(est)),
        name="moe",
    )(*items, xs, wg, wu, wd, bg, bu, bd)


def _moe_items(counts, n_rows):
    n_blk = n_rows // MOE_BM
    n_items = n_blk + N_EXPERTS - 1
    ends = jnp.cumsum(counts)
    starts = ends - counts
    first_blk = starts // MOE_BM
    last_blk = (ends - 1) // MOE_BM
    per_e = jnp.where(counts > 0, last_blk - first_blk + 1, 0)
    item_end = jnp.cumsum(per_e)
    item_start = item_end - per_e
    total = item_end[-1]
    i = jnp.arange(n_items, dtype=I32)
    last_e = jnp.max(jnp.where(counts > 0, jnp.arange(N_EXPERTS, dtype=I32), 0))
    e = jnp.minimum(jnp.sum(item_end[None, :] <= i[:, None], axis=1), last_e).astype(I32)
    onehot = e[:, None] == jnp.arange(N_EXPERTS, dtype=I32)[None, :]

    def pick(table):
        return jnp.sum(jnp.where(onehot, table[None, :], 0), axis=1)

    blk = pick(first_blk) + (i - pick(item_start))
    lo = jnp.maximum(pick(starts), blk * MOE_BM) - blk * MOE_BM
    hi = jnp.minimum(pick(ends), (blk + 1) * MOE_BM) - blk * MOE_BM
    live = i < total
    blk = jnp.where(live, blk, n_blk - 1)
    lo = jnp.where(live, lo, 0)
    hi = jnp.where(live, hi, 0)
    prev_blk = jnp.concatenate([jnp.full((1,), -1, I32), blk[:-1]])
    first = jnp.where(jnp.logical_and(live, blk != prev_blk), 1, 0)
    return tuple(z.astype(I32) for z in (e, blk, lo, hi, first))


def _combine_kernel(dest_ref, dnext_ref, ys_ref, tw_ref, x1_ref, mod_ref, gpost_ref, o_ref, buf, sem,
                    *, tiles_per_batch):
    tm, d = x1_ref.shape
    i = pl.program_id(0)
    slot = i % 2

    def issue(d_ref, s):
        def start(r, carry):
            for jj in range(TOP_K):
                pltpu.make_async_copy(ys_ref.at[pl.ds(d_ref[r * TOP_K + jj], 1), :],
                                      buf.at[s, jj, pl.ds(r, 1), :],
                                      sem.at[s]).start(priority=jj % 2)
            return carry

        lax.fori_loop(0, tm, start, 0, unroll=4)

    @pl.when(i == 0)
    def _():
        issue(dest_ref, 0)

    @pl.when(i + 1 < pl.num_programs(0))
    def _():
        issue(dnext_ref, 1 - slot)

    for jj in range(TOP_K):
        pltpu.make_async_copy(ys_ref.at[pl.ds(0, tm), :], buf.at[slot, jj], sem.at[slot]).wait()

    tw = tw_ref[...]
    y = tw[:, 0:1] * buf[slot, 0]
    for jj in range(1, TOP_K):
        y = y + tw[:, jj:jj + 1] * buf[slot, jj]
    r = i // tiles_per_batch
    gt2 = mod_ref[pl.ds(r, 1), pl.ds(5 * d, d)]
    o_ref[...] = x1_ref[...] + gt2 * _rms(y, gpost_ref[...])


def _combine(dest, ys, tw, x1, mod, gpost, tokens_per_batch):
    t, d = x1.shape
    tm = min(256, tokens_per_batch)
    n_tiles = t // tm
    kern = functools.partial(_combine_kernel, tiles_per_batch=tokens_per_batch // tm)
    est = 2 * TOP_K * tm * d * 4 + 2 * (2 * tm * d * 4 + tm * LANES * 4) + 3 * tm * d * 4
    return pl.pallas_call(
        kern,
        out_shape=jax.ShapeDtypeStruct((t, d), F32),
        grid=(n_tiles,),
        in_specs=[pl.BlockSpec((tm * TOP_K,), lambda i: (i,), memory_space=pltpu.SMEM),
                  pl.BlockSpec((tm * TOP_K,), lambda i: (jnp.minimum(i + 1, n_tiles - 1),),
                               memory_space=pltpu.SMEM),
                  pl.BlockSpec(memory_space=pl.ANY),
                  pl.BlockSpec((tm, LANES), lambda i: (i, 0)),
                  pl.BlockSpec((tm, d), lambda i: (i, 0)),
                  pl.BlockSpec(mod.shape, lambda i: (0, 0)),
                  pl.BlockSpec((1, d), lambda i: (0, 0))],
        out_specs=pl.BlockSpec((tm, d), lambda i: (i, 0)),
        scratch_shapes=[pltpu.VMEM((2, TOP_K, tm, d), F32),
                        pltpu.SemaphoreType.DMA((2,))],
        compiler_params=pltpu.CompilerParams(
            dimension_semantics=("arbitrary",),
            vmem_limit_bytes=_vmem_limit(est)),
        name="combine",
    )(dest, dest, ys, tw, x1, mod, gpost)


def kernel(x, c, ctx, c_ctx, w_ada, b_ada, g_pre_mix, g_post_mix, g_pre_ffn, g_post_ffn,
           w_in, w_out, hgrn_lb_logits, hgrn_norm, pool_w, pool_scale,
           w_router, b_router, w_gate, b_gate, w_up, b_up, w_down, b_down):
    b, s, d = x.shape
    layer = 0
    n_e = w_router.shape[-1]

    cc = jnp.concatenate([c, c_ctx[None, :]], axis=0)
    cc = jnp.pad(cc, ((0, -(b + 1) % SUBLANES), (0, 0)))
    mod = _ada(cc, w_ada[layer], b_ada[layer][None, :])

    w_in_b = w_in[layer].astype(BF16)
    lbl = hgrn_lb_logits[:2].reshape(4, HG_WIDTH)
    gn = hgrn_norm[layer][None, :]
    gpre = g_pre_mix[layer][None, :]

    lff_c, kf_c, lfb_c, kb_c, v_c = _inproj(ctx, mod, gpre, w_in_b, lbl, gn, mod_row=b, full=False)
    zeros = jnp.zeros((b, HG_HEADS, HEAD_DIM, HEAD_DIM), F32)
    (s_f,) = _hgrn(None, kf_c, v_c, lff_c, zeros, rev=False, need_o=False, want_state=True,
                   name="hgrn_ctx_fwd")
    (s_b,) = _hgrn(None, kb_c, v_c, lfb_c, zeros, rev=True, need_o=False, want_state=True,
                   name="hgrn_ctx_bwd")

    q, lff, kf, lfb, kb, v, gate, u = _inproj(x, mod, gpre, w_in_b, lbl, gn, mod_row=None, full=True)
    (o_f,) = _hgrn(q, kf, v, lff, s_f, rev=False, need_o=True, want_state=False, name="hgrn_fwd")
    (o_hg,) = _hgrn(q, kb, v, lfb, s_b, o_f, gate, rev=True, need_o=True, want_state=False,
                    name="hgrn_bwd")
    o_pool = _pool(u, pool_w[layer].astype(BF16), pool_scale[layer].reshape(len(POOL_WINDOWS), 1, POOL_GROUP))

    wr = jnp.pad(w_router[layer], ((0, 0), (0, LANES - n_e))).astype(BF16)
    br = jnp.pad(b_router[layer], (0, LANES - n_e))[None, :]
    x1, h2, ti, tw, cnt = _outproj(o_hg, o_pool, x, w_out[layer].astype(BF16), mod,
                                   g_post_mix[layer][None, :], g_pre_ffn[layer][None, :], wr, br)

    t = b * s
    starts = jnp.cumsum(cnt, axis=1) - cnt
    dest = _rank(ti.reshape(t, LANES), starts)[:, :TOP_K].reshape(t * TOP_K)
    items = _moe_items(cnt[0, :n_e].astype(I32), t * TOP_K)

    xs = _dispatch(dest, h2.reshape(t, d))
    ys = _moe(items, xs, w_gate[layer], w_up[layer], w_down[layer],
              b_gate[layer][:, None, :], b_up[layer][:, None, :], b_down[layer][:, None, :])
    out = _combine(dest, ys, tw.reshape(t, LANES), x1.reshape(t, d), mod,
                   g_post_ffn[layer][None, :], s)
    return out.reshape(b, s, d)
```

```python
import functools
import math

import numpy as np
import jax
import jax.numpy as jnp
from jax import lax
from jax.experimental import pallas as pl
from jax.experimental.pallas import tpu as pltpu

F32 = jnp.float32
BF16 = jnp.bfloat16
I32 = jnp.int32

GRID_W = 64
HG_HEADS = 4
HEAD_DIM = 128
HG_WIDTH = HG_HEADS * HEAD_DIM
POOL_WINDOWS = (2, 4, 8, 16)
POOL_GROUP = 128
N_EXPERTS = 32
TOP_K = 4
SWIGLU_LIMIT = 7.0
SWIGLU_ALPHA = 1.702
EPS = 1e-6

LANES = 128
SUBLANES = 8
V7X_VMEM_BYTES = 64 * 1024 * 1024

CHUNK = 64
LEVELS = (32, 16, 8, 4, 2, 1)
MOE_BM = 256
LOG2_E = math.log2(math.e)


def _vmem_limit(nbytes):
    return int(min(nbytes * 3 // 2 + (4 << 20), V7X_VMEM_BYTES - (6 << 20)))


def _log2(n):
    assert n & (n - 1) == 0
    return n.bit_length() - 1


def _sigmoid(x):
    return 1.0 / (1.0 + jnp.exp(-x))


def _rms(x, gain):
    return x * lax.rsqrt(jnp.mean(x * x, axis=-1, keepdims=True) + EPS) * gain


def _dot(a, b):
    return jnp.dot(a, b, preferred_element_type=F32)


ROW_TILE = SUBLANES * LANES


def _store_row_tiles(ref, val):
    rows = val.shape[0]
    for c in range(SUBLANES):
        ref[pl.ds(c, rows, stride=SUBLANES), :] = val[:, c * LANES:(c + 1) * LANES]


def _load_row_tiles(ref):
    rows = ref.shape[0] // SUBLANES
    return jnp.concatenate([ref[pl.ds(c, rows, stride=SUBLANES), :] for c in range(SUBLANES)], axis=1)


def _dot_nt(a, b):
    return lax.dot_general(a, b, (((1,), (1,)), ((), ())), preferred_element_type=F32)


def _ada_kernel(c_ref, w_ref, b_ref, o_ref):
    c = c_ref[...]
    s = (c * _sigmoid(c)).astype(BF16)
    o_ref[...] = _dot(s, w_ref[...].astype(BF16)) + b_ref[...]


def _ada(cc, w, b):
    rows, d = cc.shape
    n = w.shape[1]
    tn = 1536 if n % 1536 == 0 else n
    return pl.pallas_call(
        _ada_kernel,
        out_shape=jax.ShapeDtypeStruct((rows, n), F32),
        grid=(n // tn,),
        in_specs=[pl.BlockSpec((rows, d), lambda j: (0, 0)),
                  pl.BlockSpec((d, tn), lambda j: (0, j)),
                  pl.BlockSpec((1, tn), lambda j: (0, j))],
        out_specs=pl.BlockSpec((rows, tn), lambda j: (0, j)),
        compiler_params=pltpu.CompilerParams(
            dimension_semantics=("arbitrary",),
            vmem_limit_bytes=_vmem_limit(2 * d * tn * 4 + d * tn * 2)),
        name="ada",
    )(cc, w, b)


def _inproj_kernel(x_ref, mod_ref, gpre_ref, w_ref, lbl_ref, gn_ref, *outs, mod_row, full):
    d = x_ref.shape[-1]
    r = pl.program_id(0) if mod_row is None else mod_row
    sh = mod_ref[pl.ds(r, 1), pl.ds(0, d)]
    sc = mod_ref[pl.ds(r, 1), pl.ds(d, d)]
    h = (_rms(x_ref[0], gpre_ref[...]) * (1.0 + sc) + sh).astype(BF16)

    def proj(g):
        return _dot(h, w_ref[:, g * HG_WIDTH:(g + 1) * HG_WIDTH])

    def put(ref, val):
        for hh in range(HG_HEADS):
            ref[0, hh] = val[:, hh * HEAD_DIM:(hh + 1) * HEAD_DIM]

    def lower_bound(direction):
        l0 = lbl_ref[pl.ds(direction, 1), :]
        l1 = lbl_ref[pl.ds(2 + direction, 1), :]
        m = jnp.maximum(l0, l1)
        e0 = jnp.exp(l0 - m)
        return e0 / (e0 + jnp.exp(l1 - m))

    if full:
        q_o, lff_o, kf_o, lfb_o, kb_o, v_o, gate_o, u_o = outs
        q = proj(0)
        put(q_o, q * _sigmoid(q))
    else:
        lff_o, kf_o, lfb_o, kb_o, v_o = outs
    for direction, (lf_o, k_o) in enumerate(((lff_o, kf_o), (lfb_o, kb_o))):
        lb = lower_bound(direction)
        sg = _sigmoid(proj(1 + direction))
        put(lf_o, jnp.log(lb + (1.0 - lb) * sg))
        put(k_o, (1.0 - lb) * (1.0 - sg))
    put(v_o, proj(3))
    if full:
        put(gate_o, gn_ref[...] * _sigmoid(proj(4)))
        put(u_o, proj(5))


def _inproj(x, mod, gpre, w_in, lbl, gn, *, mod_row, full):
    b, t, d = x.shape
    tm = min(512, t)
    n_out = 8 if full else 5
    n_cols = w_in.shape[1]
    head = pl.BlockSpec((1, HG_HEADS, tm, HEAD_DIM), lambda i, j: (i, 0, j, 0))
    kern = functools.partial(_inproj_kernel, mod_row=mod_row, full=full)
    est = 2 * tm * d * 4 + 2 * d * n_cols * 2 + n_out * 2 * tm * HG_WIDTH * 4 + 4 * tm * HG_WIDTH * 4
    return pl.pallas_call(
        kern,
        out_shape=[jax.ShapeDtypeStruct((b, HG_HEADS, t, HEAD_DIM), F32)] * n_out,
        grid=(b, t // tm),
        in_specs=[pl.BlockSpec((1, tm, d), lambda i, j: (i, j, 0)),
                  pl.BlockSpec(mod.shape, lambda i, j: (0, 0)),
                  pl.BlockSpec((1, d), lambda i, j: (0, 0)),
                  pl.BlockSpec(w_in.shape, lambda i, j: (0, 0)),
                  pl.BlockSpec(lbl.shape, lambda i, j: (0, 0)),
                  pl.BlockSpec((1, HG_WIDTH), lambda i, j: (0, 0))],
        out_specs=[head] * n_out,
        compiler_params=pltpu.CompilerParams(
            dimension_semantics=("arbitrary", "arbitrary"),
            vmem_limit_bytes=_vmem_limit(est)),
        name="inproj_full" if full else "inproj_ctx",
    )(x, mod, gpre, w_in, lbl, gn)


def _hgrn_tables(rev):
    row = np.arange(CHUNK)[:, None]
    col = np.arange(CHUNK)[None, :]
    tri = ((col >= row) if rev else (col <= row)).astype(np.float32)
    blocks, masks = [tri], []
    for h in LEVELS:
        start = (np.arange(CHUNK) // (2 * h)) * (2 * h)
        blocks.append(tri - tri[start + (h if rev else h - 1)])
        is_q = ((row // h) % 2) == (0 if rev else 1)
        key_half = ((col // h) % 2) == (1 if rev else 0)
        masks.append(((row // (2 * h)) == (col // (2 * h))) & is_q & key_half)
    pair = np.stack([np.concatenate(masks[i:i + 2], axis=1) for i in range(0, len(LEVELS), 2)])
    cums = np.concatenate(blocks, axis=0)
    return (jnp.asarray(np.concatenate([cums, cums], axis=1), BF16), jnp.asarray(pair, F32),
            jnp.asarray(np.eye(CHUNK), F32))


def _hgrn_chunk(qs, ks, vs, lfs, sts, tabs, rev, need_o):
    cums_ref, pair_ref, eye_ref = tabs
    heads = range(len(ks))
    n_pairs = len(LEVELS) // 2
    n_cum = CHUNK * (1 + len(LEVELS)) if need_o else CHUNK
    cums = cums_ref[0:n_cum, :]

    his, los = [], []
    for h in heads:
        lf2 = lfs[h] * LOG2_E
        hi = lf2.astype(BF16)
        his.append(hi)
        los.append((lf2 - hi.astype(F32)).astype(BF16))
    both = _dot(cums, jnp.concatenate([jnp.concatenate(his, axis=1), jnp.concatenate(los, axis=1)], axis=0))
    bd = [both[:, h * LANES:(h + 1) * LANES] for h in heads]

    k16 = [ks[h].astype(BF16) for h in heads]
    v16 = [vs[h].astype(BF16) for h in heads]
    b = [bd[h][0:CHUNK] for h in heads]
    edge = [b[h][0:1, :] if rev else b[h][CHUNK - 1:CHUNK, :] for h in heads]
    kd = [k16[h] * jnp.exp2(edge[h] - b[h]).astype(BF16) for h in heads]
    grow = [lax.dot_general(v16[h], kd[h], (((0,), (0,)), ((), ())), preferred_element_type=F32)
            for h in heads]
    st_new = [sts[h] * jnp.exp2(edge[h]) + grow[h] for h in heads]
    if not need_o:
        return None, st_new

    q16 = [qs[h].astype(BF16) for h in heads]
    carried = [_dot_nt(q16[h] * jnp.exp2(b[h]).astype(BF16), sts[h].astype(BF16)) for h in heads]

    zero = jnp.zeros((CHUNK, LANES), BF16)
    att = [[] for _ in heads]
    for p in range(n_pairs):
        lhs, rhs = [], []
        for h in heads:
            qe, ke = [], []
            for i in (2 * p, 2 * p + 1):
                e = jnp.exp2(-jnp.abs(bd[h][CHUNK * (1 + i):CHUNK * (2 + i)])).astype(BF16)
                qe.append(q16[h] * e)
                ke.append(k16[h] * e)
            lhs.append(jnp.concatenate(qe, axis=1))
            rhs.append(jnp.concatenate([jnp.concatenate([ke[0], zero], axis=1),
                                        jnp.concatenate([zero, ke[1]], axis=1)], axis=0))
        scores = [_dot_nt(lhs[h], rhs[h]) for h in heads]
        for h in heads:
            att[h].append((scores[h] * pair_ref[p]).astype(BF16))
    same_row = [_dot_nt(q16[h], k16[h]) for h in heads]
    o = []
    for h in heads:
        att[h].append((same_row[h] * eye_ref[...]).astype(BF16))
        vals = jnp.concatenate([v16[h]] * (2 * n_pairs + 1), axis=0)
        o.append(carried[h] + _dot(jnp.concatenate(att[h], axis=1), vals))
    return o, st_new


def _hgrn_kernel(*refs, rev, need_o, final, want_state, n_chunks):
    it = iter(refs)
    tabs = tuple(next(it) for _ in range(3))
    q_ref = next(it) if need_o else None
    k_ref, v_ref, lf_ref, s0_ref = next(it), next(it), next(it), next(it)
    prev_ref = next(it) if final else None
    gate_ref = next(it) if final else None
    o_ref = next(it) if need_o else None
    sout_ref = next(it) if want_state else None
    st_ref = next(it)

    j = pl.program_id(1)
    nb = pl.num_programs(1)

    @pl.when(j == 0)
    def _():
        st_ref[...] = s0_ref[0]

    def body(ci, carry):
        c = (n_chunks - 1 - ci) if rev else ci
        base = pl.multiple_of(c * CHUNK, CHUNK)
        rows = pl.ds(base, CHUNK)
        heads = range(HG_HEADS)
        o, st_new = _hgrn_chunk([q_ref[0, hh, rows, :] for hh in heads] if need_o else None,
                                [k_ref[0, hh, rows, :] for hh in heads],
                                [v_ref[0, hh, rows, :] for hh in heads],
                                [lf_ref[0, hh, rows, :] for hh in heads],
                                [st_ref[hh] for hh in heads], tabs, rev, need_o)
        for hh in heads:
            st_ref[hh] = st_new[hh]
            if need_o:
                out = o[hh]
                if final:
                    tot = out + prev_ref[0, hh, rows, :]
                    out = tot * lax.rsqrt(jnp.mean(tot * tot, axis=-1, keepdims=True) + EPS)
                    out = out * gate_ref[0, hh, rows, :]
                o_ref[0, hh, rows, :] = out
        return carry

    lax.fori_loop(0, n_chunks, body, 0, unroll=4)

    if want_state:
        @pl.when(j == nb - 1)
        def _():
            sout_ref[0] = st_ref[...]


def _hgrn(q, k, v, lf, s0, prev=None, gate=None, *, rev, need_o, want_state, name):
    b, hh, t, _ = k.shape
    tb = min(512, t)
    nb = t // tb
    final = prev is not None
    tabs = _hgrn_tables(rev)

    seq = pl.BlockSpec((1, hh, tb, HEAD_DIM), lambda i, j: (i, 0, (nb - 1 - j) if rev else j, 0))
    state = pl.BlockSpec((1, hh, HEAD_DIM, HEAD_DIM), lambda i, j: (i, 0, 0, 0))
    args = list(tabs)
    in_specs = [pl.BlockSpec(a.shape, functools.partial(lambda nd, i, j: (0,) * nd, a.ndim)) for a in tabs]
    for a in ((q,) if need_o else ()) + (k, v, lf):
        args.append(a)
        in_specs.append(seq)
    args.append(s0)
    in_specs.append(state)
    if final:
        args += [prev, gate]
        in_specs += [seq, seq]
    out_shape, out_specs = [], []
    if need_o:
        out_shape.append(jax.ShapeDtypeStruct((b, hh, t, HEAD_DIM), F32))
        out_specs.append(seq)
    if want_state:
        out_shape.append(jax.ShapeDtypeStruct((b, hh, HEAD_DIM, HEAD_DIM), F32))
        out_specs.append(state)
    kern = functools.partial(_hgrn_kernel, rev=rev, need_o=need_o, final=final,
                             want_state=want_state, n_chunks=tb // CHUNK)
    est = 2 * (len(args) + len(out_shape)) * hh * tb * HEAD_DIM * 4
    return pl.pallas_call(
        kern,
        out_shape=out_shape,
        grid=(b, nb),
        in_specs=in_specs,
        out_specs=out_specs,
        scratch_shapes=[pltpu.VMEM((hh, HEAD_DIM, HEAD_DIM), F32)],
        compiler_params=pltpu.CompilerParams(
            dimension_semantics=("arbitrary", "arbitrary"),
            vmem_limit_bytes=_vmem_limit(est)),
        name=name,
    )(*args)


def _window_sum(x, w, stride, pos, extent):
    n = x.shape[0]

    def ahead(y, dist):
        return jnp.where(pos + dist < extent, pltpu.roll(y, (n - dist * stride) % n, 0), 0.0)

    def behind(y, dist):
        return jnp.where(pos >= dist, pltpu.roll(y, dist * stride, 0), 0.0)

    fwd, bwd, h = x, x, 1
    while h < w // 2:
        fwd = fwd + ahead(fwd, h)
        bwd = bwd + behind(bwd, h)
        h *= 2
    return fwd + behind(bwd, 1)


def _window_count(pos, w, extent):
    return (jnp.minimum(pos + w // 2, extent) - jnp.maximum(pos - w // 2, 0)).astype(F32)


def _pool_kernel(u_ref, pw_ref, ps_ref, o_ref):
    g = pl.program_id(1)
    n = u_ref.shape[2]
    tok = lax.broadcasted_iota(I32, (n, LANES), 0)
    col = tok & (GRID_W - 1)
    row = tok >> _log2(GRID_W)
    for gi, w in enumerate(POOL_WINDOWS):
        @pl.when(g == gi)
        def _(w=w):
            u = u_ref[0, 0]
            rows = n // GRID_W
            s = _window_sum(_window_sum(u, w, GRID_W, row, rows), w, 1, col, GRID_W)
            m = s / (_window_count(row, w, rows) * _window_count(col, w, GRID_W))
            o_ref[0, 0] = _dot((m - u).astype(BF16), pw_ref[0]) * ps_ref[0]


def _pool(u, pw, ps):
    b, g, t, c = u.shape
    blk = pl.BlockSpec((1, 1, t, c), lambda i, j: (i, j, 0, 0))
    return pl.pallas_call(
        _pool_kernel,
        out_shape=jax.ShapeDtypeStruct(u.shape, F32),
        grid=(b, g),
        in_specs=[blk,
                  pl.BlockSpec((1, c, c), lambda i, j: (j, 0, 0)),
                  pl.BlockSpec((1, 1, c), lambda i, j: (j, 0, 0))],
        out_specs=blk,
        compiler_params=pltpu.CompilerParams(
            dimension_semantics=("arbitrary", "arbitrary"),
            vmem_limit_bytes=_vmem_limit(12 * t * c * 4)),
        name="pool",
    )(u, pw, ps)


def _outproj_kernel(hg_ref, pool_ref, x_ref, w_ref, mod_ref, gpost_ref, gpre_ref, wr_ref, br_ref,
                    x1_ref, h2_ref, ti_ref, tw_ref, cnt_ref):
    d = x_ref.shape[-1]
    r = pl.program_id(0)

    @pl.when(jnp.logical_and(r == 0, pl.program_id(1) == 0))
    def _():
        cnt_ref[...] = jnp.zeros(cnt_ref.shape, F32)

    gt1 = mod_ref[pl.ds(r, 1), pl.ds(2 * d, d)]
    sh2 = mod_ref[pl.ds(r, 1), pl.ds(3 * d, d)]
    sc2 = mod_ref[pl.ds(r, 1), pl.ds(4 * d, d)]
    cat = jnp.concatenate([hg_ref[0, hh] for hh in range(HG_HEADS)]
                          + [pool_ref[0, g] for g in range(len(POOL_WINDOWS))], axis=1)
    y = _dot(cat.astype(BF16), w_ref[...])
    x1 = x_ref[0] + gt1 * _rms(y, gpost_ref[...])
    x1_ref[0] = x1
    h2 = _rms(x1, gpre_ref[...]) * (1.0 + sc2) + sh2
    _store_row_tiles(h2_ref.at[0], h2)

    lane = lax.broadcasted_iota(I32, (h2.shape[0], LANES), 1).astype(F32)
    logits = jnp.where(lane < N_EXPERTS, _dot(h2.astype(BF16), wr_ref[...]) + br_ref[...], -jnp.inf)
    ti = jnp.zeros(logits.shape, F32)
    tw = jnp.zeros(logits.shape, F32)
    chosen = jnp.zeros(logits.shape, F32)
    top = None
    for jj in range(TOP_K):
        m = jnp.max(logits, axis=1, keepdims=True)
        idx = jnp.min(jnp.where(logits == m, lane, float(LANES)), axis=1, keepdims=True)
        logits = jnp.where(lane == idx, -jnp.inf, logits)
        chosen = jnp.where(lane == idx, 1.0, chosen)
        top = m if top is None else top
        ti = jnp.where(lane == jj, idx, ti)
        tw = jnp.where(lane == jj, jnp.exp(m - top), tw)
    ti_ref[0] = ti.astype(I32)
    tw_ref[0] = tw / jnp.sum(tw, axis=1, keepdims=True)
    cnt_ref[...] += jnp.sum(chosen, axis=0, keepdims=True)


def _outproj(hg, pool, x, w_out, mod, gpost, gpre, wr, br):
    b, t, d = x.shape
    tm = min(512, t)
    head = pl.BlockSpec((1, HG_HEADS, tm, HEAD_DIM), lambda i, j: (i, 0, j, 0))
    tok = pl.BlockSpec((1, tm, d), lambda i, j: (i, j, 0))
    lanes = pl.BlockSpec((1, tm, LANES), lambda i, j: (i, j, 0))
    vec = pl.BlockSpec((1, d), lambda i, j: (0, 0))
    est = 2 * (2 * tm * HG_WIDTH * 4 + 3 * tm * d * 4 + 2 * tm * LANES * 4) + 2 * d * d * 2 + 6 * tm * d * 4
    return pl.pallas_call(
        _outproj_kernel,
        out_shape=[jax.ShapeDtypeStruct((b, t, d), F32),
                   jax.ShapeDtypeStruct((b, t * d // LANES, LANES), F32),
                   jax.ShapeDtypeStruct((b, t, LANES), I32), jax.ShapeDtypeStruct((b, t, LANES), F32),
                   jax.ShapeDtypeStruct((1, LANES), F32)],
        grid=(b, t // tm),
        in_specs=[head, head, tok,
                  pl.BlockSpec(w_out.shape, lambda i, j: (0, 0)),
                  pl.BlockSpec(mod.shape, lambda i, j: (0, 0)),
                  vec, vec,
                  pl.BlockSpec(wr.shape, lambda i, j: (0, 0)),
                  pl.BlockSpec((1, LANES), lambda i, j: (0, 0))],
        out_specs=[tok, pl.BlockSpec((1, tm * d // LANES, LANES), lambda i, j: (i, j, 0)),
                   lanes, lanes, pl.BlockSpec((1, LANES), lambda i, j: (0, 0))],
        compiler_params=pltpu.CompilerParams(
            dimension_semantics=("arbitrary", "arbitrary"),
            vmem_limit_bytes=_vmem_limit(est)),
        name="outproj",
    )(hg, pool, x, w_out, mod, gpost, gpre, wr, br)


def _rank_kernel(ti_ref, start_ref, dest_ref, carry_ref):
    i = pl.program_id(0)

    @pl.when(i == 0)
    def _():
        carry_ref[...] = start_ref[...]

    ti = ti_ref[...]
    tm = ti.shape[0]
    lane = lax.broadcasted_iota(I32, (tm, LANES), 1)
    onehot = [lane == ti[:, jj:jj + 1] for jj in range(TOP_K)]
    cnt = sum(jnp.where(oh, 1.0, 0.0) for oh in onehot)
    row = lax.broadcasted_iota(I32, (tm, tm), 0)
    col = lax.broadcasted_iota(I32, (tm, tm), 1)
    before = jnp.where(col < row, 1.0, 0.0).astype(BF16)
    prefix = _dot(before, cnt.astype(BF16)) + carry_ref[...]
    rank = jnp.zeros((tm, LANES), F32)
    for jj in range(TOP_K):
        rj = jnp.sum(jnp.where(onehot[jj], prefix, 0.0), axis=1, keepdims=True)
        rank = jnp.where(lane == jj, rj, rank)
    dest_ref[...] = rank.astype(I32) * SUBLANES
    carry_ref[...] += jnp.sum(cnt, axis=0, keepdims=True)


def _rank(ti, starts):
    t = ti.shape[0]
    tm = min(512, t)
    return pl.pallas_call(
        _rank_kernel,
        out_shape=jax.ShapeDtypeStruct((t, LANES), I32),
        grid=(t // tm,),
        in_specs=[pl.BlockSpec((tm, LANES), lambda i: (i, 0)),
                  pl.BlockSpec((1, LANES), lambda i: (0, 0))],
        out_specs=pl.BlockSpec((tm, LANES), lambda i: (i, 0)),
        scratch_shapes=[pltpu.VMEM((1, LANES), F32)],
        compiler_params=pltpu.CompilerParams(dimension_semantics=("arbitrary",)),
        name="rank",
    )(ti, starts)


def _tile_rows(start):
    return pl.ds(pl.multiple_of(start, SUBLANES), SUBLANES)


def _dispatch_kernel(dest_ref, h_ref, xs_ref, sem):
    tm = h_ref.shape[0] // SUBLANES

    def start(r, carry):
        for jj in range(TOP_K):
            pltpu.make_async_copy(h_ref.at[_tile_rows(r * SUBLANES), :],
                                  xs_ref.at[_tile_rows(dest_ref[r * TOP_K + jj]), :],
                                  sem).start(priority=jj % 2)
        return carry

    lax.fori_loop(0, tm, start, 0, unroll=4)
    for jj in range(TOP_K):
        pltpu.make_async_copy(h_ref, xs_ref.at[pl.ds(0, tm * SUBLANES), :], sem).wait()


def _dispatch(dest, h2):
    rows, l = h2.shape
    t = rows // SUBLANES
    tm = min(256, t)
    return pl.pallas_call(
        _dispatch_kernel,
        out_shape=jax.ShapeDtypeStruct((rows * TOP_K, l), h2.dtype),
        grid=(t // tm,),
        in_specs=[pl.BlockSpec((tm * TOP_K,), lambda i: (i,), memory_space=pltpu.SMEM),
                  pl.BlockSpec((tm * SUBLANES, l), lambda i: (i, 0))],
        out_specs=pl.BlockSpec(memory_space=pl.ANY),
        scratch_shapes=[pltpu.SemaphoreType.DMA],
        compiler_params=pltpu.CompilerParams(dimension_semantics=("arbitrary",)),
        name="dispatch",
    )(dest, h2)


def _moe_kernel(ie_ref, ib_ref, lo_ref, hi_ref, first_ref,
                x_ref, wg_ref, wu_ref, wd_ref, bg_ref, bu_ref, bd_ref, y_ref,
                wgb, wub, wdb):
    i = pl.program_id(0)
    prev = jnp.maximum(i - 1, 0)
    changed = jnp.logical_or(i == 0, ie_ref[i] != ie_ref[prev])
    lo, hi = lo_ref[i], hi_ref[i]

    @pl.when(jnp.logical_and(changed, hi > lo))
    def _():
        wgb[...] = wg_ref[...].astype(BF16)
        wub[...] = wu_ref[...].astype(BF16)
        wdb[...] = wd_ref[...].astype(BF16)

    @pl.when(hi > lo)
    def _():
        x = _load_row_tiles(x_ref).astype(BF16)
        gate = jnp.minimum(_dot(x, wgb[...]) + bg_ref[...], SWIGLU_LIMIT)
        up = jnp.clip(_dot(x, wub[...]) + bu_ref[...], -SWIGLU_LIMIT, SWIGLU_LIMIT)
        hmid = (up + 1.0) * gate * _sigmoid(SWIGLU_ALPHA * gate)
        y = _dot(hmid.astype(BF16), wdb[...]) + bd_ref[...]
        row = lax.broadcasted_iota(I32, y.shape, 0)
        mine = jnp.logical_and(row >= lo, row < hi)

        @pl.when(first_ref[i] == 1)
        def _():
            _store_row_tiles(y_ref, jnp.where(mine, y, 0.0))

        @pl.when(first_ref[i] == 0)
        def _():
            _store_row_tiles(y_ref, jnp.where(mine, y, _load_row_tiles(y_ref)))


def _moe(items, xs, wg, wu, wd, bg, bu, bd):
    d, f = wg.shape[-2:]
    n_items = items[0].shape[0]
    rows = pl.BlockSpec((MOE_BM * SUBLANES, LANES), lambda i, ie, ib, lo, hi, fi: (ib[i], 0))

    def wspec(k, n):
        return pl.BlockSpec((None, k, n), lambda i, ie, ib, lo, hi, fi: (ie[i], 0, 0))

    def bspec(n):
        return pl.BlockSpec((None, 1, n), lambda i, ie, ib, lo, hi, fi: (ie[i], 0, 0))

    est = 2 * 3 * d * f * 4 + 3 * d * f * 2 + 4 * MOE_BM * d * 4 + 4 * MOE_BM * f * 4
    return pl.pallas_call(
        _moe_kernel,
        out_shape=jax.ShapeDtypeStruct(xs.shape, F32),
        grid_spec=pltpu.PrefetchScalarGridSpec(
            num_scalar_prefetch=5,
            grid=(n_items,),
            in_specs=[rows, wspec(d, f), wspec(d, f), wspec(f, d), bspec(f), bspec(f), bspec(d)],
            out_specs=rows,
            scratch_shapes=[pltpu.VMEM((d, f), BF16), pltpu.VMEM((d, f), BF16),
                            pltpu.VMEM((f, d), BF16)]),
        compiler_params=pltpu.CompilerParams(
            dimension_semantics=("arbitrary",),
            vmem_limit_bytes=_vmem_limit(est)),
        name="moe",
    )(*items, xs, wg, wu, wd, bg, bu, bd)


def _moe_items(counts, n_rows):
    n_blk = n_rows // MOE_BM
    n_items = n_blk + N_EXPERTS - 1
    ends = jnp.cumsum(counts)
    starts = ends - counts
    first_blk = starts // MOE_BM
    last_blk = (ends - 1) // MOE_BM
    per_e = jnp.where(counts > 0, last_blk - first_blk + 1, 0)
    item_end = jnp.cumsum(per_e)
    item_start = item_end - per_e
    total = item_end[-1]
    i = jnp.arange(n_items, dtype=I32)
    last_e = jnp.max(jnp.where(counts > 0, jnp.arange(N_EXPERTS, dtype=I32), 0))
    e = jnp.minimum(jnp.sum(item_end[None, :] <= i[:, None], axis=1), last_e).astype(I32)
    onehot = e[:, None] == jnp.arange(N_EXPERTS, dtype=I32)[None, :]

    def pick(table):
        return jnp.sum(jnp.where(onehot, table[None, :], 0), axis=1)

    blk = pick(first_blk) + (i - pick(item_start))
    lo = jnp.maximum(pick(starts), blk * MOE_BM) - blk * MOE_BM
    hi = jnp.minimum(pick(ends), (blk + 1) * MOE_BM) - blk * MOE_BM
    live = i < total
    blk = jnp.where(live, blk, n_blk - 1)
    lo = jnp.where(live, lo, 0)
    hi = jnp.where(live, hi, 0)
    prev_blk = jnp.concatenate([jnp.full((1,), -1, I32), blk[:-1]])
    first = jnp.where(jnp.logical_and(live, blk != prev_blk), 1, 0)
    return tuple(z.astype(I32) for z in (e, blk, lo, hi, first))


def _combine_kernel(dest_ref, dnext_ref, ys_ref, tw_ref, x1_ref, mod_ref, gpost_ref, o_ref, buf, sem,
                    *, tiles_per_batch):
    tm, d = x1_ref.shape
    i = pl.program_id(0)
    slot = i % 2

    def issue(d_ref, s):
        def start(r, carry):
            for jj in range(TOP_K):
                pltpu.make_async_copy(ys_ref.at[_tile_rows(d_ref[r * TOP_K + jj]), :],
                                      buf.at[s, jj, _tile_rows(r * SUBLANES), :],
                                      sem.at[s]).start(priority=jj % 2)
            return carry

        lax.fori_loop(0, tm, start, 0, unroll=4)

    @pl.when(i == 0)
    def _():
        issue(dest_ref, 0)

    @pl.when(i + 1 < pl.num_programs(0))
    def _():
        issue(dnext_ref, 1 - slot)

    for jj in range(TOP_K):
        pltpu.make_async_copy(ys_ref.at[pl.ds(0, tm * SUBLANES), :], buf.at[slot, jj],
                              sem.at[slot]).wait()

    tw = tw_ref[...]
    y = tw[:, 0:1] * _load_row_tiles(buf.at[slot, 0])
    for jj in range(1, TOP_K):
        y = y + tw[:, jj:jj + 1] * _load_row_tiles(buf.at[slot, jj])
    r = i // tiles_per_batch
    gt2 = mod_ref[pl.ds(r, 1), pl.ds(5 * d, d)]
    o_ref[...] = x1_ref[...] + gt2 * _rms(y, gpost_ref[...])


def _combine(dest, ys, tw, x1, mod, gpost, tokens_per_batch):
    t, d = x1.shape
    tm = min(256, tokens_per_batch)
    n_tiles = t // tm
    kern = functools.partial(_combine_kernel, tiles_per_batch=tokens_per_batch // tm)
    est = 2 * TOP_K * tm * d * 4 + 2 * (2 * tm * d * 4 + tm * LANES * 4) + 3 * tm * d * 4
    return pl.pallas_call(
        kern,
        out_shape=jax.ShapeDtypeStruct((t, d), F32),
        grid=(n_tiles,),
        in_specs=[pl.BlockSpec((tm * TOP_K,), lambda i: (i,), memory_space=pltpu.SMEM),
                  pl.BlockSpec((tm * TOP_K,), lambda i: (jnp.minimum(i + 1, n_tiles - 1),),
                               memory_space=pltpu.SMEM),
                  pl.BlockSpec(memory_space=pl.ANY),
                  pl.BlockSpec((tm, LANES), lambda i: (i, 0)),
                  pl.BlockSpec((tm, d), lambda i: (i, 0)),
                  pl.BlockSpec(mod.shape, lambda i: (0, 0)),
                  pl.BlockSpec((1, d), lambda i: (0, 0))],
        out_specs=pl.BlockSpec((tm, d), lambda i: (i, 0)),
        scratch_shapes=[pltpu.VMEM((2, TOP_K, tm * SUBLANES, LANES), F32),
                        pltpu.SemaphoreType.DMA((2,))],
        compiler_params=pltpu.CompilerParams(
            dimension_semantics=("arbitrary",),
            vmem_limit_bytes=_vmem_limit(est)),
        name="combine",
    )(dest, dest, ys, tw, x1, mod, gpost)


def kernel(x, c, ctx, c_ctx, w_ada, b_ada, g_pre_mix, g_post_mix, g_pre_ffn, g_post_ffn,
           w_in, w_out, hgrn_lb_logits, hgrn_norm, pool_w, pool_scale,
           w_router, b_router, w_gate, b_gate, w_up, b_up, w_down, b_down):
    b, s, d = x.shape
    layer = 0
    n_e = w_router.shape[-1]

    cc = jnp.concatenate([c, c_ctx[None, :]], axis=0)
    cc = jnp.pad(cc, ((0, -(b + 1) % SUBLANES), (0, 0)))
    mod = _ada(cc, w_ada[layer], b_ada[layer][None, :])

    w_in_b = w_in[layer].astype(BF16)
    lbl = hgrn_lb_logits[:2].reshape(4, HG_WIDTH)
    gn = hgrn_norm[layer][None, :]
    gpre = g_pre_mix[layer][None, :]

    lff_c, kf_c, lfb_c, kb_c, v_c = _inproj(ctx, mod, gpre, w_in_b, lbl, gn, mod_row=b, full=False)
    zeros = jnp.zeros((b, HG_HEADS, HEAD_DIM, HEAD_DIM), F32)
    (s_f,) = _hgrn(None, kf_c, v_c, lff_c, zeros, rev=False, need_o=False, want_state=True,
                   name="hgrn_ctx_fwd")
    (s_b,) = _hgrn(None, kb_c, v_c, lfb_c, zeros, rev=True, need_o=False, want_state=True,
                   name="hgrn_ctx_bwd")

    q, lff, kf, lfb, kb, v, gate, u = _inproj(x, mod, gpre, w_in_b, lbl, gn, mod_row=None, full=True)
    (o_f,) = _hgrn(q, kf, v, lff, s_f, rev=False, need_o=True, want_state=False, name="hgrn_fwd")
    (o_hg,) = _hgrn(q, kb, v, lfb, s_b, o_f, gate, rev=True, need_o=True, want_state=False,
                    name="hgrn_bwd")
    o_pool = _pool(u, pool_w[layer].astype(BF16), pool_scale[layer].reshape(len(POOL_WINDOWS), 1, POOL_GROUP))

    wr = jnp.pad(w_router[layer], ((0, 0), (0, LANES - n_e))).astype(BF16)
    br = jnp.pad(b_router[layer], (0, LANES - n_e))[None, :]
    x1, h2, ti, tw, cnt = _outproj(o_hg, o_pool, x, w_out[layer].astype(BF16), mod,
                                   g_post_mix[layer][None, :], g_pre_ffn[layer][None, :], wr, br)

    t = b * s
    starts = jnp.cumsum(cnt, axis=1) - cnt
    dest = _rank(ti.reshape(t, LANES), starts)[:, :TOP_K].reshape(t * TOP_K)
    items = _moe_items(cnt[0, :n_e].astype(I32), t * TOP_K)

    xs = _dispatch(dest, h2.reshape(t * d // LANES, LANES))
    ys = _moe(items, xs, w_gate[layer], w_up[layer], w_down[layer],
              b_gate[layer][:, None, :], b_up[layer][:, None, :], b_down[layer][:, None, :])
    out = _combine(dest, ys, tw.reshape(t, LANES), x1.reshape(t, d), mod,
                   g_post_ffn[layer][None, :], s)
    return out.reshape(b, s, d)
```

```python
import functools
import math

import numpy as np
import jax
import jax.numpy as jnp
from jax import lax
from jax.experimental import pallas as pl
from jax.experimental.pallas import tpu as pltpu

F32 = jnp.float32
BF16 = jnp.bfloat16
I32 = jnp.int32

GRID_W = 64
HG_HEADS = 4
HEAD_DIM = 128
HG_WIDTH = HG_HEADS * HEAD_DIM
POOL_WINDOWS = (2, 4, 8, 16)
POOL_GROUP = 128
N_EXPERTS = 32
TOP_K = 4
SWIGLU_LIMIT = 7.0
SWIGLU_ALPHA = 1.702
EPS = 1e-6

LANES = 128
SUBLANES = 8
V7X_VMEM_BYTES = 64 * 1024 * 1024

CHUNK = 64
HGRN_GROUP = 4
LEVELS = (32, 16, 8, 4, 2, 1)
MOE_BM = 256
LOG2_E = math.log2(math.e)


def _vmem_limit(nbytes):
    return int(min(nbytes * 3 // 2 + (4 << 20), V7X_VMEM_BYTES - (6 << 20)))


def _log2(n):
    assert n & (n - 1) == 0
    return n.bit_length() - 1


def _sigmoid(x):
    return 1.0 / (1.0 + jnp.exp(-x))


def _rms(x, gain):
    return x * lax.rsqrt(jnp.mean(x * x, axis=-1, keepdims=True) + EPS) * gain


def _dot(a, b):
    return jnp.dot(a, b, preferred_element_type=F32)


ROW_TILE = SUBLANES * LANES


def _store_row_tiles(ref, val):
    rows = val.shape[0]
    for c in range(SUBLANES):
        ref[pl.ds(c, rows, stride=SUBLANES), :] = val[:, c * LANES:(c + 1) * LANES]


def _load_row_tiles(ref):
    rows = ref.shape[0] // SUBLANES
    return jnp.concatenate([ref[pl.ds(c, rows, stride=SUBLANES), :] for c in range(SUBLANES)], axis=1)


def _dot_nt(a, b):
    return lax.dot_general(a, b, (((1,), (1,)), ((), ())), preferred_element_type=F32)


def _ada_kernel(c_ref, w_ref, b_ref, o_ref):
    c = c_ref[...]
    s = (c * _sigmoid(c)).astype(BF16)
    o_ref[...] = _dot(s, w_ref[...].astype(BF16)) + b_ref[...]


def _ada(cc, w, b):
    rows, d = cc.shape
    n = w.shape[1]
    tn = 1536 if n % 1536 == 0 else n
    return pl.pallas_call(
        _ada_kernel,
        out_shape=jax.ShapeDtypeStruct((rows, n), F32),
        grid=(n // tn,),
        in_specs=[pl.BlockSpec((rows, d), lambda j: (0, 0)),
                  pl.BlockSpec((d, tn), lambda j: (0, j)),
                  pl.BlockSpec((1, tn), lambda j: (0, j))],
        out_specs=pl.BlockSpec((rows, tn), lambda j: (0, j)),
        compiler_params=pltpu.CompilerParams(
            dimension_semantics=("arbitrary",),
            vmem_limit_bytes=_vmem_limit(2 * d * tn * 4 + d * tn * 2)),
        name="ada",
    )(cc, w, b)


def _inproj_kernel(x_ref, mod_ref, gpre_ref, w_ref, lbl_ref, gn_ref, *outs, mod_row, full):
    d = x_ref.shape[-1]
    r = pl.program_id(0) if mod_row is None else mod_row
    sh = mod_ref[pl.ds(r, 1), pl.ds(0, d)]
    sc = mod_ref[pl.ds(r, 1), pl.ds(d, d)]
    h = (_rms(x_ref[0], gpre_ref[...]) * (1.0 + sc) + sh).astype(BF16)

    def proj(g):
        return _dot(h, w_ref[:, g * HG_WIDTH:(g + 1) * HG_WIDTH])

    def put(ref, val):
        for hh in range(HG_HEADS):
            ref[0, hh] = val[:, hh * HEAD_DIM:(hh + 1) * HEAD_DIM]

    def lower_bound(direction):
        l0 = lbl_ref[pl.ds(direction, 1), :]
        l1 = lbl_ref[pl.ds(2 + direction, 1), :]
        m = jnp.maximum(l0, l1)
        e0 = jnp.exp(l0 - m)
        return e0 / (e0 + jnp.exp(l1 - m))

    if full:
        q_o, lff_o, kf_o, lfb_o, kb_o, v_o, gate_o, u_o = outs
        q = proj(0)
        put(q_o, q * _sigmoid(q))
    else:
        lff_o, kf_o, lfb_o, kb_o, v_o = outs
    for direction, (lf_o, k_o) in enumerate(((lff_o, kf_o), (lfb_o, kb_o))):
        lb = lower_bound(direction)
        sg = _sigmoid(proj(1 + direction))
        put(lf_o, jnp.log(lb + (1.0 - lb) * sg))
        put(k_o, (1.0 - lb) * (1.0 - sg))
    put(v_o, proj(3))
    if full:
        put(gate_o, gn_ref[...] * _sigmoid(proj(4)))
        put(u_o, proj(5))


def _inproj(x, mod, gpre, w_in, lbl, gn, *, mod_row, full):
    b, t, d = x.shape
    tm = min(512, t)
    n_out = 8 if full else 5
    n_cols = w_in.shape[1]
    head = pl.BlockSpec((1, HG_HEADS, tm, HEAD_DIM), lambda i, j: (i, 0, j, 0))
    kern = functools.partial(_inproj_kernel, mod_row=mod_row, full=full)
    est = 2 * tm * d * 4 + 2 * d * n_cols * 2 + n_out * 2 * tm * HG_WIDTH * 4 + 4 * tm * HG_WIDTH * 4
    return pl.pallas_call(
        kern,
        out_shape=[jax.ShapeDtypeStruct((b, HG_HEADS, t, HEAD_DIM), F32)] * n_out,
        grid=(b, t // tm),
        in_specs=[pl.BlockSpec((1, tm, d), lambda i, j: (i, j, 0)),
                  pl.BlockSpec(mod.shape, lambda i, j: (0, 0)),
                  pl.BlockSpec((1, d), lambda i, j: (0, 0)),
                  pl.BlockSpec(w_in.shape, lambda i, j: (0, 0)),
                  pl.BlockSpec(lbl.shape, lambda i, j: (0, 0)),
                  pl.BlockSpec((1, HG_WIDTH), lambda i, j: (0, 0))],
        out_specs=[head] * n_out,
        compiler_params=pltpu.CompilerParams(
            dimension_semantics=("arbitrary", "arbitrary"),
            vmem_limit_bytes=_vmem_limit(est)),
        name="inproj_full" if full else "inproj_ctx",
    )(x, mod, gpre, w_in, lbl, gn)


def _hgrn_tables(rev):
    row = np.arange(CHUNK)[:, None]
    col = np.arange(CHUNK)[None, :]
    tri = ((col >= row) if rev else (col <= row)).astype(np.float32)
    blocks, masks = [tri], []
    for h in LEVELS:
        start = (np.arange(CHUNK) // (2 * h)) * (2 * h)
        blocks.append(tri - tri[start + (h if rev else h - 1)])
        is_q = ((row // h) % 2) == (0 if rev else 1)
        key_half = ((col // h) % 2) == (1 if rev else 0)
        masks.append(((row // (2 * h)) == (col // (2 * h))) & is_q & key_half)
    pair = np.stack([np.concatenate(masks[i:i + 2], axis=1) for i in range(0, len(LEVELS), 2)])
    cums = np.concatenate(blocks, axis=0)
    return (jnp.asarray(np.concatenate([cums, cums], axis=1), BF16), jnp.asarray(pair, F32),
            jnp.asarray(np.eye(CHUNK), F32))


def _hgrn_chunks(qs, ks, vs, lfs, sts, tabs, rev, need_o):
    cums_ref, pair_ref, eye_ref = tabs
    n_heads = len(sts)
    heads = range(len(ks))
    n_pairs = len(LEVELS) // 2
    n_cum = CHUNK * (1 + len(LEVELS)) if need_o else CHUNK
    cums = cums_ref[0:n_cum, :]

    his, los = [], []
    for h in heads:
        lf2 = lfs[h] * LOG2_E
        hi = lf2.astype(BF16)
        his.append(hi)
        los.append((lf2 - hi.astype(F32)).astype(BF16))
    both = _dot(cums, jnp.concatenate([jnp.concatenate(his, axis=1), jnp.concatenate(los, axis=1)], axis=0))
    bd = [both[:, h * LANES:(h + 1) * LANES] for h in heads]

    k16 = [ks[h].astype(BF16) for h in heads]
    v16 = [vs[h].astype(BF16) for h in heads]
    b = [bd[h][0:CHUNK] for h in heads]
    edge = [b[h][0:1, :] if rev else b[h][CHUNK - 1:CHUNK, :] for h in heads]
    kd = [k16[h] * jnp.exp2(edge[h] - b[h]).astype(BF16) for h in heads]
    grow = [lax.dot_general(v16[h], kd[h], (((0,), (0,)), ((), ())), preferred_element_type=F32)
            for h in heads]
    q16 = [qs[h].astype(BF16) for h in heads] if need_o else None
    st = list(sts)
    carried = []
    for h in heads:
        if need_o:
            carried.append(_dot_nt(q16[h] * jnp.exp2(b[h]).astype(BF16), st[h % n_heads].astype(BF16)))
        st[h % n_heads] = st[h % n_heads] * jnp.exp2(edge[h]) + grow[h]
    if not need_o:
        return None, st

    zero = jnp.zeros((CHUNK, LANES), BF16)
    att = [[] for _ in heads]
    for p in range(n_pairs):
        lhs, rhs = [], []
        for h in heads:
            qe, ke = [], []
            for i in (2 * p, 2 * p + 1):
                e = jnp.exp2(-jnp.abs(bd[h][CHUNK * (1 + i):CHUNK * (2 + i)])).astype(BF16)
                qe.append(q16[h] * e)
                ke.append(k16[h] * e)
            lhs.append(jnp.concatenate(qe, axis=1))
            rhs.append(jnp.concatenate([jnp.concatenate([ke[0], zero], axis=1),
                                        jnp.concatenate([zero, ke[1]], axis=1)], axis=0))
        scores = [_dot_nt(lhs[h], rhs[h]) for h in heads]
        for h in heads:
            att[h].append((scores[h] * pair_ref[p]).astype(BF16))
    same_row = [_dot_nt(q16[h], k16[h]) for h in heads]
    o = []
    for h in heads:
        att[h].append((same_row[h] * eye_ref[...]).astype(BF16))
        vals = jnp.concatenate([v16[h]] * (2 * n_pairs + 1), axis=0)
        o.append(carried[h] + _dot(jnp.concatenate(att[h], axis=1), vals))
    return o, st


def _hgrn_kernel(*refs, rev, need_o, final, want_state, n_chunks):
    it = iter(refs)
    tabs = tuple(next(it) for _ in range(3))
    q_ref = next(it) if need_o else None
    k_ref, v_ref, lf_ref, s0_ref = next(it), next(it), next(it), next(it)
    prev_ref = next(it) if final else None
    gate_ref = next(it) if final else None
    o_ref = next(it) if need_o else None
    sout_ref = next(it) if want_state else None
    st_ref = next(it)

    j = pl.program_id(1)
    nb = pl.num_programs(1)

    @pl.when(j == 0)
    def _():
        st_ref[...] = s0_ref[0]

    group = HGRN_GROUP if n_chunks % HGRN_GROUP == 0 else 1

    def body(gi, carry):
        rows = []
        for s in range(group):
            c = gi * group + s
            c = (n_chunks - 1 - c) if rev else c
            rows.append(pl.ds(pl.multiple_of(c * CHUNK, CHUNK), CHUNK))
        pairs = [(r, hh) for r in rows for hh in range(HG_HEADS)]
        o, st_new = _hgrn_chunks([q_ref[0, hh, r, :] for r, hh in pairs] if need_o else None,
                                 [k_ref[0, hh, r, :] for r, hh in pairs],
                                 [v_ref[0, hh, r, :] for r, hh in pairs],
                                 [lf_ref[0, hh, r, :] for r, hh in pairs],
                                 [st_ref[hh] for hh in range(HG_HEADS)], tabs, rev, need_o)
        for hh in range(HG_HEADS):
            st_ref[hh] = st_new[hh]
        if need_o:
            for (r, hh), out in zip(pairs, o):
                if final:
                    tot = out + prev_ref[0, hh, r, :]
                    out = tot * lax.rsqrt(jnp.mean(tot * tot, axis=-1, keepdims=True) + EPS)
                    out = out * gate_ref[0, hh, r, :]
                o_ref[0, hh, r, :] = out
        return carry

    lax.fori_loop(0, n_chunks // group, body, 0)

    if want_state:
        @pl.when(j == nb - 1)
        def _():
            sout_ref[0] = st_ref[...]


def _hgrn(q, k, v, lf, s0, prev=None, gate=None, *, rev, need_o, want_state, name):
    b, hh, t, _ = k.shape
    tb = min(512, t)
    nb = t // tb
    final = prev is not None
    tabs = _hgrn_tables(rev)

    seq = pl.BlockSpec((1, hh, tb, HEAD_DIM), lambda i, j: (i, 0, (nb - 1 - j) if rev else j, 0))
    state = pl.BlockSpec((1, hh, HEAD_DIM, HEAD_DIM), lambda i, j: (i, 0, 0, 0))
    args = list(tabs)
    in_specs = [pl.BlockSpec(a.shape, functools.partial(lambda nd, i, j: (0,) * nd, a.ndim)) for a in tabs]
    for a in ((q,) if need_o else ()) + (k, v, lf):
        args.append(a)
        in_specs.append(seq)
    args.append(s0)
    in_specs.append(state)
    if final:
        args += [prev, gate]
        in_specs += [seq, seq]
    out_shape, out_specs = [], []
    if need_o:
        out_shape.append(jax.ShapeDtypeStruct((b, hh, t, HEAD_DIM), F32))
        out_specs.append(seq)
    if want_state:
        out_shape.append(jax.ShapeDtypeStruct((b, hh, HEAD_DIM, HEAD_DIM), F32))
        out_specs.append(state)
    kern = functools.partial(_hgrn_kernel, rev=rev, need_o=need_o, final=final,
                             want_state=want_state, n_chunks=tb // CHUNK)
    est = 2 * (len(args) + len(out_shape)) * hh * tb * HEAD_DIM * 4
    return pl.pallas_call(
        kern,
        out_shape=out_shape,
        grid=(b, nb),
        in_specs=in_specs,
        out_specs=out_specs,
        scratch_shapes=[pltpu.VMEM((hh, HEAD_DIM, HEAD_DIM), F32)],
        compiler_params=pltpu.CompilerParams(
            dimension_semantics=("arbitrary", "arbitrary"),
            vmem_limit_bytes=_vmem_limit(est)),
        name=name,
    )(*args)


def _window_sum(x, w, stride, pos, extent):
    n = x.shape[0]

    def ahead(y, dist):
        return jnp.where(pos + dist < extent, pltpu.roll(y, (n - dist * stride) % n, 0), 0.0)

    def behind(y, dist):
        return jnp.where(pos >= dist, pltpu.roll(y, dist * stride, 0), 0.0)

    fwd, bwd, h = x, x, 1
    while h < w // 2:
        fwd = fwd + ahead(fwd, h)
        bwd = bwd + behind(bwd, h)
        h *= 2
    return fwd + behind(bwd, 1)


def _window_count(pos, w, extent):
    return (jnp.minimum(pos + w // 2, extent) - jnp.maximum(pos - w // 2, 0)).astype(F32)


def _pool_kernel(u_ref, pw_ref, ps_ref, o_ref):
    g = pl.program_id(1)
    n = u_ref.shape[2]
    tok = lax.broadcasted_iota(I32, (n, LANES), 0)
    col = tok & (GRID_W - 1)
    row = tok >> _log2(GRID_W)
    for gi, w in enumerate(POOL_WINDOWS):
        @pl.when(g == gi)
        def _(w=w):
            u = u_ref[0, 0]
            rows = n // GRID_W
            s = _window_sum(_window_sum(u, w, GRID_W, row, rows), w, 1, col, GRID_W)
            m = s / (_window_count(row, w, rows) * _window_count(col, w, GRID_W))
            o_ref[0, 0] = _dot((m - u).astype(BF16), pw_ref[0]) * ps_ref[0]


def _pool(u, pw, ps):
    b, g, t, c = u.shape
    blk = pl.BlockSpec((1, 1, t, c), lambda i, j: (i, j, 0, 0))
    return pl.pallas_call(
        _pool_kernel,
        out_shape=jax.ShapeDtypeStruct(u.shape, F32),
        grid=(b, g),
        in_specs=[blk,
                  pl.BlockSpec((1, c, c), lambda i, j: (j, 0, 0)),
                  pl.BlockSpec((1, 1, c), lambda i, j: (j, 0, 0))],
        out_specs=blk,
        compiler_params=pltpu.CompilerParams(
            dimension_semantics=("arbitrary", "arbitrary"),
            vmem_limit_bytes=_vmem_limit(12 * t * c * 4)),
        name="pool",
    )(u, pw, ps)


def _outproj_kernel(hg_ref, pool_ref, x_ref, w_ref, mod_ref, gpost_ref, gpre_ref, wr_ref, br_ref,
                    x1_ref, h2_ref, ti_ref, tw_ref, cnt_ref):
    d = x_ref.shape[-1]
    r = pl.program_id(0)

    @pl.when(jnp.logical_and(r == 0, pl.program_id(1) == 0))
    def _():
        cnt_ref[...] = jnp.zeros(cnt_ref.shape, F32)

    gt1 = mod_ref[pl.ds(r, 1), pl.ds(2 * d, d)]
    sh2 = mod_ref[pl.ds(r, 1), pl.ds(3 * d, d)]
    sc2 = mod_ref[pl.ds(r, 1), pl.ds(4 * d, d)]
    cat = jnp.concatenate([hg_ref[0, hh] for hh in range(HG_HEADS)]
                          + [pool_ref[0, g] for g in range(len(POOL_WINDOWS))], axis=1)
    y = _dot(cat.astype(BF16), w_ref[...])
    x1 = x_ref[0] + gt1 * _rms(y, gpost_ref[...])
    x1_ref[0] = x1
    h2 = _rms(x1, gpre_ref[...]) * (1.0 + sc2) + sh2
    _store_row_tiles(h2_ref.at[0], h2)

    lane = lax.broadcasted_iota(I32, (h2.shape[0], LANES), 1).astype(F32)
    logits = jnp.where(lane < N_EXPERTS, _dot(h2.astype(BF16), wr_ref[...]) + br_ref[...], -jnp.inf)
    ti = jnp.zeros(logits.shape, F32)
    tw = jnp.zeros(logits.shape, F32)
    chosen = jnp.zeros(logits.shape, F32)
    top = None
    for jj in range(TOP_K):
        m = jnp.max(logits, axis=1, keepdims=True)
        idx = jnp.min(jnp.where(logits == m, lane, float(LANES)), axis=1, keepdims=True)
        logits = jnp.where(lane == idx, -jnp.inf, logits)
        chosen = jnp.where(lane == idx, 1.0, chosen)
        top = m if top is None else top
        ti = jnp.where(lane == jj, idx, ti)
        tw = jnp.where(lane == jj, jnp.exp(m - top), tw)
    ti_ref[0] = ti.astype(I32)
    tw_ref[0] = tw / jnp.sum(tw, axis=1, keepdims=True)
    cnt_ref[...] += jnp.sum(chosen, axis=0, keepdims=True)


def _outproj(hg, pool, x, w_out, mod, gpost, gpre, wr, br):
    b, t, d = x.shape
    tm = min(512, t)
    head = pl.BlockSpec((1, HG_HEADS, tm, HEAD_DIM), lambda i, j: (i, 0, j, 0))
    tok = pl.BlockSpec((1, tm, d), lambda i, j: (i, j, 0))
    lanes = pl.BlockSpec((1, tm, LANES), lambda i, j: (i, j, 0))
    vec = pl.BlockSpec((1, d), lambda i, j: (0, 0))
    est = 2 * (2 * tm * HG_WIDTH * 4 + 3 * tm * d * 4 + 2 * tm * LANES * 4) + 2 * d * d * 2 + 6 * tm * d * 4
    return pl.pallas_call(
        _outproj_kernel,
        out_shape=[jax.ShapeDtypeStruct((b, t, d), F32),
                   jax.ShapeDtypeStruct((b, t * d // LANES, LANES), F32),
                   jax.ShapeDtypeStruct((b, t, LANES), I32), jax.ShapeDtypeStruct((b, t, LANES), F32),
                   jax.ShapeDtypeStruct((1, LANES), F32)],
        grid=(b, t // tm),
        in_specs=[head, head, tok,
                  pl.BlockSpec(w_out.shape, lambda i, j: (0, 0)),
                  pl.BlockSpec(mod.shape, lambda i, j: (0, 0)),
                  vec, vec,
                  pl.BlockSpec(wr.shape, lambda i, j: (0, 0)),
                  pl.BlockSpec((1, LANES), lambda i, j: (0, 0))],
        out_specs=[tok, pl.BlockSpec((1, tm * d // LANES, LANES), lambda i, j: (i, j, 0)),
                   lanes, lanes, pl.BlockSpec((1, LANES), lambda i, j: (0, 0))],
        compiler_params=pltpu.CompilerParams(
            dimension_semantics=("arbitrary", "arbitrary"),
            vmem_limit_bytes=_vmem_limit(est)),
        name="outproj",
    )(hg, pool, x, w_out, mod, gpost, gpre, wr, br)


def _rank_kernel(ti_ref, start_ref, dest_ref, carry_ref):
    i = pl.program_id(0)

    @pl.when(i == 0)
    def _():
        carry_ref[...] = start_ref[...]

    ti = ti_ref[...]
    tm = ti.shape[0]
    lane = lax.broadcasted_iota(I32, (tm, LANES), 1)
    onehot = [lane == ti[:, jj:jj + 1] for jj in range(TOP_K)]
    cnt = sum(jnp.where(oh, 1.0, 0.0) for oh in onehot)
    row = lax.broadcasted_iota(I32, (tm, tm), 0)
    col = lax.broadcasted_iota(I32, (tm, tm), 1)
    before = jnp.where(col < row, 1.0, 0.0).astype(BF16)
    prefix = _dot(before, cnt.astype(BF16)) + carry_ref[...]
    rank = jnp.zeros((tm, LANES), F32)
    for jj in range(TOP_K):
        rj = jnp.sum(jnp.where(onehot[jj], prefix, 0.0), axis=1, keepdims=True)
        rank = jnp.where(lane == jj, rj, rank)
    dest_ref[...] = rank.astype(I32) * SUBLANES
    carry_ref[...] += jnp.sum(cnt, axis=0, keepdims=True)


def _rank(ti, starts):
    t = ti.shape[0]
    tm = min(512, t)
    return pl.pallas_call(
        _rank_kernel,
        out_shape=jax.ShapeDtypeStruct((t, LANES), I32),
        grid=(t // tm,),
        in_specs=[pl.BlockSpec((tm, LANES), lambda i: (i, 0)),
                  pl.BlockSpec((1, LANES), lambda i: (0, 0))],
        out_specs=pl.BlockSpec((tm, LANES), lambda i: (i, 0)),
        scratch_shapes=[pltpu.VMEM((1, LANES), F32)],
        compiler_params=pltpu.CompilerParams(dimension_semantics=("arbitrary",)),
        name="rank",
    )(ti, starts)


def _tile_rows(start):
    return pl.ds(pl.multiple_of(start, SUBLANES), SUBLANES)


def _gap_copies(e, gap_start_ref, gap_len_ref, zeros, xs_ref, sem):
    off, n = gap_start_ref[e], gap_len_ref[e]
    out = []
    for bit in range(_log2(MOE_BM)):
        size = (1 << bit) * SUBLANES
        used = (n >> bit) & 1
        out.append((used == 1,
                    pltpu.make_async_copy(zeros.at[pl.ds(0, size), :],
                                          xs_ref.at[pl.ds(pl.multiple_of(off * SUBLANES, SUBLANES), size), :],
                                          sem)))
        off = off + used * (1 << bit)
    return out


def _dispatch_kernel(dest_ref, gap_start_ref, gap_len_ref, n_live_ref, h_ref, xs_ref, zeros, sem, zsem):
    tm = h_ref.shape[0] // SUBLANES
    block_rows = MOE_BM * SUBLANES
    n_blocks = xs_ref.shape[0] // block_rows

    @pl.when(pl.program_id(0) == 0)
    def _():
        zeros[...] = jnp.zeros(zeros.shape, zeros.dtype)

        def tail_copy(blk):
            return pltpu.make_async_copy(
                zeros, xs_ref.at[pl.ds(pl.multiple_of(blk * block_rows, block_rows), block_rows), :], zsem)

        def tail_start(blk, carry):
            tail_copy(blk).start()
            return carry

        def tail_wait(blk, carry):
            tail_copy(blk).wait()
            return carry

        lax.fori_loop(n_live_ref[0], n_blocks, tail_start, 0)
        lax.fori_loop(n_live_ref[0], n_blocks, tail_wait, 0)

        def fill(e, carry):
            for used, cp in _gap_copies(e, gap_start_ref, gap_len_ref, zeros, xs_ref, zsem):
                @pl.when(used)
                def _(cp=cp):
                    cp.start()
            return carry

        def drain(e, carry):
            for used, cp in _gap_copies(e, gap_start_ref, gap_len_ref, zeros, xs_ref, zsem):
                @pl.when(used)
                def _(cp=cp):
                    cp.wait()
            return carry

        lax.fori_loop(0, N_EXPERTS, fill, 0)
        lax.fori_loop(0, N_EXPERTS, drain, 0)

    def start(r, carry):
        for jj in range(TOP_K):
            pltpu.make_async_copy(h_ref.at[_tile_rows(r * SUBLANES), :],
                                  xs_ref.at[_tile_rows(dest_ref[r * TOP_K + jj]), :],
                                  sem).start(priority=jj % 2)
        return carry

    lax.fori_loop(0, tm, start, 0, unroll=4)
    for jj in range(TOP_K):
        pltpu.make_async_copy(h_ref, xs_ref.at[pl.ds(0, tm * SUBLANES), :], sem).wait()


def _dispatch(dest, gap_start, gap_len, n_live, h2, n_sorted):
    rows, l = h2.shape
    t = rows // SUBLANES
    tm = min(256, t)
    smem = pl.BlockSpec(memory_space=pltpu.SMEM)
    return pl.pallas_call(
        _dispatch_kernel,
        out_shape=jax.ShapeDtypeStruct((n_sorted * SUBLANES, l), h2.dtype),
        grid=(t // tm,),
        in_specs=[pl.BlockSpec((tm * TOP_K,), lambda i: (i,), memory_space=pltpu.SMEM), smem, smem, smem,
                  pl.BlockSpec((tm * SUBLANES, l), lambda i: (i, 0))],
        out_specs=pl.BlockSpec(memory_space=pl.ANY),
        scratch_shapes=[pltpu.VMEM((MOE_BM * SUBLANES, l), h2.dtype),
                        pltpu.SemaphoreType.DMA, pltpu.SemaphoreType.DMA],
        compiler_params=pltpu.CompilerParams(dimension_semantics=("arbitrary",)),
        name="dispatch",
    )(dest, gap_start, gap_len, n_live, h2)


def _moe_kernel(block_e_ref, n_live_ref,
                x_ref, wg_ref, wu_ref, wd_ref, bg_ref, bu_ref, bd_ref, y_ref,
                wgb, wub, wdb):
    i = pl.program_id(0)
    live = i < n_live_ref[0]
    changed = jnp.logical_or(i == 0, block_e_ref[i] != block_e_ref[jnp.maximum(i - 1, 0)])

    @pl.when(jnp.logical_and(live, changed))
    def _():
        wgb[...] = wg_ref[...].astype(BF16)
        wub[...] = wu_ref[...].astype(BF16)
        wdb[...] = wd_ref[...].astype(BF16)

    @pl.when(live)
    def _():
        x = _load_row_tiles(x_ref).astype(BF16)
        gate = jnp.minimum(_dot(x, wgb[...]) + bg_ref[...], SWIGLU_LIMIT)
        up = jnp.clip(_dot(x, wub[...]) + bu_ref[...], -SWIGLU_LIMIT, SWIGLU_LIMIT)
        hmid = (up + 1.0) * gate * _sigmoid(SWIGLU_ALPHA * gate)
        _store_row_tiles(y_ref, _dot(hmid.astype(BF16), wdb[...]) + bd_ref[...])

    @pl.when(jnp.logical_not(live))
    def _():
        y_ref[...] = jnp.zeros(y_ref.shape, y_ref.dtype)


def _moe(block_e, n_live, xs, wg, wu, wd, bg, bu, bd):
    d, f = wg.shape[-2:]
    n_blocks = block_e.shape[0]
    rows_in = pl.BlockSpec((MOE_BM * SUBLANES, LANES),
                           lambda i, be, nl: (jnp.minimum(i, nl[0] - 1), 0))
    rows_out = pl.BlockSpec((MOE_BM * SUBLANES, LANES), lambda i, be, nl: (i, 0))

    def wspec(k, n):
        return pl.BlockSpec((None, k, n), lambda i, be, nl: (be[i], 0, 0))

    def bspec(n):
        return pl.BlockSpec((None, 1, n), lambda i, be, nl: (be[i], 0, 0))

    est = 2 * 3 * d * f * 4 + 3 * d * f * 2 + 4 * MOE_BM * d * 4 + 4 * MOE_BM * f * 4
    return pl.pallas_call(
        _moe_kernel,
        out_shape=jax.ShapeDtypeStruct(xs.shape, F32),
        grid_spec=pltpu.PrefetchScalarGridSpec(
            num_scalar_prefetch=2,
            grid=(n_blocks,),
            in_specs=[rows_in, wspec(d, f), wspec(d, f), wspec(f, d), bspec(f), bspec(f), bspec(d)],
            out_specs=rows_out,
            scratch_shapes=[pltpu.VMEM((d, f), BF16), pltpu.VMEM((d, f), BF16),
                            pltpu.VMEM((f, d), BF16)]),
        compiler_params=pltpu.CompilerParams(
            dimension_semantics=("arbitrary",),
            vmem_limit_bytes=_vmem_limit(est)),
        name="moe",
    )(block_e, n_live, xs, wg, wu, wd, bg, bu, bd)


def _moe_layout(counts, n_rows):
    n_blocks = n_rows // MOE_BM + N_EXPERTS
    padded = (counts + MOE_BM - 1) // MOE_BM * MOE_BM
    ends = jnp.cumsum(padded)
    starts = ends - padded
    block_end = ends // MOE_BM
    i = jnp.arange(n_blocks, dtype=I32)
    last_e = jnp.max(jnp.where(counts > 0, jnp.arange(N_EXPERTS, dtype=I32), 0))
    block_e = jnp.minimum(jnp.sum(block_end[None, :] <= i[:, None], axis=1), last_e)
    return (starts.astype(I32), (starts + counts).astype(I32), (padded - counts).astype(I32),
            block_e.astype(I32), block_end[-1:].astype(I32), n_blocks)


def _combine_kernel(dest_ref, dnext_ref, ys_ref, tw_ref, x1_ref, mod_ref, gpost_ref, o_ref, buf, sem,
                    *, tiles_per_batch):
    tm, d = x1_ref.shape
    i = pl.program_id(0)
    slot = i % 2

    def issue(d_ref, s):
        def start(r, carry):
            for jj in range(TOP_K):
                pltpu.make_async_copy(ys_ref.at[_tile_rows(d_ref[r * TOP_K + jj]), :],
                                      buf.at[s, jj, _tile_rows(r * SUBLANES), :],
                                      sem.at[s]).start(priority=jj % 2)
            return carry

        lax.fori_loop(0, tm, start, 0, unroll=4)

    @pl.when(i == 0)
    def _():
        issue(dest_ref, 0)

    @pl.when(i + 1 < pl.num_programs(0))
    def _():
        issue(dnext_ref, 1 - slot)

    for jj in range(TOP_K):
        pltpu.make_async_copy(ys_ref.at[pl.ds(0, tm * SUBLANES), :], buf.at[slot, jj],
                              sem.at[slot]).wait()

    tw = tw_ref[...]
    y = tw[:, 0:1] * _load_row_tiles(buf.at[slot, 0])
    for jj in range(1, TOP_K):
        y = y + tw[:, jj:jj + 1] * _load_row_tiles(buf.at[slot, jj])
    r = i // tiles_per_batch
    gt2 = mod_ref[pl.ds(r, 1), pl.ds(5 * d, d)]
    o_ref[...] = x1_ref[...] + gt2 * _rms(y, gpost_ref[...])


def _combine(dest, ys, tw, x1, mod, gpost, tokens_per_batch):
    t, d = x1.shape
    tm = min(256, tokens_per_batch)
    n_tiles = t // tm
    kern = functools.partial(_combine_kernel, tiles_per_batch=tokens_per_batch // tm)
    est = 2 * TOP_K * tm * d * 4 + 2 * (2 * tm * d * 4 + tm * LANES * 4) + 3 * tm * d * 4
    return pl.pallas_call(
        kern,
        out_shape=jax.ShapeDtypeStruct((t, d), F32),
        grid=(n_tiles,),
        in_specs=[pl.BlockSpec((tm * TOP_K,), lambda i: (i,), memory_space=pltpu.SMEM),
                  pl.BlockSpec((tm * TOP_K,), lambda i: (jnp.minimum(i + 1, n_tiles - 1),),
                               memory_space=pltpu.SMEM),
                  pl.BlockSpec(memory_space=pl.ANY),
                  pl.BlockSpec((tm, LANES), lambda i: (i, 0)),
                  pl.BlockSpec((tm, d), lambda i: (i, 0)),
                  pl.BlockSpec(mod.shape, lambda i: (0, 0)),
                  pl.BlockSpec((1, d), lambda i: (0, 0))],
        out_specs=pl.BlockSpec((tm, d), lambda i: (i, 0)),
        scratch_shapes=[pltpu.VMEM((2, TOP_K, tm * SUBLANES, LANES), F32),
                        pltpu.SemaphoreType.DMA((2,))],
        compiler_params=pltpu.CompilerParams(
            dimension_semantics=("arbitrary",),
            vmem_limit_bytes=_vmem_limit(est)),
        name="combine",
    )(dest, dest, ys, tw, x1, mod, gpost)


def kernel(x, c, ctx, c_ctx, w_ada, b_ada, g_pre_mix, g_post_mix, g_pre_ffn, g_post_ffn,
           w_in, w_out, hgrn_lb_logits, hgrn_norm, pool_w, pool_scale,
           w_router, b_router, w_gate, b_gate, w_up, b_up, w_down, b_down):
    b, s, d = x.shape
    layer = 0
    n_e = w_router.shape[-1]

    cc = jnp.concatenate([c, c_ctx[None, :]], axis=0)
    cc = jnp.pad(cc, ((0, -(b + 1) % SUBLANES), (0, 0)))
    mod = _ada(cc, w_ada[layer], b_ada[layer][None, :])

    w_in_b = w_in[layer].astype(BF16)
    lbl = hgrn_lb_logits[:2].reshape(4, HG_WIDTH)
    gn = hgrn_norm[layer][None, :]
    gpre = g_pre_mix[layer][None, :]

    lff_c, kf_c, lfb_c, kb_c, v_c = _inproj(ctx, mod, gpre, w_in_b, lbl, gn, mod_row=b, full=False)
    zeros = jnp.zeros((b, HG_HEADS, HEAD_DIM, HEAD_DIM), F32)
    (s_f,) = _hgrn(None, kf_c, v_c, lff_c, zeros, rev=False, need_o=False, want_state=True,
                   name="hgrn_ctx_fwd")
    (s_b,) = _hgrn(None, kb_c, v_c, lfb_c, zeros, rev=True, need_o=False, want_state=True,
                   name="hgrn_ctx_bwd")

    q, lff, kf, lfb, kb, v, gate, u = _inproj(x, mod, gpre, w_in_b, lbl, gn, mod_row=None, full=True)
    (o_f,) = _hgrn(q, kf, v, lff, s_f, rev=False, need_o=True, want_state=False, name="hgrn_fwd")
    (o_hg,) = _hgrn(q, kb, v, lfb, s_b, o_f, gate, rev=True, need_o=True, want_state=False,
                    name="hgrn_bwd")
    o_pool = _pool(u, pool_w[layer].astype(BF16), pool_scale[layer].reshape(len(POOL_WINDOWS), 1, POOL_GROUP))

    wr = jnp.pad(w_router[layer], ((0, 0), (0, LANES - n_e))).astype(BF16)
    br = jnp.pad(b_router[layer], (0, LANES - n_e))[None, :]
    x1, h2, ti, tw, cnt = _outproj(o_hg, o_pool, x, w_out[layer].astype(BF16), mod,
                                   g_post_mix[layer][None, :], g_pre_ffn[layer][None, :], wr, br)

    t = b * s
    starts, gap_start, gap_len, block_e, n_live, n_blocks = _moe_layout(cnt[0, :n_e].astype(I32), t * TOP_K)
    starts = jnp.pad(starts.astype(F32), (0, LANES - n_e))[None, :]
    dest = _rank(ti.reshape(t, LANES), starts)[:, :TOP_K].reshape(t * TOP_K)

    xs = _dispatch(dest, gap_start, gap_len, n_live, h2.reshape(t * d // LANES, LANES), n_blocks * MOE_BM)
    ys = _moe(block_e, n_live, xs, w_gate[layer], w_up[layer], w_down[layer],
              b_gate[layer][:, None, :], b_up[layer][:, None, :], b_down[layer][:, None, :])
    out = _combine(dest, ys, tw.reshape(t, LANES), x1.reshape(t, d), mod,
                   g_post_ffn[layer][None, :], s)
    return out.reshape(b, s, d)
```

```python
import functools
import math

import numpy as np
import jax
import jax.numpy as jnp
from jax import lax
from jax.experimental import pallas as pl
from jax.experimental.pallas import tpu as pltpu

F32 = jnp.float32
BF16 = jnp.bfloat16
I32 = jnp.int32

GRID_W = 64
HG_HEADS = 4
HEAD_DIM = 128
HG_WIDTH = HG_HEADS * HEAD_DIM
POOL_WINDOWS = (2, 4, 8, 16)
POOL_GROUP = 128
N_EXPERTS = 32
TOP_K = 4
SWIGLU_LIMIT = 7.0
SWIGLU_ALPHA = 1.702
EPS = 1e-6

LANES = 128
SUBLANES = 8
V7X_VMEM_BYTES = 64 * 1024 * 1024

CHUNK = 64
HGRN_GROUP = 4
LEVELS = (32, 16, 8, 4, 2, 1)
FINE_LEVELS = (2, 1)
MOE_BM = 256
LOG2_E = math.log2(math.e)


def _vmem_limit(nbytes):
    return int(min(nbytes * 3 // 2 + (4 << 20), V7X_VMEM_BYTES - (6 << 20)))


def _log2(n):
    assert n & (n - 1) == 0
    return n.bit_length() - 1


def _sigmoid(x):
    return 1.0 / (1.0 + jnp.exp(-x))


def _rms(x, gain):
    return x * lax.rsqrt(jnp.mean(x * x, axis=-1, keepdims=True) + EPS) * gain


def _dot(a, b):
    return jnp.dot(a, b, preferred_element_type=F32)


ROW_TILE = SUBLANES * LANES


def _store_row_tiles(ref, val):
    rows = val.shape[0]
    for c in range(SUBLANES):
        ref[pl.ds(c, rows, stride=SUBLANES), :] = val[:, c * LANES:(c + 1) * LANES]


def _load_row_tiles(ref):
    rows = ref.shape[0] // SUBLANES
    return jnp.concatenate([ref[pl.ds(c, rows, stride=SUBLANES), :] for c in range(SUBLANES)], axis=1)


def _dot_nt(a, b):
    return lax.dot_general(a, b, (((1,), (1,)), ((), ())), preferred_element_type=F32)


def _ada_kernel(c_ref, w_ref, b_ref, o_ref):
    c = c_ref[...]
    s = (c * _sigmoid(c)).astype(BF16)
    o_ref[...] = _dot(s, w_ref[...].astype(BF16)) + b_ref[...]


def _ada(cc, w, b):
    rows, d = cc.shape
    n = w.shape[1]
    tn = 1536 if n % 1536 == 0 else n
    return pl.pallas_call(
        _ada_kernel,
        out_shape=jax.ShapeDtypeStruct((rows, n), F32),
        grid=(n // tn,),
        in_specs=[pl.BlockSpec((rows, d), lambda j: (0, 0)),
                  pl.BlockSpec((d, tn), lambda j: (0, j)),
                  pl.BlockSpec((1, tn), lambda j: (0, j))],
        out_specs=pl.BlockSpec((rows, tn), lambda j: (0, j)),
        compiler_params=pltpu.CompilerParams(
            dimension_semantics=("arbitrary",),
            vmem_limit_bytes=_vmem_limit(2 * d * tn * 4 + d * tn * 2)),
        name="ada",
    )(cc, w, b)


def _inproj_kernel(x_ref, mod_ref, gpre_ref, w_ref, lbl_ref, gn_ref, *outs, mod_row, full):
    d = x_ref.shape[-1]
    r = pl.program_id(0) if mod_row is None else mod_row
    sh = mod_ref[pl.ds(r, 1), pl.ds(0, d)]
    sc = mod_ref[pl.ds(r, 1), pl.ds(d, d)]
    h = (_rms(x_ref[0], gpre_ref[...]) * (1.0 + sc) + sh).astype(BF16)

    def proj(g):
        return _dot(h, w_ref[:, g * HG_WIDTH:(g + 1) * HG_WIDTH])

    def put(ref, val):
        for hh in range(HG_HEADS):
            ref[0, hh] = val[:, hh * HEAD_DIM:(hh + 1) * HEAD_DIM]

    def lower_bound(direction):
        l0 = lbl_ref[pl.ds(direction, 1), :]
        l1 = lbl_ref[pl.ds(2 + direction, 1), :]
        m = jnp.maximum(l0, l1)
        e0 = jnp.exp(l0 - m)
        return e0 / (e0 + jnp.exp(l1 - m))

    if full:
        q_o, lff_o, kf_o, lfb_o, kb_o, v_o, gate_o, u_o = outs
        q = proj(0)
        put(q_o, q * _sigmoid(q))
    else:
        lff_o, kf_o, lfb_o, kb_o, v_o = outs
    for direction, (lf_o, k_o) in enumerate(((lff_o, kf_o), (lfb_o, kb_o))):
        lb = lower_bound(direction)
        sg = _sigmoid(proj(1 + direction))
        put(lf_o, jnp.log(lb + (1.0 - lb) * sg))
        put(k_o, (1.0 - lb) * (1.0 - sg))
    put(v_o, proj(3))
    if full:
        put(gate_o, gn_ref[...] * _sigmoid(proj(4)))
        put(u_o, proj(5))


def _inproj(x, mod, gpre, w_in, lbl, gn, *, mod_row, full):
    b, t, d = x.shape
    tm = min(512, t)
    n_out = 8 if full else 5
    n_cols = w_in.shape[1]
    head = pl.BlockSpec((1, HG_HEADS, tm, HEAD_DIM), lambda i, j: (i, 0, j, 0))
    kern = functools.partial(_inproj_kernel, mod_row=mod_row, full=full)
    est = 2 * tm * d * 4 + 2 * d * n_cols * 2 + n_out * 2 * tm * HG_WIDTH * 4 + 4 * tm * HG_WIDTH * 4
    return pl.pallas_call(
        kern,
        out_shape=[jax.ShapeDtypeStruct((b, HG_HEADS, t, HEAD_DIM), F32)] * n_out,
        grid=(b, t // tm),
        in_specs=[pl.BlockSpec((1, tm, d), lambda i, j: (i, j, 0)),
                  pl.BlockSpec(mod.shape, lambda i, j: (0, 0)),
                  pl.BlockSpec((1, d), lambda i, j: (0, 0)),
                  pl.BlockSpec(w_in.shape, lambda i, j: (0, 0)),
                  pl.BlockSpec(lbl.shape, lambda i, j: (0, 0)),
                  pl.BlockSpec((1, HG_WIDTH), lambda i, j: (0, 0))],
        out_specs=[head] * n_out,
        compiler_params=pltpu.CompilerParams(
            dimension_semantics=("arbitrary", "arbitrary"),
            vmem_limit_bytes=_vmem_limit(est)),
        name="inproj_full" if full else "inproj_ctx",
    )(x, mod, gpre, w_in, lbl, gn)


def _hgrn_tables(rev):
    row = np.arange(CHUNK)[:, None]
    col = np.arange(CHUNK)[None, :]
    tri = ((col >= row) if rev else (col <= row)).astype(np.float32)
    blocks, masks = [tri], []
    for h in LEVELS:
        if h in FINE_LEVELS:
            blocks.append(tri - tri[_ref_row(np.arange(CHUNK), h, rev)])
        is_q = ((row // h) % 2) == (0 if rev else 1)
        key_half = ((col // h) % 2) == (1 if rev else 0)
        masks.append(((row // (2 * h)) == (col // (2 * h))) & is_q & key_half)
    pair = np.stack([np.concatenate(masks[i:i + 2], axis=1) for i in range(0, len(LEVELS), 2)])
    cums = np.concatenate(blocks, axis=0)
    return jnp.asarray(np.concatenate([cums, cums], axis=1), BF16), jnp.asarray(pair, F32)


def _ref_row(t, h, rev):
    return (t // (2 * h)) * (2 * h) + (h if rev else h - 1)


def _hgrn_chunks(qs, ks, vs, lfs, sts, tabs, rev, need_o):
    cums_ref, pair_ref = tabs
    n_heads = len(sts)
    heads = range(len(ks))
    n_pairs = len(LEVELS) // 2
    n_cum = CHUNK * (1 + len(FINE_LEVELS)) if need_o else CHUNK
    cums = cums_ref[0:n_cum, :]

    his, los = [], []
    for h in heads:
        lf2 = lfs[h] * LOG2_E
        hi = lf2.astype(BF16)
        his.append(hi)
        los.append((lf2 - hi.astype(F32)).astype(BF16))
    both = _dot(cums, jnp.concatenate([jnp.concatenate(his, axis=1), jnp.concatenate(los, axis=1)], axis=0))
    bd = [both[:, h * LANES:(h + 1) * LANES] for h in heads]

    k16 = [ks[h].astype(BF16) for h in heads]
    v16 = [vs[h].astype(BF16) for h in heads]
    b = [bd[h][0:CHUNK] for h in heads]

    def level_diff(h, level):
        if level in FINE_LEVELS:
            i = FINE_LEVELS.index(level)
            return bd[h][CHUNK * (1 + i):CHUNK * (2 + i)]
        refs = [_ref_row(m * 2 * level, level, rev) for m in range(CHUNK // (2 * level))]
        pieces = [jnp.broadcast_to(b[h][r:r + 1, :], (2 * level, LANES)) for r in refs]
        return b[h] - (pieces[0] if len(pieces) == 1 else jnp.concatenate(pieces, axis=0))
    edge = [b[h][0:1, :] if rev else b[h][CHUNK - 1:CHUNK, :] for h in heads]
    kd = [k16[h] * jnp.exp2(edge[h] - b[h]).astype(BF16) for h in heads]
    grow = [lax.dot_general(v16[h], kd[h], (((0,), (0,)), ((), ())), preferred_element_type=F32)
            for h in heads]
    q16 = [qs[h].astype(BF16) for h in heads] if need_o else None
    st = list(sts)
    carried = []
    for h in heads:
        if need_o:
            carried.append(_dot_nt(q16[h] * jnp.exp2(b[h]).astype(BF16), st[h % n_heads].astype(BF16)))
        st[h % n_heads] = st[h % n_heads] * jnp.exp2(edge[h]) + grow[h]
    if not need_o:
        return None, st

    zero = jnp.zeros((CHUNK, LANES), BF16)
    att = [[] for _ in heads]
    for p in range(n_pairs):
        lhs, rhs = [], []
        for h in heads:
            qe, ke = [], []
            for i in (2 * p, 2 * p + 1):
                e = jnp.exp2(-jnp.abs(level_diff(h, LEVELS[i]))).astype(BF16)
                qe.append(q16[h] * e)
                ke.append(k16[h] * e)
            lhs.append(jnp.concatenate(qe, axis=1))
            rhs.append(jnp.concatenate([jnp.concatenate([ke[0], zero], axis=1),
                                        jnp.concatenate([zero, ke[1]], axis=1)], axis=0))
        scores = [_dot_nt(lhs[h], rhs[h]) for h in heads]
        for h in heads:
            att[h].append((scores[h] * pair_ref[p]).astype(BF16))
    o = []
    for h in heads:
        vals = jnp.concatenate([v16[h]] * (2 * n_pairs), axis=0)
        same_row = jnp.sum(qs[h] * ks[h], axis=1, keepdims=True) * vs[h]
        o.append(carried[h] + _dot(jnp.concatenate(att[h], axis=1), vals) + same_row)
    return o, st


def _hgrn_kernel(*refs, rev, need_o, final, want_state, n_chunks):
    it = iter(refs)
    tabs = tuple(next(it) for _ in range(2))
    q_ref = next(it) if need_o else None
    k_ref, v_ref, lf_ref, s0_ref = next(it), next(it), next(it), next(it)
    prev_ref = next(it) if final else None
    gate_ref = next(it) if final else None
    o_ref = next(it) if need_o else None
    sout_ref = next(it) if want_state else None
    st_ref = next(it)

    j = pl.program_id(1)
    nb = pl.num_programs(1)

    @pl.when(j == 0)
    def _():
        st_ref[...] = s0_ref[0]

    group = HGRN_GROUP if n_chunks % HGRN_GROUP == 0 else 1

    def body(gi, carry):
        rows = []
        for s in range(group):
            c = gi * group + s
            c = (n_chunks - 1 - c) if rev else c
            rows.append(pl.ds(pl.multiple_of(c * CHUNK, CHUNK), CHUNK))
        pairs = [(r, hh) for r in rows for hh in range(HG_HEADS)]
        o, st_new = _hgrn_chunks([q_ref[0, hh, r, :] for r, hh in pairs] if need_o else None,
                                 [k_ref[0, hh, r, :] for r, hh in pairs],
                                 [v_ref[0, hh, r, :] for r, hh in pairs],
                                 [lf_ref[0, hh, r, :] for r, hh in pairs],
                                 [st_ref[hh] for hh in range(HG_HEADS)], tabs, rev, need_o)
        for hh in range(HG_HEADS):
            st_ref[hh] = st_new[hh]
        if need_o:
            for (r, hh), out in zip(pairs, o):
                if final:
                    tot = out + prev_ref[0, hh, r, :]
                    out = tot * lax.rsqrt(jnp.mean(tot * tot, axis=-1, keepdims=True) + EPS)
                    out = out * gate_ref[0, hh, r, :]
                o_ref[0, hh, r, :] = out
        return carry

    lax.fori_loop(0, n_chunks // group, body, 0)

    if want_state:
        @pl.when(j == nb - 1)
        def _():
            sout_ref[0] = st_ref[...]


def _hgrn(q, k, v, lf, s0, prev=None, gate=None, *, rev, need_o, want_state, name):
    b, hh, t, _ = k.shape
    tb = min(512, t)
    nb = t // tb
    final = prev is not None
    tabs = _hgrn_tables(rev)

    seq = pl.BlockSpec((1, hh, tb, HEAD_DIM), lambda i, j: (i, 0, (nb - 1 - j) if rev else j, 0))
    state = pl.BlockSpec((1, hh, HEAD_DIM, HEAD_DIM), lambda i, j: (i, 0, 0, 0))
    args = list(tabs)
    in_specs = [pl.BlockSpec(a.shape, functools.partial(lambda nd, i, j: (0,) * nd, a.ndim)) for a in tabs]
    for a in ((q,) if need_o else ()) + (k, v, lf):
        args.append(a)
        in_specs.append(seq)
    args.append(s0)
    in_specs.append(state)
    if final:
        args += [prev, gate]
        in_specs += [seq, seq]
    out_shape, out_specs = [], []
    if need_o:
        out_shape.append(jax.ShapeDtypeStruct((b, hh, t, HEAD_DIM), F32))
        out_specs.append(seq)
    if want_state:
        out_shape.append(jax.ShapeDtypeStruct((b, hh, HEAD_DIM, HEAD_DIM), F32))
        out_specs.append(state)
    kern = functools.partial(_hgrn_kernel, rev=rev, need_o=need_o, final=final,
                             want_state=want_state, n_chunks=tb // CHUNK)
    est = 2 * (len(args) + len(out_shape)) * hh * tb * HEAD_DIM * 4
    return pl.pallas_call(
        kern,
        out_shape=out_shape,
        grid=(b, nb),
        in_specs=in_specs,
        out_specs=out_specs,
        scratch_shapes=[pltpu.VMEM((hh, HEAD_DIM, HEAD_DIM), F32)],
        compiler_params=pltpu.CompilerParams(
            dimension_semantics=("arbitrary", "arbitrary"),
            vmem_limit_bytes=_vmem_limit(est)),
        name=name,
    )(*args)


def _window_sum(x, w, stride, pos, extent):
    n = x.shape[0]

    def ahead(y, dist):
        return jnp.where(pos + dist < extent, pltpu.roll(y, (n - dist * stride) % n, 0), 0.0)

    def behind(y, dist):
        return jnp.where(pos >= dist, pltpu.roll(y, dist * stride, 0), 0.0)

    fwd, bwd, h = x, x, 1
    while h < w // 2:
        fwd = fwd + ahead(fwd, h)
        bwd = bwd + behind(bwd, h)
        h *= 2
    return fwd + behind(bwd, 1)


def _window_count(pos, w, extent):
    return (jnp.minimum(pos + w // 2, extent) - jnp.maximum(pos - w // 2, 0)).astype(F32)


def _pool_kernel(u_ref, pw_ref, ps_ref, o_ref):
    g = pl.program_id(1)
    n = u_ref.shape[2]
    tok = lax.broadcasted_iota(I32, (n, LANES), 0)
    col = tok & (GRID_W - 1)
    row = tok >> _log2(GRID_W)
    for gi, w in enumerate(POOL_WINDOWS):
        @pl.when(g == gi)
        def _(w=w):
            u = u_ref[0, 0]
            rows = n // GRID_W
            s = _window_sum(_window_sum(u, w, GRID_W, row, rows), w, 1, col, GRID_W)
            m = s / (_window_count(row, w, rows) * _window_count(col, w, GRID_W))
            o_ref[0, 0] = _dot((m - u).astype(BF16), pw_ref[0]) * ps_ref[0]


def _pool(u, pw, ps):
    b, g, t, c = u.shape
    blk = pl.BlockSpec((1, 1, t, c), lambda i, j: (i, j, 0, 0))
    return pl.pallas_call(
        _pool_kernel,
        out_shape=jax.ShapeDtypeStruct(u.shape, F32),
        grid=(b, g),
        in_specs=[blk,
                  pl.BlockSpec((1, c, c), lambda i, j: (j, 0, 0)),
                  pl.BlockSpec((1, 1, c), lambda i, j: (j, 0, 0))],
        out_specs=blk,
        compiler_params=pltpu.CompilerParams(
            dimension_semantics=("arbitrary", "arbitrary"),
            vmem_limit_bytes=_vmem_limit(12 * t * c * 4)),
        name="pool",
    )(u, pw, ps)


def _outproj_kernel(hg_ref, pool_ref, x_ref, w_ref, mod_ref, gpost_ref, gpre_ref, wr_ref, br_ref,
                    x1_ref, h2_ref, ti_ref, tw_ref, cnt_ref):
    d = x_ref.shape[-1]
    r = pl.program_id(0)

    @pl.when(jnp.logical_and(r == 0, pl.program_id(1) == 0))
    def _():
        cnt_ref[...] = jnp.zeros(cnt_ref.shape, F32)

    gt1 = mod_ref[pl.ds(r, 1), pl.ds(2 * d, d)]
    sh2 = mod_ref[pl.ds(r, 1), pl.ds(3 * d, d)]
    sc2 = mod_ref[pl.ds(r, 1), pl.ds(4 * d, d)]
    cat = jnp.concatenate([hg_ref[0, hh] for hh in range(HG_HEADS)]
                          + [pool_ref[0, g] for g in range(len(POOL_WINDOWS))], axis=1)
    y = _dot(cat.astype(BF16), w_ref[...])
    x1 = x_ref[0] + gt1 * _rms(y, gpost_ref[...])
    x1_ref[0] = x1
    h2 = _rms(x1, gpre_ref[...]) * (1.0 + sc2) + sh2
    _store_row_tiles(h2_ref.at[0], h2)

    lane = lax.broadcasted_iota(I32, (h2.shape[0], LANES), 1).astype(F32)
    logits = jnp.where(lane < N_EXPERTS, _dot(h2.astype(BF16), wr_ref[...]) + br_ref[...], -jnp.inf)
    ti = jnp.zeros(logits.shape, F32)
    tw = jnp.zeros(logits.shape, F32)
    chosen = jnp.zeros(logits.shape, F32)
    top = None
    for jj in range(TOP_K):
        m = jnp.max(logits, axis=1, keepdims=True)
        idx = jnp.min(jnp.where(logits == m, lane, float(LANES)), axis=1, keepdims=True)
        logits = jnp.where(lane == idx, -jnp.inf, logits)
        chosen = jnp.where(lane == idx, 1.0, chosen)
        top = m if top is None else top
        ti = jnp.where(lane == jj, idx, ti)
        tw = jnp.where(lane == jj, jnp.exp(m - top), tw)
    ti_ref[0] = ti.astype(I32)
    tw_ref[0] = tw / jnp.sum(tw, axis=1, keepdims=True)
    cnt_ref[...] += jnp.sum(chosen, axis=0, keepdims=True)


def _outproj(hg, pool, x, w_out, mod, gpost, gpre, wr, br):
    b, t, d = x.shape
    tm = min(512, t)
    head = pl.BlockSpec((1, HG_HEADS, tm, HEAD_DIM), lambda i, j: (i, 0, j, 0))
    tok = pl.BlockSpec((1, tm, d), lambda i, j: (i, j, 0))
    lanes = pl.BlockSpec((1, tm, LANES), lambda i, j: (i, j, 0))
    vec = pl.BlockSpec((1, d), lambda i, j: (0, 0))
    est = 2 * (2 * tm * HG_WIDTH * 4 + 3 * tm * d * 4 + 2 * tm * LANES * 4) + 2 * d * d * 2 + 6 * tm * d * 4
    return pl.pallas_call(
        _outproj_kernel,
        out_shape=[jax.ShapeDtypeStruct((b, t, d), F32),
                   jax.ShapeDtypeStruct((b, t * d // LANES, LANES), F32),
                   jax.ShapeDtypeStruct((b, t, LANES), I32), jax.ShapeDtypeStruct((b, t, LANES), F32),
                   jax.ShapeDtypeStruct((1, LANES), F32)],
        grid=(b, t // tm),
        in_specs=[head, head, tok,
                  pl.BlockSpec(w_out.shape, lambda i, j: (0, 0)),
                  pl.BlockSpec(mod.shape, lambda i, j: (0, 0)),
                  vec, vec,
                  pl.BlockSpec(wr.shape, lambda i, j: (0, 0)),
                  pl.BlockSpec((1, LANES), lambda i, j: (0, 0))],
        out_specs=[tok, pl.BlockSpec((1, tm * d // LANES, LANES), lambda i, j: (i, j, 0)),
                   lanes, lanes, pl.BlockSpec((1, LANES), lambda i, j: (0, 0))],
        compiler_params=pltpu.CompilerParams(
            dimension_semantics=("arbitrary", "arbitrary"),
            vmem_limit_bytes=_vmem_limit(est)),
        name="outproj",
    )(hg, pool, x, w_out, mod, gpost, gpre, wr, br)


def _rank_kernel(ti_ref, start_ref, dest_ref, carry_ref):
    i = pl.program_id(0)

    @pl.when(i == 0)
    def _():
        carry_ref[...] = start_ref[...]

    ti = ti_ref[...]
    tm = ti.shape[0]
    lane = lax.broadcasted_iota(I32, (tm, LANES), 1)
    onehot = [lane == ti[:, jj:jj + 1] for jj in range(TOP_K)]
    cnt = sum(jnp.where(oh, 1.0, 0.0) for oh in onehot)
    row = lax.broadcasted_iota(I32, (tm, tm), 0)
    col = lax.broadcasted_iota(I32, (tm, tm), 1)
    before = jnp.where(col < row, 1.0, 0.0).astype(BF16)
    prefix = _dot(before, cnt.astype(BF16)) + carry_ref[...]
    rank = jnp.zeros((tm, LANES), F32)
    for jj in range(TOP_K):
        rj = jnp.sum(jnp.where(onehot[jj], prefix, 0.0), axis=1, keepdims=True)
        rank = jnp.where(lane == jj, rj, rank)
    dest_ref[...] = rank.astype(I32) * SUBLANES
    carry_ref[...] += jnp.sum(cnt, axis=0, keepdims=True)


def _rank(ti, starts):
    t = ti.shape[0]
    tm = min(512, t)
    return pl.pallas_call(
        _rank_kernel,
        out_shape=jax.ShapeDtypeStruct((t, LANES), I32),
        grid=(t // tm,),
        in_specs=[pl.BlockSpec((tm, LANES), lambda i: (i, 0)),
                  pl.BlockSpec((1, LANES), lambda i: (0, 0))],
        out_specs=pl.BlockSpec((tm, LANES), lambda i: (i, 0)),
        scratch_shapes=[pltpu.VMEM((1, LANES), F32)],
        compiler_params=pltpu.CompilerParams(dimension_semantics=("arbitrary",)),
        name="rank",
    )(ti, starts)


def _tile_rows(start):
    return pl.ds(pl.multiple_of(start, SUBLANES), SUBLANES)


def _gap_copies(e, gap_start_ref, gap_len_ref, zeros, xs_ref, sem):
    off, n = gap_start_ref[e], gap_len_ref[e]
    out = []
    for bit in range(_log2(MOE_BM)):
        size = (1 << bit) * SUBLANES
        used = (n >> bit) & 1
        out.append((used == 1,
                    pltpu.make_async_copy(zeros.at[pl.ds(0, size), :],
                                          xs_ref.at[pl.ds(pl.multiple_of(off * SUBLANES, SUBLANES), size), :],
                                          sem)))
        off = off + used * (1 << bit)
    return out


def _dispatch_kernel(dest_ref, gap_start_ref, gap_len_ref, n_live_ref, h_ref, xs_ref, zeros, sem, zsem):
    tm = h_ref.shape[0] // SUBLANES
    block_rows = MOE_BM * SUBLANES
    n_blocks = xs_ref.shape[0] // block_rows

    @pl.when(pl.program_id(0) == 0)
    def _():
        zeros[...] = jnp.zeros(zeros.shape, zeros.dtype)

        def tail_copy(blk):
            return pltpu.make_async_copy(
                zeros, xs_ref.at[pl.ds(pl.multiple_of(blk * block_rows, block_rows), block_rows), :], zsem)

        def tail_start(blk, carry):
            tail_copy(blk).start()
            return carry

        def tail_wait(blk, carry):
            tail_copy(blk).wait()
            return carry

        lax.fori_loop(n_live_ref[0], n_blocks, tail_start, 0)
        lax.fori_loop(n_live_ref[0], n_blocks, tail_wait, 0)

        def fill(e, carry):
            for used, cp in _gap_copies(e, gap_start_ref, gap_len_ref, zeros, xs_ref, zsem):
                @pl.when(used)
                def _(cp=cp):
                    cp.start()
            return carry

        def drain(e, carry):
            for used, cp in _gap_copies(e, gap_start_ref, gap_len_ref, zeros, xs_ref, zsem):
                @pl.when(used)
                def _(cp=cp):
                    cp.wait()
            return carry

        lax.fori_loop(0, N_EXPERTS, fill, 0)
        lax.fori_loop(0, N_EXPERTS, drain, 0)

    def start(r, carry):
        for jj in range(TOP_K):
            pltpu.make_async_copy(h_ref.at[_tile_rows(r * SUBLANES), :],
                                  xs_ref.at[_tile_rows(dest_ref[r * TOP_K + jj]), :],
                                  sem).start(priority=jj % 2)
        return carry

    lax.fori_loop(0, tm, start, 0, unroll=4)
    for jj in range(TOP_K):
        pltpu.make_async_copy(h_ref, xs_ref.at[pl.ds(0, tm * SUBLANES), :], sem).wait()


def _dispatch(dest, gap_start, gap_len, n_live, h2, n_sorted):
    rows, l = h2.shape
    t = rows // SUBLANES
    tm = min(256, t)
    smem = pl.BlockSpec(memory_space=pltpu.SMEM)
    return pl.pallas_call(
        _dispatch_kernel,
        out_shape=jax.ShapeDtypeStruct((n_sorted * SUBLANES, l), h2.dtype),
        grid=(t // tm,),
        in_specs=[pl.BlockSpec((tm * TOP_K,), lambda i: (i,), memory_space=pltpu.SMEM), smem, smem, smem,
                  pl.BlockSpec((tm * SUBLANES, l), lambda i: (i, 0))],
        out_specs=pl.BlockSpec(memory_space=pl.ANY),
        scratch_shapes=[pltpu.VMEM((MOE_BM * SUBLANES, l), h2.dtype),
                        pltpu.SemaphoreType.DMA, pltpu.SemaphoreType.DMA],
        compiler_params=pltpu.CompilerParams(dimension_semantics=("arbitrary",)),
        name="dispatch",
    )(dest, gap_start, gap_len, n_live, h2)


def _moe_kernel(block_e_ref, next_e_ref, n_live_ref,
                x_ref, wg_ref, wu_ref, wd_ref, bg_ref, bu_ref, bd_ref, y_ref,
                wgs, wus, wds, wgb, wub, wdb, sem):
    i = pl.program_id(0)
    live = i < n_live_ref[0]
    e = block_e_ref[i]
    changed = jnp.logical_or(i == 0, e != block_e_ref[jnp.maximum(i - 1, 0)])

    def fetch(expert):
        return [pltpu.make_async_copy(w_ref.at[expert], stage, sem.at[n])
                for n, (w_ref, stage) in enumerate(((wg_ref, wgs), (wu_ref, wus), (wd_ref, wds)))]

    @pl.when(i == 0)
    def _():
        for cp in fetch(e):
            cp.start()

    @pl.when(jnp.logical_and(live, changed))
    def _():
        for cp in fetch(e):
            cp.wait()
        def cast(c, carry):
            rows = pl.ds(pl.multiple_of(c * LANES, LANES), LANES)
            for src, dst in ((wgs, wgb), (wus, wub), (wds, wdb)):
                dst[rows, :] = src[rows, :].astype(BF16)
            return carry

        lax.fori_loop(0, wgs.shape[0] // LANES, cast, 0)

        @pl.when(next_e_ref[i] != e)
        def _():
            for cp in fetch(next_e_ref[i]):
                cp.start()

    @pl.when(live)
    def _():
        x = _load_row_tiles(x_ref).astype(BF16)
        gate = jnp.minimum(_dot(x, wgb[...]) + bg_ref[...], SWIGLU_LIMIT)
        up = jnp.clip(_dot(x, wub[...]) + bu_ref[...], -SWIGLU_LIMIT, SWIGLU_LIMIT)
        hmid = (up + 1.0) * gate * _sigmoid(SWIGLU_ALPHA * gate)
        _store_row_tiles(y_ref, _dot(hmid.astype(BF16), wdb[...]) + bd_ref[...])

    @pl.when(jnp.logical_not(live))
    def _():
        y_ref[...] = jnp.zeros(y_ref.shape, y_ref.dtype)


def _moe(block_e, next_e, n_live, xs, wg, wu, wd, bg, bu, bd):
    d, f = wg.shape[-2:]
    n_blocks = block_e.shape[0]
    rows_in = pl.BlockSpec((MOE_BM * SUBLANES, LANES),
                           lambda i, be, ne, nl: (jnp.minimum(i, nl[0] - 1), 0))
    rows_out = pl.BlockSpec((MOE_BM * SUBLANES, LANES), lambda i, be, ne, nl: (i, 0))
    hbm = pl.BlockSpec(memory_space=pl.ANY)

    def bspec(n):
        return pl.BlockSpec((None, 1, n), lambda i, be, ne, nl: (be[i], 0, 0))

    est = 3 * d * f * 4 + 3 * d * f * 2 + 4 * MOE_BM * d * 4 + 4 * MOE_BM * f * 4
    return pl.pallas_call(
        _moe_kernel,
        out_shape=jax.ShapeDtypeStruct(xs.shape, F32),
        grid_spec=pltpu.PrefetchScalarGridSpec(
            num_scalar_prefetch=3,
            grid=(n_blocks,),
            in_specs=[rows_in, hbm, hbm, hbm, bspec(f), bspec(f), bspec(d)],
            out_specs=rows_out,
            scratch_shapes=[pltpu.VMEM((d, f), F32), pltpu.VMEM((d, f), F32), pltpu.VMEM((f, d), F32),
                            pltpu.VMEM((d, f), BF16), pltpu.VMEM((d, f), BF16), pltpu.VMEM((f, d), BF16),
                            pltpu.SemaphoreType.DMA((3,))]),
        compiler_params=pltpu.CompilerParams(
            dimension_semantics=("arbitrary",),
            vmem_limit_bytes=_vmem_limit(est)),
        name="moe",
    )(block_e, next_e, n_live, xs, wg, wu, wd, bg, bu, bd)


def _moe_layout(counts, n_rows):
    n_blocks = n_rows // MOE_BM + N_EXPERTS
    padded = (counts + MOE_BM - 1) // MOE_BM * MOE_BM
    ends = jnp.cumsum(padded)
    starts = ends - padded
    block_end = ends // MOE_BM
    i = jnp.arange(n_blocks, dtype=I32)
    ids = jnp.arange(N_EXPERTS, dtype=I32)
    last_e = jnp.max(jnp.where(counts > 0, ids, 0))
    block_e = jnp.minimum(jnp.sum(block_end[None, :] <= i[:, None], axis=1), last_e).astype(I32)
    later = jnp.logical_and(ids[None, :] > ids[:, None], counts[None, :] > 0)
    following = jnp.min(jnp.where(later, ids[None, :], N_EXPERTS), axis=1)
    following = jnp.where(following == N_EXPERTS, ids, following)
    next_e = jnp.sum(jnp.where(block_e[:, None] == ids[None, :], following[None, :], 0), axis=1)
    return (starts.astype(I32), (starts + counts).astype(I32), (padded - counts).astype(I32),
            block_e, next_e.astype(I32), block_end[-1:].astype(I32), n_blocks)


def _combine_kernel(dest_ref, dnext_ref, ys_ref, tw_ref, x1_ref, mod_ref, gpost_ref, o_ref, buf, sem,
                    *, tiles_per_batch):
    tm, d = x1_ref.shape
    i = pl.program_id(0)
    slot = i % 2

    def issue(d_ref, s):
        def start(r, carry):
            for jj in range(TOP_K):
                pltpu.make_async_copy(ys_ref.at[_tile_rows(d_ref[r * TOP_K + jj]), :],
                                      buf.at[s, jj, _tile_rows(r * SUBLANES), :],
                                      sem.at[s]).start(priority=jj % 2)
            return carry

        lax.fori_loop(0, tm, start, 0, unroll=4)

    @pl.when(i == 0)
    def _():
        issue(dest_ref, 0)

    @pl.when(i + 1 < pl.num_programs(0))
    def _():
        issue(dnext_ref, 1 - slot)

    for jj in range(TOP_K):
        pltpu.make_async_copy(ys_ref.at[pl.ds(0, tm * SUBLANES), :], buf.at[slot, jj],
                              sem.at[slot]).wait()

    tw = tw_ref[...]
    y = tw[:, 0:1] * _load_row_tiles(buf.at[slot, 0])
    for jj in range(1, TOP_K):
        y = y + tw[:, jj:jj + 1] * _load_row_tiles(buf.at[slot, jj])
    r = i // tiles_per_batch
    gt2 = mod_ref[pl.ds(r, 1), pl.ds(5 * d, d)]
    o_ref[...] = x1_ref[...] + gt2 * _rms(y, gpost_ref[...])


def _combine(dest, ys, tw, x1, mod, gpost, tokens_per_batch):
    t, d = x1.shape
    tm = min(256, tokens_per_batch)
    n_tiles = t // tm
    kern = functools.partial(_combine_kernel, tiles_per_batch=tokens_per_batch // tm)
    est = 2 * TOP_K * tm * d * 4 + 2 * (2 * tm * d * 4 + tm * LANES * 4) + 3 * tm * d * 4
    return pl.pallas_call(
        kern,
        out_shape=jax.ShapeDtypeStruct((t, d), F32),
        grid=(n_tiles,),
        in_specs=[pl.BlockSpec((tm * TOP_K,), lambda i: (i,), memory_space=pltpu.SMEM),
                  pl.BlockSpec((tm * TOP_K,), lambda i: (jnp.minimum(i + 1, n_tiles - 1),),
                               memory_space=pltpu.SMEM),
                  pl.BlockSpec(memory_space=pl.ANY),
                  pl.BlockSpec((tm, LANES), lambda i: (i, 0)),
                  pl.BlockSpec((tm, d), lambda i: (i, 0)),
                  pl.BlockSpec(mod.shape, lambda i: (0, 0)),
                  pl.BlockSpec((1, d), lambda i: (0, 0))],
        out_specs=pl.BlockSpec((tm, d), lambda i: (i, 0)),
        scratch_shapes=[pltpu.VMEM((2, TOP_K, tm * SUBLANES, LANES), F32),
                        pltpu.SemaphoreType.DMA((2,))],
        compiler_params=pltpu.CompilerParams(
            dimension_semantics=("arbitrary",),
            vmem_limit_bytes=_vmem_limit(est)),
        name="combine",
    )(dest, dest, ys, tw, x1, mod, gpost)


def kernel(x, c, ctx, c_ctx, w_ada, b_ada, g_pre_mix, g_post_mix, g_pre_ffn, g_post_ffn,
           w_in, w_out, hgrn_lb_logits, hgrn_norm, pool_w, pool_scale,
           w_router, b_router, w_gate, b_gate, w_up, b_up, w_down, b_down):
    b, s, d = x.shape
    layer = 0
    n_e = w_router.shape[-1]

    cc = jnp.concatenate([c, c_ctx[None, :]], axis=0)
    cc = jnp.pad(cc, ((0, -(b + 1) % SUBLANES), (0, 0)))
    mod = _ada(cc, w_ada[layer], b_ada[layer][None, :])

    w_in_b = w_in[layer].astype(BF16)
    lbl = hgrn_lb_logits[:2].reshape(4, HG_WIDTH)
    gn = hgrn_norm[layer][None, :]
    gpre = g_pre_mix[layer][None, :]

    lff_c, kf_c, lfb_c, kb_c, v_c = _inproj(ctx, mod, gpre, w_in_b, lbl, gn, mod_row=b, full=False)
    zeros = jnp.zeros((b, HG_HEADS, HEAD_DIM, HEAD_DIM), F32)
    (s_f,) = _hgrn(None, kf_c, v_c, lff_c, zeros, rev=False, need_o=False, want_state=True,
                   name="hgrn_ctx_fwd")
    (s_b,) = _hgrn(None, kb_c, v_c, lfb_c, zeros, rev=True, need_o=False, want_state=True,
                   name="hgrn_ctx_bwd")

    q, lff, kf, lfb, kb, v, gate, u = _inproj(x, mod, gpre, w_in_b, lbl, gn, mod_row=None, full=True)
    (o_f,) = _hgrn(q, kf, v, lff, s_f, rev=False, need_o=True, want_state=False, name="hgrn_fwd")
    (o_hg,) = _hgrn(q, kb, v, lfb, s_b, o_f, gate, rev=True, need_o=True, want_state=False,
                    name="hgrn_bwd")
    o_pool = _pool(u, pool_w[layer].astype(BF16), pool_scale[layer].reshape(len(POOL_WINDOWS), 1, POOL_GROUP))

    wr = jnp.pad(w_router[layer], ((0, 0), (0, LANES - n_e))).astype(BF16)
    br = jnp.pad(b_router[layer], (0, LANES - n_e))[None, :]
    x1, h2, ti, tw, cnt = _outproj(o_hg, o_pool, x, w_out[layer].astype(BF16), mod,
                                   g_post_mix[layer][None, :], g_pre_ffn[layer][None, :], wr, br)

    t = b * s
    starts, gap_start, gap_len, block_e, next_e, n_live, n_blocks = _moe_layout(
        cnt[0, :n_e].astype(I32), t * TOP_K)
    starts = jnp.pad(starts.astype(F32), (0, LANES - n_e))[None, :]
    dest = _rank(ti.reshape(t, LANES), starts)[:, :TOP_K].reshape(t * TOP_K)

    xs = _dispatch(dest, gap_start, gap_len, n_live, h2.reshape(t * d // LANES, LANES), n_blocks * MOE_BM)
    ys = _moe(block_e, next_e, n_live, xs, w_gate[layer], w_up[layer], w_down[layer],
              b_gate[layer][:, None, :], b_up[layer][:, None, :], b_down[layer][:, None, :])
    out = _combine(dest, ys, tw.reshape(t, LANES), x1.reshape(t, d), mod,
                   g_post_ffn[layer][None, :], s)
    return out.reshape(b, s, d)
```

```python
import functools
import math

import numpy as np
import jax
import jax.numpy as jnp
from jax import lax
from jax.experimental import pallas as pl
from jax.experimental.pallas import tpu as pltpu

F32 = jnp.float32
BF16 = jnp.bfloat16
I32 = jnp.int32

GRID_W = 64
HG_HEADS = 4
HEAD_DIM = 128
HG_WIDTH = HG_HEADS * HEAD_DIM
POOL_WINDOWS = (2, 4, 8, 16)
POOL_GROUP = 128
N_EXPERTS = 32
TOP_K = 4
SWIGLU_LIMIT = 7.0
SWIGLU_ALPHA = 1.702
EPS = 1e-6

LANES = 128
SUBLANES = 8
V7X_VMEM_BYTES = 64 * 1024 * 1024

CHUNK = 64
HGRN_GROUP = 4
LEVELS = (32, 16, 8, 4, 2, 1)
FINE_LEVELS = (2, 1)
MOE_BM = 256
OUTPROJ_SLICES = 4
LOG2_E = math.log2(math.e)


def _vmem_limit(nbytes):
    return int(min(nbytes * 3 // 2 + (4 << 20), V7X_VMEM_BYTES - (6 << 20)))


def _log2(n):
    assert n & (n - 1) == 0
    return n.bit_length() - 1


def _sigmoid(x):
    return 1.0 / (1.0 + jnp.exp(-x))


def _rms(x, gain):
    return x * lax.rsqrt(jnp.mean(x * x, axis=-1, keepdims=True) + EPS) * gain


def _dot(a, b):
    return jnp.dot(a, b, preferred_element_type=F32)


ROW_TILE = SUBLANES * LANES


def _store_row_tiles(ref, val):
    rows = val.shape[0]
    for c in range(SUBLANES):
        ref[pl.ds(c, rows, stride=SUBLANES), :] = val[:, c * LANES:(c + 1) * LANES]


def _load_row_tiles(ref):
    rows = ref.shape[0] // SUBLANES
    return jnp.concatenate([ref[pl.ds(c, rows, stride=SUBLANES), :] for c in range(SUBLANES)], axis=1)


def _dot_nt(a, b):
    return lax.dot_general(a, b, (((1,), (1,)), ((), ())), preferred_element_type=F32)


def _ada_kernel(c_ref, w_ref, b_ref, o_ref):
    c = c_ref[...]
    s = (c * _sigmoid(c)).astype(BF16)
    o_ref[...] = _dot(s, w_ref[...].astype(BF16)) + b_ref[...]


def _ada(cc, w, b):
    rows, d = cc.shape
    n = w.shape[1]
    tn = 1536 if n % 1536 == 0 else n
    return pl.pallas_call(
        _ada_kernel,
        out_shape=jax.ShapeDtypeStruct((rows, n), F32),
        grid=(n // tn,),
        in_specs=[pl.BlockSpec((rows, d), lambda j: (0, 0)),
                  pl.BlockSpec((d, tn), lambda j: (0, j)),
                  pl.BlockSpec((1, tn), lambda j: (0, j))],
        out_specs=pl.BlockSpec((rows, tn), lambda j: (0, j)),
        compiler_params=pltpu.CompilerParams(
            dimension_semantics=("arbitrary",),
            vmem_limit_bytes=_vmem_limit(2 * d * tn * 4 + d * tn * 2)),
        name="ada",
    )(cc, w, b)


def _inproj_kernel(x_ref, mod_ref, gpre_ref, w_ref, lbl_ref, gn_ref, *outs, mod_row, full):
    d = x_ref.shape[-1]
    r = pl.program_id(0) if mod_row is None else mod_row
    sh = mod_ref[pl.ds(r, 1), pl.ds(0, d)]
    sc = mod_ref[pl.ds(r, 1), pl.ds(d, d)]
    tm = x_ref.shape[1]
    n_slices = 2 if tm % (2 * SUBLANES) == 0 else 1
    sub = tm // n_slices
    slices = [pl.ds(i * sub, sub) for i in range(n_slices)]
    hs = [(_rms(x_ref[0, rows, :], gpre_ref[...]) * (1.0 + sc) + sh).astype(BF16) for rows in slices]

    def proj(g):
        return [_dot(h, w_ref[:, g * HG_WIDTH:(g + 1) * HG_WIDTH]) for h in hs]

    def put(ref, vals):
        for rows, val in zip(slices, vals):
            for hh in range(HG_HEADS):
                ref[0, hh, rows, :] = val[:, hh * HEAD_DIM:(hh + 1) * HEAD_DIM]

    def lower_bound(direction):
        l0 = lbl_ref[pl.ds(direction, 1), :]
        l1 = lbl_ref[pl.ds(2 + direction, 1), :]
        m = jnp.maximum(l0, l1)
        e0 = jnp.exp(l0 - m)
        return e0 / (e0 + jnp.exp(l1 - m))

    if full:
        q_o, lff_o, kf_o, lfb_o, kb_o, v_o, gate_o, u_o = outs
        put(q_o, [q * _sigmoid(q) for q in proj(0)])
    else:
        lff_o, kf_o, lfb_o, kb_o, v_o = outs
    for direction, (lf_o, k_o) in enumerate(((lff_o, kf_o), (lfb_o, kb_o))):
        lb = lower_bound(direction)
        sgs = [_sigmoid(z) for z in proj(1 + direction)]
        put(lf_o, [jnp.log(lb + (1.0 - lb) * sg) for sg in sgs])
        put(k_o, [(1.0 - lb) * (1.0 - sg) for sg in sgs])
    put(v_o, proj(3))
    if full:
        put(gate_o, [gn_ref[...] * _sigmoid(z) for z in proj(4)])
        put(u_o, proj(5))


def _inproj(x, mod, gpre, w_in, lbl, gn, *, mod_row, full):
    b, t, d = x.shape
    tm = min(512, t)
    n_out = 8 if full else 5
    n_cols = w_in.shape[1]
    head = pl.BlockSpec((1, HG_HEADS, tm, HEAD_DIM), lambda i, j: (i, 0, j, 0))
    kern = functools.partial(_inproj_kernel, mod_row=mod_row, full=full)
    est = 2 * tm * d * 4 + 2 * d * n_cols * 2 + n_out * 2 * tm * HG_WIDTH * 4 + 4 * tm * HG_WIDTH * 4
    return pl.pallas_call(
        kern,
        out_shape=[jax.ShapeDtypeStruct((b, HG_HEADS, t, HEAD_DIM), F32)] * n_out,
        grid=(b, t // tm),
        in_specs=[pl.BlockSpec((1, tm, d), lambda i, j: (i, j, 0)),
                  pl.BlockSpec(mod.shape, lambda i, j: (0, 0)),
                  pl.BlockSpec((1, d), lambda i, j: (0, 0)),
                  pl.BlockSpec(w_in.shape, lambda i, j: (0, 0)),
                  pl.BlockSpec(lbl.shape, lambda i, j: (0, 0)),
                  pl.BlockSpec((1, HG_WIDTH), lambda i, j: (0, 0))],
        out_specs=[head] * n_out,
        compiler_params=pltpu.CompilerParams(
            dimension_semantics=("arbitrary", "arbitrary"),
            vmem_limit_bytes=_vmem_limit(est)),
        name="inproj_full" if full else "inproj_ctx",
    )(x, mod, gpre, w_in, lbl, gn)


def _hgrn_tables(rev):
    row = np.arange(CHUNK)[:, None]
    col = np.arange(CHUNK)[None, :]
    tri = ((col >= row) if rev else (col <= row)).astype(np.float32)
    blocks, masks = [tri], []
    for h in LEVELS:
        if h in FINE_LEVELS:
            blocks.append(tri - tri[_ref_row(np.arange(CHUNK), h, rev)])
        is_q = ((row // h) % 2) == (0 if rev else 1)
        key_half = ((col // h) % 2) == (1 if rev else 0)
        masks.append(((row // (2 * h)) == (col // (2 * h))) & is_q & key_half)
    pair = np.stack([np.concatenate(masks[i:i + 2], axis=1) for i in range(0, len(LEVELS), 2)])
    cums = np.concatenate(blocks, axis=0)
    return jnp.asarray(np.concatenate([cums, cums], axis=1), BF16), jnp.asarray(pair, F32)


def _ref_row(t, h, rev):
    return (t // (2 * h)) * (2 * h) + (h if rev else h - 1)


def _hgrn_chunks(qs, ks, vs, lfs, sts, tabs, rev, need_o):
    cums_ref, pair_ref = tabs
    n_heads = len(sts)
    heads = range(len(ks))
    n_pairs = len(LEVELS) // 2
    n_cum = CHUNK * (1 + len(FINE_LEVELS)) if need_o else CHUNK
    cums = cums_ref[0:n_cum, :]

    his, los = [], []
    for h in heads:
        lf2 = lfs[h] * LOG2_E
        hi = lf2.astype(BF16)
        his.append(hi)
        los.append((lf2 - hi.astype(F32)).astype(BF16))
    both = _dot(cums, jnp.concatenate([jnp.concatenate(his, axis=1), jnp.concatenate(los, axis=1)], axis=0))
    bd = [both[:, h * LANES:(h + 1) * LANES] for h in heads]

    k16 = [ks[h].astype(BF16) for h in heads]
    v16 = [vs[h].astype(BF16) for h in heads]
    b = [bd[h][0:CHUNK] for h in heads]

    def level_diff(h, level):
        if level in FINE_LEVELS:
            i = FINE_LEVELS.index(level)
            return bd[h][CHUNK * (1 + i):CHUNK * (2 + i)]
        refs = [_ref_row(m * 2 * level, level, rev) for m in range(CHUNK // (2 * level))]
        pieces = [jnp.broadcast_to(b[h][r:r + 1, :], (2 * level, LANES)) for r in refs]
        return b[h] - (pieces[0] if len(pieces) == 1 else jnp.concatenate(pieces, axis=0))
    edge = [b[h][0:1, :] if rev else b[h][CHUNK - 1:CHUNK, :] for h in heads]
    kd = [k16[h] * jnp.exp2(edge[h] - b[h]).astype(BF16) for h in heads]
    grow = [lax.dot_general(v16[h], kd[h], (((0,), (0,)), ((), ())), preferred_element_type=F32)
            for h in heads]
    q16 = [qs[h].astype(BF16) for h in heads] if need_o else None
    st = list(sts)
    carried = []
    for h in heads:
        if need_o:
            carried.append(_dot_nt(q16[h] * jnp.exp2(b[h]).astype(BF16), st[h % n_heads].astype(BF16)))
        st[h % n_heads] = st[h % n_heads] * jnp.exp2(edge[h]) + grow[h]
    if not need_o:
        return None, st

    zero = jnp.zeros((CHUNK, LANES), BF16)
    att = [[] for _ in heads]
    for p in range(n_pairs):
        lhs, rhs = [], []
        for h in heads:
            qe, ke = [], []
            for i in (2 * p, 2 * p + 1):
                e = jnp.exp2(-jnp.abs(level_diff(h, LEVELS[i]))).astype(BF16)
                qe.append(q16[h] * e)
                ke.append(k16[h] * e)
            lhs.append(jnp.concatenate(qe, axis=1))
            rhs.append(jnp.concatenate([jnp.concatenate([ke[0], zero], axis=1),
                                        jnp.concatenate([zero, ke[1]], axis=1)], axis=0))
        scores = [_dot_nt(lhs[h], rhs[h]) for h in heads]
        for h in heads:
            att[h].append((scores[h] * pair_ref[p]).astype(BF16))
    o = []
    for h in heads:
        vals = jnp.concatenate([v16[h]] * (2 * n_pairs), axis=0)
        same_row = jnp.sum(qs[h] * ks[h], axis=1, keepdims=True) * vs[h]
        o.append(carried[h] + _dot(jnp.concatenate(att[h], axis=1), vals) + same_row)
    return o, st


def _hgrn_kernel(*refs, rev, need_o, final, want_state, n_chunks):
    it = iter(refs)
    tabs = tuple(next(it) for _ in range(2))
    q_ref = next(it) if need_o else None
    k_ref, v_ref, lf_ref, s0_ref = next(it), next(it), next(it), next(it)
    prev_ref = next(it) if final else None
    gate_ref = next(it) if final else None
    o_ref = next(it) if need_o else None
    sout_ref = next(it) if want_state else None
    st_ref = next(it)

    j = pl.program_id(1)
    nb = pl.num_programs(1)

    @pl.when(j == 0)
    def _():
        st_ref[...] = s0_ref[0]

    group = HGRN_GROUP if n_chunks % HGRN_GROUP == 0 else 1

    def body(gi, carry):
        rows = []
        for s in range(group):
            c = gi * group + s
            c = (n_chunks - 1 - c) if rev else c
            rows.append(pl.ds(pl.multiple_of(c * CHUNK, CHUNK), CHUNK))
        pairs = [(r, hh) for r in rows for hh in range(HG_HEADS)]
        o, st_new = _hgrn_chunks([q_ref[0, hh, r, :] for r, hh in pairs] if need_o else None,
                                 [k_ref[0, hh, r, :] for r, hh in pairs],
                                 [v_ref[0, hh, r, :] for r, hh in pairs],
                                 [lf_ref[0, hh, r, :] for r, hh in pairs],
                                 [st_ref[hh] for hh in range(HG_HEADS)], tabs, rev, need_o)
        for hh in range(HG_HEADS):
            st_ref[hh] = st_new[hh]
        if need_o:
            for (r, hh), out in zip(pairs, o):
                if final:
                    tot = out + prev_ref[0, hh, r, :]
                    out = tot * lax.rsqrt(jnp.mean(tot * tot, axis=-1, keepdims=True) + EPS)
                    out = out * gate_ref[0, hh, r, :]
                o_ref[0, hh, r, :] = out
        return carry

    lax.fori_loop(0, n_chunks // group, body, 0)

    if want_state:
        @pl.when(j == nb - 1)
        def _():
            sout_ref[0] = st_ref[...]


def _hgrn(q, k, v, lf, s0, prev=None, gate=None, *, rev, need_o, want_state, name):
    b, hh, t, _ = k.shape
    tb = min(512, t)
    nb = t // tb
    final = prev is not None
    tabs = _hgrn_tables(rev)

    seq = pl.BlockSpec((1, hh, tb, HEAD_DIM), lambda i, j: (i, 0, (nb - 1 - j) if rev else j, 0))
    state = pl.BlockSpec((1, hh, HEAD_DIM, HEAD_DIM), lambda i, j: (i, 0, 0, 0))
    args = list(tabs)
    in_specs = [pl.BlockSpec(a.shape, functools.partial(lambda nd, i, j: (0,) * nd, a.ndim)) for a in tabs]
    for a in ((q,) if need_o else ()) + (k, v, lf):
        args.append(a)
        in_specs.append(seq)
    args.append(s0)
    in_specs.append(state)
    if final:
        args += [prev, gate]
        in_specs += [seq, seq]
    out_shape, out_specs = [], []
    if need_o:
        out_shape.append(jax.ShapeDtypeStruct((b, hh, t, HEAD_DIM), F32))
        out_specs.append(seq)
    if want_state:
        out_shape.append(jax.ShapeDtypeStruct((b, hh, HEAD_DIM, HEAD_DIM), F32))
        out_specs.append(state)
    kern = functools.partial(_hgrn_kernel, rev=rev, need_o=need_o, final=final,
                             want_state=want_state, n_chunks=tb // CHUNK)
    est = 2 * (len(args) + len(out_shape)) * hh * tb * HEAD_DIM * 4
    return pl.pallas_call(
        kern,
        out_shape=out_shape,
        grid=(b, nb),
        in_specs=in_specs,
        out_specs=out_specs,
        scratch_shapes=[pltpu.VMEM((hh, HEAD_DIM, HEAD_DIM), F32)],
        compiler_params=pltpu.CompilerParams(
            dimension_semantics=("arbitrary", "arbitrary"),
            vmem_limit_bytes=_vmem_limit(est)),
        name=name,
    )(*args)


def _window_sum(x, w, stride, pos, extent):
    n = x.shape[0]
    whole_tiles = stride % SUBLANES == 0 and stride * extent == n

    def ahead(y, dist):
        if whole_tiles:
            k = min(dist, extent) * stride
            return jnp.concatenate([y[k:], jnp.zeros((k, y.shape[1]), y.dtype)], axis=0)
        return jnp.where(pos + dist < extent, pltpu.roll(y, (n - dist * stride) % n, 0), 0.0)

    def behind(y, dist):
        if whole_tiles:
            k = min(dist, extent) * stride
            return jnp.concatenate([jnp.zeros((k, y.shape[1]), y.dtype), y[:n - k]], axis=0)
        return jnp.where(pos >= dist, pltpu.roll(y, dist * stride, 0), 0.0)

    fwd, bwd, h = x, x, 1
    while h < w // 2:
        fwd = fwd + ahead(fwd, h)
        bwd = bwd + behind(bwd, h)
        h *= 2
    return fwd + behind(bwd, 1)


def _window_count(pos, w, extent):
    return (jnp.minimum(pos + w // 2, extent) - jnp.maximum(pos - w // 2, 0)).astype(F32)


def _pool_kernel(u_ref, pw_ref, ps_ref, o_ref):
    g = pl.program_id(1)
    n = u_ref.shape[2]
    tok = lax.broadcasted_iota(I32, (n, LANES), 0)
    col = tok & (GRID_W - 1)
    row = tok >> _log2(GRID_W)
    for gi, w in enumerate(POOL_WINDOWS):
        @pl.when(g == gi)
        def _(w=w):
            u = u_ref[0, 0]
            rows = n // GRID_W
            s = _window_sum(_window_sum(u, w, GRID_W, row, rows), w, 1, col, GRID_W)
            m = s / (_window_count(row, w, rows) * _window_count(col, w, GRID_W))
            o_ref[0, 0] = _dot((m - u).astype(BF16), pw_ref[0]) * ps_ref[0]


def _pool(u, pw, ps):
    b, g, t, c = u.shape
    blk = pl.BlockSpec((1, 1, t, c), lambda i, j: (i, j, 0, 0))
    return pl.pallas_call(
        _pool_kernel,
        out_shape=jax.ShapeDtypeStruct(u.shape, F32),
        grid=(b, g),
        in_specs=[blk,
                  pl.BlockSpec((1, c, c), lambda i, j: (j, 0, 0)),
                  pl.BlockSpec((1, 1, c), lambda i, j: (j, 0, 0))],
        out_specs=blk,
        compiler_params=pltpu.CompilerParams(
            dimension_semantics=("arbitrary", "arbitrary"),
            vmem_limit_bytes=_vmem_limit(12 * t * c * 4)),
        name="pool",
    )(u, pw, ps)


def _outproj_kernel(hg_ref, pool_ref, x_ref, w_ref, mod_ref, gpost_ref, gpre_ref, wr_ref, br_ref,
                    x1_ref, h2_ref, ti_ref, tw_ref, cnt_ref):
    d = x_ref.shape[-1]
    r = pl.program_id(0)

    @pl.when(jnp.logical_and(r == 0, pl.program_id(1) == 0))
    def _():
        cnt_ref[...] = jnp.zeros(cnt_ref.shape, F32)

    gt1 = mod_ref[pl.ds(r, 1), pl.ds(2 * d, d)]
    sh2 = mod_ref[pl.ds(r, 1), pl.ds(3 * d, d)]
    sc2 = mod_ref[pl.ds(r, 1), pl.ds(4 * d, d)]
    tm = x_ref.shape[1]
    sub = tm // OUTPROJ_SLICES
    slices = [pl.ds(i * sub, sub) for i in range(OUTPROJ_SLICES)]
    ys = []
    for rows in slices:
        cat = jnp.concatenate([hg_ref[0, hh, rows, :] for hh in range(HG_HEADS)]
                              + [pool_ref[0, g, rows, :] for g in range(len(POOL_WINDOWS))], axis=1)
        ys.append(_dot(cat.astype(BF16), w_ref[...]))
    logits = []
    for i, rows in enumerate(slices):
        x1 = x_ref[0, rows, :] + gt1 * _rms(ys[i], gpost_ref[...])
        x1_ref[0, rows, :] = x1
        h2 = _rms(x1, gpre_ref[...]) * (1.0 + sc2) + sh2
        _store_row_tiles(h2_ref.at[0, pl.ds(i * sub * SUBLANES, sub * SUBLANES), :], h2)
        logits.append(_dot(h2.astype(BF16), wr_ref[...]) + br_ref[...])

    lane = lax.broadcasted_iota(I32, (sub, LANES), 1).astype(F32)
    picked = jnp.zeros((1, LANES), F32)
    for i, rows in enumerate(slices):
        lg = jnp.where(lane < N_EXPERTS, logits[i], -jnp.inf)
        ti = jnp.zeros(lg.shape, F32)
        tw = jnp.zeros(lg.shape, F32)
        chosen = jnp.zeros(lg.shape, F32)
        top = None
        for jj in range(TOP_K):
            m = jnp.max(lg, axis=1, keepdims=True)
            idx = jnp.min(jnp.where(lg == m, lane, float(LANES)), axis=1, keepdims=True)
            lg = jnp.where(lane == idx, -jnp.inf, lg)
            chosen = jnp.where(lane == idx, 1.0, chosen)
            top = m if top is None else top
            ti = jnp.where(lane == jj, idx, ti)
            tw = jnp.where(lane == jj, jnp.exp(m - top), tw)
        ti_ref[0, rows, :] = ti.astype(I32)
        tw_ref[0, rows, :] = tw / jnp.sum(tw, axis=1, keepdims=True)
        picked = picked + jnp.sum(chosen, axis=0, keepdims=True)
    cnt_ref[...] += picked


def _outproj(hg, pool, x, w_out, mod, gpost, gpre, wr, br):
    b, t, d = x.shape
    tm = min(512, t)
    head = pl.BlockSpec((1, HG_HEADS, tm, HEAD_DIM), lambda i, j: (i, 0, j, 0))
    tok = pl.BlockSpec((1, tm, d), lambda i, j: (i, j, 0))
    lanes = pl.BlockSpec((1, tm, LANES), lambda i, j: (i, j, 0))
    vec = pl.BlockSpec((1, d), lambda i, j: (0, 0))
    est = 2 * (2 * tm * HG_WIDTH * 4 + 3 * tm * d * 4 + 2 * tm * LANES * 4) + 2 * d * d * 2 + 6 * tm * d * 4
    return pl.pallas_call(
        _outproj_kernel,
        out_shape=[jax.ShapeDtypeStruct((b, t, d), F32),
                   jax.ShapeDtypeStruct((b, t * d // LANES, LANES), F32),
                   jax.ShapeDtypeStruct((b, t, LANES), I32), jax.ShapeDtypeStruct((b, t, LANES), F32),
                   jax.ShapeDtypeStruct((1, LANES), F32)],
        grid=(b, t // tm),
        in_specs=[head, head, tok,
                  pl.BlockSpec(w_out.shape, lambda i, j: (0, 0)),
                  pl.BlockSpec(mod.shape, lambda i, j: (0, 0)),
                  vec, vec,
                  pl.BlockSpec(wr.shape, lambda i, j: (0, 0)),
                  pl.BlockSpec((1, LANES), lambda i, j: (0, 0))],
        out_specs=[tok, pl.BlockSpec((1, tm * d // LANES, LANES), lambda i, j: (i, j, 0)),
                   lanes, lanes, pl.BlockSpec((1, LANES), lambda i, j: (0, 0))],
        compiler_params=pltpu.CompilerParams(
            dimension_semantics=("arbitrary", "arbitrary"),
            vmem_limit_bytes=_vmem_limit(est)),
        name="outproj",
    )(hg, pool, x, w_out, mod, gpost, gpre, wr, br)


def _rank_kernel(ti_ref, start_ref, dest_ref, carry_ref):
    i = pl.program_id(0)

    @pl.when(i == 0)
    def _():
        carry_ref[...] = start_ref[...]

    ti = ti_ref[...]
    tm = ti.shape[0]
    lane = lax.broadcasted_iota(I32, (tm, LANES), 1)
    onehot = [lane == ti[:, jj:jj + 1] for jj in range(TOP_K)]
    cnt = sum(jnp.where(oh, 1.0, 0.0) for oh in onehot)
    row = lax.broadcasted_iota(I32, (tm, tm), 0)
    col = lax.broadcasted_iota(I32, (tm, tm), 1)
    before = jnp.where(col < row, 1.0, 0.0).astype(BF16)
    prefix = _dot(before, cnt.astype(BF16)) + carry_ref[...]
    rank = jnp.zeros((tm, LANES), F32)
    for jj in range(TOP_K):
        rj = jnp.sum(jnp.where(onehot[jj], prefix, 0.0), axis=1, keepdims=True)
        rank = jnp.where(lane == jj, rj, rank)
    dest_ref[...] = rank.astype(I32) * SUBLANES
    carry_ref[...] += jnp.sum(cnt, axis=0, keepdims=True)


def _rank(ti, starts):
    t = ti.shape[0]
    tm = min(512, t)
    return pl.pallas_call(
        _rank_kernel,
        out_shape=jax.ShapeDtypeStruct((t, LANES), I32),
        grid=(t // tm,),
        in_specs=[pl.BlockSpec((tm, LANES), lambda i: (i, 0)),
                  pl.BlockSpec((1, LANES), lambda i: (0, 0))],
        out_specs=pl.BlockSpec((tm, LANES), lambda i: (i, 0)),
        scratch_shapes=[pltpu.VMEM((1, LANES), F32)],
        compiler_params=pltpu.CompilerParams(dimension_semantics=("arbitrary",)),
        name="rank",
    )(ti, starts)


def _tile_rows(start):
    return pl.ds(pl.multiple_of(start, SUBLANES), SUBLANES)


def _gap_copies(e, gap_start_ref, gap_len_ref, zeros, xs_ref, sem):
    off, n = gap_start_ref[e], gap_len_ref[e]
    out = []
    for bit in range(_log2(MOE_BM)):
        size = (1 << bit) * SUBLANES
        used = (n >> bit) & 1
        out.append((used == 1,
                    pltpu.make_async_copy(zeros.at[pl.ds(0, size), :],
                                          xs_ref.at[pl.ds(pl.multiple_of(off * SUBLANES, SUBLANES), size), :],
                                          sem)))
        off = off + used * (1 << bit)
    return out


def _dispatch_kernel(dest_ref, gap_start_ref, gap_len_ref, n_live_ref, h_ref, xs_ref, zeros, sem, zsem):
    tm = h_ref.shape[0] // SUBLANES
    block_rows = MOE_BM * SUBLANES
    n_blocks = xs_ref.shape[0] // block_rows

    @pl.when(pl.program_id(0) == 0)
    def _():
        zeros[...] = jnp.zeros(zeros.shape, zeros.dtype)

        def tail_copy(blk):
            return pltpu.make_async_copy(
                zeros, xs_ref.at[pl.ds(pl.multiple_of(blk * block_rows, block_rows), block_rows), :], zsem)

        def tail_start(blk, carry):
            tail_copy(blk).start()
            return carry

        def tail_wait(blk, carry):
            tail_copy(blk).wait()
            return carry

        lax.fori_loop(n_live_ref[0], n_blocks, tail_start, 0)
        lax.fori_loop(n_live_ref[0], n_blocks, tail_wait, 0)

        def fill(e, carry):
            for used, cp in _gap_copies(e, gap_start_ref, gap_len_ref, zeros, xs_ref, zsem):
                @pl.when(used)
                def _(cp=cp):
                    cp.start()
            return carry

        def drain(e, carry):
            for used, cp in _gap_copies(e, gap_start_ref, gap_len_ref, zeros, xs_ref, zsem):
                @pl.when(used)
                def _(cp=cp):
                    cp.wait()
            return carry

        lax.fori_loop(0, N_EXPERTS, fill, 0)
        lax.fori_loop(0, N_EXPERTS, drain, 0)

    def start(r, carry):
        for jj in range(TOP_K):
            pltpu.make_async_copy(h_ref.at[_tile_rows(r * SUBLANES), :],
                                  xs_ref.at[_tile_rows(dest_ref[r * TOP_K + jj]), :],
                                  sem).start(priority=jj % 2)
        return carry

    lax.fori_loop(0, tm, start, 0, unroll=4)
    for jj in range(TOP_K):
        pltpu.make_async_copy(h_ref, xs_ref.at[pl.ds(0, tm * SUBLANES), :], sem).wait()


def _dispatch(dest, gap_start, gap_len, n_live, h2, n_sorted):
    rows, l = h2.shape
    t = rows // SUBLANES
    tm = min(256, t)
    smem = pl.BlockSpec(memory_space=pltpu.SMEM)
    return pl.pallas_call(
        _dispatch_kernel,
        out_shape=jax.ShapeDtypeStruct((n_sorted * SUBLANES, l), h2.dtype),
        grid=(t // tm,),
        in_specs=[pl.BlockSpec((tm * TOP_K,), lambda i: (i,), memory_space=pltpu.SMEM), smem, smem, smem,
                  pl.BlockSpec((tm * SUBLANES, l), lambda i: (i, 0))],
        out_specs=pl.BlockSpec(memory_space=pl.ANY),
        scratch_shapes=[pltpu.VMEM((MOE_BM * SUBLANES, l), h2.dtype),
                        pltpu.SemaphoreType.DMA, pltpu.SemaphoreType.DMA],
        compiler_params=pltpu.CompilerParams(dimension_semantics=("arbitrary",)),
        name="dispatch",
    )(dest, gap_start, gap_len, n_live, h2)


def _moe_kernel(block_e_ref, next_e_ref, n_live_ref,
                x_ref, wg_ref, wu_ref, wd_ref, bg_ref, bu_ref, bd_ref, y_ref,
                wgs, wus, wds, wgb, wub, wdb, sem):
    i = pl.program_id(0)
    live = i < n_live_ref[0]
    e = block_e_ref[i]
    changed = jnp.logical_or(i == 0, e != block_e_ref[jnp.maximum(i - 1, 0)])

    def fetch(expert):
        return [pltpu.make_async_copy(w_ref.at[expert], stage, sem.at[n])
                for n, (w_ref, stage) in enumerate(((wg_ref, wgs), (wu_ref, wus), (wd_ref, wds)))]

    @pl.when(i == 0)
    def _():
        for cp in fetch(e):
            cp.start()

    @pl.when(jnp.logical_and(live, changed))
    def _():
        for cp in fetch(e):
            cp.wait()
        def cast(c, carry):
            rows = pl.ds(pl.multiple_of(c * LANES, LANES), LANES)
            for src, dst in ((wgs, wgb), (wus, wub), (wds, wdb)):
                dst[rows, :] = src[rows, :].astype(BF16)
            return carry

        lax.fori_loop(0, wgs.shape[0] // LANES, cast, 0)

        @pl.when(next_e_ref[i] != e)
        def _():
            for cp in fetch(next_e_ref[i]):
                cp.start()

    @pl.when(live)
    def _():
        x = _load_row_tiles(x_ref).astype(BF16)
        gate = jnp.minimum(_dot(x, wgb[...]) + bg_ref[...], SWIGLU_LIMIT)
        up = jnp.clip(_dot(x, wub[...]) + bu_ref[...], -SWIGLU_LIMIT, SWIGLU_LIMIT)
        hmid = (up + 1.0) * gate * _sigmoid(SWIGLU_ALPHA * gate)
        _store_row_tiles(y_ref, _dot(hmid.astype(BF16), wdb[...]) + bd_ref[...])

    @pl.when(jnp.logical_not(live))
    def _():
        y_ref[...] = jnp.zeros(y_ref.shape, y_ref.dtype)


def _moe(block_e, next_e, n_live, xs, wg, wu, wd, bg, bu, bd):
    d, f = wg.shape[-2:]
    n_blocks = block_e.shape[0]
    rows_in = pl.BlockSpec((MOE_BM * SUBLANES, LANES),
                           lambda i, be, ne, nl: (jnp.minimum(i, nl[0] - 1), 0))
    rows_out = pl.BlockSpec((MOE_BM * SUBLANES, LANES), lambda i, be, ne, nl: (i, 0))
    hbm = pl.BlockSpec(memory_space=pl.ANY)

    def bspec(n):
        return pl.BlockSpec((None, 1, n), lambda i, be, ne, nl: (be[i], 0, 0))

    est = 3 * d * f * 4 + 3 * d * f * 2 + 4 * MOE_BM * d * 4 + 4 * MOE_BM * f * 4
    return pl.pallas_call(
        _moe_kernel,
        out_shape=jax.ShapeDtypeStruct(xs.shape, F32),
        grid_spec=pltpu.PrefetchScalarGridSpec(
            num_scalar_prefetch=3,
            grid=(n_blocks,),
            in_specs=[rows_in, hbm, hbm, hbm, bspec(f), bspec(f), bspec(d)],
            out_specs=rows_out,
            scratch_shapes=[pltpu.VMEM((d, f), F32), pltpu.VMEM((d, f), F32), pltpu.VMEM((f, d), F32),
                            pltpu.VMEM((d, f), BF16), pltpu.VMEM((d, f), BF16), pltpu.VMEM((f, d), BF16),
                            pltpu.SemaphoreType.DMA((3,))]),
        compiler_params=pltpu.CompilerParams(
            dimension_semantics=("arbitrary",),
            vmem_limit_bytes=_vmem_limit(est)),
        name="moe",
    )(block_e, next_e, n_live, xs, wg, wu, wd, bg, bu, bd)


def _moe_layout(counts, n_rows):
    n_blocks = n_rows // MOE_BM + N_EXPERTS
    padded = (counts + MOE_BM - 1) // MOE_BM * MOE_BM
    ends = jnp.cumsum(padded)
    starts = ends - padded
    block_end = ends // MOE_BM
    i = jnp.arange(n_blocks, dtype=I32)
    ids = jnp.arange(N_EXPERTS, dtype=I32)
    last_e = jnp.max(jnp.where(counts > 0, ids, 0))
    block_e = jnp.minimum(jnp.sum(block_end[None, :] <= i[:, None], axis=1), last_e).astype(I32)
    later = jnp.logical_and(ids[None, :] > ids[:, None], counts[None, :] > 0)
    following = jnp.min(jnp.where(later, ids[None, :], N_EXPERTS), axis=1)
    following = jnp.where(following == N_EXPERTS, ids, following)
    next_e = jnp.sum(jnp.where(block_e[:, None] == ids[None, :], following[None, :], 0), axis=1)
    return (starts.astype(I32), (starts + counts).astype(I32), (padded - counts).astype(I32),
            block_e, next_e.astype(I32), block_end[-1:].astype(I32), n_blocks)


def _combine_kernel(dest_ref, dnext_ref, ys_ref, tw_ref, x1_ref, mod_ref, gpost_ref, o_ref, buf, sem,
                    *, tiles_per_batch):
    tm, d = x1_ref.shape
    i = pl.program_id(0)
    slot = i % 2

    def issue(d_ref, s):
        def start(r, carry):
            for jj in range(TOP_K):
                pltpu.make_async_copy(ys_ref.at[_tile_rows(d_ref[r * TOP_K + jj]), :],
                                      buf.at[s, jj, _tile_rows(r * SUBLANES), :],
                                      sem.at[s]).start(priority=jj % 2)
            return carry

        lax.fori_loop(0, tm, start, 0, unroll=4)

    @pl.when(i == 0)
    def _():
        issue(dest_ref, 0)

    @pl.when(i + 1 < pl.num_programs(0))
    def _():
        issue(dnext_ref, 1 - slot)

    for jj in range(TOP_K):
        pltpu.make_async_copy(ys_ref.at[pl.ds(0, tm * SUBLANES), :], buf.at[slot, jj],
                              sem.at[slot]).wait()

    tw = tw_ref[...]
    y = tw[:, 0:1] * _load_row_tiles(buf.at[slot, 0])
    for jj in range(1, TOP_K):
        y = y + tw[:, jj:jj + 1] * _load_row_tiles(buf.at[slot, jj])
    r = i // tiles_per_batch
    gt2 = mod_ref[pl.ds(r, 1), pl.ds(5 * d, d)]
    o_ref[...] = x1_ref[...] + gt2 * _rms(y, gpost_ref[...])


def _combine(dest, ys, tw, x1, mod, gpost, tokens_per_batch):
    t, d = x1.shape
    tm = min(256, tokens_per_batch)
    n_tiles = t // tm
    kern = functools.partial(_combine_kernel, tiles_per_batch=tokens_per_batch // tm)
    est = 2 * TOP_K * tm * d * 4 + 2 * (2 * tm * d * 4 + tm * LANES * 4) + 3 * tm * d * 4
    return pl.pallas_call(
        kern,
        out_shape=jax.ShapeDtypeStruct((t, d), F32),
        grid=(n_tiles,),
        in_specs=[pl.BlockSpec((tm * TOP_K,), lambda i: (i,), memory_space=pltpu.SMEM),
                  pl.BlockSpec((tm * TOP_K,), lambda i: (jnp.minimum(i + 1, n_tiles - 1),),
                               memory_space=pltpu.SMEM),
                  pl.BlockSpec(memory_space=pl.ANY),
                  pl.BlockSpec((tm, LANES), lambda i: (i, 0)),
                  pl.BlockSpec((tm, d), lambda i: (i, 0)),
                  pl.BlockSpec(mod.shape, lambda i: (0, 0)),
                  pl.BlockSpec((1, d), lambda i: (0, 0))],
        out_specs=pl.BlockSpec((tm, d), lambda i: (i, 0)),
        scratch_shapes=[pltpu.VMEM((2, TOP_K, tm * SUBLANES, LANES), F32),
                        pltpu.SemaphoreType.DMA((2,))],
        compiler_params=pltpu.CompilerParams(
            dimension_semantics=("arbitrary",),
            vmem_limit_bytes=_vmem_limit(est)),
        name="combine",
    )(dest, dest, ys, tw, x1, mod, gpost)


def kernel(x, c, ctx, c_ctx, w_ada, b_ada, g_pre_mix, g_post_mix, g_pre_ffn, g_post_ffn,
           w_in, w_out, hgrn_lb_logits, hgrn_norm, pool_w, pool_scale,
           w_router, b_router, w_gate, b_gate, w_up, b_up, w_down, b_down):
    b, s, d = x.shape
    layer = 0
    n_e = w_router.shape[-1]

    cc = jnp.concatenate([c, c_ctx[None, :]], axis=0)
    cc = jnp.pad(cc, ((0, -(b + 1) % SUBLANES), (0, 0)))
    mod = _ada(cc, w_ada[layer], b_ada[layer][None, :])

    w_in_b = w_in[layer].astype(BF16)
    lbl = hgrn_lb_logits[:2].reshape(4, HG_WIDTH)
    gn = hgrn_norm[layer][None, :]
    gpre = g_pre_mix[layer][None, :]

    lff_c, kf_c, lfb_c, kb_c, v_c = _inproj(ctx, mod, gpre, w_in_b, lbl, gn, mod_row=b, full=False)
    zeros = jnp.zeros((b, HG_HEADS, HEAD_DIM, HEAD_DIM), F32)
    (s_f,) = _hgrn(None, kf_c, v_c, lff_c, zeros, rev=False, need_o=False, want_state=True,
                   name="hgrn_ctx_fwd")
    (s_b,) = _hgrn(None, kb_c, v_c, lfb_c, zeros, rev=True, need_o=False, want_state=True,
                   name="hgrn_ctx_bwd")

    q, lff, kf, lfb, kb, v, gate, u = _inproj(x, mod, gpre, w_in_b, lbl, gn, mod_row=None, full=True)
    (o_f,) = _hgrn(q, kf, v, lff, s_f, rev=False, need_o=True, want_state=False, name="hgrn_fwd")
    (o_hg,) = _hgrn(q, kb, v, lfb, s_b, o_f, gate, rev=True, need_o=True, want_state=False,
                    name="hgrn_bwd")
    o_pool = _pool(u, pool_w[layer].astype(BF16), pool_scale[layer].reshape(len(POOL_WINDOWS), 1, POOL_GROUP))

    wr = jnp.pad(w_router[layer], ((0, 0), (0, LANES - n_e))).astype(BF16)
    br = jnp.pad(b_router[layer], (0, LANES - n_e))[None, :]
    x1, h2, ti, tw, cnt = _outproj(o_hg, o_pool, x, w_out[layer].astype(BF16), mod,
                                   g_post_mix[layer][None, :], g_pre_ffn[layer][None, :], wr, br)

    t = b * s
    starts, gap_start, gap_len, block_e, next_e, n_live, n_blocks = _moe_layout(
        cnt[0, :n_e].astype(I32), t * TOP_K)
    starts = jnp.pad(starts.astype(F32), (0, LANES - n_e))[None, :]
    dest = _rank(ti.reshape(t, LANES), starts)[:, :TOP_K].reshape(t * TOP_K)

    xs = _dispatch(dest, gap_start, gap_len, n_live, h2.reshape(t * d // LANES, LANES), n_blocks * MOE_BM)
    ys = _moe(block_e, next_e, n_live, xs, w_gate[layer], w_up[layer], w_down[layer],
              b_gate[layer][:, None, :], b_up[layer][:, None, :], b_down[layer][:, None, :])
    out = _combine(dest, ys, tw.reshape(t, LANES), x1.reshape(t, d), mod,
                   g_post_ffn[layer][None, :], s)
    return out.reshape(b, s, d)
```

```python
import functools
import math

import numpy as np
import jax
import jax.numpy as jnp
from jax import lax
from jax.experimental import pallas as pl
from jax.experimental.pallas import tpu as pltpu

F32 = jnp.float32
BF16 = jnp.bfloat16
I32 = jnp.int32

GRID_W = 64
HG_HEADS = 4
HEAD_DIM = 128
HG_WIDTH = HG_HEADS * HEAD_DIM
POOL_WINDOWS = (2, 4, 8, 16)
POOL_GROUP = 128
N_EXPERTS = 32
TOP_K = 4
SWIGLU_LIMIT = 7.0
SWIGLU_ALPHA = 1.702
EPS = 1e-6

LANES = 128
SUBLANES = 8
V7X_VMEM_BYTES = 64 * 1024 * 1024

CHUNK = 64
HGRN_GROUP = 4
LEVELS = (32, 16, 8, 4, 2, 1)
FINE_LEVELS = (2, 1)
MOE_BM = 256
OUTPROJ_SLICES = 4
LOG2_E = math.log2(math.e)


def _vmem_limit(nbytes):
    return int(min(nbytes * 3 // 2 + (4 << 20), V7X_VMEM_BYTES - (6 << 20)))


def _log2(n):
    assert n & (n - 1) == 0
    return n.bit_length() - 1


def _sigmoid(x):
    return 1.0 / (1.0 + jnp.exp(-x))


def _rms(x, gain):
    return x * lax.rsqrt(jnp.mean(x * x, axis=-1, keepdims=True) + EPS) * gain


def _dot(a, b):
    return jnp.dot(a, b, preferred_element_type=F32)


ROW_TILE = SUBLANES * LANES


def _store_row_tiles(ref, val):
    rows = val.shape[0]
    for c in range(SUBLANES):
        ref[pl.ds(c, rows, stride=SUBLANES), :] = val[:, c * LANES:(c + 1) * LANES]


def _load_row_tiles(ref):
    rows = ref.shape[0] // SUBLANES
    return jnp.concatenate([ref[pl.ds(c, rows, stride=SUBLANES), :] for c in range(SUBLANES)], axis=1)


def _dot_nt(a, b):
    return lax.dot_general(a, b, (((1,), (1,)), ((), ())), preferred_element_type=F32)


def _ada_kernel(c_ref, w_ref, b_ref, o_ref):
    c = c_ref[...]
    s = (c * _sigmoid(c)).astype(BF16)
    o_ref[...] = _dot(s, w_ref[...].astype(BF16)) + b_ref[...]


def _ada(cc, w, b):
    rows, d = cc.shape
    n = w.shape[1]
    tn = 1536 if n % 1536 == 0 else n
    return pl.pallas_call(
        _ada_kernel,
        out_shape=jax.ShapeDtypeStruct((rows, n), F32),
        grid=(n // tn,),
        in_specs=[pl.BlockSpec((rows, d), lambda j: (0, 0)),
                  pl.BlockSpec((d, tn), lambda j: (0, j)),
                  pl.BlockSpec((1, tn), lambda j: (0, j))],
        out_specs=pl.BlockSpec((rows, tn), lambda j: (0, j)),
        compiler_params=pltpu.CompilerParams(
            dimension_semantics=("arbitrary",),
            vmem_limit_bytes=_vmem_limit(2 * d * tn * 4 + d * tn * 2)),
        name="ada",
    )(cc, w, b)


def _inproj_kernel(x_ref, mod_ref, gpre_ref, w_ref, lbl_ref, gn_ref, *outs, mod_row, full):
    d = x_ref.shape[-1]
    r = pl.program_id(0) if mod_row is None else mod_row
    sh = mod_ref[pl.ds(r, 1), pl.ds(0, d)]
    sc = mod_ref[pl.ds(r, 1), pl.ds(d, d)]
    tm = x_ref.shape[1]
    n_slices = 2 if tm % (2 * SUBLANES) == 0 else 1
    sub = tm // n_slices
    slices = [pl.ds(i * sub, sub) for i in range(n_slices)]
    hs = [(_rms(x_ref[0, rows, :], gpre_ref[...]) * (1.0 + sc) + sh).astype(BF16) for rows in slices]

    def proj(g):
        return [_dot(h, w_ref[:, g * HG_WIDTH:(g + 1) * HG_WIDTH]) for h in hs]

    def put(ref, vals):
        for rows, val in zip(slices, vals):
            for hh in range(HG_HEADS):
                ref[0, hh, rows, :] = val[:, hh * HEAD_DIM:(hh + 1) * HEAD_DIM]

    def lower_bound(direction):
        l0 = lbl_ref[pl.ds(direction, 1), :]
        l1 = lbl_ref[pl.ds(2 + direction, 1), :]
        m = jnp.maximum(l0, l1)
        e0 = jnp.exp(l0 - m)
        return e0 / (e0 + jnp.exp(l1 - m))

    if full:
        q_o, lff_o, kf_o, lfb_o, kb_o, v_o, gate_o, u_o = outs
        put(q_o, [q * _sigmoid(q) for q in proj(0)])
    else:
        lff_o, kf_o, lfb_o, kb_o, v_o = outs
    for direction, (lf_o, k_o) in enumerate(((lff_o, kf_o), (lfb_o, kb_o))):
        lb = lower_bound(direction)
        sgs = [_sigmoid(z) for z in proj(1 + direction)]
        put(lf_o, [jnp.log(lb + (1.0 - lb) * sg) for sg in sgs])
        put(k_o, [(1.0 - lb) * (1.0 - sg) for sg in sgs])
    put(v_o, proj(3))
    if full:
        put(gate_o, [gn_ref[...] * _sigmoid(z) for z in proj(4)])
        put(u_o, proj(5))


def _inproj(x, mod, gpre, w_in, lbl, gn, *, mod_row, full):
    b, t, d = x.shape
    tm = min(512, t)
    n_out = 8 if full else 5
    n_cols = w_in.shape[1]
    head = pl.BlockSpec((1, HG_HEADS, tm, HEAD_DIM), lambda i, j: (i, 0, j, 0))
    kern = functools.partial(_inproj_kernel, mod_row=mod_row, full=full)
    est = 2 * tm * d * 4 + 2 * d * n_cols * 2 + n_out * 2 * tm * HG_WIDTH * 4 + 4 * tm * HG_WIDTH * 4
    return pl.pallas_call(
        kern,
        out_shape=[jax.ShapeDtypeStruct((b, HG_HEADS, t, HEAD_DIM), F32)] * n_out,
        grid=(b, t // tm),
        in_specs=[pl.BlockSpec((1, tm, d), lambda i, j: (i, j, 0)),
                  pl.BlockSpec(mod.shape, lambda i, j: (0, 0)),
                  pl.BlockSpec((1, d), lambda i, j: (0, 0)),
                  pl.BlockSpec(w_in.shape, lambda i, j: (0, 0)),
                  pl.BlockSpec(lbl.shape, lambda i, j: (0, 0)),
                  pl.BlockSpec((1, HG_WIDTH), lambda i, j: (0, 0))],
        out_specs=[head] * n_out,
        compiler_params=pltpu.CompilerParams(
            dimension_semantics=("arbitrary", "arbitrary"),
            vmem_limit_bytes=_vmem_limit(est)),
        name="inproj_full" if full else "inproj_ctx",
    )(x, mod, gpre, w_in, lbl, gn)


def _hgrn_tables(rev):
    row = np.arange(CHUNK)[:, None]
    col = np.arange(CHUNK)[None, :]
    tri = ((col >= row) if rev else (col <= row)).astype(np.float32)
    blocks, masks = [tri], []
    for h in LEVELS:
        if h in FINE_LEVELS:
            blocks.append(tri - tri[_ref_row(np.arange(CHUNK), h, rev)])
        is_q = ((row // h) % 2) == (0 if rev else 1)
        key_half = ((col // h) % 2) == (1 if rev else 0)
        masks.append(((row // (2 * h)) == (col // (2 * h))) & is_q & key_half)
    pair = np.stack([np.concatenate(masks[i:i + 2], axis=1) for i in range(0, len(LEVELS), 2)])
    cums = np.concatenate(blocks, axis=0)
    return jnp.asarray(np.concatenate([cums, cums], axis=1), BF16), jnp.asarray(pair, F32)


def _ref_row(t, h, rev):
    return (t // (2 * h)) * (2 * h) + (h if rev else h - 1)


def _hgrn_chunks(qs, ks, vs, lfs, sts, tabs, rev, need_o):
    cums_ref, pair_ref = tabs
    n_heads = len(sts)
    heads = range(len(ks))
    n_pairs = len(LEVELS) // 2
    n_cum = CHUNK * (1 + len(FINE_LEVELS)) if need_o else CHUNK
    cums = cums_ref[0:n_cum, :]

    his, los = [], []
    for h in heads:
        lf2 = lfs[h] * LOG2_E
        hi = lf2.astype(BF16)
        his.append(hi)
        los.append((lf2 - hi.astype(F32)).astype(BF16))
    both = _dot(cums, jnp.concatenate([jnp.concatenate(his, axis=1), jnp.concatenate(los, axis=1)], axis=0))
    bd = [both[:, h * LANES:(h + 1) * LANES] for h in heads]

    k16 = [ks[h].astype(BF16) for h in heads]
    v16 = [vs[h].astype(BF16) for h in heads]
    b = [bd[h][0:CHUNK] for h in heads]

    def level_diff(h, level):
        if level in FINE_LEVELS:
            i = FINE_LEVELS.index(level)
            return bd[h][CHUNK * (1 + i):CHUNK * (2 + i)]
        refs = [_ref_row(m * 2 * level, level, rev) for m in range(CHUNK // (2 * level))]
        pieces = [jnp.broadcast_to(b[h][r:r + 1, :], (2 * level, LANES)) for r in refs]
        return b[h] - (pieces[0] if len(pieces) == 1 else jnp.concatenate(pieces, axis=0))
    edge = [b[h][0:1, :] if rev else b[h][CHUNK - 1:CHUNK, :] for h in heads]
    kd = [k16[h] * jnp.exp2(edge[h] - b[h]).astype(BF16) for h in heads]
    grow = [lax.dot_general(v16[h], kd[h], (((0,), (0,)), ((), ())), preferred_element_type=F32)
            for h in heads]
    q16 = [qs[h].astype(BF16) for h in heads] if need_o else None
    st = list(sts)
    carried = []
    for h in heads:
        if need_o:
            carried.append(_dot_nt(q16[h] * jnp.exp2(b[h]).astype(BF16), st[h % n_heads].astype(BF16)))
        st[h % n_heads] = st[h % n_heads] * jnp.exp2(edge[h]) + grow[h]
    if not need_o:
        return None, st

    zero = jnp.zeros((CHUNK, LANES), BF16)
    att = [[] for _ in heads]
    for p in range(n_pairs):
        lhs, rhs = [], []
        for h in heads:
            qe, ke = [], []
            for i in (2 * p, 2 * p + 1):
                e = jnp.exp2(-jnp.abs(level_diff(h, LEVELS[i]))).astype(BF16)
                qe.append(q16[h] * e)
                ke.append(k16[h] * e)
            lhs.append(jnp.concatenate(qe, axis=1))
            rhs.append(jnp.concatenate([jnp.concatenate([ke[0], zero], axis=1),
                                        jnp.concatenate([zero, ke[1]], axis=1)], axis=0))
        scores = [_dot_nt(lhs[h], rhs[h]) for h in heads]
        for h in heads:
            att[h].append((scores[h] * pair_ref[p]).astype(BF16))
    o = []
    for h in heads:
        vals = jnp.concatenate([v16[h]] * (2 * n_pairs), axis=0)
        same_row = jnp.sum(qs[h] * ks[h], axis=1, keepdims=True) * vs[h]
        o.append(carried[h] + _dot(jnp.concatenate(att[h], axis=1), vals) + same_row)
    return o, st


def _hgrn_kernel(*refs, rev, need_o, final, want_state, n_chunks):
    it = iter(refs)
    tabs = tuple(next(it) for _ in range(2))
    q_ref = next(it) if need_o else None
    k_ref, v_ref, lf_ref, s0_ref = next(it), next(it), next(it), next(it)
    prev_ref = next(it) if final else None
    gate_ref = next(it) if final else None
    o_ref = next(it) if need_o else None
    sout_ref = next(it) if want_state else None
    st_ref = next(it)

    j = pl.program_id(1)
    nb = pl.num_programs(1)

    @pl.when(j == 0)
    def _():
        st_ref[...] = s0_ref[0]

    group = HGRN_GROUP if n_chunks % HGRN_GROUP == 0 else 1

    def body(gi, carry):
        rows = []
        for s in range(group):
            c = gi * group + s
            c = (n_chunks - 1 - c) if rev else c
            rows.append(pl.ds(pl.multiple_of(c * CHUNK, CHUNK), CHUNK))
        pairs = [(r, hh) for r in rows for hh in range(HG_HEADS)]
        o, st_new = _hgrn_chunks([q_ref[0, hh, r, :] for r, hh in pairs] if need_o else None,
                                 [k_ref[0, hh, r, :] for r, hh in pairs],
                                 [v_ref[0, hh, r, :] for r, hh in pairs],
                                 [lf_ref[0, hh, r, :] for r, hh in pairs],
                                 [st_ref[hh] for hh in range(HG_HEADS)], tabs, rev, need_o)
        for hh in range(HG_HEADS):
            st_ref[hh] = st_new[hh]
        if need_o:
            for (r, hh), out in zip(pairs, o):
                if final:
                    tot = out + prev_ref[0, hh, r, :]
                    out = tot * lax.rsqrt(jnp.mean(tot * tot, axis=-1, keepdims=True) + EPS)
                    out = out * gate_ref[0, hh, r, :]
                o_ref[0, hh, r, :] = out
        return carry

    lax.fori_loop(0, n_chunks // group, body, 0)

    if want_state:
        @pl.when(j == nb - 1)
        def _():
            sout_ref[0] = st_ref[...]


def _hgrn(q, k, v, lf, s0, prev=None, gate=None, *, rev, need_o, want_state, name):
    b, hh, t, _ = k.shape
    tb = min(1024, t)
    nb = t // tb
    final = prev is not None
    tabs = _hgrn_tables(rev)

    seq = pl.BlockSpec((1, hh, tb, HEAD_DIM), lambda i, j: (i, 0, (nb - 1 - j) if rev else j, 0))
    state = pl.BlockSpec((1, hh, HEAD_DIM, HEAD_DIM), lambda i, j: (i, 0, 0, 0))
    args = list(tabs)
    in_specs = [pl.BlockSpec(a.shape, functools.partial(lambda nd, i, j: (0,) * nd, a.ndim)) for a in tabs]
    for a in ((q,) if need_o else ()) + (k, v, lf):
        args.append(a)
        in_specs.append(seq)
    args.append(s0)
    in_specs.append(state)
    if final:
        args += [prev, gate]
        in_specs += [seq, seq]
    out_shape, out_specs = [], []
    if need_o:
        out_shape.append(jax.ShapeDtypeStruct((b, hh, t, HEAD_DIM), F32))
        out_specs.append(seq)
    if want_state:
        out_shape.append(jax.ShapeDtypeStruct((b, hh, HEAD_DIM, HEAD_DIM), F32))
        out_specs.append(state)
    kern = functools.partial(_hgrn_kernel, rev=rev, need_o=need_o, final=final,
                             want_state=want_state, n_chunks=tb // CHUNK)
    est = 2 * (len(args) + len(out_shape)) * hh * tb * HEAD_DIM * 4
    return pl.pallas_call(
        kern,
        out_shape=out_shape,
        grid=(b, nb),
        in_specs=in_specs,
        out_specs=out_specs,
        scratch_shapes=[pltpu.VMEM((hh, HEAD_DIM, HEAD_DIM), F32)],
        compiler_params=pltpu.CompilerParams(
            dimension_semantics=("arbitrary", "arbitrary"),
            vmem_limit_bytes=_vmem_limit(est)),
        name=name,
    )(*args)


def _window_sum(x, w, stride, pos, extent):
    n = x.shape[0]
    whole_tiles = stride % SUBLANES == 0 and stride * extent == n

    def ahead(y, dist):
        if whole_tiles:
            k = min(dist, extent) * stride
            return jnp.concatenate([y[k:], jnp.zeros((k, y.shape[1]), y.dtype)], axis=0)
        return jnp.where(pos + dist < extent, pltpu.roll(y, (n - dist * stride) % n, 0), 0.0)

    def behind(y, dist):
        if whole_tiles:
            k = min(dist, extent) * stride
            return jnp.concatenate([jnp.zeros((k, y.shape[1]), y.dtype), y[:n - k]], axis=0)
        return jnp.where(pos >= dist, pltpu.roll(y, dist * stride, 0), 0.0)

    fwd, bwd, h = x, x, 1
    while h < w // 2:
        fwd = fwd + ahead(fwd, h)
        bwd = bwd + behind(bwd, h)
        h *= 2
    return fwd + behind(bwd, 1)


def _window_count(pos, w, extent):
    return (jnp.minimum(pos + w // 2, extent) - jnp.maximum(pos - w // 2, 0)).astype(F32)


def _pool_kernel(u_ref, pw_ref, ps_ref, o_ref):
    g = pl.program_id(1)
    n = u_ref.shape[2]
    tok = lax.broadcasted_iota(I32, (n, LANES), 0)
    col = tok & (GRID_W - 1)
    row = tok >> _log2(GRID_W)
    for gi, w in enumerate(POOL_WINDOWS):
        @pl.when(g == gi)
        def _(w=w):
            u = u_ref[0, 0]
            rows = n // GRID_W
            s = _window_sum(_window_sum(u, w, GRID_W, row, rows), w, 1, col, GRID_W)
            m = s / (_window_count(row, w, rows) * _window_count(col, w, GRID_W))
            o_ref[0, 0] = _dot((m - u).astype(BF16), pw_ref[0]) * ps_ref[0]


def _pool(u, pw, ps):
    b, g, t, c = u.shape
    blk = pl.BlockSpec((1, 1, t, c), lambda i, j: (i, j, 0, 0))
    return pl.pallas_call(
        _pool_kernel,
        out_shape=jax.ShapeDtypeStruct(u.shape, F32),
        grid=(b, g),
        in_specs=[blk,
                  pl.BlockSpec((1, c, c), lambda i, j: (j, 0, 0)),
                  pl.BlockSpec((1, 1, c), lambda i, j: (j, 0, 0))],
        out_specs=blk,
        compiler_params=pltpu.CompilerParams(
            dimension_semantics=("arbitrary", "arbitrary"),
            vmem_limit_bytes=_vmem_limit(12 * t * c * 4)),
        name="pool",
    )(u, pw, ps)


def _outproj_kernel(hg_ref, pool_ref, x_ref, w_ref, mod_ref, gpost_ref, gpre_ref, wr_ref, br_ref,
                    x1_ref, h2_ref, ti_ref, tw_ref, cnt_ref):
    d = x_ref.shape[-1]
    r = pl.program_id(0)

    @pl.when(jnp.logical_and(r == 0, pl.program_id(1) == 0))
    def _():
        cnt_ref[...] = jnp.zeros(cnt_ref.shape, F32)

    gt1 = mod_ref[pl.ds(r, 1), pl.ds(2 * d, d)]
    sh2 = mod_ref[pl.ds(r, 1), pl.ds(3 * d, d)]
    sc2 = mod_ref[pl.ds(r, 1), pl.ds(4 * d, d)]
    tm = x_ref.shape[1]
    sub = tm // OUTPROJ_SLICES
    slices = [pl.ds(i * sub, sub) for i in range(OUTPROJ_SLICES)]
    ys = []
    for rows in slices:
        cat = jnp.concatenate([hg_ref[0, hh, rows, :] for hh in range(HG_HEADS)]
                              + [pool_ref[0, g, rows, :] for g in range(len(POOL_WINDOWS))], axis=1)
        ys.append(_dot(cat.astype(BF16), w_ref[...]))
    logits = []
    for i, rows in enumerate(slices):
        x1 = x_ref[0, rows, :] + gt1 * _rms(ys[i], gpost_ref[...])
        x1_ref[0, rows, :] = x1
        h2 = _rms(x1, gpre_ref[...]) * (1.0 + sc2) + sh2
        _store_row_tiles(h2_ref.at[0, pl.ds(i * sub * SUBLANES, sub * SUBLANES), :], h2)
        logits.append(_dot(h2.astype(BF16), wr_ref[...]) + br_ref[...])

    lane = lax.broadcasted_iota(I32, (sub, LANES), 1).astype(F32)
    picked = jnp.zeros((1, LANES), F32)
    for i, rows in enumerate(slices):
        lg = jnp.where(lane < N_EXPERTS, logits[i], -jnp.inf)
        ti = jnp.zeros(lg.shape, F32)
        tw = jnp.zeros(lg.shape, F32)
        chosen = jnp.zeros(lg.shape, F32)
        top = None
        for jj in range(TOP_K):
            m = jnp.max(lg, axis=1, keepdims=True)
            idx = jnp.min(jnp.where(lg == m, lane, float(LANES)), axis=1, keepdims=True)
            lg = jnp.where(lane == idx, -jnp.inf, lg)
            chosen = jnp.where(lane == idx, 1.0, chosen)
            top = m if top is None else top
            ti = jnp.where(lane == jj, idx, ti)
            tw = jnp.where(lane == jj, jnp.exp(m - top), tw)
        ti_ref[0, rows, :] = ti.astype(I32)
        tw_ref[0, rows, :] = tw / jnp.sum(tw, axis=1, keepdims=True)
        picked = picked + jnp.sum(chosen, axis=0, keepdims=True)
    cnt_ref[...] += picked


def _outproj(hg, pool, x, w_out, mod, gpost, gpre, wr, br):
    b, t, d = x.shape
    tm = min(512, t)
    head = pl.BlockSpec((1, HG_HEADS, tm, HEAD_DIM), lambda i, j: (i, 0, j, 0))
    tok = pl.BlockSpec((1, tm, d), lambda i, j: (i, j, 0))
    lanes = pl.BlockSpec((1, tm, LANES), lambda i, j: (i, j, 0))
    vec = pl.BlockSpec((1, d), lambda i, j: (0, 0))
    est = 2 * (2 * tm * HG_WIDTH * 4 + 3 * tm * d * 4 + 2 * tm * LANES * 4) + 2 * d * d * 2 + 6 * tm * d * 4
    return pl.pallas_call(
        _outproj_kernel,
        out_shape=[jax.ShapeDtypeStruct((b, t, d), F32),
                   jax.ShapeDtypeStruct((b, t * d // LANES, LANES), F32),
                   jax.ShapeDtypeStruct((b, t, LANES), I32), jax.ShapeDtypeStruct((b, t, LANES), F32),
                   jax.ShapeDtypeStruct((1, LANES), F32)],
        grid=(b, t // tm),
        in_specs=[head, head, tok,
                  pl.BlockSpec(w_out.shape, lambda i, j: (0, 0)),
                  pl.BlockSpec(mod.shape, lambda i, j: (0, 0)),
                  vec, vec,
                  pl.BlockSpec(wr.shape, lambda i, j: (0, 0)),
                  pl.BlockSpec((1, LANES), lambda i, j: (0, 0))],
        out_specs=[tok, pl.BlockSpec((1, tm * d // LANES, LANES), lambda i, j: (i, j, 0)),
                   lanes, lanes, pl.BlockSpec((1, LANES), lambda i, j: (0, 0))],
        compiler_params=pltpu.CompilerParams(
            dimension_semantics=("arbitrary", "arbitrary"),
            vmem_limit_bytes=_vmem_limit(est)),
        name="outproj",
    )(hg, pool, x, w_out, mod, gpost, gpre, wr, br)


def _rank_kernel(ti_ref, start_ref, dest_ref, carry_ref):
    i = pl.program_id(0)

    @pl.when(i == 0)
    def _():
        carry_ref[...] = start_ref[...]

    ti = ti_ref[...]
    tm = ti.shape[0]
    lane = lax.broadcasted_iota(I32, (tm, LANES), 1)
    onehot = [lane == ti[:, jj:jj + 1] for jj in range(TOP_K)]
    cnt = sum(jnp.where(oh, 1.0, 0.0) for oh in onehot)
    row = lax.broadcasted_iota(I32, (tm, tm), 0)
    col = lax.broadcasted_iota(I32, (tm, tm), 1)
    before = jnp.where(col < row, 1.0, 0.0).astype(BF16)
    prefix = _dot(before, cnt.astype(BF16)) + carry_ref[...]
    rank = jnp.zeros((tm, LANES), F32)
    for jj in range(TOP_K):
        rj = jnp.sum(jnp.where(onehot[jj], prefix, 0.0), axis=1, keepdims=True)
        rank = jnp.where(lane == jj, rj, rank)
    dest_ref[...] = rank.astype(I32) * SUBLANES
    carry_ref[...] += jnp.sum(cnt, axis=0, keepdims=True)


def _rank(ti, starts):
    t = ti.shape[0]
    tm = min(512, t)
    return pl.pallas_call(
        _rank_kernel,
        out_shape=jax.ShapeDtypeStruct((t, LANES), I32),
        grid=(t // tm,),
        in_specs=[pl.BlockSpec((tm, LANES), lambda i: (i, 0)),
                  pl.BlockSpec((1, LANES), lambda i: (0, 0))],
        out_specs=pl.BlockSpec((tm, LANES), lambda i: (i, 0)),
        scratch_shapes=[pltpu.VMEM((1, LANES), F32)],
        compiler_params=pltpu.CompilerParams(dimension_semantics=("arbitrary",)),
        name="rank",
    )(ti, starts)


def _tile_rows(start):
    return pl.ds(pl.multiple_of(start, SUBLANES), SUBLANES)


def _gap_copies(e, gap_start_ref, gap_len_ref, zeros, xs_ref, sem):
    off, n = gap_start_ref[e], gap_len_ref[e]
    out = []
    for bit in range(_log2(MOE_BM)):
        size = (1 << bit) * SUBLANES
        used = (n >> bit) & 1
        out.append((used == 1,
                    pltpu.make_async_copy(zeros.at[pl.ds(0, size), :],
                                          xs_ref.at[pl.ds(pl.multiple_of(off * SUBLANES, SUBLANES), size), :],
                                          sem)))
        off = off + used * (1 << bit)
    return out


def _dispatch_kernel(dest_ref, gap_start_ref, gap_len_ref, n_live_ref, h_ref, xs_ref, zeros, sem, zsem):
    tm = h_ref.shape[0] // SUBLANES
    block_rows = MOE_BM * SUBLANES
    n_blocks = xs_ref.shape[0] // block_rows

    @pl.when(pl.program_id(0) == 0)
    def _():
        zeros[...] = jnp.zeros(zeros.shape, zeros.dtype)

        def tail_copy(blk):
            return pltpu.make_async_copy(
                zeros, xs_ref.at[pl.ds(pl.multiple_of(blk * block_rows, block_rows), block_rows), :], zsem)

        def tail_start(blk, carry):
            tail_copy(blk).start()
            return carry

        def tail_wait(blk, carry):
            tail_copy(blk).wait()
            return carry

        lax.fori_loop(n_live_ref[0], n_blocks, tail_start, 0)
        lax.fori_loop(n_live_ref[0], n_blocks, tail_wait, 0)

        def fill(e, carry):
            for used, cp in _gap_copies(e, gap_start_ref, gap_len_ref, zeros, xs_ref, zsem):
                @pl.when(used)
                def _(cp=cp):
                    cp.start()
            return carry

        def drain(e, carry):
            for used, cp in _gap_copies(e, gap_start_ref, gap_len_ref, zeros, xs_ref, zsem):
                @pl.when(used)
                def _(cp=cp):
                    cp.wait()
            return carry

        lax.fori_loop(0, N_EXPERTS, fill, 0)
        lax.fori_loop(0, N_EXPERTS, drain, 0)

    def start(r, carry):
        for jj in range(TOP_K):
            pltpu.make_async_copy(h_ref.at[_tile_rows(r * SUBLANES), :],
                                  xs_ref.at[_tile_rows(dest_ref[r * TOP_K + jj]), :],
                                  sem).start(priority=jj % 2)
        return carry

    lax.fori_loop(0, tm, start, 0, unroll=4)
    for jj in range(TOP_K):
        pltpu.make_async_copy(h_ref, xs_ref.at[pl.ds(0, tm * SUBLANES), :], sem).wait()


def _dispatch(dest, gap_start, gap_len, n_live, h2, n_sorted):
    rows, l = h2.shape
    t = rows // SUBLANES
    tm = min(512, t)
    smem = pl.BlockSpec(memory_space=pltpu.SMEM)
    return pl.pallas_call(
        _dispatch_kernel,
        out_shape=jax.ShapeDtypeStruct((n_sorted * SUBLANES, l), h2.dtype),
        grid=(t // tm,),
        in_specs=[pl.BlockSpec((tm * TOP_K,), lambda i: (i,), memory_space=pltpu.SMEM), smem, smem, smem,
                  pl.BlockSpec((tm * SUBLANES, l), lambda i: (i, 0))],
        out_specs=pl.BlockSpec(memory_space=pl.ANY),
        scratch_shapes=[pltpu.VMEM((MOE_BM * SUBLANES, l), h2.dtype),
                        pltpu.SemaphoreType.DMA, pltpu.SemaphoreType.DMA],
        compiler_params=pltpu.CompilerParams(dimension_semantics=("arbitrary",)),
        name="dispatch",
    )(dest, gap_start, gap_len, n_live, h2)


def _moe_kernel(block_e_ref, next_e_ref, n_live_ref,
                x_ref, wg_ref, wu_ref, wd_ref, bg_ref, bu_ref, bd_ref, y_ref,
                wgs, wus, wds, wgb, wub, wdb, sem):
    i = pl.program_id(0)
    live = i < n_live_ref[0]
    e = block_e_ref[i]
    changed = jnp.logical_or(i == 0, e != block_e_ref[jnp.maximum(i - 1, 0)])

    def fetch(expert):
        return [pltpu.make_async_copy(w_ref.at[expert], stage, sem.at[n])
                for n, (w_ref, stage) in enumerate(((wg_ref, wgs), (wu_ref, wus), (wd_ref, wds)))]

    @pl.when(i == 0)
    def _():
        for cp in fetch(e):
            cp.start()

    @pl.when(jnp.logical_and(live, changed))
    def _():
        for cp in fetch(e):
            cp.wait()
        def cast(c, carry):
            rows = pl.ds(pl.multiple_of(c * LANES, LANES), LANES)
            for src, dst in ((wgs, wgb), (wus, wub), (wds, wdb)):
                dst[rows, :] = src[rows, :].astype(BF16)
            return carry

        lax.fori_loop(0, wgs.shape[0] // LANES, cast, 0)

        @pl.when(next_e_ref[i] != e)
        def _():
            for cp in fetch(next_e_ref[i]):
                cp.start()

    @pl.when(live)
    def _():
        x = _load_row_tiles(x_ref).astype(BF16)
        gate = jnp.minimum(_dot(x, wgb[...]) + bg_ref[...], SWIGLU_LIMIT)
        up = jnp.clip(_dot(x, wub[...]) + bu_ref[...], -SWIGLU_LIMIT, SWIGLU_LIMIT)
        hmid = (up + 1.0) * gate * _sigmoid(SWIGLU_ALPHA * gate)
        _store_row_tiles(y_ref, _dot(hmid.astype(BF16), wdb[...]) + bd_ref[...])

    @pl.when(jnp.logical_not(live))
    def _():
        y_ref[...] = jnp.zeros(y_ref.shape, y_ref.dtype)


def _moe(block_e, next_e, n_live, xs, wg, wu, wd, bg, bu, bd):
    d, f = wg.shape[-2:]
    n_blocks = block_e.shape[0]
    rows_in = pl.BlockSpec((MOE_BM * SUBLANES, LANES),
                           lambda i, be, ne, nl: (jnp.minimum(i, nl[0] - 1), 0))
    rows_out = pl.BlockSpec((MOE_BM * SUBLANES, LANES), lambda i, be, ne, nl: (i, 0))
    hbm = pl.BlockSpec(memory_space=pl.ANY)

    def bspec(n):
        return pl.BlockSpec((None, 1, n), lambda i, be, ne, nl: (be[i], 0, 0))

    est = 3 * d * f * 4 + 3 * d * f * 2 + 4 * MOE_BM * d * 4 + 4 * MOE_BM * f * 4
    return pl.pallas_call(
        _moe_kernel,
        out_shape=jax.ShapeDtypeStruct(xs.shape, F32),
        grid_spec=pltpu.PrefetchScalarGridSpec(
            num_scalar_prefetch=3,
            grid=(n_blocks,),
            in_specs=[rows_in, hbm, hbm, hbm, bspec(f), bspec(f), bspec(d)],
            out_specs=rows_out,
            scratch_shapes=[pltpu.VMEM((d, f), F32), pltpu.VMEM((d, f), F32), pltpu.VMEM((f, d), F32),
                            pltpu.VMEM((d, f), BF16), pltpu.VMEM((d, f), BF16), pltpu.VMEM((f, d), BF16),
                            pltpu.SemaphoreType.DMA((3,))]),
        compiler_params=pltpu.CompilerParams(
            dimension_semantics=("arbitrary",),
            vmem_limit_bytes=_vmem_limit(est)),
        name="moe",
    )(block_e, next_e, n_live, xs, wg, wu, wd, bg, bu, bd)


def _moe_layout(counts, n_rows):
    n_blocks = n_rows // MOE_BM + N_EXPERTS
    padded = (counts + MOE_BM - 1) // MOE_BM * MOE_BM
    ends = jnp.cumsum(padded)
    starts = ends - padded
    block_end = ends // MOE_BM
    i = jnp.arange(n_blocks, dtype=I32)
    ids = jnp.arange(N_EXPERTS, dtype=I32)
    last_e = jnp.max(jnp.where(counts > 0, ids, 0))
    block_e = jnp.minimum(jnp.sum(block_end[None, :] <= i[:, None], axis=1), last_e).astype(I32)
    later = jnp.logical_and(ids[None, :] > ids[:, None], counts[None, :] > 0)
    following = jnp.min(jnp.where(later, ids[None, :], N_EXPERTS), axis=1)
    following = jnp.where(following == N_EXPERTS, ids, following)
    next_e = jnp.sum(jnp.where(block_e[:, None] == ids[None, :], following[None, :], 0), axis=1)
    return (starts.astype(I32), (starts + counts).astype(I32), (padded - counts).astype(I32),
            block_e, next_e.astype(I32), block_end[-1:].astype(I32), n_blocks)


def _combine_kernel(dest_ref, dnext_ref, ys_ref, tw_ref, x1_ref, mod_ref, gpost_ref, o_ref, buf, sem,
                    *, tiles_per_batch):
    tm, d = x1_ref.shape
    i = pl.program_id(0)
    slot = i % 2

    def issue(d_ref, s):
        def start(r, carry):
            for jj in range(TOP_K):
                pltpu.make_async_copy(ys_ref.at[_tile_rows(d_ref[r * TOP_K + jj]), :],
                                      buf.at[s, jj, _tile_rows(r * SUBLANES), :],
                                      sem.at[s]).start(priority=jj % 2)
            return carry

        lax.fori_loop(0, tm, start, 0, unroll=4)

    @pl.when(i == 0)
    def _():
        issue(dest_ref, 0)

    @pl.when(i + 1 < pl.num_programs(0))
    def _():
        issue(dnext_ref, 1 - slot)

    for jj in range(TOP_K):
        pltpu.make_async_copy(ys_ref.at[pl.ds(0, tm * SUBLANES), :], buf.at[slot, jj],
                              sem.at[slot]).wait()

    tw = tw_ref[...]
    y = tw[:, 0:1] * _load_row_tiles(buf.at[slot, 0])
    for jj in range(1, TOP_K):
        y = y + tw[:, jj:jj + 1] * _load_row_tiles(buf.at[slot, jj])
    r = i // tiles_per_batch
    gt2 = mod_ref[pl.ds(r, 1), pl.ds(5 * d, d)]
    o_ref[...] = x1_ref[...] + gt2 * _rms(y, gpost_ref[...])


def _combine(dest, ys, tw, x1, mod, gpost, tokens_per_batch):
    t, d = x1.shape
    tm = min(512, tokens_per_batch)
    n_tiles = t // tm
    kern = functools.partial(_combine_kernel, tiles_per_batch=tokens_per_batch // tm)
    est = 2 * TOP_K * tm * d * 4 + 2 * (2 * tm * d * 4 + tm * LANES * 4) + 3 * tm * d * 4
    return pl.pallas_call(
        kern,
        out_shape=jax.ShapeDtypeStruct((t, d), F32),
        grid=(n_tiles,),
        in_specs=[pl.BlockSpec((tm * TOP_K,), lambda i: (i,), memory_space=pltpu.SMEM),
                  pl.BlockSpec((tm * TOP_K,), lambda i: (jnp.minimum(i + 1, n_tiles - 1),),
                               memory_space=pltpu.SMEM),
                  pl.BlockSpec(memory_space=pl.ANY),
                  pl.BlockSpec((tm, LANES), lambda i: (i, 0)),
                  pl.BlockSpec((tm, d), lambda i: (i, 0)),
                  pl.BlockSpec(mod.shape, lambda i: (0, 0)),
                  pl.BlockSpec((1, d), lambda i: (0, 0))],
        out_specs=pl.BlockSpec((tm, d), lambda i: (i, 0)),
        scratch_shapes=[pltpu.VMEM((2, TOP_K, tm * SUBLANES, LANES), F32),
                        pltpu.SemaphoreType.DMA((2,))],
        compiler_params=pltpu.CompilerParams(
            dimension_semantics=("arbitrary",),
            vmem_limit_bytes=_vmem_limit(est)),
        name="combine",
    )(dest, dest, ys, tw, x1, mod, gpost)


def kernel(x, c, ctx, c_ctx, w_ada, b_ada, g_pre_mix, g_post_mix, g_pre_ffn, g_post_ffn,
           w_in, w_out, hgrn_lb_logits, hgrn_norm, pool_w, pool_scale,
           w_router, b_router, w_gate, b_gate, w_up, b_up, w_down, b_down):
    b, s, d = x.shape
    layer = 0
    n_e = w_router.shape[-1]

    cc = jnp.concatenate([c, c_ctx[None, :]], axis=0)
    cc = jnp.pad(cc, ((0, -(b + 1) % SUBLANES), (0, 0)))
    mod = _ada(cc, w_ada[layer], b_ada[layer][None, :])

    w_in_b = w_in[layer].astype(BF16)
    lbl = hgrn_lb_logits[:2].reshape(4, HG_WIDTH)
    gn = hgrn_norm[layer][None, :]
    gpre = g_pre_mix[layer][None, :]

    lff_c, kf_c, lfb_c, kb_c, v_c = _inproj(ctx, mod, gpre, w_in_b, lbl, gn, mod_row=b, full=False)
    zeros = jnp.zeros((b, HG_HEADS, HEAD_DIM, HEAD_DIM), F32)
    (s_f,) = _hgrn(None, kf_c, v_c, lff_c, zeros, rev=False, need_o=False, want_state=True,
                   name="hgrn_ctx_fwd")
    (s_b,) = _hgrn(None, kb_c, v_c, lfb_c, zeros, rev=True, need_o=False, want_state=True,
                   name="hgrn_ctx_bwd")

    q, lff, kf, lfb, kb, v, gate, u = _inproj(x, mod, gpre, w_in_b, lbl, gn, mod_row=None, full=True)
    (o_f,) = _hgrn(q, kf, v, lff, s_f, rev=False, need_o=True, want_state=False, name="hgrn_fwd")
    (o_hg,) = _hgrn(q, kb, v, lfb, s_b, o_f, gate, rev=True, need_o=True, want_state=False,
                    name="hgrn_bwd")
    o_pool = _pool(u, pool_w[layer].astype(BF16), pool_scale[layer].reshape(len(POOL_WINDOWS), 1, POOL_GROUP))

    wr = jnp.pad(w_router[layer], ((0, 0), (0, LANES - n_e))).astype(BF16)
    br = jnp.pad(b_router[layer], (0, LANES - n_e))[None, :]
    x1, h2, ti, tw, cnt = _outproj(o_hg, o_pool, x, w_out[layer].astype(BF16), mod,
                                   g_post_mix[layer][None, :], g_pre_ffn[layer][None, :], wr, br)

    t = b * s
    starts, gap_start, gap_len, block_e, next_e, n_live, n_blocks = _moe_layout(
        cnt[0, :n_e].astype(I32), t * TOP_K)
    starts = jnp.pad(starts.astype(F32), (0, LANES - n_e))[None, :]
    dest = _rank(ti.reshape(t, LANES), starts)[:, :TOP_K].reshape(t * TOP_K)

    xs = _dispatch(dest, gap_start, gap_len, n_live, h2.reshape(t * d // LANES, LANES), n_blocks * MOE_BM)
    ys = _moe(block_e, next_e, n_live, xs, w_gate[layer], w_up[layer], w_down[layer],
              b_gate[layer][:, None, :], b_up[layer][:, None, :], b_down[layer][:, None, :])
    out = _combine(dest, ys, tw.reshape(t, LANES), x1.reshape(t, d), mod,
                   g_post_ffn[layer][None, :], s)
    return out.reshape(b, s, d)
```

```python
import functools
import math

import numpy as np
import jax
import jax.numpy as jnp
from jax import lax
from jax.experimental import pallas as pl
from jax.experimental.pallas import tpu as pltpu

F32 = jnp.float32
BF16 = jnp.bfloat16
I32 = jnp.int32

GRID_W = 64
HG_HEADS = 4
HEAD_DIM = 128
HG_WIDTH = HG_HEADS * HEAD_DIM
POOL_WINDOWS = (2, 4, 8, 16)
POOL_GROUP = 128
N_EXPERTS = 32
TOP_K = 4
SWIGLU_LIMIT = 7.0
SWIGLU_ALPHA = 1.702
EPS = 1e-6

LANES = 128
SUBLANES = 8
V7X_VMEM_BYTES = 64 * 1024 * 1024

CHUNK = 64
HGRN_GROUP = 4
LEVELS = (32, 16, 8, 4, 2, 1)
FINE_LEVELS = (2, 1)
MOE_BM = 256
OUTPROJ_SLICES = 4
LOG2_E = math.log2(math.e)


def _vmem_limit(nbytes):
    return int(min(nbytes * 3 // 2 + (4 << 20), V7X_VMEM_BYTES - (6 << 20)))


def _log2(n):
    assert n & (n - 1) == 0
    return n.bit_length() - 1


def _sigmoid(x):
    return 1.0 / (1.0 + jnp.exp(-x))


def _rms(x, gain):
    return x * lax.rsqrt(jnp.mean(x * x, axis=-1, keepdims=True) + EPS) * gain


def _dot(a, b):
    return jnp.dot(a, b, preferred_element_type=F32)


ROW_TILE = SUBLANES * LANES


def _store_row_tiles(ref, val):
    rows = val.shape[0]
    for c in range(SUBLANES):
        ref[pl.ds(c, rows, stride=SUBLANES), :] = val[:, c * LANES:(c + 1) * LANES]


def _load_row_tiles(ref):
    rows = ref.shape[0] // SUBLANES
    return jnp.concatenate([ref[pl.ds(c, rows, stride=SUBLANES), :] for c in range(SUBLANES)], axis=1)


def _dot_nt(a, b):
    return lax.dot_general(a, b, (((1,), (1,)), ((), ())), preferred_element_type=F32)


def _ada_kernel(c_ref, w_ref, b_ref, o_ref):
    c = c_ref[...]
    s = (c * _sigmoid(c)).astype(BF16)
    o_ref[...] = _dot(s, w_ref[...].astype(BF16)) + b_ref[...]


def _ada(cc, w, b):
    rows, d = cc.shape
    n = w.shape[1]
    tn = 1536 if n % 1536 == 0 else n
    return pl.pallas_call(
        _ada_kernel,
        out_shape=jax.ShapeDtypeStruct((rows, n), F32),
        grid=(n // tn,),
        in_specs=[pl.BlockSpec((rows, d), lambda j: (0, 0)),
                  pl.BlockSpec((d, tn), lambda j: (0, j)),
                  pl.BlockSpec((1, tn), lambda j: (0, j))],
        out_specs=pl.BlockSpec((rows, tn), lambda j: (0, j)),
        compiler_params=pltpu.CompilerParams(
            dimension_semantics=("arbitrary",),
            vmem_limit_bytes=_vmem_limit(2 * d * tn * 4 + d * tn * 2)),
        name="ada",
    )(cc, w, b)


def _inproj_kernel(x_ref, mod_ref, gpre_ref, w_ref, lbl_ref, gn_ref, *outs, mod_row, full):
    d = x_ref.shape[-1]
    r = pl.program_id(0) if mod_row is None else mod_row
    sh = mod_ref[pl.ds(r, 1), pl.ds(0, d)]
    sc = mod_ref[pl.ds(r, 1), pl.ds(d, d)]
    tm = x_ref.shape[1]
    n_slices = 2 if tm % (2 * SUBLANES) == 0 else 1
    sub = tm // n_slices
    slices = [pl.ds(i * sub, sub) for i in range(n_slices)]
    hs = [(_rms(x_ref[0, rows, :], gpre_ref[...]) * (1.0 + sc) + sh).astype(BF16) for rows in slices]

    def proj(g):
        return [_dot(h, w_ref[:, g * HG_WIDTH:(g + 1) * HG_WIDTH]) for h in hs]

    def put(ref, vals):
        for rows, val in zip(slices, vals):
            for hh in range(HG_HEADS):
                ref[0, hh, rows, :] = val[:, hh * HEAD_DIM:(hh + 1) * HEAD_DIM]

    def lower_bound(direction):
        l0 = lbl_ref[pl.ds(direction, 1), :]
        l1 = lbl_ref[pl.ds(2 + direction, 1), :]
        m = jnp.maximum(l0, l1)
        e0 = jnp.exp(l0 - m)
        return e0 / (e0 + jnp.exp(l1 - m))

    if full:
        q_o, lff_o, kf_o, lfb_o, kb_o, v_o, gate_o, u_o = outs
        put(q_o, [q * _sigmoid(q) for q in proj(0)])
    else:
        lff_o, kf_o, lfb_o, kb_o, v_o = outs
    for direction, (lf_o, k_o) in enumerate(((lff_o, kf_o), (lfb_o, kb_o))):
        lb = lower_bound(direction)
        sgs = [_sigmoid(z) for z in proj(1 + direction)]
        put(lf_o, [jnp.log(lb + (1.0 - lb) * sg) for sg in sgs])
        put(k_o, [(1.0 - lb) * (1.0 - sg) for sg in sgs])
    put(v_o, proj(3))
    if full:
        put(gate_o, [gn_ref[...] * _sigmoid(z) for z in proj(4)])
        put(u_o, proj(5))


def _inproj(x, mod, gpre, w_in, lbl, gn, *, mod_row, full):
    b, t, d = x.shape
    tm = min(512, t)
    n_out = 8 if full else 5
    n_cols = w_in.shape[1]
    head = pl.BlockSpec((1, HG_HEADS, tm, HEAD_DIM), lambda i, j: (i, 0, j, 0))
    kern = functools.partial(_inproj_kernel, mod_row=mod_row, full=full)
    est = 2 * tm * d * 4 + 2 * d * n_cols * 2 + n_out * 2 * tm * HG_WIDTH * 4 + 4 * tm * HG_WIDTH * 4
    return pl.pallas_call(
        kern,
        out_shape=[jax.ShapeDtypeStruct((b, HG_HEADS, t, HEAD_DIM), F32)] * n_out,
        grid=(b, t // tm),
        in_specs=[pl.BlockSpec((1, tm, d), lambda i, j: (i, j, 0)),
                  pl.BlockSpec(mod.shape, lambda i, j: (0, 0)),
                  pl.BlockSpec((1, d), lambda i, j: (0, 0)),
                  pl.BlockSpec(w_in.shape, lambda i, j: (0, 0)),
                  pl.BlockSpec(lbl.shape, lambda i, j: (0, 0)),
                  pl.BlockSpec((1, HG_WIDTH), lambda i, j: (0, 0))],
        out_specs=[head] * n_out,
        compiler_params=pltpu.CompilerParams(
            dimension_semantics=("arbitrary", "arbitrary"),
            vmem_limit_bytes=_vmem_limit(est)),
        name="inproj_full" if full else "inproj_ctx",
    )(x, mod, gpre, w_in, lbl, gn)


def _hgrn_tables(rev):
    row = np.arange(CHUNK)[:, None]
    col = np.arange(CHUNK)[None, :]
    tri = ((col >= row) if rev else (col <= row)).astype(np.float32)
    blocks, masks = [tri], []
    for h in LEVELS:
        if h in FINE_LEVELS:
            blocks.append(tri - tri[_ref_row(np.arange(CHUNK), h, rev)])
        is_q = ((row // h) % 2) == (0 if rev else 1)
        key_half = ((col // h) % 2) == (1 if rev else 0)
        masks.append(((row // (2 * h)) == (col // (2 * h))) & is_q & key_half)
    pair = np.stack([np.concatenate(masks[i:i + 2], axis=1) for i in range(0, len(LEVELS), 2)])
    cums = np.concatenate(blocks, axis=0)
    return jnp.asarray(np.concatenate([cums, cums], axis=1), BF16), jnp.asarray(pair, F32)


def _ref_row(t, h, rev):
    return (t // (2 * h)) * (2 * h) + (h if rev else h - 1)


def _hgrn_chunks(qs, ks, vs, lfs, sts, tabs, rev, need_o):
    cums_ref, pair_ref = tabs
    n_heads = len(sts)
    heads = range(len(ks))
    n_pairs = len(LEVELS) // 2
    n_cum = CHUNK * (1 + len(FINE_LEVELS)) if need_o else CHUNK
    cums = cums_ref[0:n_cum, :]

    his, los = [], []
    for h in heads:
        lf2 = lfs[h] * LOG2_E
        hi = lf2.astype(BF16)
        his.append(hi)
        los.append((lf2 - hi.astype(F32)).astype(BF16))
    both = _dot(cums, jnp.concatenate([jnp.concatenate(his, axis=1), jnp.concatenate(los, axis=1)], axis=0))
    bd = [both[:, h * LANES:(h + 1) * LANES] for h in heads]

    k16 = [ks[h].astype(BF16) for h in heads]
    v16 = [vs[h].astype(BF16) for h in heads]
    b = [bd[h][0:CHUNK] for h in heads]

    def level_diff(h, level):
        if level in FINE_LEVELS:
            i = FINE_LEVELS.index(level)
            return bd[h][CHUNK * (1 + i):CHUNK * (2 + i)]
        refs = [_ref_row(m * 2 * level, level, rev) for m in range(CHUNK // (2 * level))]
        pieces = [jnp.broadcast_to(b[h][r:r + 1, :], (2 * level, LANES)) for r in refs]
        return b[h] - (pieces[0] if len(pieces) == 1 else jnp.concatenate(pieces, axis=0))
    edge = [b[h][0:1, :] if rev else b[h][CHUNK - 1:CHUNK, :] for h in heads]
    kd = [k16[h] * jnp.exp2(edge[h] - b[h]).astype(BF16) for h in heads]
    grow = [lax.dot_general(v16[h], kd[h], (((0,), (0,)), ((), ())), preferred_element_type=F32)
            for h in heads]
    q16 = [qs[h].astype(BF16) for h in heads] if need_o else None
    st = list(sts)
    carried = []
    for h in heads:
        if need_o:
            carried.append(_dot_nt(q16[h] * jnp.exp2(b[h]).astype(BF16), st[h % n_heads].astype(BF16)))
        st[h % n_heads] = st[h % n_heads] * jnp.exp2(edge[h]) + grow[h]
    if not need_o:
        return None, st

    zero = jnp.zeros((CHUNK, LANES), BF16)
    att = [[] for _ in heads]
    for p in range(n_pairs):
        lhs, rhs = [], []
        for h in heads:
            qe, ke = [], []
            for i in (2 * p, 2 * p + 1):
                e = jnp.exp2(-jnp.abs(level_diff(h, LEVELS[i]))).astype(BF16)
                qe.append(q16[h] * e)
                ke.append(k16[h] * e)
            lhs.append(jnp.concatenate(qe, axis=1))
            rhs.append(jnp.concatenate([jnp.concatenate([ke[0], zero], axis=1),
                                        jnp.concatenate([zero, ke[1]], axis=1)], axis=0))
        scores = [_dot_nt(lhs[h], rhs[h]) for h in heads]
        for h in heads:
            att[h].append((scores[h] * pair_ref[p]).astype(BF16))
    o = []
    for h in heads:
        vals = jnp.concatenate([v16[h]] * (2 * n_pairs), axis=0)
        same_row = jnp.sum(qs[h] * ks[h], axis=1, keepdims=True) * vs[h]
        o.append(carried[h] + _dot(jnp.concatenate(att[h], axis=1), vals) + same_row)
    return o, st


def _hgrn_kernel(*refs, rev, need_o, final, want_state, n_chunks):
    it = iter(refs)
    tabs = tuple(next(it) for _ in range(2))
    q_ref = next(it) if need_o else None
    k_ref, v_ref, lf_ref, s0_ref = next(it), next(it), next(it), next(it)
    prev_ref = next(it) if final else None
    gate_ref = next(it) if final else None
    o_ref = next(it) if need_o else None
    sout_ref = next(it) if want_state else None
    st_ref = next(it)

    j = pl.program_id(1)
    nb = pl.num_programs(1)

    @pl.when(j == 0)
    def _():
        st_ref[...] = s0_ref[0]

    group = HGRN_GROUP if n_chunks % HGRN_GROUP == 0 else 1

    def body(gi, carry):
        rows = []
        for s in range(group):
            c = gi * group + s
            c = (n_chunks - 1 - c) if rev else c
            rows.append(pl.ds(pl.multiple_of(c * CHUNK, CHUNK), CHUNK))
        pairs = [(r, hh) for r in rows for hh in range(HG_HEADS)]
        o, st_new = _hgrn_chunks([q_ref[0, hh, r, :] for r, hh in pairs] if need_o else None,
                                 [k_ref[0, hh, r, :] for r, hh in pairs],
                                 [v_ref[0, hh, r, :] for r, hh in pairs],
                                 [lf_ref[0, hh, r, :] for r, hh in pairs],
                                 [st_ref[hh] for hh in range(HG_HEADS)], tabs, rev, need_o)
        for hh in range(HG_HEADS):
            st_ref[hh] = st_new[hh]
        if need_o:
            for (r, hh), out in zip(pairs, o):
                if final:
                    tot = out + prev_ref[0, hh, r, :]
                    out = tot * lax.rsqrt(jnp.mean(tot * tot, axis=-1, keepdims=True) + EPS)
                    out = out * gate_ref[0, hh, r, :]
                o_ref[0, hh, r, :] = out
        return carry

    lax.fori_loop(0, n_chunks // group, body, 0)

    if want_state:
        @pl.when(j == nb - 1)
        def _():
            sout_ref[0] = st_ref[...]


def _hgrn(q, k, v, lf, s0, prev=None, gate=None, *, rev, need_o, want_state, name):
    b, hh, t, _ = k.shape
    tb = min(1024, t)
    nb = t // tb
    final = prev is not None
    tabs = _hgrn_tables(rev)

    seq = pl.BlockSpec((1, hh, tb, HEAD_DIM), lambda i, j: (i, 0, (nb - 1 - j) if rev else j, 0))
    state = pl.BlockSpec((1, hh, HEAD_DIM, HEAD_DIM), lambda i, j: (i, 0, 0, 0))
    args = list(tabs)
    in_specs = [pl.BlockSpec(a.shape, functools.partial(lambda nd, i, j: (0,) * nd, a.ndim)) for a in tabs]
    for a in ((q,) if need_o else ()) + (k, v, lf):
        args.append(a)
        in_specs.append(seq)
    args.append(s0)
    in_specs.append(state)
    if final:
        args += [prev, gate]
        in_specs += [seq, seq]
    out_shape, out_specs = [], []
    if need_o:
        out_shape.append(jax.ShapeDtypeStruct((b, hh, t, HEAD_DIM), F32))
        out_specs.append(seq)
    if want_state:
        out_shape.append(jax.ShapeDtypeStruct((b, hh, HEAD_DIM, HEAD_DIM), F32))
        out_specs.append(state)
    kern = functools.partial(_hgrn_kernel, rev=rev, need_o=need_o, final=final,
                             want_state=want_state, n_chunks=tb // CHUNK)
    est = 2 * (len(args) + len(out_shape)) * hh * tb * HEAD_DIM * 4
    return pl.pallas_call(
        kern,
        out_shape=out_shape,
        grid=(b, nb),
        in_specs=in_specs,
        out_specs=out_specs,
        scratch_shapes=[pltpu.VMEM((hh, HEAD_DIM, HEAD_DIM), F32)],
        compiler_params=pltpu.CompilerParams(
            dimension_semantics=("arbitrary", "arbitrary"),
            vmem_limit_bytes=_vmem_limit(est)),
        name=name,
    )(*args)


def _window_sum(x, w, stride, pos, extent):
    n = x.shape[0]
    whole_tiles = stride % SUBLANES == 0 and stride * extent == n

    def ahead(y, dist):
        if whole_tiles:
            k = min(dist, extent) * stride
            return jnp.concatenate([y[k:], jnp.zeros((k, y.shape[1]), y.dtype)], axis=0)
        return jnp.where(pos + dist < extent, pltpu.roll(y, (n - dist * stride) % n, 0), 0.0)

    def behind(y, dist):
        if whole_tiles:
            k = min(dist, extent) * stride
            return jnp.concatenate([jnp.zeros((k, y.shape[1]), y.dtype), y[:n - k]], axis=0)
        return jnp.where(pos >= dist, pltpu.roll(y, dist * stride, 0), 0.0)

    fwd, bwd, h = x, x, 1
    while h < w // 2:
        fwd = fwd + ahead(fwd, h)
        bwd = bwd + behind(bwd, h)
        h *= 2
    return fwd + behind(bwd, 1)


def _window_count(pos, w, extent):
    return (jnp.minimum(pos + w // 2, extent) - jnp.maximum(pos - w // 2, 0)).astype(F32)


def _pool_kernel(u_ref, pw_ref, ps_ref, o_ref):
    g = pl.program_id(1)
    n = u_ref.shape[2]
    tok = lax.broadcasted_iota(I32, (n, LANES), 0)
    col = tok & (GRID_W - 1)
    row = tok >> _log2(GRID_W)
    for gi, w in enumerate(POOL_WINDOWS):
        @pl.when(g == gi)
        def _(w=w):
            u = u_ref[0, 0]
            rows = n // GRID_W
            s = _window_sum(_window_sum(u, w, GRID_W, row, rows), w, 1, col, GRID_W)
            m = s / (_window_count(row, w, rows) * _window_count(col, w, GRID_W))
            o_ref[0, 0] = _dot((m - u).astype(BF16), pw_ref[0]) * ps_ref[0]


def _pool(u, pw, ps):
    b, g, t, c = u.shape
    blk = pl.BlockSpec((1, 1, t, c), lambda i, j: (i, j, 0, 0))
    return pl.pallas_call(
        _pool_kernel,
        out_shape=jax.ShapeDtypeStruct(u.shape, F32),
        grid=(b, g),
        in_specs=[blk,
                  pl.BlockSpec((1, c, c), lambda i, j: (j, 0, 0)),
                  pl.BlockSpec((1, 1, c), lambda i, j: (j, 0, 0))],
        out_specs=blk,
        compiler_params=pltpu.CompilerParams(
            dimension_semantics=("arbitrary", "arbitrary"),
            vmem_limit_bytes=_vmem_limit(12 * t * c * 4)),
        name="pool",
    )(u, pw, ps)


def _outproj_kernel(hg_ref, pool_ref, x_ref, w_ref, mod_ref, gpost_ref, gpre_ref, wr_ref, br_ref,
                    x1_ref, h2_ref, ti_ref, tw_ref, chosen_ref, cnt_ref):
    d = x_ref.shape[-1]
    r = pl.program_id(0)

    @pl.when(jnp.logical_and(r == 0, pl.program_id(1) == 0))
    def _():
        cnt_ref[...] = jnp.zeros(cnt_ref.shape, F32)

    gt1 = mod_ref[pl.ds(r, 1), pl.ds(2 * d, d)]
    sh2 = mod_ref[pl.ds(r, 1), pl.ds(3 * d, d)]
    sc2 = mod_ref[pl.ds(r, 1), pl.ds(4 * d, d)]
    tm = x_ref.shape[1]
    sub = tm // OUTPROJ_SLICES
    slices = [pl.ds(i * sub, sub) for i in range(OUTPROJ_SLICES)]
    ys = []
    for rows in slices:
        cat = jnp.concatenate([hg_ref[0, hh, rows, :] for hh in range(HG_HEADS)]
                              + [pool_ref[0, g, rows, :] for g in range(len(POOL_WINDOWS))], axis=1)
        ys.append(_dot(cat.astype(BF16), w_ref[...]))
    logits = []
    for i, rows in enumerate(slices):
        x1 = x_ref[0, rows, :] + gt1 * _rms(ys[i], gpost_ref[...])
        x1_ref[0, rows, :] = x1
        h2 = _rms(x1, gpre_ref[...]) * (1.0 + sc2) + sh2
        _store_row_tiles(h2_ref.at[0, pl.ds(i * sub * SUBLANES, sub * SUBLANES), :], h2)
        logits.append(_dot(h2.astype(BF16), wr_ref[...]) + br_ref[...])

    lane = lax.broadcasted_iota(I32, (sub, LANES), 1).astype(F32)
    picked = jnp.zeros((1, LANES), F32)
    for i, rows in enumerate(slices):
        lg = jnp.where(lane < N_EXPERTS, logits[i], -jnp.inf)
        ti = jnp.zeros(lg.shape, F32)
        tw = jnp.zeros(lg.shape, F32)
        chosen = jnp.zeros(lg.shape, F32)
        top = None
        for jj in range(TOP_K):
            m = jnp.max(lg, axis=1, keepdims=True)
            idx = jnp.min(jnp.where(lg == m, lane, float(LANES)), axis=1, keepdims=True)
            lg = jnp.where(lane == idx, -jnp.inf, lg)
            chosen = jnp.where(lane == idx, 1.0, chosen)
            top = m if top is None else top
            ti = jnp.where(lane == jj, idx, ti)
            tw = jnp.where(lane == jj, jnp.exp(m - top), tw)
        ti_ref[0, rows, :] = ti.astype(I32)
        tw_ref[0, rows, :] = tw / jnp.sum(tw, axis=1, keepdims=True)
        chosen_ref[0, rows, :] = chosen
        picked = picked + jnp.sum(chosen, axis=0, keepdims=True)
    cnt_ref[...] += picked


def _outproj(hg, pool, x, w_out, mod, gpost, gpre, wr, br):
    b, t, d = x.shape
    tm = min(512, t)
    head = pl.BlockSpec((1, HG_HEADS, tm, HEAD_DIM), lambda i, j: (i, 0, j, 0))
    tok = pl.BlockSpec((1, tm, d), lambda i, j: (i, j, 0))
    lanes = pl.BlockSpec((1, tm, LANES), lambda i, j: (i, j, 0))
    vec = pl.BlockSpec((1, d), lambda i, j: (0, 0))
    est = 2 * (2 * tm * HG_WIDTH * 4 + 3 * tm * d * 4 + 2 * tm * LANES * 4) + 2 * d * d * 2 + 6 * tm * d * 4
    return pl.pallas_call(
        _outproj_kernel,
        out_shape=[jax.ShapeDtypeStruct((b, t, d), F32),
                   jax.ShapeDtypeStruct((b, t * d // LANES, LANES), F32),
                   jax.ShapeDtypeStruct((b, t, LANES), I32), jax.ShapeDtypeStruct((b, t, LANES), F32),
                   jax.ShapeDtypeStruct((b, t, LANES), F32), jax.ShapeDtypeStruct((1, LANES), F32)],
        grid=(b, t // tm),
        in_specs=[head, head, tok,
                  pl.BlockSpec(w_out.shape, lambda i, j: (0, 0)),
                  pl.BlockSpec(mod.shape, lambda i, j: (0, 0)),
                  vec, vec,
                  pl.BlockSpec(wr.shape, lambda i, j: (0, 0)),
                  pl.BlockSpec((1, LANES), lambda i, j: (0, 0))],
        out_specs=[tok, pl.BlockSpec((1, tm * d // LANES, LANES), lambda i, j: (i, j, 0)),
                   lanes, lanes, lanes, pl.BlockSpec((1, LANES), lambda i, j: (0, 0))],
        compiler_params=pltpu.CompilerParams(
            dimension_semantics=("arbitrary", "arbitrary"),
            vmem_limit_bytes=_vmem_limit(est)),
        name="outproj",
    )(hg, pool, x, w_out, mod, gpost, gpre, wr, br)


def _rank_kernel(ti_ref, chosen_ref, start_ref, dest_ref, carry_ref):
    i = pl.program_id(0)

    @pl.when(i == 0)
    def _():
        carry_ref[...] = start_ref[...]

    cnt = chosen_ref[...]
    tm = cnt.shape[0]
    row = lax.broadcasted_iota(I32, (tm, tm), 0)
    col = lax.broadcasted_iota(I32, (tm, tm), 1)
    before = jnp.where(col < row, 1.0, 0.0).astype(BF16)
    prefix = _dot(before, cnt.astype(BF16)) + carry_ref[...]
    rank = jnp.take_along_axis(prefix, ti_ref[...], axis=1)
    dest_ref[...] = (rank.astype(I32) * SUBLANES)[:, :TOP_K]
    carry_ref[...] += jnp.sum(cnt, axis=0, keepdims=True)


def _rank(ti, chosen, starts):
    t = ti.shape[0]
    tm = min(512, t)
    return pl.pallas_call(
        _rank_kernel,
        out_shape=jax.ShapeDtypeStruct((t, TOP_K), I32),
        grid=(t // tm,),
        in_specs=[pl.BlockSpec((tm, LANES), lambda i: (i, 0)),
                  pl.BlockSpec((tm, LANES), lambda i: (i, 0)),
                  pl.BlockSpec((1, LANES), lambda i: (0, 0))],
        out_specs=pl.BlockSpec((tm, TOP_K), lambda i: (i, 0)),
        scratch_shapes=[pltpu.VMEM((1, LANES), F32)],
        compiler_params=pltpu.CompilerParams(dimension_semantics=("arbitrary",)),
        name="rank",
    )(ti, chosen, starts)


def _tile_rows(start):
    return pl.ds(pl.multiple_of(start, SUBLANES), SUBLANES)


def _gap_copies(e, gap_start_ref, gap_len_ref, zeros, xs_ref, sem):
    off, n = gap_start_ref[e], gap_len_ref[e]
    out = []
    for bit in range(_log2(MOE_BM)):
        size = (1 << bit) * SUBLANES
        used = (n >> bit) & 1
        out.append((used == 1,
                    pltpu.make_async_copy(zeros.at[pl.ds(0, size), :],
                                          xs_ref.at[pl.ds(pl.multiple_of(off * SUBLANES, SUBLANES), size), :],
                                          sem)))
        off = off + used * (1 << bit)
    return out


def _dispatch_kernel(dest_ref, gap_start_ref, gap_len_ref, n_live_ref, h_ref, xs_ref, zeros, sem, zsem):
    tm = h_ref.shape[0] // SUBLANES
    block_rows = MOE_BM * SUBLANES
    n_blocks = xs_ref.shape[0] // block_rows

    @pl.when(pl.program_id(0) == 0)
    def _():
        zeros[...] = jnp.zeros(zeros.shape, zeros.dtype)

        def tail_copy(blk):
            return pltpu.make_async_copy(
                zeros, xs_ref.at[pl.ds(pl.multiple_of(blk * block_rows, block_rows), block_rows), :], zsem)

        def tail_start(blk, carry):
            tail_copy(blk).start()
            return carry

        def tail_wait(blk, carry):
            tail_copy(blk).wait()
            return carry

        lax.fori_loop(n_live_ref[0], n_blocks, tail_start, 0)
        lax.fori_loop(n_live_ref[0], n_blocks, tail_wait, 0)

        def fill(e, carry):
            for used, cp in _gap_copies(e, gap_start_ref, gap_len_ref, zeros, xs_ref, zsem):
                @pl.when(used)
                def _(cp=cp):
                    cp.start()
            return carry

        def drain(e, carry):
            for used, cp in _gap_copies(e, gap_start_ref, gap_len_ref, zeros, xs_ref, zsem):
                @pl.when(used)
                def _(cp=cp):
                    cp.wait()
            return carry

        lax.fori_loop(0, N_EXPERTS, fill, 0)
        lax.fori_loop(0, N_EXPERTS, drain, 0)

    def start(r, carry):
        for jj in range(TOP_K):
            pltpu.make_async_copy(h_ref.at[_tile_rows(r * SUBLANES), :],
                                  xs_ref.at[_tile_rows(dest_ref[r * TOP_K + jj]), :],
                                  sem).start(priority=jj % 2)
        return carry

    lax.fori_loop(0, tm, start, 0, unroll=4)
    for jj in range(TOP_K):
        pltpu.make_async_copy(h_ref, xs_ref.at[pl.ds(0, tm * SUBLANES), :], sem).wait()


def _dispatch(dest, gap_start, gap_len, n_live, h2, n_sorted):
    rows, l = h2.shape
    t = rows // SUBLANES
    tm = min(512, t)
    smem = pl.BlockSpec(memory_space=pltpu.SMEM)
    return pl.pallas_call(
        _dispatch_kernel,
        out_shape=jax.ShapeDtypeStruct((n_sorted * SUBLANES, l), h2.dtype),
        grid=(t // tm,),
        in_specs=[pl.BlockSpec((tm * TOP_K,), lambda i: (i,), memory_space=pltpu.SMEM), smem, smem, smem,
                  pl.BlockSpec((tm * SUBLANES, l), lambda i: (i, 0))],
        out_specs=pl.BlockSpec(memory_space=pl.ANY),
        scratch_shapes=[pltpu.VMEM((MOE_BM * SUBLANES, l), h2.dtype),
                        pltpu.SemaphoreType.DMA, pltpu.SemaphoreType.DMA],
        compiler_params=pltpu.CompilerParams(dimension_semantics=("arbitrary",)),
        name="dispatch",
    )(dest, gap_start, gap_len, n_live, h2)


def _moe_kernel(block_e_ref, next_e_ref, n_live_ref,
                x_ref, wg_ref, wu_ref, wd_ref, bg_ref, bu_ref, bd_ref, y_ref,
                wgs, wus, wds, wgb, wub, wdb, sem):
    i = pl.program_id(0)
    live = i < n_live_ref[0]
    e = block_e_ref[i]
    changed = jnp.logical_or(i == 0, e != block_e_ref[jnp.maximum(i - 1, 0)])

    def fetch(expert):
        return [pltpu.make_async_copy(w_ref.at[expert], stage, sem.at[n])
                for n, (w_ref, stage) in enumerate(((wg_ref, wgs), (wu_ref, wus), (wd_ref, wds)))]

    @pl.when(i == 0)
    def _():
        for cp in fetch(e):
            cp.start()

    @pl.when(jnp.logical_and(live, changed))
    def _():
        for cp in fetch(e):
            cp.wait()
        def cast(c, carry):
            rows = pl.ds(pl.multiple_of(c * LANES, LANES), LANES)
            for src, dst in ((wgs, wgb), (wus, wub), (wds, wdb)):
                dst[rows, :] = src[rows, :].astype(BF16)
            return carry

        lax.fori_loop(0, wgs.shape[0] // LANES, cast, 0)

        @pl.when(next_e_ref[i] != e)
        def _():
            for cp in fetch(next_e_ref[i]):
                cp.start()

    @pl.when(live)
    def _():
        x = _load_row_tiles(x_ref).astype(BF16)
        gate = jnp.minimum(_dot(x, wgb[...]) + bg_ref[...], SWIGLU_LIMIT)
        up = jnp.clip(_dot(x, wub[...]) + bu_ref[...], -SWIGLU_LIMIT, SWIGLU_LIMIT)
        hmid = (up + 1.0) * gate * _sigmoid(SWIGLU_ALPHA * gate)
        _store_row_tiles(y_ref, _dot(hmid.astype(BF16), wdb[...]) + bd_ref[...])

    @pl.when(jnp.logical_not(live))
    def _():
        y_ref[...] = jnp.zeros(y_ref.shape, y_ref.dtype)


def _moe(block_e, next_e, n_live, xs, wg, wu, wd, bg, bu, bd):
    d, f = wg.shape[-2:]
    n_blocks = block_e.shape[0]
    rows_in = pl.BlockSpec((MOE_BM * SUBLANES, LANES),
                           lambda i, be, ne, nl: (jnp.minimum(i, nl[0] - 1), 0))
    rows_out = pl.BlockSpec((MOE_BM * SUBLANES, LANES), lambda i, be, ne, nl: (i, 0))
    hbm = pl.BlockSpec(memory_space=pl.ANY)

    def bspec(n):
        return pl.BlockSpec((None, 1, n), lambda i, be, ne, nl: (be[i], 0, 0))

    est = 3 * d * f * 4 + 3 * d * f * 2 + 4 * MOE_BM * d * 4 + 4 * MOE_BM * f * 4
    return pl.pallas_call(
        _moe_kernel,
        out_shape=jax.ShapeDtypeStruct(xs.shape, F32),
        grid_spec=pltpu.PrefetchScalarGridSpec(
            num_scalar_prefetch=3,
            grid=(n_blocks,),
            in_specs=[rows_in, hbm, hbm, hbm, bspec(f), bspec(f), bspec(d)],
            out_specs=rows_out,
            scratch_shapes=[pltpu.VMEM((d, f), F32), pltpu.VMEM((d, f), F32), pltpu.VMEM((f, d), F32),
                            pltpu.VMEM((d, f), BF16), pltpu.VMEM((d, f), BF16), pltpu.VMEM((f, d), BF16),
                            pltpu.SemaphoreType.DMA((3,))]),
        compiler_params=pltpu.CompilerParams(
            dimension_semantics=("arbitrary",),
            vmem_limit_bytes=_vmem_limit(est)),
        name="moe",
    )(block_e, next_e, n_live, xs, wg, wu, wd, bg, bu, bd)


def _moe_layout(counts, n_rows):
    n_blocks = n_rows // MOE_BM + N_EXPERTS
    padded = (counts + MOE_BM - 1) // MOE_BM * MOE_BM
    ends = jnp.cumsum(padded)
    starts = ends - padded
    block_end = ends // MOE_BM
    i = jnp.arange(n_blocks, dtype=I32)
    ids = jnp.arange(N_EXPERTS, dtype=I32)
    last_e = jnp.max(jnp.where(counts > 0, ids, 0))
    block_e = jnp.minimum(jnp.sum(block_end[None, :] <= i[:, None], axis=1), last_e).astype(I32)
    later = jnp.logical_and(ids[None, :] > ids[:, None], counts[None, :] > 0)
    following = jnp.min(jnp.where(later, ids[None, :], N_EXPERTS), axis=1)
    following = jnp.where(following == N_EXPERTS, ids, following)
    next_e = jnp.sum(jnp.where(block_e[:, None] == ids[None, :], following[None, :], 0), axis=1)
    return (starts.astype(I32), (starts + counts).astype(I32), (padded - counts).astype(I32),
            block_e, next_e.astype(I32), block_end[-1:].astype(I32), n_blocks)


def _combine_kernel(dest_ref, dnext_ref, ys_ref, tw_ref, x1_ref, mod_ref, gpost_ref, o_ref, buf, sem,
                    *, tiles_per_batch):
    tm, d = x1_ref.shape
    i = pl.program_id(0)
    slot = i % 2

    def issue(d_ref, s):
        def start(r, carry):
            for jj in range(TOP_K):
                pltpu.make_async_copy(ys_ref.at[_tile_rows(d_ref[r * TOP_K + jj]), :],
                                      buf.at[s, jj, _tile_rows(r * SUBLANES), :],
                                      sem.at[s]).start(priority=jj % 2)
            return carry

        lax.fori_loop(0, tm, start, 0, unroll=4)

    @pl.when(i == 0)
    def _():
        issue(dest_ref, 0)

    @pl.when(i + 1 < pl.num_programs(0))
    def _():
        issue(dnext_ref, 1 - slot)

    for jj in range(TOP_K):
        pltpu.make_async_copy(ys_ref.at[pl.ds(0, tm * SUBLANES), :], buf.at[slot, jj],
                              sem.at[slot]).wait()

    tw = tw_ref[...]
    y = tw[:, 0:1] * _load_row_tiles(buf.at[slot, 0])
    for jj in range(1, TOP_K):
        y = y + tw[:, jj:jj + 1] * _load_row_tiles(buf.at[slot, jj])
    r = i // tiles_per_batch
    gt2 = mod_ref[pl.ds(r, 1), pl.ds(5 * d, d)]
    o_ref[...] = x1_ref[...] + gt2 * _rms(y, gpost_ref[...])


def _combine(dest, ys, tw, x1, mod, gpost, tokens_per_batch):
    t, d = x1.shape
    tm = min(512, tokens_per_batch)
    n_tiles = t // tm
    kern = functools.partial(_combine_kernel, tiles_per_batch=tokens_per_batch // tm)
    est = 2 * TOP_K * tm * d * 4 + 2 * (2 * tm * d * 4 + tm * LANES * 4) + 3 * tm * d * 4
    return pl.pallas_call(
        kern,
        out_shape=jax.ShapeDtypeStruct((t, d), F32),
        grid=(n_tiles,),
        in_specs=[pl.BlockSpec((tm * TOP_K,), lambda i: (i,), memory_space=pltpu.SMEM),
                  pl.BlockSpec((tm * TOP_K,), lambda i: (jnp.minimum(i + 1, n_tiles - 1),),
                               memory_space=pltpu.SMEM),
                  pl.BlockSpec(memory_space=pl.ANY),
                  pl.BlockSpec((tm, LANES), lambda i: (i, 0)),
                  pl.BlockSpec((tm, d), lambda i: (i, 0)),
                  pl.BlockSpec(mod.shape, lambda i: (0, 0)),
                  pl.BlockSpec((1, d), lambda i: (0, 0))],
        out_specs=pl.BlockSpec((tm, d), lambda i: (i, 0)),
        scratch_shapes=[pltpu.VMEM((2, TOP_K, tm * SUBLANES, LANES), F32),
                        pltpu.SemaphoreType.DMA((2,))],
        compiler_params=pltpu.CompilerParams(
            dimension_semantics=("arbitrary",),
            vmem_limit_bytes=_vmem_limit(est)),
        name="combine",
    )(dest, dest, ys, tw, x1, mod, gpost)


def kernel(x, c, ctx, c_ctx, w_ada, b_ada, g_pre_mix, g_post_mix, g_pre_ffn, g_post_ffn,
           w_in, w_out, hgrn_lb_logits, hgrn_norm, pool_w, pool_scale,
           w_router, b_router, w_gate, b_gate, w_up, b_up, w_down, b_down):
    b, s, d = x.shape
    layer = 0
    n_e = w_router.shape[-1]

    cc = jnp.concatenate([c, c_ctx[None, :]], axis=0)
    cc = jnp.pad(cc, ((0, -(b + 1) % SUBLANES), (0, 0)))
    mod = _ada(cc, w_ada[layer], b_ada[layer][None, :])

    w_in_b = w_in[layer].astype(BF16)
    lbl = hgrn_lb_logits[:2].reshape(4, HG_WIDTH)
    gn = hgrn_norm[layer][None, :]
    gpre = g_pre_mix[layer][None, :]

    lff_c, kf_c, lfb_c, kb_c, v_c = _inproj(ctx, mod, gpre, w_in_b, lbl, gn, mod_row=b, full=False)
    zeros = jnp.zeros((b, HG_HEADS, HEAD_DIM, HEAD_DIM), F32)
    (s_f,) = _hgrn(None, kf_c, v_c, lff_c, zeros, rev=False, need_o=False, want_state=True,
                   name="hgrn_ctx_fwd")
    (s_b,) = _hgrn(None, kb_c, v_c, lfb_c, zeros, rev=True, need_o=False, want_state=True,
                   name="hgrn_ctx_bwd")

    q, lff, kf, lfb, kb, v, gate, u = _inproj(x, mod, gpre, w_in_b, lbl, gn, mod_row=None, full=True)
    (o_f,) = _hgrn(q, kf, v, lff, s_f, rev=False, need_o=True, want_state=False, name="hgrn_fwd")
    (o_hg,) = _hgrn(q, kb, v, lfb, s_b, o_f, gate, rev=True, need_o=True, want_state=False,
                    name="hgrn_bwd")
    o_pool = _pool(u, pool_w[layer].astype(BF16), pool_scale[layer].reshape(len(POOL_WINDOWS), 1, POOL_GROUP))

    wr = jnp.pad(w_router[layer], ((0, 0), (0, LANES - n_e))).astype(BF16)
    br = jnp.pad(b_router[layer], (0, LANES - n_e))[None, :]
    x1, h2, ti, tw, chosen, cnt = _outproj(o_hg, o_pool, x, w_out[layer].astype(BF16), mod,
                                           g_post_mix[layer][None, :], g_pre_ffn[layer][None, :], wr, br)

    t = b * s
    starts, gap_start, gap_len, block_e, next_e, n_live, n_blocks = _moe_layout(
        cnt[0, :n_e].astype(I32), t * TOP_K)
    starts = jnp.pad(starts.astype(F32), (0, LANES - n_e))[None, :]
    dest = _rank(ti.reshape(t, LANES), chosen.reshape(t, LANES), starts).reshape(t * TOP_K)

    xs = _dispatch(dest, gap_start, gap_len, n_live, h2.reshape(t * d // LANES, LANES), n_blocks * MOE_BM)
    ys = _moe(block_e, next_e, n_live, xs, w_gate[layer], w_up[layer], w_down[layer],
              b_gate[layer][:, None, :], b_up[layer][:, None, :], b_down[layer][:, None, :])
    out = _combine(dest, ys, tw.reshape(t, LANES), x1.reshape(t, d), mod,
                   g_post_ffn[layer][None, :], s)
    return out.reshape(b, s, d)
```

```python
import functools
import math

import numpy as np
import jax
import jax.numpy as jnp
from jax import lax
from jax.experimental import pallas as pl
from jax.experimental.pallas import tpu as pltpu

F32 = jnp.float32
BF16 = jnp.bfloat16
I32 = jnp.int32

GRID_W = 64
HG_HEADS = 4
HEAD_DIM = 128
HG_WIDTH = HG_HEADS * HEAD_DIM
POOL_WINDOWS = (2, 4, 8, 16)
POOL_GROUP = 128
N_EXPERTS = 32
TOP_K = 4
SWIGLU_LIMIT = 7.0
SWIGLU_ALPHA = 1.702
EPS = 1e-6

LANES = 128
SUBLANES = 8
V7X_VMEM_BYTES = 64 * 1024 * 1024

CHUNK = 64
HGRN_GROUP = 4
LEVELS = (32, 16, 8, 4, 2, 1)
FINE_LEVELS = (2, 1)
MOE_BM = 256
OUTPROJ_SLICES = 4
LOG2_E = math.log2(math.e)


def _vmem_limit(nbytes):
    return int(min(nbytes * 3 // 2 + (4 << 20), V7X_VMEM_BYTES - (6 << 20)))


def _log2(n):
    assert n & (n - 1) == 0
    return n.bit_length() - 1


def _sigmoid(x):
    return 1.0 / (1.0 + jnp.exp(-x))


def _rms(x, gain):
    return x * lax.rsqrt(jnp.mean(x * x, axis=-1, keepdims=True) + EPS) * gain


def _dot(a, b):
    return jnp.dot(a, b, preferred_element_type=F32)


ROW_TILE = SUBLANES * LANES


def _store_row_tiles(ref, val):
    rows = val.shape[0]
    for c in range(SUBLANES):
        ref[pl.ds(c, rows, stride=SUBLANES), :] = val[:, c * LANES:(c + 1) * LANES]


def _load_row_tiles(ref):
    rows = ref.shape[0] // SUBLANES
    return jnp.concatenate([ref[pl.ds(c, rows, stride=SUBLANES), :] for c in range(SUBLANES)], axis=1)


def _dot_nt(a, b):
    return lax.dot_general(a, b, (((1,), (1,)), ((), ())), preferred_element_type=F32)


def _ada_kernel(c_ref, w_ref, b_ref, o_ref):
    c = c_ref[...]
    s = (c * _sigmoid(c)).astype(BF16)
    o_ref[...] = _dot(s, w_ref[...].astype(BF16)) + b_ref[...]


def _ada(cc, w, b):
    rows, d = cc.shape
    n = w.shape[1]
    tn = 1536 if n % 1536 == 0 else n
    return pl.pallas_call(
        _ada_kernel,
        out_shape=jax.ShapeDtypeStruct((rows, n), F32),
        grid=(n // tn,),
        in_specs=[pl.BlockSpec((rows, d), lambda j: (0, 0)),
                  pl.BlockSpec((d, tn), lambda j: (0, j)),
                  pl.BlockSpec((1, tn), lambda j: (0, j))],
        out_specs=pl.BlockSpec((rows, tn), lambda j: (0, j)),
        compiler_params=pltpu.CompilerParams(
            dimension_semantics=("arbitrary",),
            vmem_limit_bytes=_vmem_limit(2 * d * tn * 4 + d * tn * 2)),
        name="ada",
    )(cc, w, b)


def _inproj_kernel(x_ref, mod_ref, gpre_ref, w_ref, lbl_ref, gn_ref, *outs, mod_row, full):
    d = x_ref.shape[-1]
    r = pl.program_id(0) if mod_row is None else mod_row
    sh = mod_ref[pl.ds(r, 1), pl.ds(0, d)]
    sc = mod_ref[pl.ds(r, 1), pl.ds(d, d)]
    tm = x_ref.shape[1]
    n_slices = 2 if tm % (2 * SUBLANES) == 0 else 1
    sub = tm // n_slices
    slices = [pl.ds(i * sub, sub) for i in range(n_slices)]
    hs = [(_rms(x_ref[0, rows, :], gpre_ref[...]) * (1.0 + sc) + sh).astype(BF16) for rows in slices]

    def proj(g):
        return [_dot(h, w_ref[:, g * HG_WIDTH:(g + 1) * HG_WIDTH]) for h in hs]

    def put(ref, vals):
        for rows, val in zip(slices, vals):
            for hh in range(HG_HEADS):
                ref[0, hh, rows, :] = val[:, hh * HEAD_DIM:(hh + 1) * HEAD_DIM]

    def lower_bound(direction):
        l0 = lbl_ref[pl.ds(direction, 1), :]
        l1 = lbl_ref[pl.ds(2 + direction, 1), :]
        m = jnp.maximum(l0, l1)
        e0 = jnp.exp(l0 - m)
        return e0 / (e0 + jnp.exp(l1 - m))

    if full:
        q_o, lff_o, kf_o, lfb_o, kb_o, v_o, gate_o, u_o = outs
        put(q_o, [q * _sigmoid(q) for q in proj(0)])
    else:
        lff_o, kf_o, lfb_o, kb_o, v_o = outs
    for direction, (lf_o, k_o) in enumerate(((lff_o, kf_o), (lfb_o, kb_o))):
        lb = lower_bound(direction)
        sgs = [_sigmoid(z) for z in proj(1 + direction)]
        put(lf_o, [jnp.log(lb + (1.0 - lb) * sg) for sg in sgs])
        put(k_o, [(1.0 - lb) * (1.0 - sg) for sg in sgs])
    put(v_o, proj(3))
    if full:
        put(gate_o, [gn_ref[...] * _sigmoid(z) for z in proj(4)])
        put(u_o, proj(5))


def _inproj(x, mod, gpre, w_in, lbl, gn, *, mod_row, full):
    b, t, d = x.shape
    tm = min(512, t)
    n_out = 8 if full else 5
    n_cols = w_in.shape[1]
    head = pl.BlockSpec((1, HG_HEADS, tm, HEAD_DIM), lambda i, j: (i, 0, j, 0))
    kern = functools.partial(_inproj_kernel, mod_row=mod_row, full=full)
    est = 2 * tm * d * 4 + 2 * d * n_cols * 2 + n_out * 2 * tm * HG_WIDTH * 4 + 4 * tm * HG_WIDTH * 4
    return pl.pallas_call(
        kern,
        out_shape=[jax.ShapeDtypeStruct((b, HG_HEADS, t, HEAD_DIM), F32)] * n_out,
        grid=(b, t // tm),
        in_specs=[pl.BlockSpec((1, tm, d), lambda i, j: (i, j, 0)),
                  pl.BlockSpec(mod.shape, lambda i, j: (0, 0)),
                  pl.BlockSpec((1, d), lambda i, j: (0, 0)),
                  pl.BlockSpec(w_in.shape, lambda i, j: (0, 0)),
                  pl.BlockSpec(lbl.shape, lambda i, j: (0, 0)),
                  pl.BlockSpec((1, HG_WIDTH), lambda i, j: (0, 0))],
        out_specs=[head] * n_out,
        compiler_params=pltpu.CompilerParams(
            dimension_semantics=("arbitrary", "arbitrary"),
            vmem_limit_bytes=_vmem_limit(est)),
        name="inproj_full" if full else "inproj_ctx",
    )(x, mod, gpre, w_in, lbl, gn)


def _hgrn_tables(rev):
    row = np.arange(CHUNK)[:, None]
    col = np.arange(CHUNK)[None, :]
    tri = ((col >= row) if rev else (col <= row)).astype(np.float32)
    blocks, masks = [tri], []
    for h in LEVELS:
        if h in FINE_LEVELS:
            blocks.append(tri - tri[_ref_row(np.arange(CHUNK), h, rev)])
        is_q = ((row // h) % 2) == (0 if rev else 1)
        key_half = ((col // h) % 2) == (1 if rev else 0)
        masks.append(((row // (2 * h)) == (col // (2 * h))) & is_q & key_half)
    pair = np.stack([np.concatenate(masks[i:i + 2], axis=1) for i in range(0, len(LEVELS), 2)])
    cums = np.concatenate(blocks, axis=0)
    return jnp.asarray(np.concatenate([cums, cums], axis=1), BF16), jnp.asarray(pair, F32)


def _ref_row(t, h, rev):
    return (t // (2 * h)) * (2 * h) + (h if rev else h - 1)


def _hgrn_chunks(qs, ks, vs, lfs, sts, tabs, rev, need_o):
    cums_ref, pair_ref = tabs
    n_heads = len(sts)
    heads = range(len(ks))
    n_pairs = len(LEVELS) // 2
    n_cum = CHUNK * (1 + len(FINE_LEVELS)) if need_o else CHUNK
    cums = cums_ref[0:n_cum, :]

    his, los = [], []
    for h in heads:
        lf2 = lfs[h] * LOG2_E
        hi = lf2.astype(BF16)
        his.append(hi)
        los.append((lf2 - hi.astype(F32)).astype(BF16))
    both = _dot(cums, jnp.concatenate([jnp.concatenate(his, axis=1), jnp.concatenate(los, axis=1)], axis=0))
    bd = [both[:, h * LANES:(h + 1) * LANES] for h in heads]

    k16 = [ks[h].astype(BF16) for h in heads]
    v16 = [vs[h].astype(BF16) for h in heads]
    b = [bd[h][0:CHUNK] for h in heads]

    def level_diff(h, level):
        if level in FINE_LEVELS:
            i = FINE_LEVELS.index(level)
            return bd[h][CHUNK * (1 + i):CHUNK * (2 + i)]
        refs = [_ref_row(m * 2 * level, level, rev) for m in range(CHUNK // (2 * level))]
        pieces = [jnp.broadcast_to(b[h][r:r + 1, :], (2 * level, LANES)) for r in refs]
        return b[h] - (pieces[0] if len(pieces) == 1 else jnp.concatenate(pieces, axis=0))
    edge = [b[h][0:1, :] if rev else b[h][CHUNK - 1:CHUNK, :] for h in heads]
    kd = [k16[h] * jnp.exp2(edge[h] - b[h]).astype(BF16) for h in heads]
    grow = [lax.dot_general(v16[h], kd[h], (((0,), (0,)), ((), ())), preferred_element_type=F32)
            for h in heads]
    q16 = [qs[h].astype(BF16) for h in heads] if need_o else None
    st = list(sts)
    carried = []
    for h in heads:
        if need_o:
            carried.append(_dot_nt(q16[h] * jnp.exp2(b[h]).astype(BF16), st[h % n_heads].astype(BF16)))
        st[h % n_heads] = st[h % n_heads] * jnp.exp2(edge[h]) + grow[h]
    if not need_o:
        return None, st

    zero = jnp.zeros((CHUNK, LANES), BF16)
    att = [[] for _ in heads]
    for p in range(n_pairs):
        lhs, rhs = [], []
        for h in heads:
            qe, ke = [], []
            for i in (2 * p, 2 * p + 1):
                e = jnp.exp2(-jnp.abs(level_diff(h, LEVELS[i]))).astype(BF16)
                qe.append(q16[h] * e)
                ke.append(k16[h] * e)
            lhs.append(jnp.concatenate(qe, axis=1))
            rhs.append(jnp.concatenate([jnp.concatenate([ke[0], zero], axis=1),
                                        jnp.concatenate([zero, ke[1]], axis=1)], axis=0))
        scores = [_dot_nt(lhs[h], rhs[h]) for h in heads]
        for h in heads:
            att[h].append((scores[h] * pair_ref[p]).astype(BF16))
    o = []
    for h in heads:
        vals = jnp.concatenate([v16[h]] * (2 * n_pairs), axis=0)
        same_row = jnp.sum(qs[h] * ks[h], axis=1, keepdims=True) * vs[h]
        o.append(carried[h] + _dot(jnp.concatenate(att[h], axis=1), vals) + same_row)
    return o, st


def _hgrn_kernel(*refs, rev, need_o, final, want_state, n_chunks):
    it = iter(refs)
    tabs = tuple(next(it) for _ in range(2))
    q_ref = next(it) if need_o else None
    k_ref, v_ref, lf_ref, s0_ref = next(it), next(it), next(it), next(it)
    prev_ref = next(it) if final else None
    gate_ref = next(it) if final else None
    o_ref = next(it) if need_o else None
    sout_ref = next(it) if want_state else None
    st_ref = next(it)

    j = pl.program_id(1)
    nb = pl.num_programs(1)

    @pl.when(j == 0)
    def _():
        st_ref[...] = s0_ref[0]

    group = HGRN_GROUP if n_chunks % HGRN_GROUP == 0 else 1

    def body(gi, carry):
        rows = []
        for s in range(group):
            c = gi * group + s
            c = (n_chunks - 1 - c) if rev else c
            rows.append(pl.ds(pl.multiple_of(c * CHUNK, CHUNK), CHUNK))
        pairs = [(r, hh) for r in rows for hh in range(HG_HEADS)]
        o, st_new = _hgrn_chunks([q_ref[0, hh, r, :] for r, hh in pairs] if need_o else None,
                                 [k_ref[0, hh, r, :] for r, hh in pairs],
                                 [v_ref[0, hh, r, :] for r, hh in pairs],
                                 [lf_ref[0, hh, r, :] for r, hh in pairs],
                                 [st_ref[hh] for hh in range(HG_HEADS)], tabs, rev, need_o)
        for hh in range(HG_HEADS):
            st_ref[hh] = st_new[hh]
        if need_o:
            for (r, hh), out in zip(pairs, o):
                if final:
                    tot = out + prev_ref[0, hh, r, :]
                    out = tot * lax.rsqrt(jnp.mean(tot * tot, axis=-1, keepdims=True) + EPS)
                    out = out * gate_ref[0, hh, r, :]
                o_ref[0, hh, r, :] = out
        return carry

    lax.fori_loop(0, n_chunks // group, body, 0)

    if want_state:
        @pl.when(j == nb - 1)
        def _():
            sout_ref[0] = st_ref[...]


def _hgrn(q, k, v, lf, s0, prev=None, gate=None, *, rev, need_o, want_state, name):
    b, hh, t, _ = k.shape
    tb = min(1024, t)
    nb = t // tb
    final = prev is not None
    tabs = _hgrn_tables(rev)

    seq = pl.BlockSpec((1, hh, tb, HEAD_DIM), lambda i, j: (i, 0, (nb - 1 - j) if rev else j, 0))
    state = pl.BlockSpec((1, hh, HEAD_DIM, HEAD_DIM), lambda i, j: (i, 0, 0, 0))
    args = list(tabs)
    in_specs = [pl.BlockSpec(a.shape, functools.partial(lambda nd, i, j: (0,) * nd, a.ndim)) for a in tabs]
    for a in ((q,) if need_o else ()) + (k, v, lf):
        args.append(a)
        in_specs.append(seq)
    args.append(s0)
    in_specs.append(state)
    if final:
        args += [prev, gate]
        in_specs += [seq, seq]
    out_shape, out_specs = [], []
    if need_o:
        out_shape.append(jax.ShapeDtypeStruct((b, hh, t, HEAD_DIM), F32))
        out_specs.append(seq)
    if want_state:
        out_shape.append(jax.ShapeDtypeStruct((b, hh, HEAD_DIM, HEAD_DIM), F32))
        out_specs.append(state)
    kern = functools.partial(_hgrn_kernel, rev=rev, need_o=need_o, final=final,
                             want_state=want_state, n_chunks=tb // CHUNK)
    est = 2 * (len(args) + len(out_shape)) * hh * tb * HEAD_DIM * 4
    return pl.pallas_call(
        kern,
        out_shape=out_shape,
        grid=(b, nb),
        in_specs=in_specs,
        out_specs=out_specs,
        scratch_shapes=[pltpu.VMEM((hh, HEAD_DIM, HEAD_DIM), F32)],
        compiler_params=pltpu.CompilerParams(
            dimension_semantics=("arbitrary", "arbitrary"),
            vmem_limit_bytes=_vmem_limit(est)),
        name=name,
    )(*args)


def _window_sum(x, w, stride, pos, extent):
    n = x.shape[0]
    whole_tiles = stride % SUBLANES == 0 and stride * extent == n

    def ahead(y, dist):
        if whole_tiles:
            k = min(dist, extent) * stride
            return jnp.concatenate([y[k:], jnp.zeros((k, y.shape[1]), y.dtype)], axis=0)
        return jnp.where(pos + dist < extent, pltpu.roll(y, (n - dist * stride) % n, 0), 0.0)

    def behind(y, dist):
        if whole_tiles:
            k = min(dist, extent) * stride
            return jnp.concatenate([jnp.zeros((k, y.shape[1]), y.dtype), y[:n - k]], axis=0)
        return jnp.where(pos >= dist, pltpu.roll(y, dist * stride, 0), 0.0)

    fwd, bwd, h = x, x, 1
    while h < w // 2:
        fwd = fwd + ahead(fwd, h)
        bwd = bwd + behind(bwd, h)
        h *= 2
    return fwd + behind(bwd, 1)


def _window_count(pos, w, extent):
    return (jnp.minimum(pos + w // 2, extent) - jnp.maximum(pos - w // 2, 0)).astype(F32)


def _pool_kernel(u_ref, pw_ref, ps_ref, o_ref):
    g = pl.program_id(1)
    n = u_ref.shape[2]
    tok = lax.broadcasted_iota(I32, (n, LANES), 0)
    col = tok & (GRID_W - 1)
    row = tok >> _log2(GRID_W)
    for gi, w in enumerate(POOL_WINDOWS):
        @pl.when(g == gi)
        def _(w=w):
            u = u_ref[0, 0]
            rows = n // GRID_W
            s = _window_sum(_window_sum(u, w, GRID_W, row, rows), w, 1, col, GRID_W)
            m = s / (_window_count(row, w, rows) * _window_count(col, w, GRID_W))
            o_ref[0, 0] = _dot((m - u).astype(BF16), pw_ref[0]) * ps_ref[0]


def _pool(u, pw, ps):
    b, g, t, c = u.shape
    blk = pl.BlockSpec((1, 1, t, c), lambda i, j: (i, j, 0, 0))
    return pl.pallas_call(
        _pool_kernel,
        out_shape=jax.ShapeDtypeStruct(u.shape, F32),
        grid=(b, g),
        in_specs=[blk,
                  pl.BlockSpec((1, c, c), lambda i, j: (j, 0, 0)),
                  pl.BlockSpec((1, 1, c), lambda i, j: (j, 0, 0))],
        out_specs=blk,
        compiler_params=pltpu.CompilerParams(
            dimension_semantics=("arbitrary", "arbitrary"),
            vmem_limit_bytes=_vmem_limit(12 * t * c * 4)),
        name="pool",
    )(u, pw, ps)


def _outproj_kernel(hg_ref, pool_ref, x_ref, w_ref, mod_ref, gpost_ref, gpre_ref, wr_ref, br_ref,
                    x1_ref, h2_ref, ti_ref, tw_ref, chosen_ref, cnt_ref):
    d = x_ref.shape[-1]
    r = pl.program_id(0)

    @pl.when(jnp.logical_and(r == 0, pl.program_id(1) == 0))
    def _():
        cnt_ref[...] = jnp.zeros(cnt_ref.shape, F32)

    gt1 = mod_ref[pl.ds(r, 1), pl.ds(2 * d, d)]
    sh2 = mod_ref[pl.ds(r, 1), pl.ds(3 * d, d)]
    sc2 = mod_ref[pl.ds(r, 1), pl.ds(4 * d, d)]
    tm = x_ref.shape[1]
    sub = tm // OUTPROJ_SLICES
    slices = [pl.ds(i * sub, sub) for i in range(OUTPROJ_SLICES)]
    ys = []
    for rows in slices:
        cat = jnp.concatenate([hg_ref[0, hh, rows, :] for hh in range(HG_HEADS)]
                              + [pool_ref[0, g, rows, :] for g in range(len(POOL_WINDOWS))], axis=1)
        ys.append(_dot(cat.astype(BF16), w_ref[...]))
    logits = []
    for i, rows in enumerate(slices):
        x1 = x_ref[0, rows, :] + gt1 * _rms(ys[i], gpost_ref[...])
        x1_ref[0, rows, :] = x1
        h2 = _rms(x1, gpre_ref[...]) * (1.0 + sc2) + sh2
        _store_row_tiles(h2_ref.at[0, pl.ds(i * sub * SUBLANES, sub * SUBLANES), :], h2)
        logits.append(_dot(h2.astype(BF16), wr_ref[...]) + br_ref[...])

    lane = lax.broadcasted_iota(I32, (sub, LANES), 1).astype(F32)
    picked = jnp.zeros((1, LANES), F32)
    for i, rows in enumerate(slices):
        lg = jnp.where(lane < N_EXPERTS, logits[i], -jnp.inf)
        ti = jnp.zeros(lg.shape, F32)
        tw = jnp.zeros(lg.shape, F32)
        chosen = jnp.zeros(lg.shape, F32)
        top = None
        for jj in range(TOP_K):
            m = jnp.max(lg, axis=1, keepdims=True)
            idx = jnp.min(jnp.where(lg == m, lane, float(LANES)), axis=1, keepdims=True)
            lg = jnp.where(lane == idx, -jnp.inf, lg)
            chosen = jnp.where(lane == idx, 1.0, chosen)
            top = m if top is None else top
            ti = jnp.where(lane == jj, idx, ti)
            tw = jnp.where(lane == jj, jnp.exp(m - top), tw)
        ti_ref[0, rows, :] = ti.astype(I32)
        tw_ref[0, rows, :] = tw / jnp.sum(tw, axis=1, keepdims=True)
        chosen_ref[0, rows, :] = chosen
        picked = picked + jnp.sum(chosen, axis=0, keepdims=True)
    cnt_ref[...] += picked


def _outproj(hg, pool, x, w_out, mod, gpost, gpre, wr, br):
    b, t, d = x.shape
    tm = min(512, t)
    head = pl.BlockSpec((1, HG_HEADS, tm, HEAD_DIM), lambda i, j: (i, 0, j, 0))
    tok = pl.BlockSpec((1, tm, d), lambda i, j: (i, j, 0))
    lanes = pl.BlockSpec((1, tm, LANES), lambda i, j: (i, j, 0))
    vec = pl.BlockSpec((1, d), lambda i, j: (0, 0))
    est = 2 * (2 * tm * HG_WIDTH * 4 + 3 * tm * d * 4 + 2 * tm * LANES * 4) + 2 * d * d * 2 + 6 * tm * d * 4
    return pl.pallas_call(
        _outproj_kernel,
        out_shape=[jax.ShapeDtypeStruct((b, t, d), F32),
                   jax.ShapeDtypeStruct((b, t * d // LANES, LANES), F32),
                   jax.ShapeDtypeStruct((b, t, LANES), I32), jax.ShapeDtypeStruct((b, t, LANES), F32),
                   jax.ShapeDtypeStruct((b, t, LANES), F32), jax.ShapeDtypeStruct((1, LANES), F32)],
        grid=(b, t // tm),
        in_specs=[head, head, tok,
                  pl.BlockSpec(w_out.shape, lambda i, j: (0, 0)),
                  pl.BlockSpec(mod.shape, lambda i, j: (0, 0)),
                  vec, vec,
                  pl.BlockSpec(wr.shape, lambda i, j: (0, 0)),
                  pl.BlockSpec((1, LANES), lambda i, j: (0, 0))],
        out_specs=[tok, pl.BlockSpec((1, tm * d // LANES, LANES), lambda i, j: (i, j, 0)),
                   lanes, lanes, lanes, pl.BlockSpec((1, LANES), lambda i, j: (0, 0))],
        compiler_params=pltpu.CompilerParams(
            dimension_semantics=("arbitrary", "arbitrary"),
            vmem_limit_bytes=_vmem_limit(est)),
        name="outproj",
    )(hg, pool, x, w_out, mod, gpost, gpre, wr, br)


def _rank_kernel(ti_ref, chosen_ref, start_ref, dest_ref, carry_ref):
    i = pl.program_id(0)

    @pl.when(i == 0)
    def _():
        carry_ref[...] = start_ref[...]

    cnt = chosen_ref[...]
    tm = cnt.shape[0]
    row = lax.broadcasted_iota(I32, (tm, tm), 0)
    col = lax.broadcasted_iota(I32, (tm, tm), 1)
    before = jnp.where(col < row, 1.0, 0.0).astype(BF16)
    prefix = _dot(before, cnt.astype(BF16)) + carry_ref[...]
    rank = jnp.take_along_axis(prefix, ti_ref[...], axis=1)
    dest_ref[...] = (rank.T[:TOP_K, :] * SUBLANES).astype(I32)
    carry_ref[...] += jnp.sum(cnt, axis=0, keepdims=True)


def _rank(ti, chosen, starts):
    t = ti.shape[0]
    tm = min(512, t)
    return pl.pallas_call(
        _rank_kernel,
        out_shape=jax.ShapeDtypeStruct((TOP_K, t), I32),
        grid=(t // tm,),
        in_specs=[pl.BlockSpec((tm, LANES), lambda i: (i, 0)),
                  pl.BlockSpec((tm, LANES), lambda i: (i, 0)),
                  pl.BlockSpec((1, LANES), lambda i: (0, 0))],
        out_specs=pl.BlockSpec((TOP_K, tm), lambda i: (0, i)),
        scratch_shapes=[pltpu.VMEM((1, LANES), F32)],
        compiler_params=pltpu.CompilerParams(dimension_semantics=("arbitrary",)),
        name="rank",
    )(ti, chosen, starts)


def _tile_rows(start):
    return pl.ds(pl.multiple_of(start, SUBLANES), SUBLANES)


def _gap_copies(e, gap_start_ref, gap_len_ref, zeros, xs_ref, sem):
    off, n = gap_start_ref[e], gap_len_ref[e]
    out = []
    for bit in range(_log2(MOE_BM)):
        size = (1 << bit) * SUBLANES
        used = (n >> bit) & 1
        out.append((used == 1,
                    pltpu.make_async_copy(zeros.at[pl.ds(0, size), :],
                                          xs_ref.at[pl.ds(pl.multiple_of(off * SUBLANES, SUBLANES), size), :],
                                          sem)))
        off = off + used * (1 << bit)
    return out


def _dispatch_kernel(dest_ref, gap_start_ref, gap_len_ref, n_live_ref, h_ref, xs_ref, zeros, sem, zsem):
    tm = h_ref.shape[0] // SUBLANES
    block_rows = MOE_BM * SUBLANES
    n_blocks = xs_ref.shape[0] // block_rows

    @pl.when(pl.program_id(0) == 0)
    def _():
        zeros[...] = jnp.zeros(zeros.shape, zeros.dtype)

        def tail_copy(blk):
            return pltpu.make_async_copy(
                zeros, xs_ref.at[pl.ds(pl.multiple_of(blk * block_rows, block_rows), block_rows), :], zsem)

        def tail_start(blk, carry):
            tail_copy(blk).start()
            return carry

        def tail_wait(blk, carry):
            tail_copy(blk).wait()
            return carry

        lax.fori_loop(n_live_ref[0], n_blocks, tail_start, 0)
        lax.fori_loop(n_live_ref[0], n_blocks, tail_wait, 0)

        def fill(e, carry):
            for used, cp in _gap_copies(e, gap_start_ref, gap_len_ref, zeros, xs_ref, zsem):
                @pl.when(used)
                def _(cp=cp):
                    cp.start()
            return carry

        def drain(e, carry):
            for used, cp in _gap_copies(e, gap_start_ref, gap_len_ref, zeros, xs_ref, zsem):
                @pl.when(used)
                def _(cp=cp):
                    cp.wait()
            return carry

        lax.fori_loop(0, N_EXPERTS, fill, 0)
        lax.fori_loop(0, N_EXPERTS, drain, 0)

    def start(r, carry):
        for jj in range(TOP_K):
            pltpu.make_async_copy(h_ref.at[_tile_rows(r * SUBLANES), :],
                                  xs_ref.at[_tile_rows(dest_ref[jj, r]), :],
                                  sem).start(priority=jj % 2)
        return carry

    lax.fori_loop(0, tm, start, 0, unroll=4)
    for jj in range(TOP_K):
        pltpu.make_async_copy(h_ref, xs_ref.at[pl.ds(0, tm * SUBLANES), :], sem).wait()


def _dispatch(dest, gap_start, gap_len, n_live, h2, n_sorted):
    rows, l = h2.shape
    t = rows // SUBLANES
    tm = min(512, t)
    smem = pl.BlockSpec(memory_space=pltpu.SMEM)
    return pl.pallas_call(
        _dispatch_kernel,
        out_shape=jax.ShapeDtypeStruct((n_sorted * SUBLANES, l), h2.dtype),
        grid=(t // tm,),
        in_specs=[pl.BlockSpec((TOP_K, tm), lambda i: (0, i), memory_space=pltpu.SMEM), smem, smem, smem,
                  pl.BlockSpec((tm * SUBLANES, l), lambda i: (i, 0))],
        out_specs=pl.BlockSpec(memory_space=pl.ANY),
        scratch_shapes=[pltpu.VMEM((MOE_BM * SUBLANES, l), h2.dtype),
                        pltpu.SemaphoreType.DMA, pltpu.SemaphoreType.DMA],
        compiler_params=pltpu.CompilerParams(dimension_semantics=("arbitrary",)),
        name="dispatch",
    )(dest, gap_start, gap_len, n_live, h2)


def _moe_kernel(block_e_ref, next_e_ref, n_live_ref,
                x_ref, wg_ref, wu_ref, wd_ref, bg_ref, bu_ref, bd_ref, y_ref,
                wgs, wus, wds, wgb, wub, wdb, sem):
    i = pl.program_id(0)
    live = i < n_live_ref[0]
    e = block_e_ref[i]
    changed = jnp.logical_or(i == 0, e != block_e_ref[jnp.maximum(i - 1, 0)])

    def fetch(expert):
        return [pltpu.make_async_copy(w_ref.at[expert], stage, sem.at[n])
                for n, (w_ref, stage) in enumerate(((wg_ref, wgs), (wu_ref, wus), (wd_ref, wds)))]

    @pl.when(i == 0)
    def _():
        for cp in fetch(e):
            cp.start()

    @pl.when(jnp.logical_and(live, changed))
    def _():
        for cp in fetch(e):
            cp.wait()
        def cast(c, carry):
            rows = pl.ds(pl.multiple_of(c * LANES, LANES), LANES)
            for src, dst in ((wgs, wgb), (wus, wub), (wds, wdb)):
                dst[rows, :] = src[rows, :].astype(BF16)
            return carry

        lax.fori_loop(0, wgs.shape[0] // LANES, cast, 0)

        @pl.when(next_e_ref[i] != e)
        def _():
            for cp in fetch(next_e_ref[i]):
                cp.start()

    @pl.when(live)
    def _():
        x = _load_row_tiles(x_ref).astype(BF16)
        gate = jnp.minimum(_dot(x, wgb[...]) + bg_ref[...], SWIGLU_LIMIT)
        up = jnp.clip(_dot(x, wub[...]) + bu_ref[...], -SWIGLU_LIMIT, SWIGLU_LIMIT)
        hmid = (up + 1.0) * gate * _sigmoid(SWIGLU_ALPHA * gate)
        _store_row_tiles(y_ref, _dot(hmid.astype(BF16), wdb[...]) + bd_ref[...])

    @pl.when(jnp.logical_not(live))
    def _():
        y_ref[...] = jnp.zeros(y_ref.shape, y_ref.dtype)


def _moe(block_e, next_e, n_live, xs, wg, wu, wd, bg, bu, bd):
    d, f = wg.shape[-2:]
    n_blocks = block_e.shape[0]
    rows_in = pl.BlockSpec((MOE_BM * SUBLANES, LANES),
                           lambda i, be, ne, nl: (jnp.minimum(i, nl[0] - 1), 0))
    rows_out = pl.BlockSpec((MOE_BM * SUBLANES, LANES), lambda i, be, ne, nl: (i, 0))
    hbm = pl.BlockSpec(memory_space=pl.ANY)

    def bspec(n):
        return pl.BlockSpec((None, 1, n), lambda i, be, ne, nl: (be[i], 0, 0))

    est = 3 * d * f * 4 + 3 * d * f * 2 + 4 * MOE_BM * d * 4 + 4 * MOE_BM * f * 4
    return pl.pallas_call(
        _moe_kernel,
        out_shape=jax.ShapeDtypeStruct(xs.shape, F32),
        grid_spec=pltpu.PrefetchScalarGridSpec(
            num_scalar_prefetch=3,
            grid=(n_blocks,),
            in_specs=[rows_in, hbm, hbm, hbm, bspec(f), bspec(f), bspec(d)],
            out_specs=rows_out,
            scratch_shapes=[pltpu.VMEM((d, f), F32), pltpu.VMEM((d, f), F32), pltpu.VMEM((f, d), F32),
                            pltpu.VMEM((d, f), BF16), pltpu.VMEM((d, f), BF16), pltpu.VMEM((f, d), BF16),
                            pltpu.SemaphoreType.DMA((3,))]),
        compiler_params=pltpu.CompilerParams(
            dimension_semantics=("arbitrary",),
            vmem_limit_bytes=_vmem_limit(est)),
        name="moe",
    )(block_e, next_e, n_live, xs, wg, wu, wd, bg, bu, bd)


def _moe_layout(counts, n_rows):
    n_blocks = n_rows // MOE_BM + N_EXPERTS
    padded = (counts + MOE_BM - 1) // MOE_BM * MOE_BM
    ends = jnp.cumsum(padded)
    starts = ends - padded
    block_end = ends // MOE_BM
    i = jnp.arange(n_blocks, dtype=I32)
    ids = jnp.arange(N_EXPERTS, dtype=I32)
    last_e = jnp.max(jnp.where(counts > 0, ids, 0))
    block_e = jnp.minimum(jnp.sum(block_end[None, :] <= i[:, None], axis=1), last_e).astype(I32)
    later = jnp.logical_and(ids[None, :] > ids[:, None], counts[None, :] > 0)
    following = jnp.min(jnp.where(later, ids[None, :], N_EXPERTS), axis=1)
    following = jnp.where(following == N_EXPERTS, ids, following)
    next_e = jnp.sum(jnp.where(block_e[:, None] == ids[None, :], following[None, :], 0), axis=1)
    return (starts.astype(I32), (starts + counts).astype(I32), (padded - counts).astype(I32),
            block_e, next_e.astype(I32), block_end[-1:].astype(I32), n_blocks)


def _combine_kernel(dest_ref, dnext_ref, ys_ref, tw_ref, x1_ref, mod_ref, gpost_ref, o_ref, buf, sem,
                    *, tiles_per_batch):
    tm, d = x1_ref.shape
    i = pl.program_id(0)
    slot = i % 2

    def issue(d_ref, s):
        def start(r, carry):
            for jj in range(TOP_K):
                pltpu.make_async_copy(ys_ref.at[_tile_rows(d_ref[jj, r]), :],
                                      buf.at[s, jj, _tile_rows(r * SUBLANES), :],
                                      sem.at[s]).start(priority=jj % 2)
            return carry

        lax.fori_loop(0, tm, start, 0, unroll=4)

    @pl.when(i == 0)
    def _():
        issue(dest_ref, 0)

    @pl.when(i + 1 < pl.num_programs(0))
    def _():
        issue(dnext_ref, 1 - slot)

    for jj in range(TOP_K):
        pltpu.make_async_copy(ys_ref.at[pl.ds(0, tm * SUBLANES), :], buf.at[slot, jj],
                              sem.at[slot]).wait()

    tw = tw_ref[...]
    y = tw[:, 0:1] * _load_row_tiles(buf.at[slot, 0])
    for jj in range(1, TOP_K):
        y = y + tw[:, jj:jj + 1] * _load_row_tiles(buf.at[slot, jj])
    r = i // tiles_per_batch
    gt2 = mod_ref[pl.ds(r, 1), pl.ds(5 * d, d)]
    o_ref[...] = x1_ref[...] + gt2 * _rms(y, gpost_ref[...])


def _combine(dest, ys, tw, x1, mod, gpost, tokens_per_batch):
    t, d = x1.shape
    tm = min(512, tokens_per_batch)
    n_tiles = t // tm
    kern = functools.partial(_combine_kernel, tiles_per_batch=tokens_per_batch // tm)
    est = 2 * TOP_K * tm * d * 4 + 2 * (2 * tm * d * 4 + tm * LANES * 4) + 3 * tm * d * 4
    return pl.pallas_call(
        kern,
        out_shape=jax.ShapeDtypeStruct((t, d), F32),
        grid=(n_tiles,),
        in_specs=[pl.BlockSpec((TOP_K, tm), lambda i: (0, i), memory_space=pltpu.SMEM),
                  pl.BlockSpec((TOP_K, tm), lambda i: (0, jnp.minimum(i + 1, n_tiles - 1)),
                               memory_space=pltpu.SMEM),
                  pl.BlockSpec(memory_space=pl.ANY),
                  pl.BlockSpec((tm, LANES), lambda i: (i, 0)),
                  pl.BlockSpec((tm, d), lambda i: (i, 0)),
                  pl.BlockSpec(mod.shape, lambda i: (0, 0)),
                  pl.BlockSpec((1, d), lambda i: (0, 0))],
        out_specs=pl.BlockSpec((tm, d), lambda i: (i, 0)),
        scratch_shapes=[pltpu.VMEM((2, TOP_K, tm * SUBLANES, LANES), F32),
                        pltpu.SemaphoreType.DMA((2,))],
        compiler_params=pltpu.CompilerParams(
            dimension_semantics=("arbitrary",),
            vmem_limit_bytes=_vmem_limit(est)),
        name="combine",
    )(dest, dest, ys, tw, x1, mod, gpost)


def kernel(x, c, ctx, c_ctx, w_ada, b_ada, g_pre_mix, g_post_mix, g_pre_ffn, g_post_ffn,
           w_in, w_out, hgrn_lb_logits, hgrn_norm, pool_w, pool_scale,
           w_router, b_router, w_gate, b_gate, w_up, b_up, w_down, b_down):
    b, s, d = x.shape
    layer = 0
    n_e = w_router.shape[-1]

    cc = jnp.concatenate([c, c_ctx[None, :]], axis=0)
    cc = jnp.pad(cc, ((0, -(b + 1) % SUBLANES), (0, 0)))
    mod = _ada(cc, w_ada[layer], b_ada[layer][None, :])

    w_in_b = w_in[layer].astype(BF16)
    lbl = hgrn_lb_logits[:2].reshape(4, HG_WIDTH)
    gn = hgrn_norm[layer][None, :]
    gpre = g_pre_mix[layer][None, :]

    lff_c, kf_c, lfb_c, kb_c, v_c = _inproj(ctx, mod, gpre, w_in_b, lbl, gn, mod_row=b, full=False)
    zeros = jnp.zeros((b, HG_HEADS, HEAD_DIM, HEAD_DIM), F32)
    (s_f,) = _hgrn(None, kf_c, v_c, lff_c, zeros, rev=False, need_o=False, want_state=True,
                   name="hgrn_ctx_fwd")
    (s_b,) = _hgrn(None, kb_c, v_c, lfb_c, zeros, rev=True, need_o=False, want_state=True,
                   name="hgrn_ctx_bwd")

    q, lff, kf, lfb, kb, v, gate, u = _inproj(x, mod, gpre, w_in_b, lbl, gn, mod_row=None, full=True)
    (o_f,) = _hgrn(q, kf, v, lff, s_f, rev=False, need_o=True, want_state=False, name="hgrn_fwd")
    (o_hg,) = _hgrn(q, kb, v, lfb, s_b, o_f, gate, rev=True, need_o=True, want_state=False,
                    name="hgrn_bwd")
    o_pool = _pool(u, pool_w[layer].astype(BF16), pool_scale[layer].reshape(len(POOL_WINDOWS), 1, POOL_GROUP))

    wr = jnp.pad(w_router[layer], ((0, 0), (0, LANES - n_e))).astype(BF16)
    br = jnp.pad(b_router[layer], (0, LANES - n_e))[None, :]
    x1, h2, ti, tw, chosen, cnt = _outproj(o_hg, o_pool, x, w_out[layer].astype(BF16), mod,
                                           g_post_mix[layer][None, :], g_pre_ffn[layer][None, :], wr, br)

    t = b * s
    starts, gap_start, gap_len, block_e, next_e, n_live, n_blocks = _moe_layout(
        cnt[0, :n_e].astype(I32), t * TOP_K)
    starts = jnp.pad(starts.astype(F32), (0, LANES - n_e))[None, :]
    dest = _rank(ti.reshape(t, LANES), chosen.reshape(t, LANES), starts)

    xs = _dispatch(dest, gap_start, gap_len, n_live, h2.reshape(t * d // LANES, LANES), n_blocks * MOE_BM)
    ys = _moe(block_e, next_e, n_live, xs, w_gate[layer], w_up[layer], w_down[layer],
              b_gate[layer][:, None, :], b_up[layer][:, None, :], b_down[layer][:, None, :])
    out = _combine(dest, ys, tw.reshape(t, LANES), x1.reshape(t, d), mod,
                   g_post_ffn[layer][None, :], s)
    return out.reshape(b, s, d)
```

```python
import functools
import math

import numpy as np
import jax
import jax.numpy as jnp
from jax import lax
from jax.experimental import pallas as pl
from jax.experimental.pallas import tpu as pltpu

F32 = jnp.float32
BF16 = jnp.bfloat16
I32 = jnp.int32

GRID_W = 64
HG_HEADS = 4
HEAD_DIM = 128
HG_WIDTH = HG_HEADS * HEAD_DIM
POOL_WINDOWS = (2, 4, 8, 16)
POOL_GROUP = 128
N_EXPERTS = 32
TOP_K = 4
SWIGLU_LIMIT = 7.0
SWIGLU_ALPHA = 1.702
EPS = 1e-6

LANES = 128
SUBLANES = 8
V7X_VMEM_BYTES = 64 * 1024 * 1024

CHUNK = 64
HGRN_GROUP = 4
LEVELS = (32, 16, 8, 4, 2, 1)
FINE_LEVELS = (2, 1)
MOE_BM = 256
OUTPROJ_SLICES = 4
LOG2_E = math.log2(math.e)


def _vmem_limit(nbytes):
    return int(min(nbytes * 3 // 2 + (4 << 20), V7X_VMEM_BYTES - (6 << 20)))


def _log2(n):
    assert n & (n - 1) == 0
    return n.bit_length() - 1


def _sigmoid(x):
    return 1.0 / (1.0 + jnp.exp(-x))


def _rms(x, gain):
    return x * lax.rsqrt(jnp.mean(x * x, axis=-1, keepdims=True) + EPS) * gain


def _dot(a, b):
    return jnp.dot(a, b, preferred_element_type=F32)


ROW_TILE = SUBLANES * LANES


def _store_row_tiles(ref, val):
    rows = val.shape[0]
    for c in range(SUBLANES):
        ref[pl.ds(c, rows, stride=SUBLANES), :] = val[:, c * LANES:(c + 1) * LANES]


def _load_row_tiles(ref):
    rows = ref.shape[0] // SUBLANES
    return jnp.concatenate([ref[pl.ds(c, rows, stride=SUBLANES), :] for c in range(SUBLANES)], axis=1)


def _dot_nt(a, b):
    return lax.dot_general(a, b, (((1,), (1,)), ((), ())), preferred_element_type=F32)


def _ada_kernel(c_ref, w_ref, b_ref, o_ref):
    c = c_ref[...]
    s = (c * _sigmoid(c)).astype(BF16)
    o_ref[...] = _dot(s, w_ref[...].astype(BF16)) + b_ref[...]


def _ada(cc, w, b):
    rows, d = cc.shape
    n = w.shape[1]
    tn = 1536 if n % 1536 == 0 else n
    return pl.pallas_call(
        _ada_kernel,
        out_shape=jax.ShapeDtypeStruct((rows, n), F32),
        grid=(n // tn,),
        in_specs=[pl.BlockSpec((rows, d), lambda j: (0, 0)),
                  pl.BlockSpec((d, tn), lambda j: (0, j)),
                  pl.BlockSpec((1, tn), lambda j: (0, j))],
        out_specs=pl.BlockSpec((rows, tn), lambda j: (0, j)),
        compiler_params=pltpu.CompilerParams(
            dimension_semantics=("arbitrary",),
            vmem_limit_bytes=_vmem_limit(2 * d * tn * 4 + d * tn * 2)),
        name="ada",
    )(cc, w, b)


def _inproj_kernel(x_ref, mod_ref, gpre_ref, w_ref, lbl_ref, gn_ref, *outs, mod_row, full):
    d = x_ref.shape[-1]
    r = pl.program_id(0) if mod_row is None else mod_row
    sh = mod_ref[pl.ds(r, 1), pl.ds(0, d)]
    sc = mod_ref[pl.ds(r, 1), pl.ds(d, d)]
    tm = x_ref.shape[1]
    n_slices = 2 if tm % (2 * SUBLANES) == 0 else 1
    sub = tm // n_slices
    slices = [pl.ds(i * sub, sub) for i in range(n_slices)]
    hs = [(_rms(x_ref[0, rows, :], gpre_ref[...]) * (1.0 + sc) + sh).astype(BF16) for rows in slices]

    def proj(g):
        return [_dot(h, w_ref[:, g * HG_WIDTH:(g + 1) * HG_WIDTH]) for h in hs]

    def put(ref, vals):
        for rows, val in zip(slices, vals):
            for hh in range(HG_HEADS):
                ref[0, hh, rows, :] = val[:, hh * HEAD_DIM:(hh + 1) * HEAD_DIM]

    def lower_bound(direction):
        l0 = lbl_ref[pl.ds(direction, 1), :]
        l1 = lbl_ref[pl.ds(2 + direction, 1), :]
        m = jnp.maximum(l0, l1)
        e0 = jnp.exp(l0 - m)
        return e0 / (e0 + jnp.exp(l1 - m))

    if full:
        q_o, lff_o, kf_o, lfb_o, kb_o, v_o, gate_o, u_o = outs
        put(q_o, [q * _sigmoid(q) for q in proj(0)])
    else:
        lff_o, kf_o, lfb_o, kb_o, v_o = outs
    for direction, (lf_o, k_o) in enumerate(((lff_o, kf_o), (lfb_o, kb_o))):
        lb = lower_bound(direction)
        sgs = [_sigmoid(z) for z in proj(1 + direction)]
        put(lf_o, [jnp.log(lb + (1.0 - lb) * sg) for sg in sgs])
        put(k_o, [(1.0 - lb) * (1.0 - sg) for sg in sgs])
    put(v_o, proj(3))
    if full:
        put(gate_o, [gn_ref[...] * _sigmoid(z) for z in proj(4)])
        put(u_o, proj(5))


def _inproj(x, mod, gpre, w_in, lbl, gn, *, mod_row, full):
    b, t, d = x.shape
    tm = min(512, t)
    n_out = 8 if full else 5
    n_cols = w_in.shape[1]
    head = pl.BlockSpec((1, HG_HEADS, tm, HEAD_DIM), lambda i, j: (i, 0, j, 0))
    kern = functools.partial(_inproj_kernel, mod_row=mod_row, full=full)
    est = 2 * tm * d * 4 + 2 * d * n_cols * 2 + n_out * 2 * tm * HG_WIDTH * 4 + 4 * tm * HG_WIDTH * 4
    return pl.pallas_call(
        kern,
        out_shape=[jax.ShapeDtypeStruct((b, HG_HEADS, t, HEAD_DIM), F32)] * n_out,
        grid=(b, t // tm),
        in_specs=[pl.BlockSpec((1, tm, d), lambda i, j: (i, j, 0)),
                  pl.BlockSpec(mod.shape, lambda i, j: (0, 0)),
                  pl.BlockSpec((1, d), lambda i, j: (0, 0)),
                  pl.BlockSpec(w_in.shape, lambda i, j: (0, 0)),
                  pl.BlockSpec(lbl.shape, lambda i, j: (0, 0)),
                  pl.BlockSpec((1, HG_WIDTH), lambda i, j: (0, 0))],
        out_specs=[head] * n_out,
        compiler_params=pltpu.CompilerParams(
            dimension_semantics=("arbitrary", "arbitrary"),
            vmem_limit_bytes=_vmem_limit(est)),
        name="inproj_full" if full else "inproj_ctx",
    )(x, mod, gpre, w_in, lbl, gn)


def _hgrn_tables(rev):
    row = np.arange(CHUNK)[:, None]
    col = np.arange(CHUNK)[None, :]
    tri = ((col >= row) if rev else (col <= row)).astype(np.float32)
    blocks, masks = [tri], []
    for h in LEVELS:
        if h in FINE_LEVELS:
            blocks.append(tri - tri[_ref_row(np.arange(CHUNK), h, rev)])
        is_q = ((row // h) % 2) == (0 if rev else 1)
        key_half = ((col // h) % 2) == (1 if rev else 0)
        masks.append(((row // (2 * h)) == (col // (2 * h))) & is_q & key_half)
    pair = np.stack([np.concatenate(masks[i:i + 2], axis=1) for i in range(0, len(LEVELS), 2)])
    cums = np.concatenate(blocks, axis=0)
    return jnp.asarray(np.concatenate([cums, cums], axis=1), BF16), jnp.asarray(pair, F32)


def _ref_row(t, h, rev):
    return (t // (2 * h)) * (2 * h) + (h if rev else h - 1)


def _hgrn_chunks(qs, ks, vs, lfs, sts, tabs, rev, need_o):
    cums_ref, pair_ref = tabs
    n_heads = len(sts)
    heads = range(len(ks))
    n_pairs = len(LEVELS) // 2
    n_cum = CHUNK * (1 + len(FINE_LEVELS)) if need_o else CHUNK
    cums = cums_ref[0:n_cum, :]

    his, los = [], []
    for h in heads:
        lf2 = lfs[h] * LOG2_E
        hi = lf2.astype(BF16)
        his.append(hi)
        los.append((lf2 - hi.astype(F32)).astype(BF16))
    both = _dot(cums, jnp.concatenate([jnp.concatenate(his, axis=1), jnp.concatenate(los, axis=1)], axis=0))
    bd = [both[:, h * LANES:(h + 1) * LANES] for h in heads]

    k16 = [ks[h].astype(BF16) for h in heads]
    v16 = [vs[h].astype(BF16) for h in heads]
    b = [bd[h][0:CHUNK] for h in heads]

    def level_diff(h, level):
        if level in FINE_LEVELS:
            i = FINE_LEVELS.index(level)
            return bd[h][CHUNK * (1 + i):CHUNK * (2 + i)]
        refs = [_ref_row(m * 2 * level, level, rev) for m in range(CHUNK // (2 * level))]
        pieces = [jnp.broadcast_to(b[h][r:r + 1, :], (2 * level, LANES)) for r in refs]
        return b[h] - (pieces[0] if len(pieces) == 1 else jnp.concatenate(pieces, axis=0))
    edge = [b[h][0:1, :] if rev else b[h][CHUNK - 1:CHUNK, :] for h in heads]
    kd = [k16[h] * jnp.exp2(edge[h] - b[h]).astype(BF16) for h in heads]
    grow = [lax.dot_general(v16[h], kd[h], (((0,), (0,)), ((), ())), preferred_element_type=F32)
            for h in heads]
    q16 = [qs[h].astype(BF16) for h in heads] if need_o else None
    st = list(sts)
    carried = []
    for h in heads:
        if need_o:
            carried.append(_dot_nt(q16[h] * jnp.exp2(b[h]).astype(BF16), st[h % n_heads].astype(BF16)))
        st[h % n_heads] = st[h % n_heads] * jnp.exp2(edge[h]) + grow[h]
    if not need_o:
        return None, st

    zero = jnp.zeros((CHUNK, LANES), BF16)
    att = [[] for _ in heads]
    for p in range(n_pairs):
        lhs, rhs = [], []
        for h in heads:
            qe, ke = [], []
            for i in (2 * p, 2 * p + 1):
                e = jnp.exp2(-jnp.abs(level_diff(h, LEVELS[i]))).astype(BF16)
                qe.append(q16[h] * e)
                ke.append(k16[h] * e)
            lhs.append(jnp.concatenate(qe, axis=1))
            rhs.append(jnp.concatenate([jnp.concatenate([ke[0], zero], axis=1),
                                        jnp.concatenate([zero, ke[1]], axis=1)], axis=0))
        scores = [_dot_nt(lhs[h], rhs[h]) for h in heads]
        for h in heads:
            att[h].append((scores[h] * pair_ref[p]).astype(BF16))
    o = []
    for h in heads:
        vals = jnp.concatenate([v16[h]] * (2 * n_pairs), axis=0)
        same_row = jnp.sum(qs[h] * ks[h], axis=1, keepdims=True) * vs[h]
        o.append(carried[h] + _dot(jnp.concatenate(att[h], axis=1), vals) + same_row)
    return o, st


def _hgrn_kernel(*refs, rev, need_o, final, want_state, n_chunks):
    it = iter(refs)
    tabs = tuple(next(it) for _ in range(2))
    q_ref = next(it) if need_o else None
    k_ref, v_ref, lf_ref, s0_ref = next(it), next(it), next(it), next(it)
    prev_ref = next(it) if final else None
    gate_ref = next(it) if final else None
    o_ref = next(it) if need_o else None
    sout_ref = next(it) if want_state else None
    st_ref = next(it)

    j = pl.program_id(1)
    nb = pl.num_programs(1)

    @pl.when(j == 0)
    def _():
        st_ref[...] = s0_ref[0]

    group = HGRN_GROUP if n_chunks % HGRN_GROUP == 0 else 1

    def body(gi, carry):
        rows = []
        for s in range(group):
            c = gi * group + s
            c = (n_chunks - 1 - c) if rev else c
            rows.append(pl.ds(pl.multiple_of(c * CHUNK, CHUNK), CHUNK))
        pairs = [(r, hh) for r in rows for hh in range(HG_HEADS)]
        o, st_new = _hgrn_chunks([q_ref[0, hh, r, :] for r, hh in pairs] if need_o else None,
                                 [k_ref[0, hh, r, :] for r, hh in pairs],
                                 [v_ref[0, hh, r, :] for r, hh in pairs],
                                 [lf_ref[0, hh, r, :] for r, hh in pairs],
                                 [st_ref[hh] for hh in range(HG_HEADS)], tabs, rev, need_o)
        for hh in range(HG_HEADS):
            st_ref[hh] = st_new[hh]
        if need_o:
            for (r, hh), out in zip(pairs, o):
                if final:
                    tot = out + prev_ref[0, hh, r, :]
                    out = tot * lax.rsqrt(jnp.mean(tot * tot, axis=-1, keepdims=True) + EPS)
                    out = out * gate_ref[0, hh, r, :]
                o_ref[0, hh, r, :] = out
        return carry

    lax.fori_loop(0, n_chunks // group, body, 0)

    if want_state:
        @pl.when(j == nb - 1)
        def _():
            sout_ref[0] = st_ref[...]


def _hgrn(q, k, v, lf, s0, prev=None, gate=None, *, rev, need_o, want_state, name):
    b, hh, t, _ = k.shape
    tb = min(1024, t)
    nb = t // tb
    final = prev is not None
    tabs = _hgrn_tables(rev)

    seq = pl.BlockSpec((1, hh, tb, HEAD_DIM), lambda i, j: (i, 0, (nb - 1 - j) if rev else j, 0))
    state = pl.BlockSpec((1, hh, HEAD_DIM, HEAD_DIM), lambda i, j: (i, 0, 0, 0))
    args = list(tabs)
    in_specs = [pl.BlockSpec(a.shape, functools.partial(lambda nd, i, j: (0,) * nd, a.ndim)) for a in tabs]
    for a in ((q,) if need_o else ()) + (k, v, lf):
        args.append(a)
        in_specs.append(seq)
    args.append(s0)
    in_specs.append(state)
    if final:
        args += [prev, gate]
        in_specs += [seq, seq]
    out_shape, out_specs = [], []
    if need_o:
        out_shape.append(jax.ShapeDtypeStruct((b, hh, t, HEAD_DIM), F32))
        out_specs.append(seq)
    if want_state:
        out_shape.append(jax.ShapeDtypeStruct((b, hh, HEAD_DIM, HEAD_DIM), F32))
        out_specs.append(state)
    kern = functools.partial(_hgrn_kernel, rev=rev, need_o=need_o, final=final,
                             want_state=want_state, n_chunks=tb // CHUNK)
    est = 2 * (len(args) + len(out_shape)) * hh * tb * HEAD_DIM * 4
    return pl.pallas_call(
        kern,
        out_shape=out_shape,
        grid=(b, nb),
        in_specs=in_specs,
        out_specs=out_specs,
        scratch_shapes=[pltpu.VMEM((hh, HEAD_DIM, HEAD_DIM), F32)],
        compiler_params=pltpu.CompilerParams(
            dimension_semantics=("arbitrary", "arbitrary"),
            vmem_limit_bytes=_vmem_limit(est)),
        name=name,
    )(*args)


def _window_sum(x, w, stride, pos, extent):
    n = x.shape[0]
    whole_tiles = stride % SUBLANES == 0 and stride * extent == n

    def ahead(y, dist):
        if whole_tiles:
            k = min(dist, extent) * stride
            return jnp.concatenate([y[k:], jnp.zeros((k, y.shape[1]), y.dtype)], axis=0)
        return jnp.where(pos + dist < extent, pltpu.roll(y, (n - dist * stride) % n, 0), 0.0)

    def behind(y, dist):
        if whole_tiles:
            k = min(dist, extent) * stride
            return jnp.concatenate([jnp.zeros((k, y.shape[1]), y.dtype), y[:n - k]], axis=0)
        return jnp.where(pos >= dist, pltpu.roll(y, dist * stride, 0), 0.0)

    fwd, bwd, h = x, x, 1
    while h < w // 2:
        fwd = fwd + ahead(fwd, h)
        bwd = bwd + behind(bwd, h)
        h *= 2
    return fwd + behind(bwd, 1)


def _window_count(pos, w, extent):
    return (jnp.minimum(pos + w // 2, extent) - jnp.maximum(pos - w // 2, 0)).astype(F32)


def _pool_kernel(u_ref, pw_ref, ps_ref, o_ref):
    g = pl.program_id(1)
    n = u_ref.shape[2]
    tok = lax.broadcasted_iota(I32, (n, LANES), 0)
    col = tok & (GRID_W - 1)
    row = tok >> _log2(GRID_W)
    for gi, w in enumerate(POOL_WINDOWS):
        @pl.when(g == gi)
        def _(w=w):
            u = u_ref[0, 0]
            rows = n // GRID_W
            s = _window_sum(_window_sum(u, w, GRID_W, row, rows), w, 1, col, GRID_W)
            m = s / (_window_count(row, w, rows) * _window_count(col, w, GRID_W))
            o_ref[0, 0] = _dot((m - u).astype(BF16), pw_ref[0]) * ps_ref[0]


def _pool(u, pw, ps):
    b, g, t, c = u.shape
    blk = pl.BlockSpec((1, 1, t, c), lambda i, j: (i, j, 0, 0))
    return pl.pallas_call(
        _pool_kernel,
        out_shape=jax.ShapeDtypeStruct(u.shape, F32),
        grid=(b, g),
        in_specs=[blk,
                  pl.BlockSpec((1, c, c), lambda i, j: (j, 0, 0)),
                  pl.BlockSpec((1, 1, c), lambda i, j: (j, 0, 0))],
        out_specs=blk,
        compiler_params=pltpu.CompilerParams(
            dimension_semantics=("arbitrary", "arbitrary"),
            vmem_limit_bytes=_vmem_limit(12 * t * c * 4)),
        name="pool",
    )(u, pw, ps)


def _outproj_kernel(hg_ref, pool_ref, x_ref, w_ref, mod_ref, gpost_ref, gpre_ref, wr_ref, br_ref,
                    x1_ref, h2_ref, ti_ref, tw_ref, chosen_ref, cnt_ref):
    d = x_ref.shape[-1]
    r = pl.program_id(0)

    @pl.when(jnp.logical_and(r == 0, pl.program_id(1) == 0))
    def _():
        cnt_ref[...] = jnp.zeros(cnt_ref.shape, F32)

    gt1 = mod_ref[pl.ds(r, 1), pl.ds(2 * d, d)]
    sh2 = mod_ref[pl.ds(r, 1), pl.ds(3 * d, d)]
    sc2 = mod_ref[pl.ds(r, 1), pl.ds(4 * d, d)]
    tm = x_ref.shape[1]
    sub = tm // OUTPROJ_SLICES
    slices = [pl.ds(i * sub, sub) for i in range(OUTPROJ_SLICES)]
    ys = []
    for rows in slices:
        cat = jnp.concatenate([hg_ref[0, hh, rows, :] for hh in range(HG_HEADS)]
                              + [pool_ref[0, g, rows, :] for g in range(len(POOL_WINDOWS))], axis=1)
        ys.append(_dot(cat.astype(BF16), w_ref[...]))
    logits = []
    for i, rows in enumerate(slices):
        x1 = x_ref[0, rows, :] + gt1 * _rms(ys[i], gpost_ref[...])
        x1_ref[0, rows, :] = x1
        h2 = _rms(x1, gpre_ref[...]) * (1.0 + sc2) + sh2
        _store_row_tiles(h2_ref.at[0, pl.ds(i * sub * SUBLANES, sub * SUBLANES), :], h2)
        logits.append(_dot(h2.astype(BF16), wr_ref[...]) + br_ref[...])

    lane = lax.broadcasted_iota(I32, (sub, LANES), 1).astype(F32)
    picked = jnp.zeros((1, LANES), F32)
    for i, rows in enumerate(slices):
        lg = jnp.where(lane < N_EXPERTS, logits[i], -jnp.inf)
        ti = jnp.zeros(lg.shape, F32)
        tw = jnp.zeros(lg.shape, F32)
        chosen = jnp.zeros(lg.shape, F32)
        top = None
        for jj in range(TOP_K):
            m = jnp.max(lg, axis=1, keepdims=True)
            idx = jnp.min(jnp.where(lg == m, lane, float(LANES)), axis=1, keepdims=True)
            lg = jnp.where(lane == idx, -jnp.inf, lg)
            chosen = jnp.where(lane == idx, 1.0, chosen)
            top = m if top is None else top
            ti = jnp.where(lane == jj, idx, ti)
            tw = jnp.where(lane == jj, jnp.exp(m - top), tw)
        ti_ref[0, rows, :] = ti.astype(I32)
        tw_ref[0, rows, :] = tw / jnp.sum(tw, axis=1, keepdims=True)
        chosen_ref[0, rows, :] = chosen
        picked = picked + jnp.sum(chosen, axis=0, keepdims=True)
    cnt_ref[...] += picked


def _outproj(hg, pool, x, w_out, mod, gpost, gpre, wr, br):
    b, t, d = x.shape
    tm = min(512, t)
    head = pl.BlockSpec((1, HG_HEADS, tm, HEAD_DIM), lambda i, j: (i, 0, j, 0))
    tok = pl.BlockSpec((1, tm, d), lambda i, j: (i, j, 0))
    lanes = pl.BlockSpec((1, tm, LANES), lambda i, j: (i, j, 0))
    vec = pl.BlockSpec((1, d), lambda i, j: (0, 0))
    est = 2 * (2 * tm * HG_WIDTH * 4 + 3 * tm * d * 4 + 2 * tm * LANES * 4) + 2 * d * d * 2 + 6 * tm * d * 4
    return pl.pallas_call(
        _outproj_kernel,
        out_shape=[jax.ShapeDtypeStruct((b, t, d), F32),
                   jax.ShapeDtypeStruct((b, t * d // LANES, LANES), F32),
                   jax.ShapeDtypeStruct((b, t, LANES), I32), jax.ShapeDtypeStruct((b, t, LANES), F32),
                   jax.ShapeDtypeStruct((b, t, LANES), F32), jax.ShapeDtypeStruct((1, LANES), F32)],
        grid=(b, t // tm),
        in_specs=[head, head, tok,
                  pl.BlockSpec(w_out.shape, lambda i, j: (0, 0)),
                  pl.BlockSpec(mod.shape, lambda i, j: (0, 0)),
                  vec, vec,
                  pl.BlockSpec(wr.shape, lambda i, j: (0, 0)),
                  pl.BlockSpec((1, LANES), lambda i, j: (0, 0))],
        out_specs=[tok, pl.BlockSpec((1, tm * d // LANES, LANES), lambda i, j: (i, j, 0)),
                   lanes, lanes, lanes, pl.BlockSpec((1, LANES), lambda i, j: (0, 0))],
        compiler_params=pltpu.CompilerParams(
            dimension_semantics=("arbitrary", "arbitrary"),
            vmem_limit_bytes=_vmem_limit(est)),
        name="outproj",
    )(hg, pool, x, w_out, mod, gpost, gpre, wr, br)


def _rank_kernel(ti_ref, chosen_ref, start_ref, dest_ref, carry_ref):
    i = pl.program_id(0)

    @pl.when(i == 0)
    def _():
        carry_ref[...] = start_ref[...]

    cnt = chosen_ref[...]
    tm = cnt.shape[0]
    row = lax.broadcasted_iota(I32, (tm, tm), 0)
    col = lax.broadcasted_iota(I32, (tm, tm), 1)
    before = jnp.where(col < row, 1.0, 0.0).astype(BF16)
    prefix = _dot(before, cnt.astype(BF16)) + carry_ref[...]
    rank = jnp.take_along_axis(prefix, ti_ref[...], axis=1)
    dest_ref[...] = (rank.T[:TOP_K, :] * SUBLANES).astype(I32)
    carry_ref[...] += jnp.sum(cnt, axis=0, keepdims=True)


def _rank(ti, chosen, starts):
    t = ti.shape[0]
    tm = min(512, t)
    return pl.pallas_call(
        _rank_kernel,
        out_shape=jax.ShapeDtypeStruct((TOP_K, t), I32),
        grid=(t // tm,),
        in_specs=[pl.BlockSpec((tm, LANES), lambda i: (i, 0)),
                  pl.BlockSpec((tm, LANES), lambda i: (i, 0)),
                  pl.BlockSpec((1, LANES), lambda i: (0, 0))],
        out_specs=pl.BlockSpec((TOP_K, tm), lambda i: (0, i)),
        scratch_shapes=[pltpu.VMEM((1, LANES), F32)],
        compiler_params=pltpu.CompilerParams(dimension_semantics=("arbitrary",)),
        name="rank",
    )(ti, chosen, starts)


def _slot_specs(tm, n_tiles, tile_of_step):
    return [pl.BlockSpec((tm,), functools.partial(lambda jj, i: (jj * n_tiles + tile_of_step(i),), jj),
                         memory_space=pltpu.SMEM) for jj in range(TOP_K)]


def _tile_rows(start):
    return pl.ds(pl.multiple_of(start, SUBLANES), SUBLANES)


def _gap_copies(e, gap_start_ref, gap_len_ref, zeros, xs_ref, sem):
    off, n = gap_start_ref[e], gap_len_ref[e]
    out = []
    for bit in range(_log2(MOE_BM)):
        size = (1 << bit) * SUBLANES
        used = (n >> bit) & 1
        out.append((used == 1,
                    pltpu.make_async_copy(zeros.at[pl.ds(0, size), :],
                                          xs_ref.at[pl.ds(pl.multiple_of(off * SUBLANES, SUBLANES), size), :],
                                          sem)))
        off = off + used * (1 << bit)
    return out


def _dispatch_kernel(*refs):
    dest_refs = refs[:TOP_K]
    gap_start_ref, gap_len_ref, n_live_ref, h_ref, xs_ref, zeros, sem, zsem = refs[TOP_K:]
    tm = h_ref.shape[0] // SUBLANES
    block_rows = MOE_BM * SUBLANES
    n_blocks = xs_ref.shape[0] // block_rows

    @pl.when(pl.program_id(0) == 0)
    def _():
        zeros[...] = jnp.zeros(zeros.shape, zeros.dtype)

        def tail_copy(blk):
            return pltpu.make_async_copy(
                zeros, xs_ref.at[pl.ds(pl.multiple_of(blk * block_rows, block_rows), block_rows), :], zsem)

        def tail_start(blk, carry):
            tail_copy(blk).start()
            return carry

        def tail_wait(blk, carry):
            tail_copy(blk).wait()
            return carry

        lax.fori_loop(n_live_ref[0], n_blocks, tail_start, 0)
        lax.fori_loop(n_live_ref[0], n_blocks, tail_wait, 0)

        def fill(e, carry):
            for used, cp in _gap_copies(e, gap_start_ref, gap_len_ref, zeros, xs_ref, zsem):
                @pl.when(used)
                def _(cp=cp):
                    cp.start()
            return carry

        def drain(e, carry):
            for used, cp in _gap_copies(e, gap_start_ref, gap_len_ref, zeros, xs_ref, zsem):
                @pl.when(used)
                def _(cp=cp):
                    cp.wait()
            return carry

        lax.fori_loop(0, N_EXPERTS, fill, 0)
        lax.fori_loop(0, N_EXPERTS, drain, 0)

    def start(r, carry):
        for jj in range(TOP_K):
            pltpu.make_async_copy(h_ref.at[_tile_rows(r * SUBLANES), :],
                                  xs_ref.at[_tile_rows(dest_refs[jj][r]), :],
                                  sem).start(priority=jj % 2)
        return carry

    lax.fori_loop(0, tm, start, 0, unroll=4)
    for jj in range(TOP_K):
        pltpu.make_async_copy(h_ref, xs_ref.at[pl.ds(0, tm * SUBLANES), :], sem).wait()


def _dispatch(dest, gap_start, gap_len, n_live, h2, n_sorted):
    rows, l = h2.shape
    t = rows // SUBLANES
    tm = min(512, t)
    smem = pl.BlockSpec(memory_space=pltpu.SMEM)
    return pl.pallas_call(
        _dispatch_kernel,
        out_shape=jax.ShapeDtypeStruct((n_sorted * SUBLANES, l), h2.dtype),
        grid=(t // tm,),
        in_specs=_slot_specs(tm, t // tm, lambda i: i) + [
            smem, smem, smem, pl.BlockSpec((tm * SUBLANES, l), lambda i: (i, 0))],
        out_specs=pl.BlockSpec(memory_space=pl.ANY),
        scratch_shapes=[pltpu.VMEM((MOE_BM * SUBLANES, l), h2.dtype),
                        pltpu.SemaphoreType.DMA, pltpu.SemaphoreType.DMA],
        compiler_params=pltpu.CompilerParams(dimension_semantics=("arbitrary",)),
        name="dispatch",
    )(*([dest] * TOP_K), gap_start, gap_len, n_live, h2)


def _moe_kernel(block_e_ref, next_e_ref, n_live_ref,
                x_ref, wg_ref, wu_ref, wd_ref, bg_ref, bu_ref, bd_ref, y_ref,
                wgs, wus, wds, wgb, wub, wdb, sem):
    i = pl.program_id(0)
    live = i < n_live_ref[0]
    e = block_e_ref[i]
    changed = jnp.logical_or(i == 0, e != block_e_ref[jnp.maximum(i - 1, 0)])

    def fetch(expert):
        return [pltpu.make_async_copy(w_ref.at[expert], stage, sem.at[n])
                for n, (w_ref, stage) in enumerate(((wg_ref, wgs), (wu_ref, wus), (wd_ref, wds)))]

    @pl.when(i == 0)
    def _():
        for cp in fetch(e):
            cp.start()

    @pl.when(jnp.logical_and(live, changed))
    def _():
        for cp in fetch(e):
            cp.wait()
        def cast(c, carry):
            rows = pl.ds(pl.multiple_of(c * LANES, LANES), LANES)
            for src, dst in ((wgs, wgb), (wus, wub), (wds, wdb)):
                dst[rows, :] = src[rows, :].astype(BF16)
            return carry

        lax.fori_loop(0, wgs.shape[0] // LANES, cast, 0)

        @pl.when(next_e_ref[i] != e)
        def _():
            for cp in fetch(next_e_ref[i]):
                cp.start()

    @pl.when(live)
    def _():
        x = _load_row_tiles(x_ref).astype(BF16)
        gate = jnp.minimum(_dot(x, wgb[...]) + bg_ref[...], SWIGLU_LIMIT)
        up = jnp.clip(_dot(x, wub[...]) + bu_ref[...], -SWIGLU_LIMIT, SWIGLU_LIMIT)
        hmid = (up + 1.0) * gate * _sigmoid(SWIGLU_ALPHA * gate)
        _store_row_tiles(y_ref, _dot(hmid.astype(BF16), wdb[...]) + bd_ref[...])

    @pl.when(jnp.logical_not(live))
    def _():
        y_ref[...] = jnp.zeros(y_ref.shape, y_ref.dtype)


def _moe(block_e, next_e, n_live, xs, wg, wu, wd, bg, bu, bd):
    d, f = wg.shape[-2:]
    n_blocks = block_e.shape[0]
    rows_in = pl.BlockSpec((MOE_BM * SUBLANES, LANES),
                           lambda i, be, ne, nl: (jnp.minimum(i, nl[0] - 1), 0))
    rows_out = pl.BlockSpec((MOE_BM * SUBLANES, LANES), lambda i, be, ne, nl: (i, 0))
    hbm = pl.BlockSpec(memory_space=pl.ANY)

    def bspec(n):
        return pl.BlockSpec((None, 1, n), lambda i, be, ne, nl: (be[i], 0, 0))

    est = 3 * d * f * 4 + 3 * d * f * 2 + 4 * MOE_BM * d * 4 + 4 * MOE_BM * f * 4
    return pl.pallas_call(
        _moe_kernel,
        out_shape=jax.ShapeDtypeStruct(xs.shape, F32),
        grid_spec=pltpu.PrefetchScalarGridSpec(
            num_scalar_prefetch=3,
            grid=(n_blocks,),
            in_specs=[rows_in, hbm, hbm, hbm, bspec(f), bspec(f), bspec(d)],
            out_specs=rows_out,
            scratch_shapes=[pltpu.VMEM((d, f), F32), pltpu.VMEM((d, f), F32), pltpu.VMEM((f, d), F32),
                            pltpu.VMEM((d, f), BF16), pltpu.VMEM((d, f), BF16), pltpu.VMEM((f, d), BF16),
                            pltpu.SemaphoreType.DMA((3,))]),
        compiler_params=pltpu.CompilerParams(
            dimension_semantics=("arbitrary",),
            vmem_limit_bytes=_vmem_limit(est)),
        name="moe",
    )(block_e, next_e, n_live, xs, wg, wu, wd, bg, bu, bd)


def _moe_layout(counts, n_rows):
    n_blocks = n_rows // MOE_BM + N_EXPERTS
    padded = (counts + MOE_BM - 1) // MOE_BM * MOE_BM
    ends = jnp.cumsum(padded)
    starts = ends - padded
    block_end = ends // MOE_BM
    i = jnp.arange(n_blocks, dtype=I32)
    ids = jnp.arange(N_EXPERTS, dtype=I32)
    last_e = jnp.max(jnp.where(counts > 0, ids, 0))
    block_e = jnp.minimum(jnp.sum(block_end[None, :] <= i[:, None], axis=1), last_e).astype(I32)
    later = jnp.logical_and(ids[None, :] > ids[:, None], counts[None, :] > 0)
    following = jnp.min(jnp.where(later, ids[None, :], N_EXPERTS), axis=1)
    following = jnp.where(following == N_EXPERTS, ids, following)
    next_e = jnp.sum(jnp.where(block_e[:, None] == ids[None, :], following[None, :], 0), axis=1)
    return (starts.astype(I32), (starts + counts).astype(I32), (padded - counts).astype(I32),
            block_e, next_e.astype(I32), block_end[-1:].astype(I32), n_blocks)


def _combine_kernel(*refs, tiles_per_batch):
    dest_refs, dnext_refs = refs[:TOP_K], refs[TOP_K:2 * TOP_K]
    ys_ref, tw_ref, x1_ref, mod_ref, gpost_ref, o_ref, buf, sem = refs[2 * TOP_K:]
    tm, d = x1_ref.shape
    i = pl.program_id(0)
    slot = i % 2

    def issue(d_refs, s):
        def start(r, carry):
            for jj in range(TOP_K):
                pltpu.make_async_copy(ys_ref.at[_tile_rows(d_refs[jj][r]), :],
                                      buf.at[s, jj, _tile_rows(r * SUBLANES), :],
                                      sem.at[s]).start(priority=jj % 2)
            return carry

        lax.fori_loop(0, tm, start, 0, unroll=4)

    @pl.when(i == 0)
    def _():
        issue(dest_refs, 0)

    @pl.when(i + 1 < pl.num_programs(0))
    def _():
        issue(dnext_refs, 1 - slot)

    for jj in range(TOP_K):
        pltpu.make_async_copy(ys_ref.at[pl.ds(0, tm * SUBLANES), :], buf.at[slot, jj],
                              sem.at[slot]).wait()

    tw = tw_ref[...]
    y = tw[:, 0:1] * _load_row_tiles(buf.at[slot, 0])
    for jj in range(1, TOP_K):
        y = y + tw[:, jj:jj + 1] * _load_row_tiles(buf.at[slot, jj])
    r = i // tiles_per_batch
    gt2 = mod_ref[pl.ds(r, 1), pl.ds(5 * d, d)]
    o_ref[...] = x1_ref[...] + gt2 * _rms(y, gpost_ref[...])


def _combine(dest, ys, tw, x1, mod, gpost, tokens_per_batch):
    t, d = x1.shape
    tm = min(512, tokens_per_batch)
    n_tiles = t // tm
    kern = functools.partial(_combine_kernel, tiles_per_batch=tokens_per_batch // tm)
    est = 2 * TOP_K * tm * d * 4 + 2 * (2 * tm * d * 4 + tm * LANES * 4) + 3 * tm * d * 4
    return pl.pallas_call(
        kern,
        out_shape=jax.ShapeDtypeStruct((t, d), F32),
        grid=(n_tiles,),
        in_specs=_slot_specs(tm, n_tiles, lambda i: i)
        + _slot_specs(tm, n_tiles, lambda i: jnp.minimum(i + 1, n_tiles - 1)) + [
                  pl.BlockSpec(memory_space=pl.ANY),
                  pl.BlockSpec((tm, LANES), lambda i: (i, 0)),
                  pl.BlockSpec((tm, d), lambda i: (i, 0)),
                  pl.BlockSpec(mod.shape, lambda i: (0, 0)),
                  pl.BlockSpec((1, d), lambda i: (0, 0))],
        out_specs=pl.BlockSpec((tm, d), lambda i: (i, 0)),
        scratch_shapes=[pltpu.VMEM((2, TOP_K, tm * SUBLANES, LANES), F32),
                        pltpu.SemaphoreType.DMA((2,))],
        compiler_params=pltpu.CompilerParams(
            dimension_semantics=("arbitrary",),
            vmem_limit_bytes=_vmem_limit(est)),
        name="combine",
    )(*([dest] * (2 * TOP_K)), ys, tw, x1, mod, gpost)


def kernel(x, c, ctx, c_ctx, w_ada, b_ada, g_pre_mix, g_post_mix, g_pre_ffn, g_post_ffn,
           w_in, w_out, hgrn_lb_logits, hgrn_norm, pool_w, pool_scale,
           w_router, b_router, w_gate, b_gate, w_up, b_up, w_down, b_down):
    b, s, d = x.shape
    layer = 0
    n_e = w_router.shape[-1]

    cc = jnp.concatenate([c, c_ctx[None, :]], axis=0)
    cc = jnp.pad(cc, ((0, -(b + 1) % SUBLANES), (0, 0)))
    mod = _ada(cc, w_ada[layer], b_ada[layer][None, :])

    w_in_b = w_in[layer].astype(BF16)
    lbl = hgrn_lb_logits[:2].reshape(4, HG_WIDTH)
    gn = hgrn_norm[layer][None, :]
    gpre = g_pre_mix[layer][None, :]

    lff_c, kf_c, lfb_c, kb_c, v_c = _inproj(ctx, mod, gpre, w_in_b, lbl, gn, mod_row=b, full=False)
    zeros = jnp.zeros((b, HG_HEADS, HEAD_DIM, HEAD_DIM), F32)
    (s_f,) = _hgrn(None, kf_c, v_c, lff_c, zeros, rev=False, need_o=False, want_state=True,
                   name="hgrn_ctx_fwd")
    (s_b,) = _hgrn(None, kb_c, v_c, lfb_c, zeros, rev=True, need_o=False, want_state=True,
                   name="hgrn_ctx_bwd")

    q, lff, kf, lfb, kb, v, gate, u = _inproj(x, mod, gpre, w_in_b, lbl, gn, mod_row=None, full=True)
    (o_f,) = _hgrn(q, kf, v, lff, s_f, rev=False, need_o=True, want_state=False, name="hgrn_fwd")
    (o_hg,) = _hgrn(q, kb, v, lfb, s_b, o_f, gate, rev=True, need_o=True, want_state=False,
                    name="hgrn_bwd")
    o_pool = _pool(u, pool_w[layer].astype(BF16), pool_scale[layer].reshape(len(POOL_WINDOWS), 1, POOL_GROUP))

    wr = jnp.pad(w_router[layer], ((0, 0), (0, LANES - n_e))).astype(BF16)
    br = jnp.pad(b_router[layer], (0, LANES - n_e))[None, :]
    x1, h2, ti, tw, chosen, cnt = _outproj(o_hg, o_pool, x, w_out[layer].astype(BF16), mod,
                                           g_post_mix[layer][None, :], g_pre_ffn[layer][None, :], wr, br)

    t = b * s
    starts, gap_start, gap_len, block_e, next_e, n_live, n_blocks = _moe_layout(
        cnt[0, :n_e].astype(I32), t * TOP_K)
    starts = jnp.pad(starts.astype(F32), (0, LANES - n_e))[None, :]
    dest = _rank(ti.reshape(t, LANES), chosen.reshape(t, LANES), starts).reshape(TOP_K * t)

    xs = _dispatch(dest, gap_start, gap_len, n_live, h2.reshape(t * d // LANES, LANES), n_blocks * MOE_BM)
    ys = _moe(block_e, next_e, n_live, xs, w_gate[layer], w_up[layer], w_down[layer],
              b_gate[layer][:, None, :], b_up[layer][:, None, :], b_down[layer][:, None, :])
    out = _combine(dest, ys, tw.reshape(t, LANES), x1.reshape(t, d), mod,
                   g_post_ffn[layer][None, :], s)
    return out.reshape(b, s, d)
```

```python
import functools
import math

import numpy as np
import jax
import jax.numpy as jnp
from jax import lax
from jax.experimental import pallas as pl
from jax.experimental.pallas import tpu as pltpu

F32 = jnp.float32
BF16 = jnp.bfloat16
I32 = jnp.int32

GRID_W = 64
HG_HEADS = 4
HEAD_DIM = 128
HG_WIDTH = HG_HEADS * HEAD_DIM
POOL_WINDOWS = (2, 4, 8, 16)
POOL_GROUP = 128
N_EXPERTS = 32
TOP_K = 4
SWIGLU_LIMIT = 7.0
SWIGLU_ALPHA = 1.702
EPS = 1e-6

LANES = 128
SUBLANES = 8
V7X_VMEM_BYTES = 64 * 1024 * 1024

TOKEN_TILE = 512
HGRN_BLOCK = 1024
ADA_COLS = 1536
DMA_UNROLL = 4
VMEM_COMPILER_BYTES = 4 << 20
VMEM_RESERVED_BYTES = 6 << 20
CHUNK = 64
HGRN_GROUP = 4
LEVELS = (32, 16, 8, 4, 2, 1)
FINE_LEVELS = (2, 1)
MOE_BM = 256
OUTPROJ_SLICES = 4
INPROJ_SLICES = 2
LOG2_E = math.log2(math.e)


def _vmem_limit(nbytes):
    return int(min(nbytes * 3 // 2 + VMEM_COMPILER_BYTES, V7X_VMEM_BYTES - VMEM_RESERVED_BYTES))


def _log2(n):
    assert n & (n - 1) == 0
    return n.bit_length() - 1


def _sigmoid(x):
    return 1.0 / (1.0 + jnp.exp(-x))


def _rms(x, gain):
    return x * lax.rsqrt(jnp.mean(x * x, axis=-1, keepdims=True) + EPS) * gain


def _dot(a, b):
    return jnp.dot(a, b, preferred_element_type=F32)


def _store_row_tiles(ref, val):
    rows = val.shape[0]
    for c in range(SUBLANES):
        ref[pl.ds(c, rows, stride=SUBLANES), :] = val[:, c * LANES:(c + 1) * LANES]


def _load_row_tiles(ref):
    rows = ref.shape[0] // SUBLANES
    return jnp.concatenate([ref[pl.ds(c, rows, stride=SUBLANES), :] for c in range(SUBLANES)], axis=1)


def _dot_nt(a, b):
    return lax.dot_general(a, b, (((1,), (1,)), ((), ())), preferred_element_type=F32)


def _ada_kernel(c_ref, w_ref, b_ref, o_ref):
    c = c_ref[...]
    s = (c * _sigmoid(c)).astype(BF16)
    o_ref[...] = _dot(s, w_ref[...].astype(BF16)) + b_ref[...]


def _ada(cc, w, b):
    rows, d = cc.shape
    n = w.shape[1]
    tn = ADA_COLS if n % ADA_COLS == 0 else n
    return pl.pallas_call(
        _ada_kernel,
        out_shape=jax.ShapeDtypeStruct((rows, n), F32),
        grid=(n // tn,),
        in_specs=[pl.BlockSpec((rows, d), lambda j: (0, 0)),
                  pl.BlockSpec((d, tn), lambda j: (0, j)),
                  pl.BlockSpec((1, tn), lambda j: (0, j))],
        out_specs=pl.BlockSpec((rows, tn), lambda j: (0, j)),
        compiler_params=pltpu.CompilerParams(
            dimension_semantics=("arbitrary",),
            vmem_limit_bytes=_vmem_limit(2 * d * tn * 4 + d * tn * 2)),
        name="ada",
    )(cc, w, b)


def _inproj_kernel(x_ref, mod_ref, gpre_ref, w_ref, lbl_ref, gn_ref, *outs, mod_row, full):
    d = x_ref.shape[-1]
    r = pl.program_id(0) if mod_row is None else mod_row
    sh = mod_ref[pl.ds(r, 1), pl.ds(0, d)]
    sc = mod_ref[pl.ds(r, 1), pl.ds(d, d)]
    tm = x_ref.shape[1]
    n_slices = INPROJ_SLICES if tm % (INPROJ_SLICES * SUBLANES) == 0 else 1
    sub = tm // n_slices
    slices = [pl.ds(i * sub, sub) for i in range(n_slices)]
    hs = [(_rms(x_ref[0, rows, :], gpre_ref[...]) * (1.0 + sc) + sh).astype(BF16) for rows in slices]

    def proj(g):
        return [_dot(h, w_ref[:, g * HG_WIDTH:(g + 1) * HG_WIDTH]) for h in hs]

    def put(ref, vals):
        for rows, val in zip(slices, vals):
            for hh in range(HG_HEADS):
                ref[0, hh, rows, :] = val[:, hh * HEAD_DIM:(hh + 1) * HEAD_DIM]

    def lower_bound(direction):
        l0 = lbl_ref[pl.ds(direction, 1), :]
        l1 = lbl_ref[pl.ds(2 + direction, 1), :]
        m = jnp.maximum(l0, l1)
        e0 = jnp.exp(l0 - m)
        return e0 / (e0 + jnp.exp(l1 - m))

    if full:
        q_o, lff_o, kf_o, lfb_o, kb_o, v_o, gate_o, u_o = outs
        put(q_o, [q * _sigmoid(q) for q in proj(0)])
    else:
        lff_o, kf_o, lfb_o, kb_o, v_o = outs
    for direction, (lf_o, k_o) in enumerate(((lff_o, kf_o), (lfb_o, kb_o))):
        lb = lower_bound(direction)
        sgs = [_sigmoid(z) for z in proj(1 + direction)]
        put(lf_o, [jnp.log(lb + (1.0 - lb) * sg) for sg in sgs])
        put(k_o, [(1.0 - lb) * (1.0 - sg) for sg in sgs])
    put(v_o, proj(3))
    if full:
        put(gate_o, [gn_ref[...] * _sigmoid(z) for z in proj(4)])
        put(u_o, proj(5))


def _inproj(x, mod, gpre, w_in, lbl, gn, *, mod_row, full):
    b, t, d = x.shape
    tm = min(TOKEN_TILE, t)
    n_out = 8 if full else 5
    n_cols = w_in.shape[1]
    head = pl.BlockSpec((1, HG_HEADS, tm, HEAD_DIM), lambda i, j: (i, 0, j, 0))
    kern = functools.partial(_inproj_kernel, mod_row=mod_row, full=full)
    est = 2 * tm * d * 4 + 2 * d * n_cols * 2 + n_out * 2 * tm * HG_WIDTH * 4 + 4 * tm * HG_WIDTH * 4
    return pl.pallas_call(
        kern,
        out_shape=[jax.ShapeDtypeStruct((b, HG_HEADS, t, HEAD_DIM), F32)] * n_out,
        grid=(b, t // tm),
        in_specs=[pl.BlockSpec((1, tm, d), lambda i, j: (i, j, 0)),
                  pl.BlockSpec(mod.shape, lambda i, j: (0, 0)),
                  pl.BlockSpec((1, d), lambda i, j: (0, 0)),
                  pl.BlockSpec(w_in.shape, lambda i, j: (0, 0)),
                  pl.BlockSpec(lbl.shape, lambda i, j: (0, 0)),
                  pl.BlockSpec((1, HG_WIDTH), lambda i, j: (0, 0))],
        out_specs=[head] * n_out,
        compiler_params=pltpu.CompilerParams(
            dimension_semantics=("arbitrary", "arbitrary"),
            vmem_limit_bytes=_vmem_limit(est)),
        name="inproj_full" if full else "inproj_ctx",
    )(x, mod, gpre, w_in, lbl, gn)


def _hgrn_tables(rev):
    row = np.arange(CHUNK)[:, None]
    col = np.arange(CHUNK)[None, :]
    tri = ((col >= row) if rev else (col <= row)).astype(np.float32)
    blocks, masks = [tri], []
    for h in LEVELS:
        if h in FINE_LEVELS:
            blocks.append(tri - tri[_ref_row(np.arange(CHUNK), h, rev)])
        is_q = ((row // h) % 2) == (0 if rev else 1)
        key_half = ((col // h) % 2) == (1 if rev else 0)
        masks.append(((row // (2 * h)) == (col // (2 * h))) & is_q & key_half)
    pair = np.stack([np.concatenate(masks[i:i + 2], axis=1) for i in range(0, len(LEVELS), 2)])
    cums = np.concatenate(blocks, axis=0)
    return jnp.asarray(np.concatenate([cums, cums], axis=1), BF16), jnp.asarray(pair, F32)


def _ref_row(t, h, rev):
    return (t // (2 * h)) * (2 * h) + (h if rev else h - 1)


def _hgrn_chunks(qs, ks, vs, lfs, sts, tabs, rev, need_o):
    cums_ref, pair_ref = tabs
    n_heads = len(sts)
    heads = range(len(ks))
    n_pairs = len(LEVELS) // 2
    n_cum = CHUNK * (1 + len(FINE_LEVELS)) if need_o else CHUNK
    cums = cums_ref[0:n_cum, :]

    his, los = [], []
    for h in heads:
        lf2 = lfs[h] * LOG2_E
        hi = lf2.astype(BF16)
        his.append(hi)
        los.append((lf2 - hi.astype(F32)).astype(BF16))
    both = _dot(cums, jnp.concatenate([jnp.concatenate(his, axis=1), jnp.concatenate(los, axis=1)], axis=0))
    bd = [both[:, h * LANES:(h + 1) * LANES] for h in heads]

    k16 = [ks[h].astype(BF16) for h in heads]
    v16 = [vs[h].astype(BF16) for h in heads]
    b = [bd[h][0:CHUNK] for h in heads]

    def level_diff(h, level):
        if level in FINE_LEVELS:
            i = FINE_LEVELS.index(level)
            return bd[h][CHUNK * (1 + i):CHUNK * (2 + i)]
        refs = [_ref_row(m * 2 * level, level, rev) for m in range(CHUNK // (2 * level))]
        pieces = [jnp.broadcast_to(b[h][r:r + 1, :], (2 * level, LANES)) for r in refs]
        return b[h] - (pieces[0] if len(pieces) == 1 else jnp.concatenate(pieces, axis=0))
    edge = [b[h][0:1, :] if rev else b[h][CHUNK - 1:CHUNK, :] for h in heads]
    kd = [k16[h] * jnp.exp2(edge[h] - b[h]).astype(BF16) for h in heads]
    grow = [lax.dot_general(v16[h], kd[h], (((0,), (0,)), ((), ())), preferred_element_type=F32)
            for h in heads]
    q16 = [qs[h].astype(BF16) for h in heads] if need_o else None
    st = list(sts)
    carried = []
    for h in heads:
        if need_o:
            carried.append(_dot_nt(q16[h] * jnp.exp2(b[h]).astype(BF16), st[h % n_heads].astype(BF16)))
        st[h % n_heads] = st[h % n_heads] * jnp.exp2(edge[h]) + grow[h]
    if not need_o:
        return None, st

    zero = jnp.zeros((CHUNK, LANES), BF16)
    att = [[] for _ in heads]
    for p in range(n_pairs):
        lhs, rhs = [], []
        for h in heads:
            qe, ke = [], []
            for i in (2 * p, 2 * p + 1):
                e = jnp.exp2(-jnp.abs(level_diff(h, LEVELS[i]))).astype(BF16)
                qe.append(q16[h] * e)
                ke.append(k16[h] * e)
            lhs.append(jnp.concatenate(qe, axis=1))
            rhs.append(jnp.concatenate([jnp.concatenate([ke[0], zero], axis=1),
                                        jnp.concatenate([zero, ke[1]], axis=1)], axis=0))
        scores = [_dot_nt(lhs[h], rhs[h]) for h in heads]
        for h in heads:
            att[h].append((scores[h] * pair_ref[p]).astype(BF16))
    o = []
    for h in heads:
        vals = jnp.concatenate([v16[h]] * (2 * n_pairs), axis=0)
        same_row = jnp.sum(qs[h] * ks[h], axis=1, keepdims=True) * vs[h]
        o.append(carried[h] + _dot(jnp.concatenate(att[h], axis=1), vals) + same_row)
    return o, st


def _hgrn_kernel(*refs, rev, need_o, final, want_state, n_chunks):
    it = iter(refs)
    tabs = tuple(next(it) for _ in range(2))
    q_ref = next(it) if need_o else None
    k_ref, v_ref, lf_ref, s0_ref = next(it), next(it), next(it), next(it)
    prev_ref = next(it) if final else None
    gate_ref = next(it) if final else None
    o_ref = next(it) if need_o else None
    sout_ref = next(it) if want_state else None
    st_ref = next(it)

    j = pl.program_id(1)
    nb = pl.num_programs(1)

    @pl.when(j == 0)
    def _():
        st_ref[...] = s0_ref[0]

    group = HGRN_GROUP if n_chunks % HGRN_GROUP == 0 else 1

    def body(gi, carry):
        rows = []
        for s in range(group):
            c = gi * group + s
            c = (n_chunks - 1 - c) if rev else c
            rows.append(pl.ds(pl.multiple_of(c * CHUNK, CHUNK), CHUNK))
        pairs = [(r, hh) for r in rows for hh in range(HG_HEADS)]
        o, st_new = _hgrn_chunks([q_ref[0, hh, r, :] for r, hh in pairs] if need_o else None,
                                 [k_ref[0, hh, r, :] for r, hh in pairs],
                                 [v_ref[0, hh, r, :] for r, hh in pairs],
                                 [lf_ref[0, hh, r, :] for r, hh in pairs],
                                 [st_ref[hh] for hh in range(HG_HEADS)], tabs, rev, need_o)
        for hh in range(HG_HEADS):
            st_ref[hh] = st_new[hh]
        if need_o:
            for (r, hh), out in zip(pairs, o):
                if final:
                    tot = out + prev_ref[0, hh, r, :]
                    out = tot * lax.rsqrt(jnp.mean(tot * tot, axis=-1, keepdims=True) + EPS)
                    out = out * gate_ref[0, hh, r, :]
                o_ref[0, hh, r, :] = out
        return carry

    lax.fori_loop(0, n_chunks // group, body, 0)

    if want_state:
        @pl.when(j == nb - 1)
        def _():
            sout_ref[0] = st_ref[...]


def _hgrn(q, k, v, lf, s0, prev=None, gate=None, *, rev, need_o, want_state, name):
    b, hh, t, _ = k.shape
    tb = min(HGRN_BLOCK, t)
    nb = t // tb
    final = prev is not None
    tabs = _hgrn_tables(rev)

    seq = pl.BlockSpec((1, hh, tb, HEAD_DIM), lambda i, j: (i, 0, (nb - 1 - j) if rev else j, 0))
    state = pl.BlockSpec((1, hh, HEAD_DIM, HEAD_DIM), lambda i, j: (i, 0, 0, 0))
    args = list(tabs)
    in_specs = [pl.BlockSpec(a.shape, functools.partial(lambda nd, i, j: (0,) * nd, a.ndim)) for a in tabs]
    for a in ((q,) if need_o else ()) + (k, v, lf):
        args.append(a)
        in_specs.append(seq)
    args.append(s0)
    in_specs.append(state)
    if final:
        args += [prev, gate]
        in_specs += [seq, seq]
    out_shape, out_specs = [], []
    if need_o:
        out_shape.append(jax.ShapeDtypeStruct((b, hh, t, HEAD_DIM), F32))
        out_specs.append(seq)
    if want_state:
        out_shape.append(jax.ShapeDtypeStruct((b, hh, HEAD_DIM, HEAD_DIM), F32))
        out_specs.append(state)
    kern = functools.partial(_hgrn_kernel, rev=rev, need_o=need_o, final=final,
                             want_state=want_state, n_chunks=tb // CHUNK)
    est = 2 * (len(args) + len(out_shape)) * hh * tb * HEAD_DIM * 4
    return pl.pallas_call(
        kern,
        out_shape=out_shape,
        grid=(b, nb),
        in_specs=in_specs,
        out_specs=out_specs,
        scratch_shapes=[pltpu.VMEM((hh, HEAD_DIM, HEAD_DIM), F32)],
        compiler_params=pltpu.CompilerParams(
            dimension_semantics=("arbitrary", "arbitrary"),
            vmem_limit_bytes=_vmem_limit(est)),
        name=name,
    )(*args)


def _window_sum(x, w, stride, pos, extent):
    n = x.shape[0]
    whole_tiles = stride % SUBLANES == 0 and stride * extent == n

    def ahead(y, dist):
        if whole_tiles:
            k = min(dist, extent) * stride
            return jnp.concatenate([y[k:], jnp.zeros((k, y.shape[1]), y.dtype)], axis=0)
        return jnp.where(pos + dist < extent, pltpu.roll(y, (n - dist * stride) % n, 0), 0.0)

    def behind(y, dist):
        if whole_tiles:
            k = min(dist, extent) * stride
            return jnp.concatenate([jnp.zeros((k, y.shape[1]), y.dtype), y[:n - k]], axis=0)
        return jnp.where(pos >= dist, pltpu.roll(y, dist * stride, 0), 0.0)

    fwd, bwd, h = x, x, 1
    while h < w // 2:
        fwd = fwd + ahead(fwd, h)
        bwd = bwd + behind(bwd, h)
        h *= 2
    return fwd + behind(bwd, 1)


def _window_count(pos, w, extent):
    return (jnp.minimum(pos + w // 2, extent) - jnp.maximum(pos - w // 2, 0)).astype(F32)


def _pool_kernel(u_ref, pw_ref, ps_ref, o_ref):
    g = pl.program_id(1)
    n = u_ref.shape[2]
    tok = lax.broadcasted_iota(I32, (n, LANES), 0)
    col = tok & (GRID_W - 1)
    row = tok >> _log2(GRID_W)
    for gi, w in enumerate(POOL_WINDOWS):
        @pl.when(g == gi)
        def _(w=w):
            u = u_ref[0, 0]
            rows = n // GRID_W
            s = _window_sum(_window_sum(u, w, GRID_W, row, rows), w, 1, col, GRID_W)
            m = s / (_window_count(row, w, rows) * _window_count(col, w, GRID_W))
            o_ref[0, 0] = _dot((m - u).astype(BF16), pw_ref[0]) * ps_ref[0]


def _pool(u, pw, ps):
    b, g, t, c = u.shape
    blk = pl.BlockSpec((1, 1, t, c), lambda i, j: (i, j, 0, 0))
    return pl.pallas_call(
        _pool_kernel,
        out_shape=jax.ShapeDtypeStruct(u.shape, F32),
        grid=(b, g),
        in_specs=[blk,
                  pl.BlockSpec((1, c, c), lambda i, j: (j, 0, 0)),
                  pl.BlockSpec((1, 1, c), lambda i, j: (j, 0, 0))],
        out_specs=blk,
        compiler_params=pltpu.CompilerParams(
            dimension_semantics=("arbitrary", "arbitrary"),
            vmem_limit_bytes=_vmem_limit(12 * t * c * 4)),
        name="pool",
    )(u, pw, ps)


def _outproj_kernel(hg_ref, pool_ref, x_ref, w_ref, mod_ref, gpost_ref, gpre_ref, wr_ref, br_ref,
                    x1_ref, h2_ref, ti_ref, tw_ref, chosen_ref, cnt_ref):
    d = x_ref.shape[-1]
    r = pl.program_id(0)

    @pl.when(jnp.logical_and(r == 0, pl.program_id(1) == 0))
    def _():
        cnt_ref[...] = jnp.zeros(cnt_ref.shape, F32)

    gt1 = mod_ref[pl.ds(r, 1), pl.ds(2 * d, d)]
    sh2 = mod_ref[pl.ds(r, 1), pl.ds(3 * d, d)]
    sc2 = mod_ref[pl.ds(r, 1), pl.ds(4 * d, d)]
    tm = x_ref.shape[1]
    sub = tm // OUTPROJ_SLICES
    slices = [pl.ds(i * sub, sub) for i in range(OUTPROJ_SLICES)]
    ys = []
    for rows in slices:
        cat = jnp.concatenate([hg_ref[0, hh, rows, :] for hh in range(HG_HEADS)]
                              + [pool_ref[0, g, rows, :] for g in range(len(POOL_WINDOWS))], axis=1)
        ys.append(_dot(cat.astype(BF16), w_ref[...]))
    logits = []
    for i, rows in enumerate(slices):
        x1 = x_ref[0, rows, :] + gt1 * _rms(ys[i], gpost_ref[...])
        x1_ref[0, rows, :] = x1
        h2 = _rms(x1, gpre_ref[...]) * (1.0 + sc2) + sh2
        _store_row_tiles(h2_ref.at[0, pl.ds(i * sub * SUBLANES, sub * SUBLANES), :], h2)
        logits.append(_dot(h2.astype(BF16), wr_ref[...]) + br_ref[...])

    lane = lax.broadcasted_iota(I32, (sub, LANES), 1).astype(F32)
    picked = jnp.zeros((1, LANES), F32)
    for i, rows in enumerate(slices):
        lg = jnp.where(lane < N_EXPERTS, logits[i], -jnp.inf)
        ti = jnp.zeros(lg.shape, F32)
        tw = jnp.zeros(lg.shape, F32)
        chosen = jnp.zeros(lg.shape, F32)
        top = None
        for jj in range(TOP_K):
            m = jnp.max(lg, axis=1, keepdims=True)
            idx = jnp.min(jnp.where(lg == m, lane, float(LANES)), axis=1, keepdims=True)
            lg = jnp.where(lane == idx, -jnp.inf, lg)
            chosen = jnp.where(lane == idx, 1.0, chosen)
            top = m if top is None else top
            ti = jnp.where(lane == jj, idx, ti)
            tw = jnp.where(lane == jj, jnp.exp(m - top), tw)
        ti_ref[0, rows, :] = ti.astype(I32)
        tw_ref[0, rows, :] = tw / jnp.sum(tw, axis=1, keepdims=True)
        chosen_ref[0, rows, :] = chosen
        picked = picked + jnp.sum(chosen, axis=0, keepdims=True)
    cnt_ref[...] += picked


def _outproj(hg, pool, x, w_out, mod, gpost, gpre, wr, br):
    b, t, d = x.shape
    tm = min(TOKEN_TILE, t)
    head = pl.BlockSpec((1, HG_HEADS, tm, HEAD_DIM), lambda i, j: (i, 0, j, 0))
    tok = pl.BlockSpec((1, tm, d), lambda i, j: (i, j, 0))
    lanes = pl.BlockSpec((1, tm, LANES), lambda i, j: (i, j, 0))
    vec = pl.BlockSpec((1, d), lambda i, j: (0, 0))
    est = 2 * (2 * tm * HG_WIDTH * 4 + 3 * tm * d * 4 + 2 * tm * LANES * 4) + 2 * d * d * 2 + 6 * tm * d * 4
    return pl.pallas_call(
        _outproj_kernel,
        out_shape=[jax.ShapeDtypeStruct((b, t, d), F32),
                   jax.ShapeDtypeStruct((b, t * d // LANES, LANES), F32),
                   jax.ShapeDtypeStruct((b, t, LANES), I32), jax.ShapeDtypeStruct((b, t, LANES), F32),
                   jax.ShapeDtypeStruct((b, t, LANES), F32), jax.ShapeDtypeStruct((1, LANES), F32)],
        grid=(b, t // tm),
        in_specs=[head, head, tok,
                  pl.BlockSpec(w_out.shape, lambda i, j: (0, 0)),
                  pl.BlockSpec(mod.shape, lambda i, j: (0, 0)),
                  vec, vec,
                  pl.BlockSpec(wr.shape, lambda i, j: (0, 0)),
                  pl.BlockSpec((1, LANES), lambda i, j: (0, 0))],
        out_specs=[tok, pl.BlockSpec((1, tm * d // LANES, LANES), lambda i, j: (i, j, 0)),
                   lanes, lanes, lanes, pl.BlockSpec((1, LANES), lambda i, j: (0, 0))],
        compiler_params=pltpu.CompilerParams(
            dimension_semantics=("arbitrary", "arbitrary"),
            vmem_limit_bytes=_vmem_limit(est)),
        name="outproj",
    )(hg, pool, x, w_out, mod, gpost, gpre, wr, br)


def _rank_kernel(ti_ref, chosen_ref, start_ref, dest_ref, carry_ref):
    i = pl.program_id(0)

    @pl.when(i == 0)
    def _():
        carry_ref[...] = start_ref[...]

    cnt = chosen_ref[...]
    tm = cnt.shape[0]
    row = lax.broadcasted_iota(I32, (tm, tm), 0)
    col = lax.broadcasted_iota(I32, (tm, tm), 1)
    before = jnp.where(col < row, 1.0, 0.0).astype(BF16)
    prefix = _dot(before, cnt.astype(BF16)) + carry_ref[...]
    rank = jnp.take_along_axis(prefix, ti_ref[...], axis=1)
    dest_ref[...] = (rank.T[:TOP_K, :] * SUBLANES).astype(I32)
    carry_ref[...] += jnp.sum(cnt, axis=0, keepdims=True)


def _rank(ti, chosen, starts):
    t = ti.shape[0]
    tm = min(TOKEN_TILE, t)
    return pl.pallas_call(
        _rank_kernel,
        out_shape=jax.ShapeDtypeStruct((TOP_K, t), I32),
        grid=(t // tm,),
        in_specs=[pl.BlockSpec((tm, LANES), lambda i: (i, 0)),
                  pl.BlockSpec((tm, LANES), lambda i: (i, 0)),
                  pl.BlockSpec((1, LANES), lambda i: (0, 0))],
        out_specs=pl.BlockSpec((TOP_K, tm), lambda i: (0, i)),
        scratch_shapes=[pltpu.VMEM((1, LANES), F32)],
        compiler_params=pltpu.CompilerParams(dimension_semantics=("arbitrary",)),
        name="rank",
    )(ti, chosen, starts)


def _slot_specs(tm, n_tiles, tile_of_step):
    return [pl.BlockSpec((tm,), functools.partial(lambda jj, i: (jj * n_tiles + tile_of_step(i),), jj),
                         memory_space=pltpu.SMEM) for jj in range(TOP_K)]


def _tile_rows(start):
    return pl.ds(pl.multiple_of(start, SUBLANES), SUBLANES)


def _gap_copies(e, gap_start_ref, gap_len_ref, zeros, xs_ref, sem):
    off, n = gap_start_ref[e], gap_len_ref[e]
    out = []
    for bit in range(_log2(MOE_BM)):
        size = (1 << bit) * SUBLANES
        used = (n >> bit) & 1
        out.append((used == 1,
                    pltpu.make_async_copy(zeros.at[pl.ds(0, size), :],
                                          xs_ref.at[pl.ds(pl.multiple_of(off * SUBLANES, SUBLANES), size), :],
                                          sem)))
        off = off + used * (1 << bit)
    return out


def _dispatch_kernel(*refs):
    dest_refs = refs[:TOP_K]
    gap_start_ref, gap_len_ref, n_live_ref, h_ref, xs_ref, zeros, sem, zsem = refs[TOP_K:]
    tm = h_ref.shape[0] // SUBLANES
    block_rows = MOE_BM * SUBLANES
    n_blocks = xs_ref.shape[0] // block_rows

    @pl.when(pl.program_id(0) == 0)
    def _():
        zeros[...] = jnp.zeros(zeros.shape, zeros.dtype)

        def tail_copy(blk):
            return pltpu.make_async_copy(
                zeros, xs_ref.at[pl.ds(pl.multiple_of(blk * block_rows, block_rows), block_rows), :], zsem)

        def tail_start(blk, carry):
            tail_copy(blk).start()
            return carry

        def tail_wait(blk, carry):
            tail_copy(blk).wait()
            return carry

        lax.fori_loop(n_live_ref[0], n_blocks, tail_start, 0)
        lax.fori_loop(n_live_ref[0], n_blocks, tail_wait, 0)

        def fill(e, carry):
            for used, cp in _gap_copies(e, gap_start_ref, gap_len_ref, zeros, xs_ref, zsem):
                @pl.when(used)
                def _(cp=cp):
                    cp.start()
            return carry

        def drain(e, carry):
            for used, cp in _gap_copies(e, gap_start_ref, gap_len_ref, zeros, xs_ref, zsem):
                @pl.when(used)
                def _(cp=cp):
                    cp.wait()
            return carry

        lax.fori_loop(0, N_EXPERTS, fill, 0)
        lax.fori_loop(0, N_EXPERTS, drain, 0)

    def start(r, carry):
        for jj in range(TOP_K):
            pltpu.make_async_copy(h_ref.at[_tile_rows(r * SUBLANES), :],
                                  xs_ref.at[_tile_rows(dest_refs[jj][r]), :],
                                  sem).start(priority=jj % 2)
        return carry

    lax.fori_loop(0, tm, start, 0, unroll=DMA_UNROLL)
    for jj in range(TOP_K):
        pltpu.make_async_copy(h_ref, xs_ref.at[pl.ds(0, tm * SUBLANES), :], sem).wait()


def _dispatch(dest, gap_start, gap_len, n_live, h2, n_sorted):
    rows, l = h2.shape
    t = rows // SUBLANES
    tm = min(TOKEN_TILE, t)
    smem = pl.BlockSpec(memory_space=pltpu.SMEM)
    return pl.pallas_call(
        _dispatch_kernel,
        out_shape=jax.ShapeDtypeStruct((n_sorted * SUBLANES, l), h2.dtype),
        grid=(t // tm,),
        in_specs=_slot_specs(tm, t // tm, lambda i: i) + [
            smem, smem, smem, pl.BlockSpec((tm * SUBLANES, l), lambda i: (i, 0))],
        out_specs=pl.BlockSpec(memory_space=pl.ANY),
        scratch_shapes=[pltpu.VMEM((MOE_BM * SUBLANES, l), h2.dtype),
                        pltpu.SemaphoreType.DMA, pltpu.SemaphoreType.DMA],
        compiler_params=pltpu.CompilerParams(dimension_semantics=("arbitrary",)),
        name="dispatch",
    )(*([dest] * TOP_K), gap_start, gap_len, n_live, h2)


def _moe_kernel(block_e_ref, next_e_ref, n_live_ref,
                x_ref, wg_ref, wu_ref, wd_ref, bg_ref, bu_ref, bd_ref, y_ref,
                wgs, wus, wds, wgb, wub, wdb, sem):
    i = pl.program_id(0)
    live = i < n_live_ref[0]
    e = block_e_ref[i]
    changed = jnp.logical_or(i == 0, e != block_e_ref[jnp.maximum(i - 1, 0)])

    def fetch(expert):
        return [pltpu.make_async_copy(w_ref.at[expert], stage, sem.at[n])
                for n, (w_ref, stage) in enumerate(((wg_ref, wgs), (wu_ref, wus), (wd_ref, wds)))]

    @pl.when(i == 0)
    def _():
        for cp in fetch(e):
            cp.start()

    @pl.when(jnp.logical_and(live, changed))
    def _():
        for cp in fetch(e):
            cp.wait()
        def cast(c, carry):
            rows = pl.ds(pl.multiple_of(c * LANES, LANES), LANES)
            for src, dst in ((wgs, wgb), (wus, wub), (wds, wdb)):
                dst[rows, :] = src[rows, :].astype(BF16)
            return carry

        lax.fori_loop(0, wgs.shape[0] // LANES, cast, 0)

        @pl.when(next_e_ref[i] != e)
        def _():
            for cp in fetch(next_e_ref[i]):
                cp.start()

    @pl.when(live)
    def _():
        x = _load_row_tiles(x_ref).astype(BF16)
        gate = jnp.minimum(_dot(x, wgb[...]) + bg_ref[...], SWIGLU_LIMIT)
        up = jnp.clip(_dot(x, wub[...]) + bu_ref[...], -SWIGLU_LIMIT, SWIGLU_LIMIT)
        hmid = (up + 1.0) * gate * _sigmoid(SWIGLU_ALPHA * gate)
        _store_row_tiles(y_ref, _dot(hmid.astype(BF16), wdb[...]) + bd_ref[...])

    @pl.when(jnp.logical_not(live))
    def _():
        y_ref[...] = jnp.zeros(y_ref.shape, y_ref.dtype)


def _moe(block_e, next_e, n_live, xs, wg, wu, wd, bg, bu, bd):
    d, f = wg.shape[-2:]
    n_blocks = block_e.shape[0]
    rows_in = pl.BlockSpec((MOE_BM * SUBLANES, LANES),
                           lambda i, be, ne, nl: (jnp.minimum(i, nl[0] - 1), 0))
    rows_out = pl.BlockSpec((MOE_BM * SUBLANES, LANES), lambda i, be, ne, nl: (i, 0))
    hbm = pl.BlockSpec(memory_space=pl.ANY)

    def bspec(n):
        return pl.BlockSpec((None, 1, n), lambda i, be, ne, nl: (be[i], 0, 0))

    est = 3 * d * f * 4 + 3 * d * f * 2 + 4 * MOE_BM * d * 4 + 4 * MOE_BM * f * 4
    return pl.pallas_call(
        _moe_kernel,
        out_shape=jax.ShapeDtypeStruct(xs.shape, F32),
        grid_spec=pltpu.PrefetchScalarGridSpec(
            num_scalar_prefetch=3,
            grid=(n_blocks,),
            in_specs=[rows_in, hbm, hbm, hbm, bspec(f), bspec(f), bspec(d)],
            out_specs=rows_out,
            scratch_shapes=[pltpu.VMEM((d, f), F32), pltpu.VMEM((d, f), F32), pltpu.VMEM((f, d), F32),
                            pltpu.VMEM((d, f), BF16), pltpu.VMEM((d, f), BF16), pltpu.VMEM((f, d), BF16),
                            pltpu.SemaphoreType.DMA((3,))]),
        compiler_params=pltpu.CompilerParams(
            dimension_semantics=("arbitrary",),
            vmem_limit_bytes=_vmem_limit(est)),
        name="moe",
    )(block_e, next_e, n_live, xs, wg, wu, wd, bg, bu, bd)


def _moe_layout(counts, n_rows):
    n_blocks = n_rows // MOE_BM + N_EXPERTS
    padded = (counts + MOE_BM - 1) // MOE_BM * MOE_BM
    ends = jnp.cumsum(padded)
    starts = ends - padded
    block_end = ends // MOE_BM
    i = jnp.arange(n_blocks, dtype=I32)
    ids = jnp.arange(N_EXPERTS, dtype=I32)
    last_e = jnp.max(jnp.where(counts > 0, ids, 0))
    block_e = jnp.minimum(jnp.sum(block_end[None, :] <= i[:, None], axis=1), last_e).astype(I32)
    later = jnp.logical_and(ids[None, :] > ids[:, None], counts[None, :] > 0)
    following = jnp.min(jnp.where(later, ids[None, :], N_EXPERTS), axis=1)
    following = jnp.where(following == N_EXPERTS, ids, following)
    next_e = jnp.sum(jnp.where(block_e[:, None] == ids[None, :], following[None, :], 0), axis=1)
    return (starts.astype(I32), (starts + counts).astype(I32), (padded - counts).astype(I32),
            block_e, next_e.astype(I32), block_end[-1:].astype(I32), n_blocks)


def _combine_kernel(*refs, tiles_per_batch):
    dest_refs, dnext_refs = refs[:TOP_K], refs[TOP_K:2 * TOP_K]
    ys_ref, tw_ref, x1_ref, mod_ref, gpost_ref, o_ref, buf, sem = refs[2 * TOP_K:]
    tm, d = x1_ref.shape
    i = pl.program_id(0)
    slot = i % 2

    def issue(d_refs, s):
        def start(r, carry):
            for jj in range(TOP_K):
                pltpu.make_async_copy(ys_ref.at[_tile_rows(d_refs[jj][r]), :],
                                      buf.at[s, jj, _tile_rows(r * SUBLANES), :],
                                      sem.at[s]).start(priority=jj % 2)
            return carry

        lax.fori_loop(0, tm, start, 0, unroll=DMA_UNROLL)

    @pl.when(i == 0)
    def _():
        issue(dest_refs, 0)

    @pl.when(i + 1 < pl.num_programs(0))
    def _():
        issue(dnext_refs, 1 - slot)

    for jj in range(TOP_K):
        pltpu.make_async_copy(ys_ref.at[pl.ds(0, tm * SUBLANES), :], buf.at[slot, jj],
                              sem.at[slot]).wait()

    tw = tw_ref[...]
    y = tw[:, 0:1] * _load_row_tiles(buf.at[slot, 0])
    for jj in range(1, TOP_K):
        y = y + tw[:, jj:jj + 1] * _load_row_tiles(buf.at[slot, jj])
    r = i // tiles_per_batch
    gt2 = mod_ref[pl.ds(r, 1), pl.ds(5 * d, d)]
    o_ref[...] = x1_ref[...] + gt2 * _rms(y, gpost_ref[...])


def _combine(dest, ys, tw, x1, mod, gpost, tokens_per_batch):
    t, d = x1.shape
    tm = min(TOKEN_TILE, tokens_per_batch)
    n_tiles = t // tm
    kern = functools.partial(_combine_kernel, tiles_per_batch=tokens_per_batch // tm)
    est = 2 * TOP_K * tm * d * 4 + 2 * (2 * tm * d * 4 + tm * LANES * 4) + 3 * tm * d * 4
    return pl.pallas_call(
        kern,
        out_shape=jax.ShapeDtypeStruct((t, d), F32),
        grid=(n_tiles,),
        in_specs=_slot_specs(tm, n_tiles, lambda i: i)
        + _slot_specs(tm, n_tiles, lambda i: jnp.minimum(i + 1, n_tiles - 1)) + [
                  pl.BlockSpec(memory_space=pl.ANY),
                  pl.BlockSpec((tm, LANES), lambda i: (i, 0)),
                  pl.BlockSpec((tm, d), lambda i: (i, 0)),
                  pl.BlockSpec(mod.shape, lambda i: (0, 0)),
                  pl.BlockSpec((1, d), lambda i: (0, 0))],
        out_specs=pl.BlockSpec((tm, d), lambda i: (i, 0)),
        scratch_shapes=[pltpu.VMEM((2, TOP_K, tm * SUBLANES, LANES), F32),
                        pltpu.SemaphoreType.DMA((2,))],
        compiler_params=pltpu.CompilerParams(
            dimension_semantics=("arbitrary",),
            vmem_limit_bytes=_vmem_limit(est)),
        name="combine",
    )(*([dest] * (2 * TOP_K)), ys, tw, x1, mod, gpost)


def kernel(x, c, ctx, c_ctx, w_ada, b_ada, g_pre_mix, g_post_mix, g_pre_ffn, g_post_ffn,
           w_in, w_out, hgrn_lb_logits, hgrn_norm, pool_w, pool_scale,
           w_router, b_router, w_gate, b_gate, w_up, b_up, w_down, b_down):
    b, s, d = x.shape
    layer = 0
    n_e = w_router.shape[-1]

    cc = jnp.concatenate([c, c_ctx[None, :]], axis=0)
    cc = jnp.pad(cc, ((0, -(b + 1) % SUBLANES), (0, 0)))
    mod = _ada(cc, w_ada[layer], b_ada[layer][None, :])

    w_in_b = w_in[layer].astype(BF16)
    lbl = hgrn_lb_logits[:2].reshape(4, HG_WIDTH)
    gn = hgrn_norm[layer][None, :]
    gpre = g_pre_mix[layer][None, :]

    lff_c, kf_c, lfb_c, kb_c, v_c = _inproj(ctx, mod, gpre, w_in_b, lbl, gn, mod_row=b, full=False)
    zeros = jnp.zeros((b, HG_HEADS, HEAD_DIM, HEAD_DIM), F32)
    (s_f,) = _hgrn(None, kf_c, v_c, lff_c, zeros, rev=False, need_o=False, want_state=True,
                   name="hgrn_ctx_fwd")
    (s_b,) = _hgrn(None, kb_c, v_c, lfb_c, zeros, rev=True, need_o=False, want_state=True,
                   name="hgrn_ctx_bwd")

    q, lff, kf, lfb, kb, v, gate, u = _inproj(x, mod, gpre, w_in_b, lbl, gn, mod_row=None, full=True)
    (o_f,) = _hgrn(q, kf, v, lff, s_f, rev=False, need_o=True, want_state=False, name="hgrn_fwd")
    (o_hg,) = _hgrn(q, kb, v, lfb, s_b, o_f, gate, rev=True, need_o=True, want_state=False,
                    name="hgrn_bwd")
    o_pool = _pool(u, pool_w[layer].astype(BF16), pool_scale[layer].reshape(len(POOL_WINDOWS), 1, POOL_GROUP))

    wr = jnp.pad(w_router[layer], ((0, 0), (0, LANES - n_e))).astype(BF16)
    br = jnp.pad(b_router[layer], (0, LANES - n_e))[None, :]
    x1, h2, ti, tw, chosen, cnt = _outproj(o_hg, o_pool, x, w_out[layer].astype(BF16), mod,
                                           g_post_mix[layer][None, :], g_pre_ffn[layer][None, :], wr, br)

    t = b * s
    starts, gap_start, gap_len, block_e, next_e, n_live, n_blocks = _moe_layout(
        cnt[0, :n_e].astype(I32), t * TOP_K)
    starts = jnp.pad(starts.astype(F32), (0, LANES - n_e))[None, :]
    dest = _rank(ti.reshape(t, LANES), chosen.reshape(t, LANES), starts).reshape(TOP_K * t)

    xs = _dispatch(dest, gap_start, gap_len, n_live, h2.reshape(t * d // LANES, LANES), n_blocks * MOE_BM)
    ys = _moe(block_e, next_e, n_live, xs, w_gate[layer], w_up[layer], w_down[layer],
              b_gate[layer][:, None, :], b_up[layer][:, None, :], b_down[layer][:, None, :])
    out = _combine(dest, ys, tw.reshape(t, LANES), x1.reshape(t, d), mod,
                   g_post_ffn[layer][None, :], s)
    return out.reshape(b, s, d)
```

```python
import functools
import math

import numpy as np
import jax
import jax.numpy as jnp
from jax import lax
from jax.experimental import pallas as pl
from jax.experimental.pallas import tpu as pltpu

F32 = jnp.float32
BF16 = jnp.bfloat16
I32 = jnp.int32

GRID_W = 64
HG_HEADS = 4
HEAD_DIM = 128
HG_WIDTH = HG_HEADS * HEAD_DIM
POOL_WINDOWS = (2, 4, 8, 16)
POOL_GROUP = 128
N_EXPERTS = 32
TOP_K = 4
SWIGLU_LIMIT = 7.0
SWIGLU_ALPHA = 1.702
EPS = 1e-6

LANES = 128
SUBLANES = 8
V7X_VMEM_BYTES = 64 * 1024 * 1024

TOKEN_TILE = 512
HGRN_BLOCK = 1024
ADA_COLS = 1536
DMA_UNROLL = 4
VMEM_COMPILER_BYTES = 4 << 20
VMEM_RESERVED_BYTES = 6 << 20
CHUNK = 64
HGRN_GROUP = 4
LEVELS = (32, 16, 8, 4, 2, 1)
FINE_LEVELS = (2, 1)
MOE_BM = 256
OUTPROJ_SLICES = 4
INPROJ_SLICES = 2
LOG2_E = math.log2(math.e)


def _vmem_limit(nbytes):
    return int(min(nbytes * 3 // 2 + VMEM_COMPILER_BYTES, V7X_VMEM_BYTES - VMEM_RESERVED_BYTES))


def _log2(n):
    assert n & (n - 1) == 0
    return n.bit_length() - 1


def _sigmoid(x):
    return 1.0 / (1.0 + jnp.exp(-x))


def _rms(x, gain):
    return x * lax.rsqrt(jnp.mean(x * x, axis=-1, keepdims=True) + EPS) * gain


def _dot(a, b):
    return jnp.dot(a, b, preferred_element_type=F32)


def _store_row_tiles(ref, val):
    rows = val.shape[0]
    for c in range(SUBLANES):
        ref[pl.ds(c, rows, stride=SUBLANES), :] = val[:, c * LANES:(c + 1) * LANES]


def _load_row_tiles(ref):
    rows = ref.shape[0] // SUBLANES
    return jnp.concatenate([ref[pl.ds(c, rows, stride=SUBLANES), :] for c in range(SUBLANES)], axis=1)


def _dot_nt(a, b):
    return lax.dot_general(a, b, (((1,), (1,)), ((), ())), preferred_element_type=F32)


def _ada_kernel(c_ref, w_ref, b_ref, o_ref):
    c = c_ref[...]
    s = (c * _sigmoid(c)).astype(BF16)
    o_ref[...] = _dot(s, w_ref[...].astype(BF16)) + b_ref[...]


def _ada(cc, w, b):
    rows, d = cc.shape
    n = w.shape[1]
    tn = ADA_COLS if n % ADA_COLS == 0 else n
    return pl.pallas_call(
        _ada_kernel,
        out_shape=jax.ShapeDtypeStruct((rows, n), F32),
        grid=(n // tn,),
        in_specs=[pl.BlockSpec((rows, d), lambda j: (0, 0)),
                  pl.BlockSpec((d, tn), lambda j: (0, j)),
                  pl.BlockSpec((1, tn), lambda j: (0, j))],
        out_specs=pl.BlockSpec((rows, tn), lambda j: (0, j)),
        compiler_params=pltpu.CompilerParams(
            dimension_semantics=("arbitrary",),
            vmem_limit_bytes=_vmem_limit(2 * d * tn * 4 + d * tn * 2)),
        name="ada",
    )(cc, w, b)


def _inproj_kernel(x_ref, mod_ref, gpre_ref, w_ref, lbl_ref, gn_ref, *outs, mod_row, full):
    d = x_ref.shape[-1]
    r = pl.program_id(0) if mod_row is None else mod_row
    sh = mod_ref[pl.ds(r, 1), pl.ds(0, d)]
    sc = mod_ref[pl.ds(r, 1), pl.ds(d, d)]
    tm = x_ref.shape[1]
    n_slices = INPROJ_SLICES if tm % (INPROJ_SLICES * SUBLANES) == 0 else 1
    sub = tm // n_slices
    slices = [pl.ds(i * sub, sub) for i in range(n_slices)]
    hs = [(_rms(x_ref[0, rows, :], gpre_ref[...]) * (1.0 + sc) + sh).astype(BF16) for rows in slices]

    def proj(g):
        return [_dot(h, w_ref[:, g * HG_WIDTH:(g + 1) * HG_WIDTH]) for h in hs]

    def put(ref, vals):
        for rows, val in zip(slices, vals):
            for hh in range(HG_HEADS):
                ref[0, hh, rows, :] = val[:, hh * HEAD_DIM:(hh + 1) * HEAD_DIM]

    def lower_bound(direction):
        l0 = lbl_ref[pl.ds(direction, 1), :]
        l1 = lbl_ref[pl.ds(2 + direction, 1), :]
        m = jnp.maximum(l0, l1)
        e0 = jnp.exp(l0 - m)
        return e0 / (e0 + jnp.exp(l1 - m))

    if full:
        q_o, lff_o, kf_o, lfb_o, kb_o, v_o, gate_o, u_o = outs
        put(q_o, [q * _sigmoid(q) for q in proj(0)])
    else:
        lff_o, kf_o, lfb_o, kb_o, v_o = outs
    for direction, (lf_o, k_o) in enumerate(((lff_o, kf_o), (lfb_o, kb_o))):
        lb = lower_bound(direction)
        sgs = [_sigmoid(z) for z in proj(1 + direction)]
        put(lf_o, [jnp.log(lb + (1.0 - lb) * sg) for sg in sgs])
        put(k_o, [(1.0 - lb) * (1.0 - sg) for sg in sgs])
    put(v_o, proj(3))
    if full:
        put(gate_o, [gn_ref[...] * _sigmoid(z) for z in proj(4)])
        put(u_o, proj(5))


def _inproj(x, mod, gpre, w_in, lbl, gn, *, mod_row, full):
    b, t, d = x.shape
    tm = min(TOKEN_TILE, t)
    n_out = 8 if full else 5
    n_cols = w_in.shape[1]
    head = pl.BlockSpec((1, HG_HEADS, tm, HEAD_DIM), lambda i, j: (i, 0, j, 0))
    kern = functools.partial(_inproj_kernel, mod_row=mod_row, full=full)
    est = 2 * tm * d * 4 + 2 * d * n_cols * 2 + n_out * 2 * tm * HG_WIDTH * 4 + 4 * tm * HG_WIDTH * 4
    return pl.pallas_call(
        kern,
        out_shape=[jax.ShapeDtypeStruct((b, HG_HEADS, t, HEAD_DIM), F32)] * n_out,
        grid=(b, t // tm),
        in_specs=[pl.BlockSpec((1, tm, d), lambda i, j: (i, j, 0)),
                  pl.BlockSpec(mod.shape, lambda i, j: (0, 0)),
                  pl.BlockSpec((1, d), lambda i, j: (0, 0)),
                  pl.BlockSpec(w_in.shape, lambda i, j: (0, 0)),
                  pl.BlockSpec(lbl.shape, lambda i, j: (0, 0)),
                  pl.BlockSpec((1, HG_WIDTH), lambda i, j: (0, 0))],
        out_specs=[head] * n_out,
        compiler_params=pltpu.CompilerParams(
            dimension_semantics=("arbitrary", "arbitrary"),
            vmem_limit_bytes=_vmem_limit(est)),
        name="inproj_full" if full else "inproj_ctx",
    )(x, mod, gpre, w_in, lbl, gn)


def _hgrn_tables(rev):
    row = np.arange(CHUNK)[:, None]
    col = np.arange(CHUNK)[None, :]
    tri = ((col >= row) if rev else (col <= row)).astype(np.float32)
    blocks, masks = [tri], []
    for h in LEVELS:
        if h in FINE_LEVELS:
            blocks.append(tri - tri[_ref_row(np.arange(CHUNK), h, rev)])
        is_q = ((row // h) % 2) == (0 if rev else 1)
        key_half = ((col // h) % 2) == (1 if rev else 0)
        masks.append(((row // (2 * h)) == (col // (2 * h))) & is_q & key_half)
    pair = np.stack([np.concatenate(masks[i:i + 2], axis=1) for i in range(0, len(LEVELS), 2)])
    cums = np.concatenate(blocks, axis=0)
    return jnp.asarray(np.concatenate([cums, cums], axis=1), BF16), jnp.asarray(pair, F32)


def _ref_row(t, h, rev):
    return (t // (2 * h)) * (2 * h) + (h if rev else h - 1)


def _hgrn_chunks(qs, ks, vs, lfs, sts, tabs, rev, need_o):
    cums_ref, pair_ref = tabs
    n_heads = len(sts)
    heads = range(len(ks))
    n_pairs = len(LEVELS) // 2
    n_cum = CHUNK * (1 + len(FINE_LEVELS)) if need_o else CHUNK
    cums = cums_ref[0:n_cum, :]

    his, los = [], []
    for h in heads:
        lf2 = lfs[h] * LOG2_E
        hi = lf2.astype(BF16)
        his.append(hi)
        los.append((lf2 - hi.astype(F32)).astype(BF16))
    both = _dot(cums, jnp.concatenate([jnp.concatenate(his, axis=1), jnp.concatenate(los, axis=1)], axis=0))
    bd = [both[:, h * LANES:(h + 1) * LANES] for h in heads]

    k16 = [ks[h].astype(BF16) for h in heads]
    v16 = [vs[h].astype(BF16) for h in heads]
    b = [bd[h][0:CHUNK] for h in heads]

    def level_diff(h, level):
        if level in FINE_LEVELS:
            i = FINE_LEVELS.index(level)
            return bd[h][CHUNK * (1 + i):CHUNK * (2 + i)]
        refs = [_ref_row(m * 2 * level, level, rev) for m in range(CHUNK // (2 * level))]
        pieces = [jnp.broadcast_to(b[h][r:r + 1, :], (2 * level, LANES)) for r in refs]
        return b[h] - (pieces[0] if len(pieces) == 1 else jnp.concatenate(pieces, axis=0))
    edge = [b[h][0:1, :] if rev else b[h][CHUNK - 1:CHUNK, :] for h in heads]
    kd = [k16[h] * jnp.exp2(edge[h] - b[h]).astype(BF16) for h in heads]
    grow = [lax.dot_general(v16[h], kd[h], (((0,), (0,)), ((), ())), preferred_element_type=F32)
            for h in heads]
    q16 = [qs[h].astype(BF16) for h in heads] if need_o else None
    st = list(sts)
    carried = []
    for h in heads:
        if need_o:
            carried.append(_dot_nt(q16[h] * jnp.exp2(b[h]).astype(BF16), st[h % n_heads].astype(BF16)))
        st[h % n_heads] = st[h % n_heads] * jnp.exp2(edge[h]) + grow[h]
    if not need_o:
        return None, st

    zero = jnp.zeros((CHUNK, LANES), BF16)
    att = [[] for _ in heads]
    for p in range(n_pairs):
        lhs, rhs = [], []
        for h in heads:
            qe, ke = [], []
            for i in (2 * p, 2 * p + 1):
                e = jnp.exp2(-jnp.abs(level_diff(h, LEVELS[i]))).astype(BF16)
                qe.append(q16[h] * e)
                ke.append(k16[h] * e)
            lhs.append(jnp.concatenate(qe, axis=1))
            rhs.append(jnp.concatenate([jnp.concatenate([ke[0], zero], axis=1),
                                        jnp.concatenate([zero, ke[1]], axis=1)], axis=0))
        scores = [_dot_nt(lhs[h], rhs[h]) for h in heads]
        for h in heads:
            att[h].append((scores[h] * pair_ref[p]).astype(BF16))
    o = []
    for h in heads:
        vals = jnp.concatenate([v16[h]] * (2 * n_pairs), axis=0)
        same_row = jnp.sum(qs[h] * ks[h], axis=1, keepdims=True) * vs[h]
        o.append(carried[h] + _dot(jnp.concatenate(att[h], axis=1), vals) + same_row)
    return o, st


def _hgrn_kernel(*refs, rev, need_o, final, want_state, n_chunks):
    it = iter(refs)
    tabs = tuple(next(it) for _ in range(2))
    q_ref = next(it) if need_o else None
    k_ref, v_ref, lf_ref, s0_ref = next(it), next(it), next(it), next(it)
    prev_ref = next(it) if final else None
    gate_ref = next(it) if final else None
    o_ref = next(it) if need_o else None
    sout_ref = next(it) if want_state else None
    st_ref = next(it)

    j = pl.program_id(1)
    nb = pl.num_programs(1)

    @pl.when(j == 0)
    def _():
        st_ref[...] = s0_ref[0]

    group = HGRN_GROUP if n_chunks % HGRN_GROUP == 0 else 1

    def body(gi, carry):
        rows = []
        for s in range(group):
            c = gi * group + s
            c = (n_chunks - 1 - c) if rev else c
            rows.append(pl.ds(pl.multiple_of(c * CHUNK, CHUNK), CHUNK))
        pairs = [(r, hh) for r in rows for hh in range(HG_HEADS)]
        o, st_new = _hgrn_chunks([q_ref[0, hh, r, :] for r, hh in pairs] if need_o else None,
                                 [k_ref[0, hh, r, :] for r, hh in pairs],
                                 [v_ref[0, hh, r, :] for r, hh in pairs],
                                 [lf_ref[0, hh, r, :] for r, hh in pairs],
                                 [st_ref[hh] for hh in range(HG_HEADS)], tabs, rev, need_o)
        for hh in range(HG_HEADS):
            st_ref[hh] = st_new[hh]
        if need_o:
            for (r, hh), out in zip(pairs, o):
                if final:
                    tot = out + prev_ref[0, hh, r, :]
                    out = tot * lax.rsqrt(jnp.mean(tot * tot, axis=-1, keepdims=True) + EPS)
                    out = out * gate_ref[0, hh, r, :]
                o_ref[0, hh, r, :] = out
        return carry

    lax.fori_loop(0, n_chunks // group, body, 0)

    if want_state:
        @pl.when(j == nb - 1)
        def _():
            sout_ref[0] = st_ref[...]


def _hgrn(q, k, v, lf, s0, prev=None, gate=None, *, rev, need_o, want_state, name):
    b, hh, t, _ = k.shape
    tb = min(HGRN_BLOCK, t)
    nb = t // tb
    final = prev is not None
    tabs = _hgrn_tables(rev)

    seq = pl.BlockSpec((1, hh, tb, HEAD_DIM), lambda i, j: (i, 0, (nb - 1 - j) if rev else j, 0))
    state = pl.BlockSpec((1, hh, HEAD_DIM, HEAD_DIM), lambda i, j: (i, 0, 0, 0))
    args = list(tabs)
    in_specs = [pl.BlockSpec(a.shape, functools.partial(lambda nd, i, j: (0,) * nd, a.ndim)) for a in tabs]
    for a in ((q,) if need_o else ()) + (k, v, lf):
        args.append(a)
        in_specs.append(seq)
    args.append(s0)
    in_specs.append(state)
    if final:
        args += [prev, gate]
        in_specs += [seq, seq]
    out_shape, out_specs = [], []
    if need_o:
        out_shape.append(jax.ShapeDtypeStruct((b, hh, t, HEAD_DIM), F32))
        out_specs.append(seq)
    if want_state:
        out_shape.append(jax.ShapeDtypeStruct((b, hh, HEAD_DIM, HEAD_DIM), F32))
        out_specs.append(state)
    kern = functools.partial(_hgrn_kernel, rev=rev, need_o=need_o, final=final,
                             want_state=want_state, n_chunks=tb // CHUNK)
    est = 2 * (len(args) + len(out_shape)) * hh * tb * HEAD_DIM * 4
    return pl.pallas_call(
        kern,
        out_shape=out_shape,
        grid=(b, nb),
        in_specs=in_specs,
        out_specs=out_specs,
        scratch_shapes=[pltpu.VMEM((hh, HEAD_DIM, HEAD_DIM), F32)],
        compiler_params=pltpu.CompilerParams(
            dimension_semantics=("arbitrary", "arbitrary"),
            vmem_limit_bytes=_vmem_limit(est)),
        name=name,
    )(*args)


def _window_sum(x, w, stride, pos, extent):
    n = x.shape[0]
    whole_tiles = stride % SUBLANES == 0 and stride * extent == n

    def ahead(y, dist):
        if whole_tiles:
            k = min(dist, extent) * stride
            return jnp.concatenate([y[k:], jnp.zeros((k, y.shape[1]), y.dtype)], axis=0)
        return jnp.where(pos + dist < extent, pltpu.roll(y, (n - dist * stride) % n, 0), 0.0)

    def behind(y, dist):
        if whole_tiles:
            k = min(dist, extent) * stride
            return jnp.concatenate([jnp.zeros((k, y.shape[1]), y.dtype), y[:n - k]], axis=0)
        return jnp.where(pos >= dist, pltpu.roll(y, dist * stride, 0), 0.0)

    fwd, bwd, h = x, x, 1
    while h < w // 2:
        fwd = fwd + ahead(fwd, h)
        bwd = bwd + behind(bwd, h)
        h *= 2
    return fwd + behind(bwd, 1)


def _window_count(pos, w, extent):
    return (jnp.minimum(pos + w // 2, extent) - jnp.maximum(pos - w // 2, 0)).astype(F32)


def _pool_kernel(u_ref, pw_ref, ps_ref, o_ref):
    g = pl.program_id(1)
    n = u_ref.shape[2]
    tok = lax.broadcasted_iota(I32, (n, LANES), 0)
    col = tok & (GRID_W - 1)
    row = tok >> _log2(GRID_W)
    for gi, w in enumerate(POOL_WINDOWS):
        @pl.when(g == gi)
        def _(w=w):
            u = u_ref[0, 0]
            rows = n // GRID_W
            s = _window_sum(_window_sum(u, w, GRID_W, row, rows), w, 1, col, GRID_W)
            m = s / (_window_count(row, w, rows) * _window_count(col, w, GRID_W))
            o_ref[0, 0] = _dot((m - u).astype(BF16), pw_ref[0]) * ps_ref[0]


def _pool(u, pw, ps):
    b, g, t, c = u.shape
    blk = pl.BlockSpec((1, 1, t, c), lambda i, j: (i, j, 0, 0))
    return pl.pallas_call(
        _pool_kernel,
        out_shape=jax.ShapeDtypeStruct(u.shape, F32),
        grid=(b, g),
        in_specs=[blk,
                  pl.BlockSpec((1, c, c), lambda i, j: (j, 0, 0)),
                  pl.BlockSpec((1, 1, c), lambda i, j: (j, 0, 0))],
        out_specs=blk,
        compiler_params=pltpu.CompilerParams(
            dimension_semantics=("arbitrary", "arbitrary"),
            vmem_limit_bytes=_vmem_limit(12 * t * c * 4)),
        name="pool",
    )(u, pw, ps)


def _outproj_kernel(hg_ref, pool_ref, x_ref, w_ref, mod_ref, gpost_ref, gpre_ref, wr_ref, br_ref,
                    x1_ref, h2_ref, ti_ref, tw_ref, chosen_ref, cnt_ref):
    d = x_ref.shape[-1]
    r = pl.program_id(0)

    @pl.when(jnp.logical_and(r == 0, pl.program_id(1) == 0))
    def _():
        cnt_ref[...] = jnp.zeros(cnt_ref.shape, F32)

    gt1 = mod_ref[pl.ds(r, 1), pl.ds(2 * d, d)]
    sh2 = mod_ref[pl.ds(r, 1), pl.ds(3 * d, d)]
    sc2 = mod_ref[pl.ds(r, 1), pl.ds(4 * d, d)]
    tm = x_ref.shape[1]
    sub = tm // OUTPROJ_SLICES
    slices = [pl.ds(i * sub, sub) for i in range(OUTPROJ_SLICES)]
    ys = []
    for rows in slices:
        cat = jnp.concatenate([hg_ref[0, hh, rows, :] for hh in range(HG_HEADS)]
                              + [pool_ref[0, g, rows, :] for g in range(len(POOL_WINDOWS))], axis=1)
        ys.append(_dot(cat.astype(BF16), w_ref[...]))
    logits = []
    for i, rows in enumerate(slices):
        x1 = x_ref[0, rows, :] + gt1 * _rms(ys[i], gpost_ref[...])
        x1_ref[0, rows, :] = x1
        h2 = _rms(x1, gpre_ref[...]) * (1.0 + sc2) + sh2
        _store_row_tiles(h2_ref.at[0, pl.ds(i * sub * SUBLANES, sub * SUBLANES), :], h2)
        logits.append(_dot(h2.astype(BF16), wr_ref[...]) + br_ref[...])

    lane = lax.broadcasted_iota(I32, (sub, LANES), 1).astype(F32)
    picked = jnp.zeros((1, LANES), F32)
    for i, rows in enumerate(slices):
        lg = jnp.where(lane < N_EXPERTS, logits[i], -jnp.inf)
        ti = jnp.zeros(lg.shape, F32)
        tw = jnp.zeros(lg.shape, F32)
        chosen = jnp.zeros(lg.shape, F32)
        top = None
        for jj in range(TOP_K):
            m = jnp.max(lg, axis=1, keepdims=True)
            idx = jnp.min(jnp.where(lg == m, lane, float(LANES)), axis=1, keepdims=True)
            lg = jnp.where(lane == idx, -jnp.inf, lg)
            chosen = jnp.where(lane == idx, 1.0, chosen)
            top = m if top is None else top
            ti = jnp.where(lane == jj, idx, ti)
            tw = jnp.where(lane == jj, jnp.exp(m - top), tw)
        ti_ref[0, rows, :] = ti.astype(I32)
        tw_ref[0, rows, :] = tw / jnp.sum(tw, axis=1, keepdims=True)
        chosen_ref[0, rows, :] = chosen
        picked = picked + jnp.sum(chosen, axis=0, keepdims=True)
    cnt_ref[...] += picked


def _outproj(hg, pool, x, w_out, mod, gpost, gpre, wr, br):
    b, t, d = x.shape
    tm = min(TOKEN_TILE, t)
    head = pl.BlockSpec((1, HG_HEADS, tm, HEAD_DIM), lambda i, j: (i, 0, j, 0))
    tok = pl.BlockSpec((1, tm, d), lambda i, j: (i, j, 0))
    lanes = pl.BlockSpec((1, tm, LANES), lambda i, j: (i, j, 0))
    vec = pl.BlockSpec((1, d), lambda i, j: (0, 0))
    est = 2 * (2 * tm * HG_WIDTH * 4 + 3 * tm * d * 4 + 2 * tm * LANES * 4) + 2 * d * d * 2 + 6 * tm * d * 4
    return pl.pallas_call(
        _outproj_kernel,
        out_shape=[jax.ShapeDtypeStruct((b, t, d), F32),
                   jax.ShapeDtypeStruct((b, t * d // LANES, LANES), F32),
                   jax.ShapeDtypeStruct((b, t, LANES), I32), jax.ShapeDtypeStruct((b, t, LANES), F32),
                   jax.ShapeDtypeStruct((b, t, LANES), F32), jax.ShapeDtypeStruct((1, LANES), F32)],
        grid=(b, t // tm),
        in_specs=[head, head, tok,
                  pl.BlockSpec(w_out.shape, lambda i, j: (0, 0)),
                  pl.BlockSpec(mod.shape, lambda i, j: (0, 0)),
                  vec, vec,
                  pl.BlockSpec(wr.shape, lambda i, j: (0, 0)),
                  pl.BlockSpec((1, LANES), lambda i, j: (0, 0))],
        out_specs=[tok, pl.BlockSpec((1, tm * d // LANES, LANES), lambda i, j: (i, j, 0)),
                   lanes, lanes, lanes, pl.BlockSpec((1, LANES), lambda i, j: (0, 0))],
        compiler_params=pltpu.CompilerParams(
            dimension_semantics=("arbitrary", "arbitrary"),
            vmem_limit_bytes=_vmem_limit(est)),
        name="outproj",
    )(hg, pool, x, w_out, mod, gpost, gpre, wr, br)


def _rank_kernel(ti_ref, chosen_ref, start_ref, dest_ref, carry_ref):
    i = pl.program_id(0)

    @pl.when(i == 0)
    def _():
        carry_ref[...] = start_ref[...]

    cnt = chosen_ref[...]
    tm = cnt.shape[0]
    row = lax.broadcasted_iota(I32, (tm, tm), 0)
    col = lax.broadcasted_iota(I32, (tm, tm), 1)
    before = jnp.where(col < row, 1.0, 0.0).astype(BF16)
    prefix = _dot(before, cnt.astype(BF16)) + carry_ref[...]
    rank = jnp.take_along_axis(prefix, ti_ref[...], axis=1)
    dest_ref[...] = (rank.T[:TOP_K, :] * SUBLANES).astype(I32)
    carry_ref[...] += jnp.sum(cnt, axis=0, keepdims=True)


def _rank(ti, chosen, starts):
    t = ti.shape[0]
    tm = min(TOKEN_TILE, t)
    return pl.pallas_call(
        _rank_kernel,
        out_shape=jax.ShapeDtypeStruct((TOP_K, t), I32),
        grid=(t // tm,),
        in_specs=[pl.BlockSpec((tm, LANES), lambda i: (i, 0)),
                  pl.BlockSpec((tm, LANES), lambda i: (i, 0)),
                  pl.BlockSpec((1, LANES), lambda i: (0, 0))],
        out_specs=pl.BlockSpec((TOP_K, tm), lambda i: (0, i)),
        scratch_shapes=[pltpu.VMEM((1, LANES), F32)],
        compiler_params=pltpu.CompilerParams(dimension_semantics=("arbitrary",)),
        name="rank",
    )(ti, chosen, starts)


def _slot_specs(tm, n_tiles, tile_of_step):
    return [pl.BlockSpec((tm,), functools.partial(lambda jj, i: (jj * n_tiles + tile_of_step(i),), jj),
                         memory_space=pltpu.SMEM) for jj in range(TOP_K)]


def _tile_rows(start):
    return pl.ds(pl.multiple_of(start, SUBLANES), SUBLANES)


def _gap_copies(e, gap_start_ref, gap_len_ref, zeros, xs_ref, sem):
    off, n = gap_start_ref[e], gap_len_ref[e]
    out = []
    for bit in range(_log2(MOE_BM)):
        size = (1 << bit) * SUBLANES
        used = (n >> bit) & 1
        out.append((used == 1,
                    pltpu.make_async_copy(zeros.at[pl.ds(0, size), :],
                                          xs_ref.at[pl.ds(pl.multiple_of(off * SUBLANES, SUBLANES), size), :],
                                          sem)))
        off = off + used * (1 << bit)
    return out


def _dispatch_kernel(*refs):
    dest_refs = refs[:TOP_K]
    gap_start_ref, gap_len_ref, n_live_ref, h_ref, xs_ref, zeros, sem, zsem = refs[TOP_K:]
    tm = h_ref.shape[0] // SUBLANES
    block_rows = MOE_BM * SUBLANES
    n_blocks = xs_ref.shape[0] // block_rows

    @pl.when(pl.program_id(0) == 0)
    def _():
        zeros[...] = jnp.zeros(zeros.shape, zeros.dtype)

        def tail_copy(blk):
            return pltpu.make_async_copy(
                zeros, xs_ref.at[pl.ds(pl.multiple_of(blk * block_rows, block_rows), block_rows), :], zsem)

        def tail_start(blk, carry):
            tail_copy(blk).start()
            return carry

        def tail_wait(blk, carry):
            tail_copy(blk).wait()
            return carry

        lax.fori_loop(n_live_ref[0], n_blocks, tail_start, 0)
        lax.fori_loop(n_live_ref[0], n_blocks, tail_wait, 0)

        def fill(e, carry):
            for used, cp in _gap_copies(e, gap_start_ref, gap_len_ref, zeros, xs_ref, zsem):
                @pl.when(used)
                def _(cp=cp):
                    cp.start()
            return carry

        def drain(e, carry):
            for used, cp in _gap_copies(e, gap_start_ref, gap_len_ref, zeros, xs_ref, zsem):
                @pl.when(used)
                def _(cp=cp):
                    cp.wait()
            return carry

        lax.fori_loop(0, N_EXPERTS, fill, 0)
        lax.fori_loop(0, N_EXPERTS, drain, 0)

    def start(r, carry):
        for jj in range(TOP_K):
            pltpu.make_async_copy(h_ref.at[_tile_rows(r * SUBLANES), :],
                                  xs_ref.at[_tile_rows(dest_refs[jj][r]), :],
                                  sem).start(priority=jj % 2)
        return carry

    lax.fori_loop(0, tm, start, 0, unroll=DMA_UNROLL)
    for jj in range(TOP_K):
        pltpu.make_async_copy(h_ref, xs_ref.at[pl.ds(0, tm * SUBLANES), :], sem).wait()


def _dispatch(dest, gap_start, gap_len, n_live, h2, n_sorted):
    rows, l = h2.shape
    t = rows // SUBLANES
    tm = min(TOKEN_TILE, t)
    smem = pl.BlockSpec(memory_space=pltpu.SMEM)
    return pl.pallas_call(
        _dispatch_kernel,
        out_shape=jax.ShapeDtypeStruct((n_sorted * SUBLANES, l), h2.dtype),
        grid=(t // tm,),
        in_specs=_slot_specs(tm, t // tm, lambda i: i) + [
            smem, smem, smem, pl.BlockSpec((tm * SUBLANES, l), lambda i: (i, 0))],
        out_specs=pl.BlockSpec(memory_space=pl.ANY),
        scratch_shapes=[pltpu.VMEM((MOE_BM * SUBLANES, l), h2.dtype),
                        pltpu.SemaphoreType.DMA, pltpu.SemaphoreType.DMA],
        compiler_params=pltpu.CompilerParams(dimension_semantics=("arbitrary",)),
        name="dispatch",
    )(*([dest] * TOP_K), gap_start, gap_len, n_live, h2)


def _moe_kernel(block_e_ref, next_e_ref, n_live_ref,
                x_ref, wg_ref, wu_ref, wd_ref, bg_ref, bu_ref, bd_ref, y_ref,
                wgs, wus, wds, wgb, wub, wdb, sem):
    i = pl.program_id(0)
    live = i < n_live_ref[0]
    e = block_e_ref[i]
    changed = jnp.logical_or(i == 0, e != block_e_ref[jnp.maximum(i - 1, 0)])

    def fetch(expert):
        return [pltpu.make_async_copy(w_ref.at[expert], stage, sem.at[n])
                for n, (w_ref, stage) in enumerate(((wg_ref, wgs), (wu_ref, wus), (wd_ref, wds)))]

    @pl.when(i == 0)
    def _():
        for cp in fetch(e):
            cp.start()

    @pl.when(jnp.logical_and(live, changed))
    def _():
        for cp in fetch(e):
            cp.wait()
        def cast(c, carry):
            rows = pl.ds(pl.multiple_of(c * LANES, LANES), LANES)
            for src, dst in ((wgs, wgb), (wus, wub), (wds, wdb)):
                dst[rows, :] = src[rows, :].astype(BF16)
            return carry

        lax.fori_loop(0, wgs.shape[0] // LANES, cast, 0)

        @pl.when(next_e_ref[i] != e)
        def _():
            for cp in fetch(next_e_ref[i]):
                cp.start()

    @pl.when(live)
    def _():
        x = _load_row_tiles(x_ref).astype(BF16)
        gate = jnp.minimum(_dot(x, wgb[...]) + bg_ref[...], SWIGLU_LIMIT)
        up = jnp.clip(_dot(x, wub[...]) + bu_ref[...], -SWIGLU_LIMIT, SWIGLU_LIMIT)
        hmid = (up + 1.0) * gate * _sigmoid(SWIGLU_ALPHA * gate)
        _store_row_tiles(y_ref, _dot(hmid.astype(BF16), wdb[...]) + bd_ref[...])

    @pl.when(jnp.logical_not(live))
    def _():
        y_ref[...] = jnp.zeros(y_ref.shape, y_ref.dtype)


def _moe(block_e, next_e, n_live, xs, wg, wu, wd, bg, bu, bd):
    d, f = wg.shape[-2:]
    n_blocks = block_e.shape[0]
    rows_in = pl.BlockSpec((MOE_BM * SUBLANES, LANES),
                           lambda i, be, ne, nl: (jnp.minimum(i, nl[0] - 1), 0))
    rows_out = pl.BlockSpec((MOE_BM * SUBLANES, LANES), lambda i, be, ne, nl: (i, 0))
    hbm = pl.BlockSpec(memory_space=pl.ANY)

    def bspec(n):
        return pl.BlockSpec((None, 1, n), lambda i, be, ne, nl: (be[i], 0, 0))

    est = 3 * d * f * 4 + 3 * d * f * 2 + 4 * MOE_BM * d * 4 + 4 * MOE_BM * f * 4
    return pl.pallas_call(
        _moe_kernel,
        out_shape=jax.ShapeDtypeStruct(xs.shape, F32),
        grid_spec=pltpu.PrefetchScalarGridSpec(
            num_scalar_prefetch=3,
            grid=(n_blocks,),
            in_specs=[rows_in, hbm, hbm, hbm, bspec(f), bspec(f), bspec(d)],
            out_specs=rows_out,
            scratch_shapes=[pltpu.VMEM((d, f), F32), pltpu.VMEM((d, f), F32), pltpu.VMEM((f, d), F32),
                            pltpu.VMEM((d, f), BF16), pltpu.VMEM((d, f), BF16), pltpu.VMEM((f, d), BF16),
                            pltpu.SemaphoreType.DMA((3,))]),
        compiler_params=pltpu.CompilerParams(
            dimension_semantics=("arbitrary",),
            vmem_limit_bytes=_vmem_limit(est)),
        name="moe",
    )(block_e, next_e, n_live, xs, wg, wu, wd, bg, bu, bd)


def _moe_layout(counts, n_rows):
    n_blocks = n_rows // MOE_BM + N_EXPERTS
    padded = (counts + MOE_BM - 1) // MOE_BM * MOE_BM
    ends = jnp.cumsum(padded)
    starts = ends - padded
    block_end = ends // MOE_BM
    i = jnp.arange(n_blocks, dtype=I32)
    ids = jnp.arange(N_EXPERTS, dtype=I32)
    last_e = jnp.max(jnp.where(counts > 0, ids, 0))
    block_e = jnp.minimum(jnp.sum(block_end[None, :] <= i[:, None], axis=1), last_e).astype(I32)
    later = jnp.logical_and(ids[None, :] > ids[:, None], counts[None, :] > 0)
    following = jnp.min(jnp.where(later, ids[None, :], N_EXPERTS), axis=1)
    following = jnp.where(following == N_EXPERTS, ids, following)
    next_e = jnp.sum(jnp.where(block_e[:, None] == ids[None, :], following[None, :], 0), axis=1)
    return (starts.astype(I32), (starts + counts).astype(I32), (padded - counts).astype(I32),
            block_e, next_e.astype(I32), block_end[-1:].astype(I32), n_blocks)


def _combine_kernel(*refs, tiles_per_batch):
    dest_refs, dnext_refs = refs[:TOP_K], refs[TOP_K:2 * TOP_K]
    ys_ref, tw_ref, x1_ref, mod_ref, gpost_ref, o_ref, buf, sem = refs[2 * TOP_K:]
    tm, d = x1_ref.shape
    i = pl.program_id(0)
    slot = i % 2

    def issue(d_refs, s):
        def start(r, carry):
            for jj in range(TOP_K):
                pltpu.make_async_copy(ys_ref.at[_tile_rows(d_refs[jj][r]), :],
                                      buf.at[s, jj, _tile_rows(r * SUBLANES), :],
                                      sem.at[s]).start(priority=1)
            return carry

        lax.fori_loop(0, tm, start, 0, unroll=DMA_UNROLL)

    @pl.when(i == 0)
    def _():
        issue(dest_refs, 0)

    @pl.when(i + 1 < pl.num_programs(0))
    def _():
        issue(dnext_refs, 1 - slot)

    for jj in range(TOP_K):
        pltpu.make_async_copy(ys_ref.at[pl.ds(0, tm * SUBLANES), :], buf.at[slot, jj],
                              sem.at[slot]).wait()

    tw = tw_ref[...]
    y = tw[:, 0:1] * _load_row_tiles(buf.at[slot, 0])
    for jj in range(1, TOP_K):
        y = y + tw[:, jj:jj + 1] * _load_row_tiles(buf.at[slot, jj])
    r = i // tiles_per_batch
    gt2 = mod_ref[pl.ds(r, 1), pl.ds(5 * d, d)]
    o_ref[...] = x1_ref[...] + gt2 * _rms(y, gpost_ref[...])


def _combine(dest, ys, tw, x1, mod, gpost, tokens_per_batch):
    t, d = x1.shape
    tm = min(TOKEN_TILE, tokens_per_batch)
    n_tiles = t // tm
    kern = functools.partial(_combine_kernel, tiles_per_batch=tokens_per_batch // tm)
    est = 2 * TOP_K * tm * d * 4 + 2 * (2 * tm * d * 4 + tm * LANES * 4) + 3 * tm * d * 4
    return pl.pallas_call(
        kern,
        out_shape=jax.ShapeDtypeStruct((t, d), F32),
        grid=(n_tiles,),
        in_specs=_slot_specs(tm, n_tiles, lambda i: i)
        + _slot_specs(tm, n_tiles, lambda i: jnp.minimum(i + 1, n_tiles - 1)) + [
                  pl.BlockSpec(memory_space=pl.ANY),
                  pl.BlockSpec((tm, LANES), lambda i: (i, 0)),
                  pl.BlockSpec((tm, d), lambda i: (i, 0)),
                  pl.BlockSpec(mod.shape, lambda i: (0, 0)),
                  pl.BlockSpec((1, d), lambda i: (0, 0))],
        out_specs=pl.BlockSpec((tm, d), lambda i: (i, 0)),
        scratch_shapes=[pltpu.VMEM((2, TOP_K, tm * SUBLANES, LANES), F32),
                        pltpu.SemaphoreType.DMA((2,))],
        compiler_params=pltpu.CompilerParams(
            dimension_semantics=("arbitrary",),
            vmem_limit_bytes=_vmem_limit(est)),
        name="combine",
    )(*([dest] * (2 * TOP_K)), ys, tw, x1, mod, gpost)


def kernel(x, c, ctx, c_ctx, w_ada, b_ada, g_pre_mix, g_post_mix, g_pre_ffn, g_post_ffn,
           w_in, w_out, hgrn_lb_logits, hgrn_norm, pool_w, pool_scale,
           w_router, b_router, w_gate, b_gate, w_up, b_up, w_down, b_down):
    b, s, d = x.shape
    layer = 0
    n_e = w_router.shape[-1]

    cc = jnp.concatenate([c, c_ctx[None, :]], axis=0)
    cc = jnp.pad(cc, ((0, -(b + 1) % SUBLANES), (0, 0)))
    mod = _ada(cc, w_ada[layer], b_ada[layer][None, :])

    w_in_b = w_in[layer].astype(BF16)
    lbl = hgrn_lb_logits[:2].reshape(4, HG_WIDTH)
    gn = hgrn_norm[layer][None, :]
    gpre = g_pre_mix[layer][None, :]

    lff_c, kf_c, lfb_c, kb_c, v_c = _inproj(ctx, mod, gpre, w_in_b, lbl, gn, mod_row=b, full=False)
    zeros = jnp.zeros((b, HG_HEADS, HEAD_DIM, HEAD_DIM), F32)
    (s_f,) = _hgrn(None, kf_c, v_c, lff_c, zeros, rev=False, need_o=False, want_state=True,
                   name="hgrn_ctx_fwd")
    (s_b,) = _hgrn(None, kb_c, v_c, lfb_c, zeros, rev=True, need_o=False, want_state=True,
                   name="hgrn_ctx_bwd")

    q, lff, kf, lfb, kb, v, gate, u = _inproj(x, mod, gpre, w_in_b, lbl, gn, mod_row=None, full=True)
    (o_f,) = _hgrn(q, kf, v, lff, s_f, rev=False, need_o=True, want_state=False, name="hgrn_fwd")
    (o_hg,) = _hgrn(q, kb, v, lfb, s_b, o_f, gate, rev=True, need_o=True, want_state=False,
                    name="hgrn_bwd")
    o_pool = _pool(u, pool_w[layer].astype(BF16), pool_scale[layer].reshape(len(POOL_WINDOWS), 1, POOL_GROUP))

    wr = jnp.pad(w_router[layer], ((0, 0), (0, LANES - n_e))).astype(BF16)
    br = jnp.pad(b_router[layer], (0, LANES - n_e))[None, :]
    x1, h2, ti, tw, chosen, cnt = _outproj(o_hg, o_pool, x, w_out[layer].astype(BF16), mod,
                                           g_post_mix[layer][None, :], g_pre_ffn[layer][None, :], wr, br)

    t = b * s
    starts, gap_start, gap_len, block_e, next_e, n_live, n_blocks = _moe_layout(
        cnt[0, :n_e].astype(I32), t * TOP_K)
    starts = jnp.pad(starts.astype(F32), (0, LANES - n_e))[None, :]
    dest = _rank(ti.reshape(t, LANES), chosen.reshape(t, LANES), starts).reshape(TOP_K * t)

    xs = _dispatch(dest, gap_start, gap_len, n_live, h2.reshape(t * d // LANES, LANES), n_blocks * MOE_BM)
    ys = _moe(block_e, next_e, n_live, xs, w_gate[layer], w_up[layer], w_down[layer],
              b_gate[layer][:, None, :], b_up[layer][:, None, :], b_down[layer][:, None, :])
    out = _combine(dest, ys, tw.reshape(t, LANES), x1.reshape(t, d), mod,
                   g_post_ffn[layer][None, :], s)
    return out.reshape(b, s, d)
```

```python
import functools
import math

import numpy as np
import jax
import jax.numpy as jnp
from jax import lax
from jax.experimental import pallas as pl
from jax.experimental.pallas import tpu as pltpu

F32 = jnp.float32
BF16 = jnp.bfloat16
I32 = jnp.int32

GRID_W = 64
HG_HEADS = 4
HEAD_DIM = 128
HG_WIDTH = HG_HEADS * HEAD_DIM
POOL_WINDOWS = (2, 4, 8, 16)
POOL_GROUP = 128
N_EXPERTS = 32
TOP_K = 4
SWIGLU_LIMIT = 7.0
SWIGLU_ALPHA = 1.702
EPS = 1e-6

LANES = 128
SUBLANES = 8
V7X_VMEM_BYTES = 64 * 1024 * 1024

TOKEN_TILE = 512
HGRN_BLOCK = 1024
ADA_COLS = 1536
DMA_UNROLL = 4
VMEM_COMPILER_BYTES = 4 << 20
VMEM_RESERVED_BYTES = 6 << 20
CHUNK = 64
HGRN_GROUP = 4
LEVELS = (32, 16, 8, 4, 2, 1)
FINE_LEVELS = (2, 1)
MOE_BM = 256
OUTPROJ_SLICES = 4
INPROJ_SLICES = 2
LOG2_E = math.log2(math.e)


def _vmem_limit(nbytes):
    return int(min(nbytes * 3 // 2 + VMEM_COMPILER_BYTES, V7X_VMEM_BYTES - VMEM_RESERVED_BYTES))


def _log2(n):
    assert n & (n - 1) == 0
    return n.bit_length() - 1


def _sigmoid(x):
    return 1.0 / (1.0 + jnp.exp(-x))


def _rms(x, gain):
    return x * lax.rsqrt(jnp.mean(x * x, axis=-1, keepdims=True) + EPS) * gain


def _dot(a, b):
    return jnp.dot(a, b, preferred_element_type=F32)


def _store_row_tiles(ref, val):
    rows = val.shape[0]
    for c in range(SUBLANES):
        ref[pl.ds(c, rows, stride=SUBLANES), :] = val[:, c * LANES:(c + 1) * LANES]


def _load_row_tiles(ref):
    rows = ref.shape[0] // SUBLANES
    return jnp.concatenate([ref[pl.ds(c, rows, stride=SUBLANES), :] for c in range(SUBLANES)], axis=1)


def _dot_nt(a, b):
    return lax.dot_general(a, b, (((1,), (1,)), ((), ())), preferred_element_type=F32)


def _ada_kernel(c_ref, w_ref, b_ref, o_ref):
    c = c_ref[...]
    s = (c * _sigmoid(c)).astype(BF16)
    o_ref[...] = _dot(s, w_ref[...].astype(BF16)) + b_ref[...]


def _ada(cc, w, b):
    rows, d = cc.shape
    n = w.shape[1]
    tn = ADA_COLS if n % ADA_COLS == 0 else n
    return pl.pallas_call(
        _ada_kernel,
        out_shape=jax.ShapeDtypeStruct((rows, n), F32),
        grid=(n // tn,),
        in_specs=[pl.BlockSpec((rows, d), lambda j: (0, 0)),
                  pl.BlockSpec((d, tn), lambda j: (0, j)),
                  pl.BlockSpec((1, tn), lambda j: (0, j))],
        out_specs=pl.BlockSpec((rows, tn), lambda j: (0, j)),
        compiler_params=pltpu.CompilerParams(
            dimension_semantics=("arbitrary",),
            vmem_limit_bytes=_vmem_limit(2 * d * tn * 4 + d * tn * 2)),
        name="ada",
    )(cc, w, b)


def _inproj_kernel(x_ref, mod_ref, gpre_ref, w_ref, lbl_ref, gn_ref, *outs, mod_row, full):
    d = x_ref.shape[-1]
    r = pl.program_id(0) if mod_row is None else mod_row
    sh = mod_ref[pl.ds(r, 1), pl.ds(0, d)]
    sc = mod_ref[pl.ds(r, 1), pl.ds(d, d)]
    tm = x_ref.shape[1]
    n_slices = INPROJ_SLICES if tm % (INPROJ_SLICES * SUBLANES) == 0 else 1
    sub = tm // n_slices
    slices = [pl.ds(i * sub, sub) for i in range(n_slices)]
    hs = [(_rms(x_ref[0, rows, :], gpre_ref[...]) * (1.0 + sc) + sh).astype(BF16) for rows in slices]

    def proj(g):
        return [_dot(h, w_ref[:, g * HG_WIDTH:(g + 1) * HG_WIDTH]) for h in hs]

    def put(ref, vals):
        for rows, val in zip(slices, vals):
            for hh in range(HG_HEADS):
                ref[0, hh, rows, :] = val[:, hh * HEAD_DIM:(hh + 1) * HEAD_DIM]

    def lower_bound(direction):
        l0 = lbl_ref[pl.ds(direction, 1), :]
        l1 = lbl_ref[pl.ds(2 + direction, 1), :]
        m = jnp.maximum(l0, l1)
        e0 = jnp.exp(l0 - m)
        return e0 / (e0 + jnp.exp(l1 - m))

    if full:
        q_o, lff_o, kf_o, lfb_o, kb_o, v_o, gate_o, u_o = outs
        put(q_o, [q * _sigmoid(q) for q in proj(0)])
    else:
        lff_o, kf_o, lfb_o, kb_o, v_o = outs
    for direction, (lf_o, k_o) in enumerate(((lff_o, kf_o), (lfb_o, kb_o))):
        lb = lower_bound(direction)
        sgs = [_sigmoid(z) for z in proj(1 + direction)]
        put(lf_o, [jnp.log(lb + (1.0 - lb) * sg) for sg in sgs])
        put(k_o, [(1.0 - lb) * (1.0 - sg) for sg in sgs])
    put(v_o, proj(3))
    if full:
        put(gate_o, [gn_ref[...] * _sigmoid(z) for z in proj(4)])
        put(u_o, proj(5))


def _inproj(x, mod, gpre, w_in, lbl, gn, *, mod_row, full):
    b, t, d = x.shape
    tm = min(TOKEN_TILE, t)
    n_out = 8 if full else 5
    n_cols = w_in.shape[1]
    head = pl.BlockSpec((1, HG_HEADS, tm, HEAD_DIM), lambda i, j: (i, 0, j, 0))
    kern = functools.partial(_inproj_kernel, mod_row=mod_row, full=full)
    est = 2 * tm * d * 4 + 2 * d * n_cols * 2 + n_out * 2 * tm * HG_WIDTH * 4 + 4 * tm * HG_WIDTH * 4
    return pl.pallas_call(
        kern,
        out_shape=[jax.ShapeDtypeStruct((b, HG_HEADS, t, HEAD_DIM), F32)] * n_out,
        grid=(b, t // tm),
        in_specs=[pl.BlockSpec((1, tm, d), lambda i, j: (i, j, 0)),
                  pl.BlockSpec(mod.shape, lambda i, j: (0, 0)),
                  pl.BlockSpec((1, d), lambda i, j: (0, 0)),
                  pl.BlockSpec(w_in.shape, lambda i, j: (0, 0)),
                  pl.BlockSpec(lbl.shape, lambda i, j: (0, 0)),
                  pl.BlockSpec((1, HG_WIDTH), lambda i, j: (0, 0))],
        out_specs=[head] * n_out,
        compiler_params=pltpu.CompilerParams(
            dimension_semantics=("arbitrary", "arbitrary"),
            vmem_limit_bytes=_vmem_limit(est)),
        name="inproj_full" if full else "inproj_ctx",
    )(x, mod, gpre, w_in, lbl, gn)


def _hgrn_tables(rev):
    row = np.arange(CHUNK)[:, None]
    col = np.arange(CHUNK)[None, :]
    tri = ((col >= row) if rev else (col <= row)).astype(np.float32)
    blocks, masks = [tri], []
    for h in LEVELS:
        if h in FINE_LEVELS:
            blocks.append(tri - tri[_ref_row(np.arange(CHUNK), h, rev)])
        is_q = ((row // h) % 2) == (0 if rev else 1)
        key_half = ((col // h) % 2) == (1 if rev else 0)
        masks.append(((row // (2 * h)) == (col // (2 * h))) & is_q & key_half)
    pair = np.stack([np.concatenate(masks[i:i + 2], axis=1) for i in range(0, len(LEVELS), 2)])
    cums = np.concatenate(blocks, axis=0)
    return jnp.asarray(np.concatenate([cums, cums], axis=1), BF16), jnp.asarray(pair, F32)


def _ref_row(t, h, rev):
    return (t // (2 * h)) * (2 * h) + (h if rev else h - 1)


def _hgrn_chunks(qs, ks, vs, lfs, sts, tabs, rev, need_o):
    cums_ref, pair_ref = tabs
    n_heads = len(sts)
    heads = range(len(ks))
    n_pairs = len(LEVELS) // 2
    n_cum = CHUNK * (1 + len(FINE_LEVELS)) if need_o else CHUNK
    cums = cums_ref[0:n_cum, :]

    his, los = [], []
    for h in heads:
        lf2 = lfs[h] * LOG2_E
        hi = lf2.astype(BF16)
        his.append(hi)
        los.append((lf2 - hi.astype(F32)).astype(BF16))
    both = _dot(cums, jnp.concatenate([jnp.concatenate(his, axis=1), jnp.concatenate(los, axis=1)], axis=0))
    bd = [both[:, h * LANES:(h + 1) * LANES] for h in heads]

    k16 = [ks[h].astype(BF16) for h in heads]
    v16 = [vs[h].astype(BF16) for h in heads]
    b = [bd[h][0:CHUNK] for h in heads]

    def level_diff(h, level):
        if level in FINE_LEVELS:
            i = FINE_LEVELS.index(level)
            return bd[h][CHUNK * (1 + i):CHUNK * (2 + i)]
        refs = [_ref_row(m * 2 * level, level, rev) for m in range(CHUNK // (2 * level))]
        pieces = [jnp.broadcast_to(b[h][r:r + 1, :], (2 * level, LANES)) for r in refs]
        return b[h] - (pieces[0] if len(pieces) == 1 else jnp.concatenate(pieces, axis=0))
    edge = [b[h][0:1, :] if rev else b[h][CHUNK - 1:CHUNK, :] for h in heads]
    kd = [k16[h] * jnp.exp2(edge[h] - b[h]).astype(BF16) for h in heads]
    grow = [lax.dot_general(v16[h], kd[h], (((0,), (0,)), ((), ())), preferred_element_type=F32)
            for h in heads]
    q16 = [qs[h].astype(BF16) for h in heads] if need_o else None
    st = list(sts)
    carried = []
    for h in heads:
        if need_o:
            carried.append(_dot_nt(q16[h] * jnp.exp2(b[h]).astype(BF16), st[h % n_heads].astype(BF16)))
        st[h % n_heads] = st[h % n_heads] * jnp.exp2(edge[h]) + grow[h]
    if not need_o:
        return None, st

    zero = jnp.zeros((CHUNK, LANES), BF16)
    att = [[] for _ in heads]
    for p in range(n_pairs):
        lhs, rhs = [], []
        for h in heads:
            qe, ke = [], []
            for i in (2 * p, 2 * p + 1):
                e = jnp.exp2(-jnp.abs(level_diff(h, LEVELS[i]))).astype(BF16)
                qe.append(q16[h] * e)
                ke.append(k16[h] * e)
            lhs.append(jnp.concatenate(qe, axis=1))
            rhs.append(jnp.concatenate([jnp.concatenate([ke[0], zero], axis=1),
                                        jnp.concatenate([zero, ke[1]], axis=1)], axis=0))
        scores = [_dot_nt(lhs[h], rhs[h]) for h in heads]
        for h in heads:
            att[h].append((scores[h] * pair_ref[p]).astype(BF16))
    o = []
    for h in heads:
        vals = jnp.concatenate([v16[h]] * (2 * n_pairs), axis=0)
        same_row = jnp.sum(qs[h] * ks[h], axis=1, keepdims=True) * vs[h]
        o.append(carried[h] + _dot(jnp.concatenate(att[h], axis=1), vals) + same_row)
    return o, st


def _hgrn_kernel(*refs, rev, need_o, final, want_state, n_chunks):
    it = iter(refs)
    tabs = tuple(next(it) for _ in range(2))
    q_ref = next(it) if need_o else None
    k_ref, v_ref, lf_ref, s0_ref = next(it), next(it), next(it), next(it)
    prev_ref = next(it) if final else None
    gate_ref = next(it) if final else None
    o_ref = next(it) if need_o else None
    sout_ref = next(it) if want_state else None
    st_ref = next(it)

    j = pl.program_id(1)
    nb = pl.num_programs(1)

    @pl.when(j == 0)
    def _():
        st_ref[...] = s0_ref[0]

    group = HGRN_GROUP if n_chunks % HGRN_GROUP == 0 else 1

    def body(gi, carry):
        rows = []
        for s in range(group):
            c = gi * group + s
            c = (n_chunks - 1 - c) if rev else c
            rows.append(pl.ds(pl.multiple_of(c * CHUNK, CHUNK), CHUNK))
        pairs = [(r, hh) for r in rows for hh in range(HG_HEADS)]
        o, st_new = _hgrn_chunks([q_ref[0, hh, r, :] for r, hh in pairs] if need_o else None,
                                 [k_ref[0, hh, r, :] for r, hh in pairs],
                                 [v_ref[0, hh, r, :] for r, hh in pairs],
                                 [lf_ref[0, hh, r, :] for r, hh in pairs],
                                 [st_ref[hh] for hh in range(HG_HEADS)], tabs, rev, need_o)
        for hh in range(HG_HEADS):
            st_ref[hh] = st_new[hh]
        if need_o:
            for (r, hh), out in zip(pairs, o):
                if final:
                    tot = out + prev_ref[0, hh, r, :]
                    out = tot * lax.rsqrt(jnp.mean(tot * tot, axis=-1, keepdims=True) + EPS)
                    out = out * gate_ref[0, hh, r, :]
                o_ref[0, hh, r, :] = out
        return carry

    lax.fori_loop(0, n_chunks // group, body, 0)

    if want_state:
        @pl.when(j == nb - 1)
        def _():
            sout_ref[0] = st_ref[...]


def _hgrn(q, k, v, lf, s0, prev=None, gate=None, *, rev, need_o, want_state, name):
    b, hh, t, _ = k.shape
    tb = min(HGRN_BLOCK, t)
    nb = t // tb
    final = prev is not None
    tabs = _hgrn_tables(rev)

    seq = pl.BlockSpec((1, hh, tb, HEAD_DIM), lambda i, j: (i, 0, (nb - 1 - j) if rev else j, 0))
    state = pl.BlockSpec((1, hh, HEAD_DIM, HEAD_DIM), lambda i, j: (i, 0, 0, 0))
    args = list(tabs)
    in_specs = [pl.BlockSpec(a.shape, functools.partial(lambda nd, i, j: (0,) * nd, a.ndim)) for a in tabs]
    for a in ((q,) if need_o else ()) + (k, v, lf):
        args.append(a)
        in_specs.append(seq)
    args.append(s0)
    in_specs.append(state)
    if final:
        args += [prev, gate]
        in_specs += [seq, seq]
    out_shape, out_specs = [], []
    if need_o:
        out_shape.append(jax.ShapeDtypeStruct((b, hh, t, HEAD_DIM), F32))
        out_specs.append(seq)
    if want_state:
        out_shape.append(jax.ShapeDtypeStruct((b, hh, HEAD_DIM, HEAD_DIM), F32))
        out_specs.append(state)
    kern = functools.partial(_hgrn_kernel, rev=rev, need_o=need_o, final=final,
                             want_state=want_state, n_chunks=tb // CHUNK)
    est = 2 * (len(args) + len(out_shape)) * hh * tb * HEAD_DIM * 4
    return pl.pallas_call(
        kern,
        out_shape=out_shape,
        grid=(b, nb),
        in_specs=in_specs,
        out_specs=out_specs,
        scratch_shapes=[pltpu.VMEM((hh, HEAD_DIM, HEAD_DIM), F32)],
        compiler_params=pltpu.CompilerParams(
            dimension_semantics=("arbitrary", "arbitrary"),
            vmem_limit_bytes=_vmem_limit(est)),
        name=name,
    )(*args)


def _window_sum(x, w, stride, pos, extent):
    n = x.shape[0]
    whole_tiles = stride % SUBLANES == 0 and stride * extent == n

    def ahead(y, dist):
        if whole_tiles:
            k = min(dist, extent) * stride
            return jnp.concatenate([y[k:], jnp.zeros((k, y.shape[1]), y.dtype)], axis=0)
        return jnp.where(pos + dist < extent, pltpu.roll(y, (n - dist * stride) % n, 0), 0.0)

    def behind(y, dist):
        if whole_tiles:
            k = min(dist, extent) * stride
            return jnp.concatenate([jnp.zeros((k, y.shape[1]), y.dtype), y[:n - k]], axis=0)
        return jnp.where(pos >= dist, pltpu.roll(y, dist * stride, 0), 0.0)

    fwd, bwd, h = x, x, 1
    while h < w // 2:
        fwd = fwd + ahead(fwd, h)
        bwd = bwd + behind(bwd, h)
        h *= 2
    return fwd + behind(bwd, 1)


def _window_count(pos, w, extent):
    return (jnp.minimum(pos + w // 2, extent) - jnp.maximum(pos - w // 2, 0)).astype(F32)


def _pool_kernel(u_ref, pw_ref, ps_ref, o_ref):
    g = pl.program_id(1)
    n = u_ref.shape[2]
    tok = lax.broadcasted_iota(I32, (n, LANES), 0)
    col = tok & (GRID_W - 1)
    row = tok >> _log2(GRID_W)
    for gi, w in enumerate(POOL_WINDOWS):
        @pl.when(g == gi)
        def _(w=w):
            u = u_ref[0, 0]
            rows = n // GRID_W
            s = _window_sum(_window_sum(u, w, GRID_W, row, rows), w, 1, col, GRID_W)
            m = s / (_window_count(row, w, rows) * _window_count(col, w, GRID_W))
            o_ref[0, 0] = _dot((m - u).astype(BF16), pw_ref[0]) * ps_ref[0]


def _pool(u, pw, ps):
    b, g, t, c = u.shape
    blk = pl.BlockSpec((1, 1, t, c), lambda i, j: (i, j, 0, 0))
    return pl.pallas_call(
        _pool_kernel,
        out_shape=jax.ShapeDtypeStruct(u.shape, F32),
        grid=(b, g),
        in_specs=[blk,
                  pl.BlockSpec((1, c, c), lambda i, j: (j, 0, 0)),
                  pl.BlockSpec((1, 1, c), lambda i, j: (j, 0, 0))],
        out_specs=blk,
        compiler_params=pltpu.CompilerParams(
            dimension_semantics=("arbitrary", "arbitrary"),
            vmem_limit_bytes=_vmem_limit(12 * t * c * 4)),
        name="pool",
    )(u, pw, ps)


def _outproj_kernel(hg_ref, pool_ref, x_ref, w_ref, mod_ref, gpost_ref, gpre_ref, wr_ref, br_ref,
                    x1_ref, h2_ref, ti_ref, tw_ref, chosen_ref, cnt_ref):
    d = x_ref.shape[-1]
    r = pl.program_id(0)

    @pl.when(jnp.logical_and(r == 0, pl.program_id(1) == 0))
    def _():
        cnt_ref[...] = jnp.zeros(cnt_ref.shape, F32)

    gt1 = mod_ref[pl.ds(r, 1), pl.ds(2 * d, d)]
    sh2 = mod_ref[pl.ds(r, 1), pl.ds(3 * d, d)]
    sc2 = mod_ref[pl.ds(r, 1), pl.ds(4 * d, d)]
    tm = x_ref.shape[1]
    sub = tm // OUTPROJ_SLICES
    slices = [pl.ds(i * sub, sub) for i in range(OUTPROJ_SLICES)]
    ys = []
    for rows in slices:
        cat = jnp.concatenate([hg_ref[0, hh, rows, :] for hh in range(HG_HEADS)]
                              + [pool_ref[0, g, rows, :] for g in range(len(POOL_WINDOWS))], axis=1)
        ys.append(_dot(cat.astype(BF16), w_ref[...]))
    logits = []
    for i, rows in enumerate(slices):
        x1 = x_ref[0, rows, :] + gt1 * _rms(ys[i], gpost_ref[...])
        x1_ref[0, rows, :] = x1
        h2 = _rms(x1, gpre_ref[...]) * (1.0 + sc2) + sh2
        _store_row_tiles(h2_ref.at[0, pl.ds(i * sub * SUBLANES, sub * SUBLANES), :], h2)
        logits.append(_dot(h2.astype(BF16), wr_ref[...]) + br_ref[...])

    lane = lax.broadcasted_iota(I32, (sub, LANES), 1).astype(F32)
    picked = jnp.zeros((1, LANES), F32)
    for i, rows in enumerate(slices):
        lg = jnp.where(lane < N_EXPERTS, logits[i], -jnp.inf)
        ti = jnp.zeros(lg.shape, F32)
        tw = jnp.zeros(lg.shape, F32)
        chosen = jnp.zeros(lg.shape, F32)
        top = None
        for jj in range(TOP_K):
            m = jnp.max(lg, axis=1, keepdims=True)
            idx = jnp.min(jnp.where(lg == m, lane, float(LANES)), axis=1, keepdims=True)
            lg = jnp.where(lane == idx, -jnp.inf, lg)
            chosen = jnp.where(lane == idx, 1.0, chosen)
            top = m if top is None else top
            ti = jnp.where(lane == jj, idx, ti)
            tw = jnp.where(lane == jj, jnp.exp(m - top), tw)
        ti_ref[0, rows, :] = ti.astype(I32)
        tw_ref[0, rows, :] = tw / jnp.sum(tw, axis=1, keepdims=True)
        chosen_ref[0, rows, :] = chosen
        picked = picked + jnp.sum(chosen, axis=0, keepdims=True)
    cnt_ref[...] += picked


def _outproj(hg, pool, x, w_out, mod, gpost, gpre, wr, br):
    b, t, d = x.shape
    tm = min(TOKEN_TILE, t)
    head = pl.BlockSpec((1, HG_HEADS, tm, HEAD_DIM), lambda i, j: (i, 0, j, 0))
    tok = pl.BlockSpec((1, tm, d), lambda i, j: (i, j, 0))
    lanes = pl.BlockSpec((1, tm, LANES), lambda i, j: (i, j, 0))
    vec = pl.BlockSpec((1, d), lambda i, j: (0, 0))
    est = 2 * (2 * tm * HG_WIDTH * 4 + 3 * tm * d * 4 + 2 * tm * LANES * 4) + 2 * d * d * 2 + 6 * tm * d * 4
    return pl.pallas_call(
        _outproj_kernel,
        out_shape=[jax.ShapeDtypeStruct((b, t, d), F32),
                   jax.ShapeDtypeStruct((b, t * d // LANES, LANES), F32),
                   jax.ShapeDtypeStruct((b, t, LANES), I32), jax.ShapeDtypeStruct((b, t, LANES), F32),
                   jax.ShapeDtypeStruct((b, t, LANES), F32), jax.ShapeDtypeStruct((1, LANES), F32)],
        grid=(b, t // tm),
        in_specs=[head, head, tok,
                  pl.BlockSpec(w_out.shape, lambda i, j: (0, 0)),
                  pl.BlockSpec(mod.shape, lambda i, j: (0, 0)),
                  vec, vec,
                  pl.BlockSpec(wr.shape, lambda i, j: (0, 0)),
                  pl.BlockSpec((1, LANES), lambda i, j: (0, 0))],
        out_specs=[tok, pl.BlockSpec((1, tm * d // LANES, LANES), lambda i, j: (i, j, 0)),
                   lanes, lanes, lanes, pl.BlockSpec((1, LANES), lambda i, j: (0, 0))],
        compiler_params=pltpu.CompilerParams(
            dimension_semantics=("arbitrary", "arbitrary"),
            vmem_limit_bytes=_vmem_limit(est)),
        name="outproj",
    )(hg, pool, x, w_out, mod, gpost, gpre, wr, br)


def _rank_kernel(ti_ref, chosen_ref, tw_ref, start_ref, lpos_ref, tws_ref, seg_start_ref, seg_cnt_ref,
                 carry_ref):
    i = pl.program_id(0)

    @pl.when(i == 0)
    def _():
        carry_ref[...] = start_ref[...]

    cnt = chosen_ref[...]
    tm = cnt.shape[0]
    row = lax.broadcasted_iota(I32, (tm, tm), 0)
    col = lax.broadcasted_iota(I32, (tm, tm), 1)
    before = jnp.where(col < row, 1.0, 0.0).astype(BF16)
    carry = carry_ref[...]
    local = _dot(before, cnt.astype(BF16))

    seg_cnt = jnp.sum(cnt, axis=0, keepdims=True)
    lane = lax.broadcasted_iota(I32, (SUBLANES, LANES), 1)
    run = jnp.broadcast_to(seg_cnt, (SUBLANES, LANES))
    shift = 1
    while shift < LANES:
        run = run + jnp.where(lane >= shift, pltpu.roll(run, shift, 1), 0.0)
        shift *= 2
    seg_off = run[0:1, :] - seg_cnt
    lpos = jnp.take_along_axis(local + seg_off, ti_ref[...], axis=1)
    lpos_ref[...] = (lpos.T[:TOP_K, :] * SUBLANES).astype(I32)
    tws_ref[...] = tw_ref[...].T[:TOP_K, :]
    seg_start_ref[0] = (carry * SUBLANES).astype(I32)
    seg_cnt_ref[0] = seg_cnt.astype(I32)
    carry_ref[...] = carry + seg_cnt


def _rank(ti, chosen, tw, starts):
    t = ti.shape[0]
    tm = min(TOKEN_TILE, t)
    tok = pl.BlockSpec((tm, LANES), lambda i: (i, 0))
    slots = pl.BlockSpec((TOP_K, tm), lambda i: (0, i))
    per_tile = pl.BlockSpec((1, 1, LANES), lambda i: (i, 0, 0))
    return pl.pallas_call(
        _rank_kernel,
        out_shape=[jax.ShapeDtypeStruct((TOP_K, t), I32), jax.ShapeDtypeStruct((TOP_K, t), F32),
                   jax.ShapeDtypeStruct((t // tm, 1, LANES), I32),
                   jax.ShapeDtypeStruct((t // tm, 1, LANES), I32)],
        grid=(t // tm,),
        in_specs=[tok, tok, tok, pl.BlockSpec((1, LANES), lambda i: (0, 0))],
        out_specs=[slots, slots, per_tile, per_tile],
        scratch_shapes=[pltpu.VMEM((1, LANES), F32)],
        compiler_params=pltpu.CompilerParams(dimension_semantics=("arbitrary",)),
        name="rank",
    )(ti, chosen, tw, starts)


def _slot_specs(tm, n_tiles, tile_of_step):
    return [pl.BlockSpec((tm,), functools.partial(lambda jj, i: (jj * n_tiles + tile_of_step(i),), jj),
                         memory_space=pltpu.SMEM) for jj in range(TOP_K)]


def _tile_rows(start):
    return pl.ds(pl.multiple_of(start, SUBLANES), SUBLANES)


def _gap_copies(e, gap_start_ref, gap_len_ref, zeros, xs_ref, sem):
    off, n = gap_start_ref[e], gap_len_ref[e]
    out = []
    for bit in range(_log2(MOE_BM)):
        size = (1 << bit) * SUBLANES
        used = (n >> bit) & 1
        out.append((used == 1,
                    pltpu.make_async_copy(zeros.at[pl.ds(0, size), :],
                                          xs_ref.at[pl.ds(pl.multiple_of(off * SUBLANES, SUBLANES), size), :],
                                          sem)))
        off = off + used * (1 << bit)
    return out


def _dispatch_kernel(*refs):
    lpos_refs = refs[:TOP_K]
    (seg_start_ref, seg_cnt_ref, gap_start_ref, gap_len_ref, n_live_ref, h_ref, xs_ref,
     zeros, stage, sem, zsem) = refs[TOP_K:]
    tm = h_ref.shape[0] // SUBLANES
    block_rows = MOE_BM * SUBLANES
    n_blocks = xs_ref.shape[0] // block_rows

    @pl.when(pl.program_id(0) == 0)
    def _():
        zeros[...] = jnp.zeros(zeros.shape, zeros.dtype)

        def tail_copy(blk):
            return pltpu.make_async_copy(
                zeros, xs_ref.at[pl.ds(pl.multiple_of(blk * block_rows, block_rows), block_rows), :], zsem)

        def tail_start(blk, carry):
            tail_copy(blk).start()
            return carry

        def tail_wait(blk, carry):
            tail_copy(blk).wait()
            return carry

        lax.fori_loop(n_live_ref[0], n_blocks, tail_start, 0)
        lax.fori_loop(n_live_ref[0], n_blocks, tail_wait, 0)

        def fill(e, carry):
            for used, cp in _gap_copies(e, gap_start_ref, gap_len_ref, zeros, xs_ref, zsem):
                @pl.when(used)
                def _(cp=cp):
                    cp.start()
            return carry

        def drain(e, carry):
            for used, cp in _gap_copies(e, gap_start_ref, gap_len_ref, zeros, xs_ref, zsem):
                @pl.when(used)
                def _(cp=cp):
                    cp.wait()
            return carry

        lax.fori_loop(0, N_EXPERTS, fill, 0)
        lax.fori_loop(0, N_EXPERTS, drain, 0)

    i = pl.program_id(0)
    n_steps = pl.num_programs(0)
    slot = i % 2

    def drain(s):
        pltpu.make_async_copy(stage.at[s], xs_ref.at[pl.ds(0, TOP_K * tm * SUBLANES), :], sem.at[s]).wait()

    @pl.when(i >= 2)
    def _():
        drain(slot)

    def place(r, carry):
        row = h_ref[_tile_rows(r * SUBLANES), :]
        for jj in range(TOP_K):
            stage[slot, _tile_rows(lpos_refs[jj][r]), :] = row
        return carry

    lax.fori_loop(0, tm, place, 0, unroll=DMA_UNROLL)
    _segment_copies(seg_start_ref, seg_cnt_ref, tm,
                    lambda off, size: stage.at[slot, pl.ds(off, size), :],
                    lambda off, size: xs_ref.at[pl.ds(off, size), :], sem.at[slot], to_sorted=True)

    @pl.when(i == n_steps - 1)
    def _():
        drain(slot)

        @pl.when(n_steps >= 2)
        def _():
            drain(1 - slot)


def _segment_copies(seg_start_ref, seg_cnt_ref, tm, staged, sorted_rows, sem, *, to_sorted):
    def segment(e, stage_off):
        n, base = seg_cnt_ref[e], seg_start_ref[e]
        done = jnp.int32(0)
        for bit in range(_log2(tm) + 1):
            size = (1 << bit) * SUBLANES
            used = (n >> bit) & 1
            a = staged(pl.multiple_of(stage_off + done, SUBLANES), size)
            b = sorted_rows(pl.multiple_of(base + done, SUBLANES), size)

            src, dst = (a, b) if to_sorted else (b, a)

            @pl.when(used == 1)
            def _(src=src, dst=dst):
                pltpu.make_async_copy(src, dst, sem).start()
            done = done + used * size
        return stage_off + n * SUBLANES

    lax.fori_loop(0, N_EXPERTS, segment, jnp.int32(0))


def _dispatch(lpos, seg_start, seg_cnt, gap_start, gap_len, n_live, h2, n_sorted):
    rows, l = h2.shape
    t = rows // SUBLANES
    tm = min(TOKEN_TILE, t)
    smem = pl.BlockSpec(memory_space=pltpu.SMEM)
    per_tile = pl.BlockSpec((LANES,), lambda i: (i,), memory_space=pltpu.SMEM)
    return pl.pallas_call(
        _dispatch_kernel,
        out_shape=jax.ShapeDtypeStruct((n_sorted * SUBLANES, l), h2.dtype),
        grid=(t // tm,),
        in_specs=_slot_specs(tm, t // tm, lambda i: i) + [
            per_tile, per_tile, smem, smem, smem, pl.BlockSpec((tm * SUBLANES, l), lambda i: (i, 0))],
        out_specs=pl.BlockSpec(memory_space=pl.ANY),
        scratch_shapes=[pltpu.VMEM((MOE_BM * SUBLANES, l), h2.dtype),
                        pltpu.VMEM((2, TOP_K * tm * SUBLANES, l), h2.dtype),
                        pltpu.SemaphoreType.DMA((2,)), pltpu.SemaphoreType.DMA],
        compiler_params=pltpu.CompilerParams(
            dimension_semantics=("arbitrary",),
            vmem_limit_bytes=_vmem_limit((2 * TOP_K + 2) * tm * SUBLANES * l * 4 + MOE_BM * SUBLANES * l * 4)),
        name="dispatch",
    )(*([lpos] * TOP_K), seg_start, seg_cnt, gap_start, gap_len, n_live, h2)


def _moe_kernel(block_e_ref, next_e_ref, n_live_ref,
                x_ref, wg_ref, wu_ref, wd_ref, bg_ref, bu_ref, bd_ref, y_ref,
                wgs, wus, wds, wgb, wub, wdb, sem):
    i = pl.program_id(0)
    live = i < n_live_ref[0]
    e = block_e_ref[i]
    changed = jnp.logical_or(i == 0, e != block_e_ref[jnp.maximum(i - 1, 0)])

    def fetch(expert):
        return [pltpu.make_async_copy(w_ref.at[expert], stage, sem.at[n])
                for n, (w_ref, stage) in enumerate(((wg_ref, wgs), (wu_ref, wus), (wd_ref, wds)))]

    @pl.when(i == 0)
    def _():
        for cp in fetch(e):
            cp.start()

    @pl.when(jnp.logical_and(live, changed))
    def _():
        for cp in fetch(e):
            cp.wait()
        def cast(c, carry):
            rows = pl.ds(pl.multiple_of(c * LANES, LANES), LANES)
            for src, dst in ((wgs, wgb), (wus, wub), (wds, wdb)):
                dst[rows, :] = src[rows, :].astype(BF16)
            return carry

        lax.fori_loop(0, wgs.shape[0] // LANES, cast, 0)

        @pl.when(next_e_ref[i] != e)
        def _():
            for cp in fetch(next_e_ref[i]):
                cp.start()

    @pl.when(live)
    def _():
        x = _load_row_tiles(x_ref).astype(BF16)
        gate = jnp.minimum(_dot(x, wgb[...]) + bg_ref[...], SWIGLU_LIMIT)
        up = jnp.clip(_dot(x, wub[...]) + bu_ref[...], -SWIGLU_LIMIT, SWIGLU_LIMIT)
        hmid = (up + 1.0) * gate * _sigmoid(SWIGLU_ALPHA * gate)
        _store_row_tiles(y_ref, _dot(hmid.astype(BF16), wdb[...]) + bd_ref[...])

    @pl.when(jnp.logical_not(live))
    def _():
        y_ref[...] = jnp.zeros(y_ref.shape, y_ref.dtype)


def _moe(block_e, next_e, n_live, xs, wg, wu, wd, bg, bu, bd):
    d, f = wg.shape[-2:]
    n_blocks = block_e.shape[0]
    rows_in = pl.BlockSpec((MOE_BM * SUBLANES, LANES),
                           lambda i, be, ne, nl: (jnp.minimum(i, nl[0] - 1), 0))
    rows_out = pl.BlockSpec((MOE_BM * SUBLANES, LANES), lambda i, be, ne, nl: (i, 0))
    hbm = pl.BlockSpec(memory_space=pl.ANY)

    def bspec(n):
        return pl.BlockSpec((None, 1, n), lambda i, be, ne, nl: (be[i], 0, 0))

    est = 3 * d * f * 4 + 3 * d * f * 2 + 4 * MOE_BM * d * 4 + 4 * MOE_BM * f * 4
    return pl.pallas_call(
        _moe_kernel,
        out_shape=jax.ShapeDtypeStruct(xs.shape, F32),
        grid_spec=pltpu.PrefetchScalarGridSpec(
            num_scalar_prefetch=3,
            grid=(n_blocks,),
            in_specs=[rows_in, hbm, hbm, hbm, bspec(f), bspec(f), bspec(d)],
            out_specs=rows_out,
            scratch_shapes=[pltpu.VMEM((d, f), F32), pltpu.VMEM((d, f), F32), pltpu.VMEM((f, d), F32),
                            pltpu.VMEM((d, f), BF16), pltpu.VMEM((d, f), BF16), pltpu.VMEM((f, d), BF16),
                            pltpu.SemaphoreType.DMA((3,))]),
        compiler_params=pltpu.CompilerParams(
            dimension_semantics=("arbitrary",),
            vmem_limit_bytes=_vmem_limit(est)),
        name="moe",
    )(block_e, next_e, n_live, xs, wg, wu, wd, bg, bu, bd)


def _moe_layout(counts, n_rows):
    n_blocks = n_rows // MOE_BM + N_EXPERTS
    padded = (counts + MOE_BM - 1) // MOE_BM * MOE_BM
    ends = jnp.cumsum(padded)
    starts = ends - padded
    block_end = ends // MOE_BM
    i = jnp.arange(n_blocks, dtype=I32)
    ids = jnp.arange(N_EXPERTS, dtype=I32)
    last_e = jnp.max(jnp.where(counts > 0, ids, 0))
    block_e = jnp.minimum(jnp.sum(block_end[None, :] <= i[:, None], axis=1), last_e).astype(I32)
    later = jnp.logical_and(ids[None, :] > ids[:, None], counts[None, :] > 0)
    following = jnp.min(jnp.where(later, ids[None, :], N_EXPERTS), axis=1)
    following = jnp.where(following == N_EXPERTS, ids, following)
    next_e = jnp.sum(jnp.where(block_e[:, None] == ids[None, :], following[None, :], 0), axis=1)
    return (starts.astype(I32), (starts + counts).astype(I32), (padded - counts).astype(I32),
            block_e, next_e.astype(I32), block_end[-1:].astype(I32), n_blocks)


def _combine_kernel(*refs, tiles_per_batch):
    lpos_refs, tw_refs = refs[:TOP_K], refs[TOP_K:2 * TOP_K]
    (seg_start_ref, seg_cnt_ref, next_start_ref, next_cnt_ref, ys_ref, x1_ref, mod_ref, gpost_ref,
     o_ref, stage, acc, sem) = refs[2 * TOP_K:]
    tm, d = x1_ref.shape
    i = pl.program_id(0)
    slot = i % 2

    def fetch(start_ref, cnt_ref, s):
        _segment_copies(start_ref, cnt_ref, tm,
                        lambda off, size: stage.at[s, pl.ds(off, size), :],
                        lambda off, size: ys_ref.at[pl.ds(off, size), :], sem.at[s], to_sorted=False)

    @pl.when(i == 0)
    def _():
        fetch(seg_start_ref, seg_cnt_ref, 0)

    @pl.when(i + 1 < pl.num_programs(0))
    def _():
        fetch(next_start_ref, next_cnt_ref, 1 - slot)

    pltpu.make_async_copy(ys_ref.at[pl.ds(0, TOP_K * tm * SUBLANES), :], stage.at[slot],
                          sem.at[slot]).wait()

    def reduce(r, carry):
        row = tw_refs[0][r] * stage[slot, _tile_rows(lpos_refs[0][r]), :]
        for jj in range(1, TOP_K):
            row = row + tw_refs[jj][r] * stage[slot, _tile_rows(lpos_refs[jj][r]), :]
        acc[_tile_rows(r * SUBLANES), :] = row
        return carry

    lax.fori_loop(0, tm, reduce, 0, unroll=DMA_UNROLL)
    y = _load_row_tiles(acc)
    r = i // tiles_per_batch
    gt2 = mod_ref[pl.ds(r, 1), pl.ds(5 * d, d)]
    o_ref[...] = x1_ref[...] + gt2 * _rms(y, gpost_ref[...])


def _combine(lpos, tws, seg_start, seg_cnt, ys, x1, mod, gpost, tokens_per_batch):
    t, d = x1.shape
    tm = min(TOKEN_TILE, tokens_per_batch)
    n_tiles = t // tm
    kern = functools.partial(_combine_kernel, tiles_per_batch=tokens_per_batch // tm)
    est = (2 * TOP_K + 1) * tm * d * 4 + 2 * 2 * tm * d * 4 + 3 * tm * d * 4
    this_tile = pl.BlockSpec((LANES,), lambda i: (i,), memory_space=pltpu.SMEM)
    next_tile = pl.BlockSpec((LANES,), lambda i: (jnp.minimum(i + 1, n_tiles - 1),),
                             memory_space=pltpu.SMEM)
    return pl.pallas_call(
        kern,
        out_shape=jax.ShapeDtypeStruct((t, d), F32),
        grid=(n_tiles,),
        in_specs=_slot_specs(tm, n_tiles, lambda i: i) + _slot_specs(tm, n_tiles, lambda i: i) + [
                  this_tile, this_tile, next_tile, next_tile,
                  pl.BlockSpec(memory_space=pl.ANY),
                  pl.BlockSpec((tm, d), lambda i: (i, 0)),
                  pl.BlockSpec(mod.shape, lambda i: (0, 0)),
                  pl.BlockSpec((1, d), lambda i: (0, 0))],
        out_specs=pl.BlockSpec((tm, d), lambda i: (i, 0)),
        scratch_shapes=[pltpu.VMEM((2, TOP_K * tm * SUBLANES, LANES), F32),
                        pltpu.VMEM((tm * SUBLANES, LANES), F32),
                        pltpu.SemaphoreType.DMA((2,))],
        compiler_params=pltpu.CompilerParams(
            dimension_semantics=("arbitrary",),
            vmem_limit_bytes=_vmem_limit(est)),
        name="combine",
    )(*([lpos] * TOP_K), *([tws] * TOP_K), seg_start, seg_cnt, seg_start, seg_cnt, ys, x1, mod, gpost)


def kernel(x, c, ctx, c_ctx, w_ada, b_ada, g_pre_mix, g_post_mix, g_pre_ffn, g_post_ffn,
           w_in, w_out, hgrn_lb_logits, hgrn_norm, pool_w, pool_scale,
           w_router, b_router, w_gate, b_gate, w_up, b_up, w_down, b_down):
    b, s, d = x.shape
    layer = 0
    n_e = w_router.shape[-1]

    cc = jnp.concatenate([c, c_ctx[None, :]], axis=0)
    cc = jnp.pad(cc, ((0, -(b + 1) % SUBLANES), (0, 0)))
    mod = _ada(cc, w_ada[layer], b_ada[layer][None, :])

    w_in_b = w_in[layer].astype(BF16)
    lbl = hgrn_lb_logits[:2].reshape(4, HG_WIDTH)
    gn = hgrn_norm[layer][None, :]
    gpre = g_pre_mix[layer][None, :]

    lff_c, kf_c, lfb_c, kb_c, v_c = _inproj(ctx, mod, gpre, w_in_b, lbl, gn, mod_row=b, full=False)
    zeros = jnp.zeros((b, HG_HEADS, HEAD_DIM, HEAD_DIM), F32)
    (s_f,) = _hgrn(None, kf_c, v_c, lff_c, zeros, rev=False, need_o=False, want_state=True,
                   name="hgrn_ctx_fwd")
    (s_b,) = _hgrn(None, kb_c, v_c, lfb_c, zeros, rev=True, need_o=False, want_state=True,
                   name="hgrn_ctx_bwd")

    q, lff, kf, lfb, kb, v, gate, u = _inproj(x, mod, gpre, w_in_b, lbl, gn, mod_row=None, full=True)
    (o_f,) = _hgrn(q, kf, v, lff, s_f, rev=False, need_o=True, want_state=False, name="hgrn_fwd")
    (o_hg,) = _hgrn(q, kb, v, lfb, s_b, o_f, gate, rev=True, need_o=True, want_state=False,
                    name="hgrn_bwd")
    o_pool = _pool(u, pool_w[layer].astype(BF16), pool_scale[layer].reshape(len(POOL_WINDOWS), 1, POOL_GROUP))

    wr = jnp.pad(w_router[layer], ((0, 0), (0, LANES - n_e))).astype(BF16)
    br = jnp.pad(b_router[layer], (0, LANES - n_e))[None, :]
    x1, h2, ti, tw, chosen, cnt = _outproj(o_hg, o_pool, x, w_out[layer].astype(BF16), mod,
                                           g_post_mix[layer][None, :], g_pre_ffn[layer][None, :], wr, br)

    t = b * s
    starts, gap_start, gap_len, block_e, next_e, n_live, n_blocks = _moe_layout(
        cnt[0, :n_e].astype(I32), t * TOP_K)
    starts = jnp.pad(starts.astype(F32), (0, LANES - n_e))[None, :]
    lpos, tws, seg_start, seg_cnt = _rank(ti.reshape(t, LANES), chosen.reshape(t, LANES),
                                          tw.reshape(t, LANES), starts)
    lpos, tws = lpos.reshape(TOP_K * t), tws.reshape(TOP_K * t)
    seg_start, seg_cnt = seg_start.reshape(-1), seg_cnt.reshape(-1)

    xs = _dispatch(lpos, seg_start, seg_cnt, gap_start, gap_len, n_live,
                   h2.reshape(t * d // LANES, LANES), n_blocks * MOE_BM)
    ys = _moe(block_e, next_e, n_live, xs, w_gate[layer], w_up[layer], w_down[layer],
              b_gate[layer][:, None, :], b_up[layer][:, None, :], b_down[layer][:, None, :])
    out = _combine(lpos, tws, seg_start, seg_cnt, ys, x1.reshape(t, d), mod,
                   g_post_ffn[layer][None, :], s)
    return out.reshape(b, s, d)
```

```python
import functools
import math

import numpy as np
import jax
import jax.numpy as jnp
from jax import lax
from jax.experimental import pallas as pl
from jax.experimental.pallas import tpu as pltpu

F32 = jnp.float32
BF16 = jnp.bfloat16
I32 = jnp.int32

GRID_W = 64
HG_HEADS = 4
HEAD_DIM = 128
HG_WIDTH = HG_HEADS * HEAD_DIM
POOL_WINDOWS = (2, 4, 8, 16)
POOL_GROUP = 128
N_EXPERTS = 32
TOP_K = 4
SWIGLU_LIMIT = 7.0
SWIGLU_ALPHA = 1.702
EPS = 1e-6

LANES = 128
SUBLANES = 8
V7X_VMEM_BYTES = 64 * 1024 * 1024

TOKEN_TILE = 512
HGRN_BLOCK = 1024
ADA_COLS = 1536
PLACE_UNROLL = 8
REDUCE_UNROLL = 16
VMEM_COMPILER_BYTES = 4 << 20
VMEM_RESERVED_BYTES = 6 << 20
CHUNK = 64
HGRN_GROUP = 4
LEVELS = (32, 16, 8, 4, 2, 1)
FINE_LEVELS = (2, 1)
MOE_BM = 256
OUTPROJ_SLICES = 4
INPROJ_SLICES = 2
LOG2_E = math.log2(math.e)


def _vmem_limit(nbytes):
    return int(min(nbytes * 3 // 2 + VMEM_COMPILER_BYTES, V7X_VMEM_BYTES - VMEM_RESERVED_BYTES))


def _log2(n):
    assert n & (n - 1) == 0
    return n.bit_length() - 1


def _sigmoid(x):
    return 1.0 / (1.0 + jnp.exp(-x))


def _rms(x, gain):
    return x * lax.rsqrt(jnp.mean(x * x, axis=-1, keepdims=True) + EPS) * gain


def _dot(a, b):
    return jnp.dot(a, b, preferred_element_type=F32)


def _store_row_tiles(ref, val):
    rows = val.shape[0]
    for c in range(SUBLANES):
        ref[pl.ds(c, rows, stride=SUBLANES), :] = val[:, c * LANES:(c + 1) * LANES]


def _load_row_tiles(ref):
    rows = ref.shape[0] // SUBLANES
    return jnp.concatenate([ref[pl.ds(c, rows, stride=SUBLANES), :] for c in range(SUBLANES)], axis=1)


def _dot_nt(a, b):
    return lax.dot_general(a, b, (((1,), (1,)), ((), ())), preferred_element_type=F32)


def _ada_kernel(c_ref, w_ref, b_ref, o_ref):
    c = c_ref[...]
    s = (c * _sigmoid(c)).astype(BF16)
    o_ref[...] = _dot(s, w_ref[...].astype(BF16)) + b_ref[...]


def _ada(cc, w, b):
    rows, d = cc.shape
    n = w.shape[1]
    tn = ADA_COLS if n % ADA_COLS == 0 else n
    return pl.pallas_call(
        _ada_kernel,
        out_shape=jax.ShapeDtypeStruct((rows, n), F32),
        grid=(n // tn,),
        in_specs=[pl.BlockSpec((rows, d), lambda j: (0, 0)),
                  pl.BlockSpec((d, tn), lambda j: (0, j)),
                  pl.BlockSpec((1, tn), lambda j: (0, j))],
        out_specs=pl.BlockSpec((rows, tn), lambda j: (0, j)),
        compiler_params=pltpu.CompilerParams(
            dimension_semantics=("arbitrary",),
            vmem_limit_bytes=_vmem_limit(2 * d * tn * 4 + d * tn * 2)),
        name="ada",
    )(cc, w, b)


def _inproj_kernel(x_ref, mod_ref, gpre_ref, w_ref, lbl_ref, gn_ref, *outs, mod_row, full):
    d = x_ref.shape[-1]
    r = pl.program_id(0) if mod_row is None else mod_row
    sh = mod_ref[pl.ds(r, 1), pl.ds(0, d)]
    sc = mod_ref[pl.ds(r, 1), pl.ds(d, d)]
    tm = x_ref.shape[1]
    n_slices = INPROJ_SLICES if tm % (INPROJ_SLICES * SUBLANES) == 0 else 1
    sub = tm // n_slices
    slices = [pl.ds(i * sub, sub) for i in range(n_slices)]
    hs = [(_rms(x_ref[0, rows, :], gpre_ref[...]) * (1.0 + sc) + sh).astype(BF16) for rows in slices]

    def proj(g):
        return [_dot(h, w_ref[:, g * HG_WIDTH:(g + 1) * HG_WIDTH]) for h in hs]

    def put(ref, vals):
        for rows, val in zip(slices, vals):
            for hh in range(HG_HEADS):
                ref[0, hh, rows, :] = val[:, hh * HEAD_DIM:(hh + 1) * HEAD_DIM]

    def lower_bound(direction):
        l0 = lbl_ref[pl.ds(direction, 1), :]
        l1 = lbl_ref[pl.ds(2 + direction, 1), :]
        m = jnp.maximum(l0, l1)
        e0 = jnp.exp(l0 - m)
        return e0 / (e0 + jnp.exp(l1 - m))

    if full:
        q_o, lff_o, kf_o, lfb_o, kb_o, v_o, gate_o, u_o = outs
        put(q_o, [q * _sigmoid(q) for q in proj(0)])
    else:
        lff_o, kf_o, lfb_o, kb_o, v_o = outs
    for direction, (lf_o, k_o) in enumerate(((lff_o, kf_o), (lfb_o, kb_o))):
        lb = lower_bound(direction)
        sgs = [_sigmoid(z) for z in proj(1 + direction)]
        put(lf_o, [jnp.log(lb + (1.0 - lb) * sg) for sg in sgs])
        put(k_o, [(1.0 - lb) * (1.0 - sg) for sg in sgs])
    put(v_o, proj(3))
    if full:
        put(gate_o, [gn_ref[...] * _sigmoid(z) for z in proj(4)])
        put(u_o, proj(5))


def _inproj(x, mod, gpre, w_in, lbl, gn, *, mod_row, full):
    b, t, d = x.shape
    tm = min(TOKEN_TILE, t)
    n_out = 8 if full else 5
    n_cols = w_in.shape[1]
    head = pl.BlockSpec((1, HG_HEADS, tm, HEAD_DIM), lambda i, j: (i, 0, j, 0))
    kern = functools.partial(_inproj_kernel, mod_row=mod_row, full=full)
    est = 2 * tm * d * 4 + 2 * d * n_cols * 2 + n_out * 2 * tm * HG_WIDTH * 4 + 4 * tm * HG_WIDTH * 4
    return pl.pallas_call(
        kern,
        out_shape=[jax.ShapeDtypeStruct((b, HG_HEADS, t, HEAD_DIM), F32)] * n_out,
        grid=(b, t // tm),
        in_specs=[pl.BlockSpec((1, tm, d), lambda i, j: (i, j, 0)),
                  pl.BlockSpec(mod.shape, lambda i, j: (0, 0)),
                  pl.BlockSpec((1, d), lambda i, j: (0, 0)),
                  pl.BlockSpec(w_in.shape, lambda i, j: (0, 0)),
                  pl.BlockSpec(lbl.shape, lambda i, j: (0, 0)),
                  pl.BlockSpec((1, HG_WIDTH), lambda i, j: (0, 0))],
        out_specs=[head] * n_out,
        compiler_params=pltpu.CompilerParams(
            dimension_semantics=("arbitrary", "arbitrary"),
            vmem_limit_bytes=_vmem_limit(est)),
        name="inproj_full" if full else "inproj_ctx",
    )(x, mod, gpre, w_in, lbl, gn)


def _hgrn_tables(rev):
    row = np.arange(CHUNK)[:, None]
    col = np.arange(CHUNK)[None, :]
    tri = ((col >= row) if rev else (col <= row)).astype(np.float32)
    blocks, masks = [tri], []
    for h in LEVELS:
        if h in FINE_LEVELS:
            blocks.append(tri - tri[_ref_row(np.arange(CHUNK), h, rev)])
        is_q = ((row // h) % 2) == (0 if rev else 1)
        key_half = ((col // h) % 2) == (1 if rev else 0)
        masks.append(((row // (2 * h)) == (col // (2 * h))) & is_q & key_half)
    pair = np.stack([np.concatenate(masks[i:i + 2], axis=1) for i in range(0, len(LEVELS), 2)])
    cums = np.concatenate(blocks, axis=0)
    return jnp.asarray(np.concatenate([cums, cums], axis=1), BF16), jnp.asarray(pair, F32)


def _ref_row(t, h, rev):
    return (t // (2 * h)) * (2 * h) + (h if rev else h - 1)


def _hgrn_chunks(qs, ks, vs, lfs, sts, tabs, rev, need_o):
    cums_ref, pair_ref = tabs
    n_heads = len(sts)
    heads = range(len(ks))
    n_pairs = len(LEVELS) // 2
    n_cum = CHUNK * (1 + len(FINE_LEVELS)) if need_o else CHUNK
    cums = cums_ref[0:n_cum, :]

    his, los = [], []
    for h in heads:
        lf2 = lfs[h] * LOG2_E
        hi = lf2.astype(BF16)
        his.append(hi)
        los.append((lf2 - hi.astype(F32)).astype(BF16))
    both = _dot(cums, jnp.concatenate([jnp.concatenate(his, axis=1), jnp.concatenate(los, axis=1)], axis=0))
    bd = [both[:, h * LANES:(h + 1) * LANES] for h in heads]

    k16 = [ks[h].astype(BF16) for h in heads]
    v16 = [vs[h].astype(BF16) for h in heads]
    b = [bd[h][0:CHUNK] for h in heads]

    def level_diff(h, level):
        if level in FINE_LEVELS:
            i = FINE_LEVELS.index(level)
            return bd[h][CHUNK * (1 + i):CHUNK * (2 + i)]
        refs = [_ref_row(m * 2 * level, level, rev) for m in range(CHUNK // (2 * level))]
        pieces = [jnp.broadcast_to(b[h][r:r + 1, :], (2 * level, LANES)) for r in refs]
        return b[h] - (pieces[0] if len(pieces) == 1 else jnp.concatenate(pieces, axis=0))
    edge = [b[h][0:1, :] if rev else b[h][CHUNK - 1:CHUNK, :] for h in heads]
    kd = [k16[h] * jnp.exp2(edge[h] - b[h]).astype(BF16) for h in heads]
    grow = [lax.dot_general(v16[h], kd[h], (((0,), (0,)), ((), ())), preferred_element_type=F32)
            for h in heads]
    q16 = [qs[h].astype(BF16) for h in heads] if need_o else None
    st = list(sts)
    carried = []
    for h in heads:
        if need_o:
            carried.append(_dot_nt(q16[h] * jnp.exp2(b[h]).astype(BF16), st[h % n_heads].astype(BF16)))
        st[h % n_heads] = st[h % n_heads] * jnp.exp2(edge[h]) + grow[h]
    if not need_o:
        return None, st

    zero = jnp.zeros((CHUNK, LANES), BF16)
    att = [[] for _ in heads]
    for p in range(n_pairs):
        lhs, rhs = [], []
        for h in heads:
            qe, ke = [], []
            for i in (2 * p, 2 * p + 1):
                e = jnp.exp2(-jnp.abs(level_diff(h, LEVELS[i]))).astype(BF16)
                qe.append(q16[h] * e)
                ke.append(k16[h] * e)
            lhs.append(jnp.concatenate(qe, axis=1))
            rhs.append(jnp.concatenate([jnp.concatenate([ke[0], zero], axis=1),
                                        jnp.concatenate([zero, ke[1]], axis=1)], axis=0))
        scores = [_dot_nt(lhs[h], rhs[h]) for h in heads]
        for h in heads:
            att[h].append((scores[h] * pair_ref[p]).astype(BF16))
    o = []
    for h in heads:
        vals = jnp.concatenate([v16[h]] * (2 * n_pairs), axis=0)
        same_row = jnp.sum(qs[h] * ks[h], axis=1, keepdims=True) * vs[h]
        o.append(carried[h] + _dot(jnp.concatenate(att[h], axis=1), vals) + same_row)
    return o, st


def _hgrn_kernel(*refs, rev, need_o, final, want_state, n_chunks):
    it = iter(refs)
    tabs = tuple(next(it) for _ in range(2))
    q_ref = next(it) if need_o else None
    k_ref, v_ref, lf_ref, s0_ref = next(it), next(it), next(it), next(it)
    prev_ref = next(it) if final else None
    gate_ref = next(it) if final else None
    o_ref = next(it) if need_o else None
    sout_ref = next(it) if want_state else None
    st_ref = next(it)

    j = pl.program_id(1)
    nb = pl.num_programs(1)

    @pl.when(j == 0)
    def _():
        st_ref[...] = s0_ref[0]

    group = HGRN_GROUP if n_chunks % HGRN_GROUP == 0 else 1

    def body(gi, carry):
        rows = []
        for s in range(group):
            c = gi * group + s
            c = (n_chunks - 1 - c) if rev else c
            rows.append(pl.ds(pl.multiple_of(c * CHUNK, CHUNK), CHUNK))
        pairs = [(r, hh) for r in rows for hh in range(HG_HEADS)]
        o, st_new = _hgrn_chunks([q_ref[0, hh, r, :] for r, hh in pairs] if need_o else None,
                                 [k_ref[0, hh, r, :] for r, hh in pairs],
                                 [v_ref[0, hh, r, :] for r, hh in pairs],
                                 [lf_ref[0, hh, r, :] for r, hh in pairs],
                                 [st_ref[hh] for hh in range(HG_HEADS)], tabs, rev, need_o)
        for hh in range(HG_HEADS):
            st_ref[hh] = st_new[hh]
        if need_o:
            for (r, hh), out in zip(pairs, o):
                if final:
                    tot = out + prev_ref[0, hh, r, :]
                    out = tot * lax.rsqrt(jnp.mean(tot * tot, axis=-1, keepdims=True) + EPS)
                    out = out * gate_ref[0, hh, r, :]
                o_ref[0, hh, r, :] = out
        return carry

    lax.fori_loop(0, n_chunks // group, body, 0)

    if want_state:
        @pl.when(j == nb - 1)
        def _():
            sout_ref[0] = st_ref[...]


def _hgrn(q, k, v, lf, s0, prev=None, gate=None, *, rev, need_o, want_state, name):
    b, hh, t, _ = k.shape
    tb = min(HGRN_BLOCK, t)
    nb = t // tb
    final = prev is not None
    tabs = _hgrn_tables(rev)

    seq = pl.BlockSpec((1, hh, tb, HEAD_DIM), lambda i, j: (i, 0, (nb - 1 - j) if rev else j, 0))
    state = pl.BlockSpec((1, hh, HEAD_DIM, HEAD_DIM), lambda i, j: (i, 0, 0, 0))
    args = list(tabs)
    in_specs = [pl.BlockSpec(a.shape, functools.partial(lambda nd, i, j: (0,) * nd, a.ndim)) for a in tabs]
    for a in ((q,) if need_o else ()) + (k, v, lf):
        args.append(a)
        in_specs.append(seq)
    args.append(s0)
    in_specs.append(state)
    if final:
        args += [prev, gate]
        in_specs += [seq, seq]
    out_shape, out_specs = [], []
    if need_o:
        out_shape.append(jax.ShapeDtypeStruct((b, hh, t, HEAD_DIM), F32))
        out_specs.append(seq)
    if want_state:
        out_shape.append(jax.ShapeDtypeStruct((b, hh, HEAD_DIM, HEAD_DIM), F32))
        out_specs.append(state)
    kern = functools.partial(_hgrn_kernel, rev=rev, need_o=need_o, final=final,
                             want_state=want_state, n_chunks=tb // CHUNK)
    est = 2 * (len(args) + len(out_shape)) * hh * tb * HEAD_DIM * 4
    return pl.pallas_call(
        kern,
        out_shape=out_shape,
        grid=(b, nb),
        in_specs=in_specs,
        out_specs=out_specs,
        scratch_shapes=[pltpu.VMEM((hh, HEAD_DIM, HEAD_DIM), F32)],
        compiler_params=pltpu.CompilerParams(
            dimension_semantics=("arbitrary", "arbitrary"),
            vmem_limit_bytes=_vmem_limit(est)),
        name=name,
    )(*args)


def _window_sum(x, w, stride, pos, extent):
    n = x.shape[0]
    whole_tiles = stride % SUBLANES == 0 and stride * extent == n

    def ahead(y, dist):
        if whole_tiles:
            k = min(dist, extent) * stride
            return jnp.concatenate([y[k:], jnp.zeros((k, y.shape[1]), y.dtype)], axis=0)
        return jnp.where(pos + dist < extent, pltpu.roll(y, (n - dist * stride) % n, 0), 0.0)

    def behind(y, dist):
        if whole_tiles:
            k = min(dist, extent) * stride
            return jnp.concatenate([jnp.zeros((k, y.shape[1]), y.dtype), y[:n - k]], axis=0)
        return jnp.where(pos >= dist, pltpu.roll(y, dist * stride, 0), 0.0)

    fwd, bwd, h = x, x, 1
    while h < w // 2:
        fwd = fwd + ahead(fwd, h)
        bwd = bwd + behind(bwd, h)
        h *= 2
    return fwd + behind(bwd, 1)


def _window_count(pos, w, extent):
    return (jnp.minimum(pos + w // 2, extent) - jnp.maximum(pos - w // 2, 0)).astype(F32)


def _pool_kernel(u_ref, pw_ref, ps_ref, o_ref):
    g = pl.program_id(1)
    n = u_ref.shape[2]
    tok = lax.broadcasted_iota(I32, (n, LANES), 0)
    col = tok & (GRID_W - 1)
    row = tok >> _log2(GRID_W)
    for gi, w in enumerate(POOL_WINDOWS):
        @pl.when(g == gi)
        def _(w=w):
            u = u_ref[0, 0]
            rows = n // GRID_W
            s = _window_sum(_window_sum(u, w, GRID_W, row, rows), w, 1, col, GRID_W)
            m = s / (_window_count(row, w, rows) * _window_count(col, w, GRID_W))
            o_ref[0, 0] = _dot((m - u).astype(BF16), pw_ref[0]) * ps_ref[0]


def _pool(u, pw, ps):
    b, g, t, c = u.shape
    blk = pl.BlockSpec((1, 1, t, c), lambda i, j: (i, j, 0, 0))
    return pl.pallas_call(
        _pool_kernel,
        out_shape=jax.ShapeDtypeStruct(u.shape, F32),
        grid=(b, g),
        in_specs=[blk,
                  pl.BlockSpec((1, c, c), lambda i, j: (j, 0, 0)),
                  pl.BlockSpec((1, 1, c), lambda i, j: (j, 0, 0))],
        out_specs=blk,
        compiler_params=pltpu.CompilerParams(
            dimension_semantics=("arbitrary", "arbitrary"),
            vmem_limit_bytes=_vmem_limit(12 * t * c * 4)),
        name="pool",
    )(u, pw, ps)


def _outproj_kernel(hg_ref, pool_ref, x_ref, w_ref, mod_ref, gpost_ref, gpre_ref, wr_ref, br_ref,
                    x1_ref, h2_ref, ti_ref, tw_ref, chosen_ref, cnt_ref):
    d = x_ref.shape[-1]
    r = pl.program_id(0)

    @pl.when(jnp.logical_and(r == 0, pl.program_id(1) == 0))
    def _():
        cnt_ref[...] = jnp.zeros(cnt_ref.shape, F32)

    gt1 = mod_ref[pl.ds(r, 1), pl.ds(2 * d, d)]
    sh2 = mod_ref[pl.ds(r, 1), pl.ds(3 * d, d)]
    sc2 = mod_ref[pl.ds(r, 1), pl.ds(4 * d, d)]
    tm = x_ref.shape[1]
    sub = tm // OUTPROJ_SLICES
    slices = [pl.ds(i * sub, sub) for i in range(OUTPROJ_SLICES)]
    ys = []
    for rows in slices:
        cat = jnp.concatenate([hg_ref[0, hh, rows, :] for hh in range(HG_HEADS)]
                              + [pool_ref[0, g, rows, :] for g in range(len(POOL_WINDOWS))], axis=1)
        ys.append(_dot(cat.astype(BF16), w_ref[...]))
    logits = []
    for i, rows in enumerate(slices):
        x1 = x_ref[0, rows, :] + gt1 * _rms(ys[i], gpost_ref[...])
        x1_ref[0, rows, :] = x1
        h2 = _rms(x1, gpre_ref[...]) * (1.0 + sc2) + sh2
        _store_row_tiles(h2_ref.at[0, pl.ds(i * sub * SUBLANES, sub * SUBLANES), :], h2)
        logits.append(_dot(h2.astype(BF16), wr_ref[...]) + br_ref[...])

    lane = lax.broadcasted_iota(I32, (sub, LANES), 1).astype(F32)
    picked = jnp.zeros((1, LANES), F32)
    for i, rows in enumerate(slices):
        lg = jnp.where(lane < N_EXPERTS, logits[i], -jnp.inf)
        ti = jnp.zeros(lg.shape, F32)
        tw = jnp.zeros(lg.shape, F32)
        chosen = jnp.zeros(lg.shape, F32)
        top = None
        for jj in range(TOP_K):
            m = jnp.max(lg, axis=1, keepdims=True)
            idx = jnp.min(jnp.where(lg == m, lane, float(LANES)), axis=1, keepdims=True)
            lg = jnp.where(lane == idx, -jnp.inf, lg)
            chosen = jnp.where(lane == idx, 1.0, chosen)
            top = m if top is None else top
            ti = jnp.where(lane == jj, idx, ti)
            tw = jnp.where(lane == jj, jnp.exp(m - top), tw)
        ti_ref[0, rows, :] = ti.astype(I32)
        tw_ref[0, rows, :] = tw / jnp.sum(tw, axis=1, keepdims=True)
        chosen_ref[0, rows, :] = chosen
        picked = picked + jnp.sum(chosen, axis=0, keepdims=True)
    cnt_ref[...] += picked


def _outproj(hg, pool, x, w_out, mod, gpost, gpre, wr, br):
    b, t, d = x.shape
    tm = min(TOKEN_TILE, t)
    head = pl.BlockSpec((1, HG_HEADS, tm, HEAD_DIM), lambda i, j: (i, 0, j, 0))
    tok = pl.BlockSpec((1, tm, d), lambda i, j: (i, j, 0))
    lanes = pl.BlockSpec((1, tm, LANES), lambda i, j: (i, j, 0))
    vec = pl.BlockSpec((1, d), lambda i, j: (0, 0))
    est = 2 * (2 * tm * HG_WIDTH * 4 + 3 * tm * d * 4 + 2 * tm * LANES * 4) + 2 * d * d * 2 + 6 * tm * d * 4
    return pl.pallas_call(
        _outproj_kernel,
        out_shape=[jax.ShapeDtypeStruct((b, t, d), F32),
                   jax.ShapeDtypeStruct((b, t * d // LANES, LANES), F32),
                   jax.ShapeDtypeStruct((b, t, LANES), I32), jax.ShapeDtypeStruct((b, t, LANES), F32),
                   jax.ShapeDtypeStruct((b, t, LANES), F32), jax.ShapeDtypeStruct((1, LANES), F32)],
        grid=(b, t // tm),
        in_specs=[head, head, tok,
                  pl.BlockSpec(w_out.shape, lambda i, j: (0, 0)),
                  pl.BlockSpec(mod.shape, lambda i, j: (0, 0)),
                  vec, vec,
                  pl.BlockSpec(wr.shape, lambda i, j: (0, 0)),
                  pl.BlockSpec((1, LANES), lambda i, j: (0, 0))],
        out_specs=[tok, pl.BlockSpec((1, tm * d // LANES, LANES), lambda i, j: (i, j, 0)),
                   lanes, lanes, lanes, pl.BlockSpec((1, LANES), lambda i, j: (0, 0))],
        compiler_params=pltpu.CompilerParams(
            dimension_semantics=("arbitrary", "arbitrary"),
            vmem_limit_bytes=_vmem_limit(est)),
        name="outproj",
    )(hg, pool, x, w_out, mod, gpost, gpre, wr, br)


def _rank_kernel(ti_ref, chosen_ref, tw_ref, start_ref, lpos_ref, tws_ref, seg_start_ref, seg_cnt_ref,
                 carry_ref):
    i = pl.program_id(0)

    @pl.when(i == 0)
    def _():
        carry_ref[...] = start_ref[...]

    cnt = chosen_ref[...]
    tm = cnt.shape[0]
    row = lax.broadcasted_iota(I32, (tm, tm), 0)
    col = lax.broadcasted_iota(I32, (tm, tm), 1)
    before = jnp.where(col < row, 1.0, 0.0).astype(BF16)
    carry = carry_ref[...]
    local = _dot(before, cnt.astype(BF16))

    seg_cnt = jnp.sum(cnt, axis=0, keepdims=True)
    lane = lax.broadcasted_iota(I32, (SUBLANES, LANES), 1)
    run = jnp.broadcast_to(seg_cnt, (SUBLANES, LANES))
    shift = 1
    while shift < LANES:
        run = run + jnp.where(lane >= shift, pltpu.roll(run, shift, 1), 0.0)
        shift *= 2
    seg_off = run[0:1, :] - seg_cnt
    lpos = jnp.take_along_axis(local + seg_off, ti_ref[...], axis=1)
    lpos_ref[...] = (lpos.T[:TOP_K, :] * SUBLANES).astype(I32)
    tws_ref[...] = tw_ref[...].T[:TOP_K, :]
    seg_start_ref[0] = (carry * SUBLANES).astype(I32)
    seg_cnt_ref[0] = seg_cnt.astype(I32)
    carry_ref[...] = carry + seg_cnt


def _rank(ti, chosen, tw, starts):
    t = ti.shape[0]
    tm = min(TOKEN_TILE, t)
    tok = pl.BlockSpec((tm, LANES), lambda i: (i, 0))
    slots = pl.BlockSpec((TOP_K, tm), lambda i: (0, i))
    per_tile = pl.BlockSpec((1, 1, LANES), lambda i: (i, 0, 0))
    return pl.pallas_call(
        _rank_kernel,
        out_shape=[jax.ShapeDtypeStruct((TOP_K, t), I32), jax.ShapeDtypeStruct((TOP_K, t), F32),
                   jax.ShapeDtypeStruct((t // tm, 1, LANES), I32),
                   jax.ShapeDtypeStruct((t // tm, 1, LANES), I32)],
        grid=(t // tm,),
        in_specs=[tok, tok, tok, pl.BlockSpec((1, LANES), lambda i: (0, 0))],
        out_specs=[slots, slots, per_tile, per_tile],
        scratch_shapes=[pltpu.VMEM((1, LANES), F32)],
        compiler_params=pltpu.CompilerParams(dimension_semantics=("arbitrary",)),
        name="rank",
    )(ti, chosen, tw, starts)


def _slot_specs(tm, n_tiles, tile_of_step):
    return [pl.BlockSpec((tm,), functools.partial(lambda jj, i: (jj * n_tiles + tile_of_step(i),), jj),
                         memory_space=pltpu.SMEM) for jj in range(TOP_K)]


def _tile_rows(start):
    return pl.ds(pl.multiple_of(start, SUBLANES), SUBLANES)


def _gap_copies(e, gap_start_ref, gap_len_ref, zeros, xs_ref, sem):
    off, n = gap_start_ref[e], gap_len_ref[e]
    out = []
    for bit in range(_log2(MOE_BM)):
        size = (1 << bit) * SUBLANES
        used = (n >> bit) & 1
        out.append((used == 1,
                    pltpu.make_async_copy(zeros.at[pl.ds(0, size), :],
                                          xs_ref.at[pl.ds(pl.multiple_of(off * SUBLANES, SUBLANES), size), :],
                                          sem)))
        off = off + used * (1 << bit)
    return out


def _dispatch_kernel(*refs):
    lpos_refs = refs[:TOP_K]
    (seg_start_ref, seg_cnt_ref, gap_start_ref, gap_len_ref, n_live_ref, h_ref, xs_ref,
     zeros, stage, sem, zsem) = refs[TOP_K:]
    tm = h_ref.shape[0] // SUBLANES
    block_rows = MOE_BM * SUBLANES
    n_blocks = xs_ref.shape[0] // block_rows

    @pl.when(pl.program_id(0) == 0)
    def _():
        zeros[...] = jnp.zeros(zeros.shape, zeros.dtype)

        def tail_copy(blk):
            return pltpu.make_async_copy(
                zeros, xs_ref.at[pl.ds(pl.multiple_of(blk * block_rows, block_rows), block_rows), :], zsem)

        def tail_start(blk, carry):
            tail_copy(blk).start()
            return carry

        def tail_wait(blk, carry):
            tail_copy(blk).wait()
            return carry

        lax.fori_loop(n_live_ref[0], n_blocks, tail_start, 0)
        lax.fori_loop(n_live_ref[0], n_blocks, tail_wait, 0)

        def fill(e, carry):
            for used, cp in _gap_copies(e, gap_start_ref, gap_len_ref, zeros, xs_ref, zsem):
                @pl.when(used)
                def _(cp=cp):
                    cp.start()
            return carry

        def drain(e, carry):
            for used, cp in _gap_copies(e, gap_start_ref, gap_len_ref, zeros, xs_ref, zsem):
                @pl.when(used)
                def _(cp=cp):
                    cp.wait()
            return carry

        lax.fori_loop(0, N_EXPERTS, fill, 0)
        lax.fori_loop(0, N_EXPERTS, drain, 0)

    i = pl.program_id(0)
    n_steps = pl.num_programs(0)
    slot = i % 2

    def drain(s):
        pltpu.make_async_copy(stage.at[s], xs_ref.at[pl.ds(0, TOP_K * tm * SUBLANES), :], sem.at[s]).wait()

    @pl.when(i >= 2)
    def _():
        drain(slot)

    def place(r, carry):
        row = h_ref[_tile_rows(r * SUBLANES), :]
        for jj in range(TOP_K):
            stage[slot, _tile_rows(lpos_refs[jj][r]), :] = row
        return carry

    lax.fori_loop(0, tm, place, 0, unroll=PLACE_UNROLL)
    _segment_copies(seg_start_ref, seg_cnt_ref, tm,
                    lambda off, size: stage.at[slot, pl.ds(off, size), :],
                    lambda off, size: xs_ref.at[pl.ds(off, size), :], sem.at[slot], to_sorted=True)

    @pl.when(i == n_steps - 1)
    def _():
        drain(slot)

        @pl.when(n_steps >= 2)
        def _():
            drain(1 - slot)


def _segment_copies(seg_start_ref, seg_cnt_ref, tm, staged, sorted_rows, sem, *, to_sorted):
    def segment(e, stage_off):
        n, base = seg_cnt_ref[e], seg_start_ref[e]
        done = jnp.int32(0)
        for bit in range(_log2(tm) + 1):
            size = (1 << bit) * SUBLANES
            used = (n >> bit) & 1
            a = staged(pl.multiple_of(stage_off + done, SUBLANES), size)
            b = sorted_rows(pl.multiple_of(base + done, SUBLANES), size)

            src, dst = (a, b) if to_sorted else (b, a)

            @pl.when(used == 1)
            def _(src=src, dst=dst):
                pltpu.make_async_copy(src, dst, sem).start()
            done = done + used * size
        return stage_off + n * SUBLANES

    lax.fori_loop(0, N_EXPERTS, segment, jnp.int32(0))


def _dispatch(lpos, seg_start, seg_cnt, gap_start, gap_len, n_live, h2, n_sorted):
    rows, l = h2.shape
    t = rows // SUBLANES
    tm = min(TOKEN_TILE, t)
    smem = pl.BlockSpec(memory_space=pltpu.SMEM)
    per_tile = pl.BlockSpec((LANES,), lambda i: (i,), memory_space=pltpu.SMEM)
    return pl.pallas_call(
        _dispatch_kernel,
        out_shape=jax.ShapeDtypeStruct((n_sorted * SUBLANES, l), h2.dtype),
        grid=(t // tm,),
        in_specs=_slot_specs(tm, t // tm, lambda i: i) + [
            per_tile, per_tile, smem, smem, smem, pl.BlockSpec((tm * SUBLANES, l), lambda i: (i, 0))],
        out_specs=pl.BlockSpec(memory_space=pl.ANY),
        scratch_shapes=[pltpu.VMEM((MOE_BM * SUBLANES, l), h2.dtype),
                        pltpu.VMEM((2, TOP_K * tm * SUBLANES, l), h2.dtype),
                        pltpu.SemaphoreType.DMA((2,)), pltpu.SemaphoreType.DMA],
        compiler_params=pltpu.CompilerParams(
            dimension_semantics=("arbitrary",),
            vmem_limit_bytes=_vmem_limit((2 * TOP_K + 2) * tm * SUBLANES * l * 4 + MOE_BM * SUBLANES * l * 4)),
        name="dispatch",
    )(*([lpos] * TOP_K), seg_start, seg_cnt, gap_start, gap_len, n_live, h2)


def _moe_kernel(block_e_ref, next_e_ref, n_live_ref,
                x_ref, wg_ref, wu_ref, wd_ref, bg_ref, bu_ref, bd_ref, y_ref,
                wgs, wus, wds, wgb, wub, wdb, sem):
    i = pl.program_id(0)
    live = i < n_live_ref[0]
    e = block_e_ref[i]
    changed = jnp.logical_or(i == 0, e != block_e_ref[jnp.maximum(i - 1, 0)])

    def fetch(expert):
        return [pltpu.make_async_copy(w_ref.at[expert], stage, sem.at[n])
                for n, (w_ref, stage) in enumerate(((wg_ref, wgs), (wu_ref, wus), (wd_ref, wds)))]

    @pl.when(i == 0)
    def _():
        for cp in fetch(e):
            cp.start()

    @pl.when(jnp.logical_and(live, changed))
    def _():
        for cp in fetch(e):
            cp.wait()
        def cast(c, carry):
            rows = pl.ds(pl.multiple_of(c * LANES, LANES), LANES)
            for src, dst in ((wgs, wgb), (wus, wub), (wds, wdb)):
                dst[rows, :] = src[rows, :].astype(BF16)
            return carry

        lax.fori_loop(0, wgs.shape[0] // LANES, cast, 0)

        @pl.when(next_e_ref[i] != e)
        def _():
            for cp in fetch(next_e_ref[i]):
                cp.start()

    @pl.when(live)
    def _():
        x = _load_row_tiles(x_ref).astype(BF16)
        gate = jnp.minimum(_dot(x, wgb[...]) + bg_ref[...], SWIGLU_LIMIT)
        up = jnp.clip(_dot(x, wub[...]) + bu_ref[...], -SWIGLU_LIMIT, SWIGLU_LIMIT)
        hmid = (up + 1.0) * gate * _sigmoid(SWIGLU_ALPHA * gate)
        _store_row_tiles(y_ref, _dot(hmid.astype(BF16), wdb[...]) + bd_ref[...])

    @pl.when(jnp.logical_not(live))
    def _():
        y_ref[...] = jnp.zeros(y_ref.shape, y_ref.dtype)


def _moe(block_e, next_e, n_live, xs, wg, wu, wd, bg, bu, bd):
    d, f = wg.shape[-2:]
    n_blocks = block_e.shape[0]
    rows_in = pl.BlockSpec((MOE_BM * SUBLANES, LANES),
                           lambda i, be, ne, nl: (jnp.minimum(i, nl[0] - 1), 0))
    rows_out = pl.BlockSpec((MOE_BM * SUBLANES, LANES), lambda i, be, ne, nl: (i, 0))
    hbm = pl.BlockSpec(memory_space=pl.ANY)

    def bspec(n):
        return pl.BlockSpec((None, 1, n), lambda i, be, ne, nl: (be[i], 0, 0))

    est = 3 * d * f * 4 + 3 * d * f * 2 + 4 * MOE_BM * d * 4 + 4 * MOE_BM * f * 4
    return pl.pallas_call(
        _moe_kernel,
        out_shape=jax.ShapeDtypeStruct(xs.shape, F32),
        grid_spec=pltpu.PrefetchScalarGridSpec(
            num_scalar_prefetch=3,
            grid=(n_blocks,),
            in_specs=[rows_in, hbm, hbm, hbm, bspec(f), bspec(f), bspec(d)],
            out_specs=rows_out,
            scratch_shapes=[pltpu.VMEM((d, f), F32), pltpu.VMEM((d, f), F32), pltpu.VMEM((f, d), F32),
                            pltpu.VMEM((d, f), BF16), pltpu.VMEM((d, f), BF16), pltpu.VMEM((f, d), BF16),
                            pltpu.SemaphoreType.DMA((3,))]),
        compiler_params=pltpu.CompilerParams(
            dimension_semantics=("arbitrary",),
            vmem_limit_bytes=_vmem_limit(est)),
        name="moe",
    )(block_e, next_e, n_live, xs, wg, wu, wd, bg, bu, bd)


def _moe_layout(counts, n_rows):
    n_blocks = n_rows // MOE_BM + N_EXPERTS
    padded = (counts + MOE_BM - 1) // MOE_BM * MOE_BM
    ends = jnp.cumsum(padded)
    starts = ends - padded
    block_end = ends // MOE_BM
    i = jnp.arange(n_blocks, dtype=I32)
    ids = jnp.arange(N_EXPERTS, dtype=I32)
    last_e = jnp.max(jnp.where(counts > 0, ids, 0))
    block_e = jnp.minimum(jnp.sum(block_end[None, :] <= i[:, None], axis=1), last_e).astype(I32)
    later = jnp.logical_and(ids[None, :] > ids[:, None], counts[None, :] > 0)
    following = jnp.min(jnp.where(later, ids[None, :], N_EXPERTS), axis=1)
    following = jnp.where(following == N_EXPERTS, ids, following)
    next_e = jnp.sum(jnp.where(block_e[:, None] == ids[None, :], following[None, :], 0), axis=1)
    return (starts.astype(I32), (starts + counts).astype(I32), (padded - counts).astype(I32),
            block_e, next_e.astype(I32), block_end[-1:].astype(I32), n_blocks)


def _combine_kernel(*refs, tiles_per_batch):
    lpos_refs, tw_refs = refs[:TOP_K], refs[TOP_K:2 * TOP_K]
    (seg_start_ref, seg_cnt_ref, next_start_ref, next_cnt_ref, ys_ref, x1_ref, mod_ref, gpost_ref,
     o_ref, stage, acc, sem) = refs[2 * TOP_K:]
    tm, d = x1_ref.shape
    i = pl.program_id(0)
    slot = i % 2

    def fetch(start_ref, cnt_ref, s):
        _segment_copies(start_ref, cnt_ref, tm,
                        lambda off, size: stage.at[s, pl.ds(off, size), :],
                        lambda off, size: ys_ref.at[pl.ds(off, size), :], sem.at[s], to_sorted=False)

    @pl.when(i == 0)
    def _():
        fetch(seg_start_ref, seg_cnt_ref, 0)

    @pl.when(i + 1 < pl.num_programs(0))
    def _():
        fetch(next_start_ref, next_cnt_ref, 1 - slot)

    pltpu.make_async_copy(ys_ref.at[pl.ds(0, TOP_K * tm * SUBLANES), :], stage.at[slot],
                          sem.at[slot]).wait()

    def reduce(r, carry):
        row = tw_refs[0][r] * stage[slot, _tile_rows(lpos_refs[0][r]), :]
        for jj in range(1, TOP_K):
            row = row + tw_refs[jj][r] * stage[slot, _tile_rows(lpos_refs[jj][r]), :]
        acc[_tile_rows(r * SUBLANES), :] = row
        return carry

    lax.fori_loop(0, tm, reduce, 0, unroll=REDUCE_UNROLL)
    y = _load_row_tiles(acc)
    r = i // tiles_per_batch
    gt2 = mod_ref[pl.ds(r, 1), pl.ds(5 * d, d)]
    o_ref[...] = x1_ref[...] + gt2 * _rms(y, gpost_ref[...])


def _combine(lpos, tws, seg_start, seg_cnt, ys, x1, mod, gpost, tokens_per_batch):
    t, d = x1.shape
    tm = min(TOKEN_TILE, tokens_per_batch)
    n_tiles = t // tm
    kern = functools.partial(_combine_kernel, tiles_per_batch=tokens_per_batch // tm)
    est = (2 * TOP_K + 1) * tm * d * 4 + 2 * 2 * tm * d * 4 + 3 * tm * d * 4
    this_tile = pl.BlockSpec((LANES,), lambda i: (i,), memory_space=pltpu.SMEM)
    next_tile = pl.BlockSpec((LANES,), lambda i: (jnp.minimum(i + 1, n_tiles - 1),),
                             memory_space=pltpu.SMEM)
    return pl.pallas_call(
        kern,
        out_shape=jax.ShapeDtypeStruct((t, d), F32),
        grid=(n_tiles,),
        in_specs=_slot_specs(tm, n_tiles, lambda i: i) + _slot_specs(tm, n_tiles, lambda i: i) + [
                  this_tile, this_tile, next_tile, next_tile,
                  pl.BlockSpec(memory_space=pl.ANY),
                  pl.BlockSpec((tm, d), lambda i: (i, 0)),
                  pl.BlockSpec(mod.shape, lambda i: (0, 0)),
                  pl.BlockSpec((1, d), lambda i: (0, 0))],
        out_specs=pl.BlockSpec((tm, d), lambda i: (i, 0)),
        scratch_shapes=[pltpu.VMEM((2, TOP_K * tm * SUBLANES, LANES), F32),
                        pltpu.VMEM((tm * SUBLANES, LANES), F32),
                        pltpu.SemaphoreType.DMA((2,))],
        compiler_params=pltpu.CompilerParams(
            dimension_semantics=("arbitrary",),
            vmem_limit_bytes=_vmem_limit(est)),
        name="combine",
    )(*([lpos] * TOP_K), *([tws] * TOP_K), seg_start, seg_cnt, seg_start, seg_cnt, ys, x1, mod, gpost)


def kernel(x, c, ctx, c_ctx, w_ada, b_ada, g_pre_mix, g_post_mix, g_pre_ffn, g_post_ffn,
           w_in, w_out, hgrn_lb_logits, hgrn_norm, pool_w, pool_scale,
           w_router, b_router, w_gate, b_gate, w_up, b_up, w_down, b_down):
    b, s, d = x.shape
    layer = 0
    n_e = w_router.shape[-1]

    cc = jnp.concatenate([c, c_ctx[None, :]], axis=0)
    cc = jnp.pad(cc, ((0, -(b + 1) % SUBLANES), (0, 0)))
    mod = _ada(cc, w_ada[layer], b_ada[layer][None, :])

    w_in_b = w_in[layer].astype(BF16)
    lbl = hgrn_lb_logits[:2].reshape(4, HG_WIDTH)
    gn = hgrn_norm[layer][None, :]
    gpre = g_pre_mix[layer][None, :]

    lff_c, kf_c, lfb_c, kb_c, v_c = _inproj(ctx, mod, gpre, w_in_b, lbl, gn, mod_row=b, full=False)
    zeros = jnp.zeros((b, HG_HEADS, HEAD_DIM, HEAD_DIM), F32)
    (s_f,) = _hgrn(None, kf_c, v_c, lff_c, zeros, rev=False, need_o=False, want_state=True,
                   name="hgrn_ctx_fwd")
    (s_b,) = _hgrn(None, kb_c, v_c, lfb_c, zeros, rev=True, need_o=False, want_state=True,
                   name="hgrn_ctx_bwd")

    q, lff, kf, lfb, kb, v, gate, u = _inproj(x, mod, gpre, w_in_b, lbl, gn, mod_row=None, full=True)
    (o_f,) = _hgrn(q, kf, v, lff, s_f, rev=False, need_o=True, want_state=False, name="hgrn_fwd")
    (o_hg,) = _hgrn(q, kb, v, lfb, s_b, o_f, gate, rev=True, need_o=True, want_state=False,
                    name="hgrn_bwd")
    o_pool = _pool(u, pool_w[layer].astype(BF16), pool_scale[layer].reshape(len(POOL_WINDOWS), 1, POOL_GROUP))

    wr = jnp.pad(w_router[layer], ((0, 0), (0, LANES - n_e))).astype(BF16)
    br = jnp.pad(b_router[layer], (0, LANES - n_e))[None, :]
    x1, h2, ti, tw, chosen, cnt = _outproj(o_hg, o_pool, x, w_out[layer].astype(BF16), mod,
                                           g_post_mix[layer][None, :], g_pre_ffn[layer][None, :], wr, br)

    t = b * s
    starts, gap_start, gap_len, block_e, next_e, n_live, n_blocks = _moe_layout(
        cnt[0, :n_e].astype(I32), t * TOP_K)
    starts = jnp.pad(starts.astype(F32), (0, LANES - n_e))[None, :]
    lpos, tws, seg_start, seg_cnt = _rank(ti.reshape(t, LANES), chosen.reshape(t, LANES),
                                          tw.reshape(t, LANES), starts)
    lpos, tws = lpos.reshape(TOP_K * t), tws.reshape(TOP_K * t)
    seg_start, seg_cnt = seg_start.reshape(-1), seg_cnt.reshape(-1)

    xs = _dispatch(lpos, seg_start, seg_cnt, gap_start, gap_len, n_live,
                   h2.reshape(t * d // LANES, LANES), n_blocks * MOE_BM)
    ys = _moe(block_e, next_e, n_live, xs, w_gate[layer], w_up[layer], w_down[layer],
              b_gate[layer][:, None, :], b_up[layer][:, None, :], b_down[layer][:, None, :])
    out = _combine(lpos, tws, seg_start, seg_cnt, ys, x1.reshape(t, d), mod,
                   g_post_ffn[layer][None, :], s)
    return out.reshape(b, s, d)
```

```python
import functools
import math

import numpy as np
import jax
import jax.numpy as jnp
from jax import lax
from jax.experimental import pallas as pl
from jax.experimental.pallas import tpu as pltpu

F32 = jnp.float32
BF16 = jnp.bfloat16
I32 = jnp.int32

GRID_W = 64
HG_HEADS = 4
HEAD_DIM = 128
HG_WIDTH = HG_HEADS * HEAD_DIM
POOL_WINDOWS = (2, 4, 8, 16)
POOL_GROUP = 128
N_EXPERTS = 32
TOP_K = 4
SWIGLU_LIMIT = 7.0
SWIGLU_ALPHA = 1.702
EPS = 1e-6

LANES = 128
SUBLANES = 8
V7X_VMEM_BYTES = 64 * 1024 * 1024

TOKEN_TILE = 512
HGRN_BLOCK = 1024
ADA_COLS = 1536
PLACE_UNROLL = 8
REDUCE_UNROLL = 16
VMEM_COMPILER_BYTES = 4 << 20
VMEM_RESERVED_BYTES = 6 << 20
CHUNK = 64
HGRN_GROUP = 4
LEVELS = (32, 16, 8, 4, 2, 1)
FINE_LEVELS = (2, 1)
MOE_BM = 256
OUTPROJ_SLICES = 4
INPROJ_SLICES = 2
LOG2_E = math.log2(math.e)


def _vmem_limit(nbytes):
    return int(min(nbytes * 3 // 2 + VMEM_COMPILER_BYTES, V7X_VMEM_BYTES - VMEM_RESERVED_BYTES))


def _log2(n):
    assert n & (n - 1) == 0
    return n.bit_length() - 1


def _sigmoid(x):
    return 1.0 / (1.0 + jnp.exp(-x))


def _rms(x, gain):
    return x * lax.rsqrt(jnp.mean(x * x, axis=-1, keepdims=True) + EPS) * gain


def _dot(a, b):
    return jnp.dot(a, b, preferred_element_type=F32)


def _store_row_tiles(ref, val):
    rows = val.shape[0]
    for c in range(SUBLANES):
        ref[pl.ds(c, rows, stride=SUBLANES), :] = val[:, c * LANES:(c + 1) * LANES]


def _load_row_tiles(ref):
    rows = ref.shape[0] // SUBLANES
    return jnp.concatenate([ref[pl.ds(c, rows, stride=SUBLANES), :] for c in range(SUBLANES)], axis=1)


def _dot_nt(a, b):
    return lax.dot_general(a, b, (((1,), (1,)), ((), ())), preferred_element_type=F32)


def _ada_kernel(c_ref, w_ref, b_ref, o_ref):
    c = c_ref[...]
    s = (c * _sigmoid(c)).astype(BF16)
    o_ref[...] = _dot(s, w_ref[...].astype(BF16)) + b_ref[...]


def _ada(cc, w, b):
    rows, d = cc.shape
    n = w.shape[1]
    tn = ADA_COLS if n % ADA_COLS == 0 else n
    return pl.pallas_call(
        _ada_kernel,
        out_shape=jax.ShapeDtypeStruct((rows, n), F32),
        grid=(n // tn,),
        in_specs=[pl.BlockSpec((rows, d), lambda j: (0, 0)),
                  pl.BlockSpec((d, tn), lambda j: (0, j)),
                  pl.BlockSpec((1, tn), lambda j: (0, j))],
        out_specs=pl.BlockSpec((rows, tn), lambda j: (0, j)),
        compiler_params=pltpu.CompilerParams(
            dimension_semantics=("arbitrary",),
            vmem_limit_bytes=_vmem_limit(2 * d * tn * 4 + d * tn * 2)),
        name="ada",
    )(cc, w, b)


def _inproj_kernel(x_ref, mod_ref, gpre_ref, w_ref, lbl_ref, gn_ref, *outs, mod_row, full):
    d = x_ref.shape[-1]
    r = pl.program_id(0) if mod_row is None else mod_row
    sh = mod_ref[pl.ds(r, 1), pl.ds(0, d)]
    sc = mod_ref[pl.ds(r, 1), pl.ds(d, d)]
    tm = x_ref.shape[1]
    n_slices = INPROJ_SLICES if tm % (INPROJ_SLICES * SUBLANES) == 0 else 1
    sub = tm // n_slices
    slices = [pl.ds(i * sub, sub) for i in range(n_slices)]
    hs = [(_rms(x_ref[0, rows, :], gpre_ref[...]) * (1.0 + sc) + sh).astype(BF16) for rows in slices]

    def proj(g):
        return [_dot(h, w_ref[:, g * HG_WIDTH:(g + 1) * HG_WIDTH]) for h in hs]

    def put(ref, vals):
        for rows, val in zip(slices, vals):
            for hh in range(HG_HEADS):
                ref[0, hh, rows, :] = val[:, hh * HEAD_DIM:(hh + 1) * HEAD_DIM]

    def lower_bound(direction):
        l0 = lbl_ref[pl.ds(direction, 1), :]
        l1 = lbl_ref[pl.ds(2 + direction, 1), :]
        m = jnp.maximum(l0, l1)
        e0 = jnp.exp(l0 - m)
        return e0 / (e0 + jnp.exp(l1 - m))

    if full:
        q_o, lff_o, kf_o, lfb_o, kb_o, v_o, gate_o, u_o = outs
        put(q_o, [q * _sigmoid(q) for q in proj(0)])
    else:
        lff_o, kf_o, lfb_o, kb_o, v_o = outs
    for direction, (lf_o, k_o) in enumerate(((lff_o, kf_o), (lfb_o, kb_o))):
        lb = lower_bound(direction)
        sgs = [_sigmoid(z) for z in proj(1 + direction)]
        put(lf_o, [jnp.log(lb + (1.0 - lb) * sg) for sg in sgs])
        put(k_o, [(1.0 - lb) * (1.0 - sg) for sg in sgs])
    put(v_o, proj(3))
    if full:
        put(gate_o, [gn_ref[...] * _sigmoid(z) for z in proj(4)])
        put(u_o, proj(5))


def _inproj(x, mod, gpre, w_in, lbl, gn, *, mod_row, full):
    b, t, d = x.shape
    tm = min(TOKEN_TILE, t)
    n_out = 8 if full else 5
    n_cols = w_in.shape[1]
    head = pl.BlockSpec((1, HG_HEADS, tm, HEAD_DIM), lambda i, j: (i, 0, j, 0))
    kern = functools.partial(_inproj_kernel, mod_row=mod_row, full=full)
    est = 2 * tm * d * 4 + 2 * d * n_cols * 2 + n_out * 2 * tm * HG_WIDTH * 4 + 4 * tm * HG_WIDTH * 4
    return pl.pallas_call(
        kern,
        out_shape=[jax.ShapeDtypeStruct((b, HG_HEADS, t, HEAD_DIM), F32)] * n_out,
        grid=(b, t // tm),
        in_specs=[pl.BlockSpec((1, tm, d), lambda i, j: (i, j, 0)),
                  pl.BlockSpec(mod.shape, lambda i, j: (0, 0)),
                  pl.BlockSpec((1, d), lambda i, j: (0, 0)),
                  pl.BlockSpec(w_in.shape, lambda i, j: (0, 0)),
                  pl.BlockSpec(lbl.shape, lambda i, j: (0, 0)),
                  pl.BlockSpec((1, HG_WIDTH), lambda i, j: (0, 0))],
        out_specs=[head] * n_out,
        compiler_params=pltpu.CompilerParams(
            dimension_semantics=("arbitrary", "arbitrary"),
            vmem_limit_bytes=_vmem_limit(est)),
        name="inproj_full" if full else "inproj_ctx",
    )(x, mod, gpre, w_in, lbl, gn)


def _hgrn_tables(rev):
    row = np.arange(CHUNK)[:, None]
    col = np.arange(CHUNK)[None, :]
    tri = ((col >= row) if rev else (col <= row)).astype(np.float32)
    blocks, masks = [tri], []
    for h in LEVELS:
        if h in FINE_LEVELS:
            blocks.append(tri - tri[_ref_row(np.arange(CHUNK), h, rev)])
        is_q = ((row // h) % 2) == (0 if rev else 1)
        key_half = ((col // h) % 2) == (1 if rev else 0)
        masks.append(((row // (2 * h)) == (col // (2 * h))) & is_q & key_half)
    pair = np.stack([np.concatenate(masks[i:i + 2], axis=1) for i in range(0, len(LEVELS), 2)])
    cums = np.concatenate(blocks, axis=0)
    return jnp.asarray(np.concatenate([cums, cums], axis=1), BF16), jnp.asarray(pair, F32)


def _ref_row(t, h, rev):
    return (t // (2 * h)) * (2 * h) + (h if rev else h - 1)


def _hgrn_chunks(qs, ks, vs, lfs, sts, tabs, rev, need_o):
    cums_ref, pair_ref = tabs
    n_heads = len(sts)
    heads = range(len(ks))
    n_pairs = len(LEVELS) // 2
    n_cum = CHUNK * (1 + len(FINE_LEVELS)) if need_o else CHUNK
    cums = cums_ref[0:n_cum, :]

    his, los = [], []
    for h in heads:
        lf2 = lfs[h] * LOG2_E
        hi = lf2.astype(BF16)
        his.append(hi)
        los.append((lf2 - hi.astype(F32)).astype(BF16))
    both = _dot(cums, jnp.concatenate([jnp.concatenate(his, axis=1), jnp.concatenate(los, axis=1)], axis=0))
    bd = [both[:, h * LANES:(h + 1) * LANES] for h in heads]

    k16 = [ks[h].astype(BF16) for h in heads]
    v16 = [vs[h].astype(BF16) for h in heads]
    b = [bd[h][0:CHUNK] for h in heads]

    def level_diff(h, level):
        if level in FINE_LEVELS:
            i = FINE_LEVELS.index(level)
            return bd[h][CHUNK * (1 + i):CHUNK * (2 + i)]
        refs = [_ref_row(m * 2 * level, level, rev) for m in range(CHUNK // (2 * level))]
        pieces = [jnp.broadcast_to(b[h][r:r + 1, :], (2 * level, LANES)) for r in refs]
        return b[h] - (pieces[0] if len(pieces) == 1 else jnp.concatenate(pieces, axis=0))
    edge = [b[h][0:1, :] if rev else b[h][CHUNK - 1:CHUNK, :] for h in heads]
    kd = [k16[h] * jnp.exp2(edge[h] - b[h]).astype(BF16) for h in heads]
    grow = [lax.dot_general(v16[h], kd[h], (((0,), (0,)), ((), ())), preferred_element_type=F32)
            for h in heads]
    q16 = [qs[h].astype(BF16) for h in heads] if need_o else None
    st = list(sts)
    carried = []
    for h in heads:
        if need_o:
            carried.append(_dot_nt(q16[h] * jnp.exp2(b[h]).astype(BF16), st[h % n_heads].astype(BF16)))
        st[h % n_heads] = st[h % n_heads] * jnp.exp2(edge[h]) + grow[h]
    if not need_o:
        return None, st

    zero = jnp.zeros((CHUNK, LANES), BF16)
    att = [[] for _ in heads]
    for p in range(n_pairs):
        lhs, rhs = [], []
        for h in heads:
            qe, ke = [], []
            for i in (2 * p, 2 * p + 1):
                e = jnp.exp2(-jnp.abs(level_diff(h, LEVELS[i]))).astype(BF16)
                qe.append(q16[h] * e)
                ke.append(k16[h] * e)
            lhs.append(jnp.concatenate(qe, axis=1))
            rhs.append(jnp.concatenate([jnp.concatenate([ke[0], zero], axis=1),
                                        jnp.concatenate([zero, ke[1]], axis=1)], axis=0))
        scores = [_dot_nt(lhs[h], rhs[h]) for h in heads]
        for h in heads:
            att[h].append((scores[h] * pair_ref[p]).astype(BF16))
    o = []
    for h in heads:
        vals = jnp.concatenate([v16[h]] * (2 * n_pairs), axis=0)
        same_row = jnp.sum(qs[h] * ks[h], axis=1, keepdims=True) * vs[h]
        o.append(carried[h] + _dot(jnp.concatenate(att[h], axis=1), vals) + same_row)
    return o, st


def _hgrn_kernel(*refs, rev, need_o, final, want_state, n_chunks):
    it = iter(refs)
    tabs = tuple(next(it) for _ in range(2))
    q_ref = next(it) if need_o else None
    k_ref, v_ref, lf_ref, s0_ref = next(it), next(it), next(it), next(it)
    prev_ref = next(it) if final else None
    gate_ref = next(it) if final else None
    o_ref = next(it) if need_o else None
    sout_ref = next(it) if want_state else None
    st_ref = next(it)

    j = pl.program_id(1)
    nb = pl.num_programs(1)

    @pl.when(j == 0)
    def _():
        st_ref[...] = s0_ref[0]

    group = HGRN_GROUP if n_chunks % HGRN_GROUP == 0 else 1

    def body(gi, carry):
        rows = []
        for s in range(group):
            c = gi * group + s
            c = (n_chunks - 1 - c) if rev else c
            rows.append(pl.ds(pl.multiple_of(c * CHUNK, CHUNK), CHUNK))
        pairs = [(r, hh) for r in rows for hh in range(HG_HEADS)]
        o, st_new = _hgrn_chunks([q_ref[0, hh, r, :] for r, hh in pairs] if need_o else None,
                                 [k_ref[0, hh, r, :] for r, hh in pairs],
                                 [v_ref[0, hh, r, :] for r, hh in pairs],
                                 [lf_ref[0, hh, r, :] for r, hh in pairs],
                                 [st_ref[hh] for hh in range(HG_HEADS)], tabs, rev, need_o)
        for hh in range(HG_HEADS):
            st_ref[hh] = st_new[hh]
        if need_o:
            for (r, hh), out in zip(pairs, o):
                if final:
                    tot = out + prev_ref[0, hh, r, :]
                    out = tot * lax.rsqrt(jnp.mean(tot * tot, axis=-1, keepdims=True) + EPS)
                    out = out * gate_ref[0, hh, r, :]
                o_ref[0, hh, r, :] = out
        return carry

    lax.fori_loop(0, n_chunks // group, body, 0)

    if want_state:
        @pl.when(j == nb - 1)
        def _():
            sout_ref[0] = st_ref[...]


def _hgrn(q, k, v, lf, s0, prev=None, gate=None, *, rev, need_o, want_state, name):
    b, hh, t, _ = k.shape
    tb = min(HGRN_BLOCK, t)
    nb = t // tb
    final = prev is not None
    tabs = _hgrn_tables(rev)

    seq = pl.BlockSpec((1, hh, tb, HEAD_DIM), lambda i, j: (i, 0, (nb - 1 - j) if rev else j, 0))
    state = pl.BlockSpec((1, hh, HEAD_DIM, HEAD_DIM), lambda i, j: (i, 0, 0, 0))
    args = list(tabs)
    in_specs = [pl.BlockSpec(a.shape, functools.partial(lambda nd, i, j: (0,) * nd, a.ndim)) for a in tabs]
    for a in ((q,) if need_o else ()) + (k, v, lf):
        args.append(a)
        in_specs.append(seq)
    args.append(s0)
    in_specs.append(state)
    if final:
        args += [prev, gate]
        in_specs += [seq, seq]
    out_shape, out_specs = [], []
    if need_o:
        out_shape.append(jax.ShapeDtypeStruct((b, hh, t, HEAD_DIM), F32))
        out_specs.append(seq)
    if want_state:
        out_shape.append(jax.ShapeDtypeStruct((b, hh, HEAD_DIM, HEAD_DIM), F32))
        out_specs.append(state)
    kern = functools.partial(_hgrn_kernel, rev=rev, need_o=need_o, final=final,
                             want_state=want_state, n_chunks=tb // CHUNK)
    est = 2 * (len(args) + len(out_shape)) * hh * tb * HEAD_DIM * 4
    return pl.pallas_call(
        kern,
        out_shape=out_shape,
        grid=(b, nb),
        in_specs=in_specs,
        out_specs=out_specs,
        scratch_shapes=[pltpu.VMEM((hh, HEAD_DIM, HEAD_DIM), F32)],
        compiler_params=pltpu.CompilerParams(
            dimension_semantics=("arbitrary", "arbitrary"),
            vmem_limit_bytes=_vmem_limit(est)),
        name=name,
    )(*args)


def _window_sum(x, w, stride, pos, extent):
    n = x.shape[0]
    whole_tiles = stride % SUBLANES == 0 and stride * extent == n

    def ahead(y, dist):
        if whole_tiles:
            k = min(dist, extent) * stride
            return jnp.concatenate([y[k:], jnp.zeros((k, y.shape[1]), y.dtype)], axis=0)
        return jnp.where(pos + dist < extent, pltpu.roll(y, (n - dist * stride) % n, 0), 0.0)

    def behind(y, dist):
        if whole_tiles:
            k = min(dist, extent) * stride
            return jnp.concatenate([jnp.zeros((k, y.shape[1]), y.dtype), y[:n - k]], axis=0)
        return jnp.where(pos >= dist, pltpu.roll(y, dist * stride, 0), 0.0)

    fwd, bwd, h = x, x, 1
    while h < w // 2:
        fwd = fwd + ahead(fwd, h)
        bwd = bwd + behind(bwd, h)
        h *= 2
    return fwd + behind(bwd, 1)


def _window_count(pos, w, extent):
    return (jnp.minimum(pos + w // 2, extent) - jnp.maximum(pos - w // 2, 0)).astype(F32)


def _pool_kernel(u_ref, pw_ref, ps_ref, o_ref):
    g = pl.program_id(1)
    n = u_ref.shape[2]
    tok = lax.broadcasted_iota(I32, (n, LANES), 0)
    col = tok & (GRID_W - 1)
    row = tok >> _log2(GRID_W)
    for gi, w in enumerate(POOL_WINDOWS):
        @pl.when(g == gi)
        def _(w=w):
            u = u_ref[0, 0]
            rows = n // GRID_W
            s = _window_sum(_window_sum(u, w, GRID_W, row, rows), w, 1, col, GRID_W)
            m = s / (_window_count(row, w, rows) * _window_count(col, w, GRID_W))
            o_ref[0, 0] = _dot((m - u).astype(BF16), pw_ref[0]) * ps_ref[0]


def _pool(u, pw, ps):
    b, g, t, c = u.shape
    blk = pl.BlockSpec((1, 1, t, c), lambda i, j: (i, j, 0, 0))
    return pl.pallas_call(
        _pool_kernel,
        out_shape=jax.ShapeDtypeStruct(u.shape, F32),
        grid=(b, g),
        in_specs=[blk,
                  pl.BlockSpec((1, c, c), lambda i, j: (j, 0, 0)),
                  pl.BlockSpec((1, 1, c), lambda i, j: (j, 0, 0))],
        out_specs=blk,
        compiler_params=pltpu.CompilerParams(
            dimension_semantics=("arbitrary", "arbitrary"),
            vmem_limit_bytes=_vmem_limit(12 * t * c * 4)),
        name="pool",
    )(u, pw, ps)


def _outproj_kernel(hg_ref, pool_ref, x_ref, w_ref, mod_ref, gpost_ref, gpre_ref, wr_ref, br_ref,
                    x1_ref, h2_ref, lpos_ref, tws_ref, seg_rel_ref, seg_cnt_ref, cnt_ref):
    d = x_ref.shape[-1]
    r = pl.program_id(0)

    @pl.when(jnp.logical_and(r == 0, pl.program_id(1) == 0))
    def _():
        cnt_ref[...] = jnp.zeros(cnt_ref.shape, F32)

    gt1 = mod_ref[pl.ds(r, 1), pl.ds(2 * d, d)]
    sh2 = mod_ref[pl.ds(r, 1), pl.ds(3 * d, d)]
    sc2 = mod_ref[pl.ds(r, 1), pl.ds(4 * d, d)]
    tm = x_ref.shape[1]
    sub = tm // OUTPROJ_SLICES
    slices = [pl.ds(i * sub, sub) for i in range(OUTPROJ_SLICES)]
    ys = []
    for rows in slices:
        cat = jnp.concatenate([hg_ref[0, hh, rows, :] for hh in range(HG_HEADS)]
                              + [pool_ref[0, g, rows, :] for g in range(len(POOL_WINDOWS))], axis=1)
        ys.append(_dot(cat.astype(BF16), w_ref[...]))
    logits = []
    for i, rows in enumerate(slices):
        x1 = x_ref[0, rows, :] + gt1 * _rms(ys[i], gpost_ref[...])
        x1_ref[0, rows, :] = x1
        h2 = _rms(x1, gpre_ref[...]) * (1.0 + sc2) + sh2
        _store_row_tiles(h2_ref.at[0, pl.ds(i * sub * SUBLANES, sub * SUBLANES), :], h2)
        logits.append(_dot(h2.astype(BF16), wr_ref[...]) + br_ref[...])

    lane = lax.broadcasted_iota(I32, (sub, LANES), 1).astype(F32)
    tis, tws, chosens = [], [], []
    for i, rows in enumerate(slices):
        lg = jnp.where(lane < N_EXPERTS, logits[i], -jnp.inf)
        ti = jnp.zeros(lg.shape, F32)
        tw = jnp.zeros(lg.shape, F32)
        chosen = jnp.zeros(lg.shape, F32)
        top = None
        for jj in range(TOP_K):
            m = jnp.max(lg, axis=1, keepdims=True)
            idx = jnp.min(jnp.where(lg == m, lane, float(LANES)), axis=1, keepdims=True)
            lg = jnp.where(lane == idx, -jnp.inf, lg)
            chosen = jnp.where(lane == idx, 1.0, chosen)
            top = m if top is None else top
            ti = jnp.where(lane == jj, idx, ti)
            tw = jnp.where(lane == jj, jnp.exp(m - top), tw)
        tis.append(ti.astype(I32))
        tws.append(tw / jnp.sum(tw, axis=1, keepdims=True))
        chosens.append(chosen)

    ti_all, tw_all, chosen = (jnp.concatenate(z, axis=0) for z in (tis, tws, chosens))
    row = lax.broadcasted_iota(I32, (tm, tm), 0)
    col = lax.broadcasted_iota(I32, (tm, tm), 1)
    before = jnp.where(col < row, 1.0, 0.0).astype(BF16)
    local = _dot(before, chosen.astype(BF16))
    seg_cnt = jnp.sum(chosen, axis=0, keepdims=True)
    lane8 = lax.broadcasted_iota(I32, (SUBLANES, LANES), 1)
    run = jnp.broadcast_to(seg_cnt, (SUBLANES, LANES))
    shift = 1
    while shift < LANES:
        run = run + jnp.where(lane8 >= shift, pltpu.roll(run, shift, 1), 0.0)
        shift *= 2
    seg_off = run[0:1, :] - seg_cnt
    lpos = jnp.take_along_axis(local + seg_off, ti_all, axis=1)
    lpos_ref[...] = (lpos.T[:TOP_K, :] * SUBLANES).astype(I32)
    tws_ref[...] = tw_all.T[:TOP_K, :]
    seg_rel_ref[0] = cnt_ref[...].astype(I32)
    seg_cnt_ref[0] = seg_cnt.astype(I32)
    cnt_ref[...] += seg_cnt


def _outproj(hg, pool, x, w_out, mod, gpost, gpre, wr, br):
    b, t, d = x.shape
    tm = min(TOKEN_TILE, t)
    head = pl.BlockSpec((1, HG_HEADS, tm, HEAD_DIM), lambda i, j: (i, 0, j, 0))
    tok = pl.BlockSpec((1, tm, d), lambda i, j: (i, j, 0))
    n_t = t // tm
    slots = pl.BlockSpec((TOP_K, tm), lambda i, j: (0, i * n_t + j))
    per_tile = pl.BlockSpec((1, 1, LANES), lambda i, j: (i * n_t + j, 0, 0))
    vec = pl.BlockSpec((1, d), lambda i, j: (0, 0))
    est = 2 * (2 * tm * HG_WIDTH * 4 + 3 * tm * d * 4 + 2 * tm * LANES * 4) + 2 * d * d * 2 + 6 * tm * d * 4
    return pl.pallas_call(
        _outproj_kernel,
        out_shape=[jax.ShapeDtypeStruct((b, t, d), F32),
                   jax.ShapeDtypeStruct((b, t * d // LANES, LANES), F32),
                   jax.ShapeDtypeStruct((TOP_K, b * t), I32), jax.ShapeDtypeStruct((TOP_K, b * t), F32),
                   jax.ShapeDtypeStruct((b * n_t, 1, LANES), I32),
                   jax.ShapeDtypeStruct((b * n_t, 1, LANES), I32),
                   jax.ShapeDtypeStruct((1, LANES), F32)],
        grid=(b, t // tm),
        in_specs=[head, head, tok,
                  pl.BlockSpec(w_out.shape, lambda i, j: (0, 0)),
                  pl.BlockSpec(mod.shape, lambda i, j: (0, 0)),
                  vec, vec,
                  pl.BlockSpec(wr.shape, lambda i, j: (0, 0)),
                  pl.BlockSpec((1, LANES), lambda i, j: (0, 0))],
        out_specs=[tok, pl.BlockSpec((1, tm * d // LANES, LANES), lambda i, j: (i, j, 0)),
                   slots, slots, per_tile, per_tile, pl.BlockSpec((1, LANES), lambda i, j: (0, 0))],
        compiler_params=pltpu.CompilerParams(
            dimension_semantics=("arbitrary", "arbitrary"),
            vmem_limit_bytes=_vmem_limit(est)),
        name="outproj",
    )(hg, pool, x, w_out, mod, gpost, gpre, wr, br)


def _slot_specs(tm, n_tiles, tile_of_step):
    return [pl.BlockSpec((tm,), functools.partial(lambda jj, i: (jj * n_tiles + tile_of_step(i),), jj),
                         memory_space=pltpu.SMEM) for jj in range(TOP_K)]


def _tile_rows(start):
    return pl.ds(pl.multiple_of(start, SUBLANES), SUBLANES)


def _gap_copies(e, gap_start_ref, gap_len_ref, zeros, xs_ref, sem):
    off, n = gap_start_ref[e], gap_len_ref[e]
    out = []
    for bit in range(_log2(MOE_BM)):
        size = (1 << bit) * SUBLANES
        used = (n >> bit) & 1
        out.append((used == 1,
                    pltpu.make_async_copy(zeros.at[pl.ds(0, size), :],
                                          xs_ref.at[pl.ds(pl.multiple_of(off * SUBLANES, SUBLANES), size), :],
                                          sem)))
        off = off + used * (1 << bit)
    return out


def _dispatch_kernel(*refs):
    lpos_refs = refs[:TOP_K]
    (seg_start_ref, seg_cnt_ref, gap_start_ref, gap_len_ref, n_live_ref, h_ref, xs_ref,
     zeros, stage, sem, zsem) = refs[TOP_K:]
    tm = h_ref.shape[0] // SUBLANES
    block_rows = MOE_BM * SUBLANES
    n_blocks = xs_ref.shape[0] // block_rows

    @pl.when(pl.program_id(0) == 0)
    def _():
        zeros[...] = jnp.zeros(zeros.shape, zeros.dtype)

        def tail_copy(blk):
            return pltpu.make_async_copy(
                zeros, xs_ref.at[pl.ds(pl.multiple_of(blk * block_rows, block_rows), block_rows), :], zsem)

        def tail_start(blk, carry):
            tail_copy(blk).start()
            return carry

        def tail_wait(blk, carry):
            tail_copy(blk).wait()
            return carry

        lax.fori_loop(n_live_ref[0], n_blocks, tail_start, 0)
        lax.fori_loop(n_live_ref[0], n_blocks, tail_wait, 0)

        def fill(e, carry):
            for used, cp in _gap_copies(e, gap_start_ref, gap_len_ref, zeros, xs_ref, zsem):
                @pl.when(used)
                def _(cp=cp):
                    cp.start()
            return carry

        def drain(e, carry):
            for used, cp in _gap_copies(e, gap_start_ref, gap_len_ref, zeros, xs_ref, zsem):
                @pl.when(used)
                def _(cp=cp):
                    cp.wait()
            return carry

        lax.fori_loop(0, N_EXPERTS, fill, 0)
        lax.fori_loop(0, N_EXPERTS, drain, 0)

    i = pl.program_id(0)
    n_steps = pl.num_programs(0)
    slot = i % 2

    def drain(s):
        pltpu.make_async_copy(stage.at[s], xs_ref.at[pl.ds(0, TOP_K * tm * SUBLANES), :], sem.at[s]).wait()

    @pl.when(i >= 2)
    def _():
        drain(slot)

    def place(r, carry):
        row = h_ref[_tile_rows(r * SUBLANES), :]
        for jj in range(TOP_K):
            stage[slot, _tile_rows(lpos_refs[jj][r]), :] = row
        return carry

    lax.fori_loop(0, tm, place, 0, unroll=PLACE_UNROLL)
    _segment_copies(seg_start_ref, seg_cnt_ref, tm,
                    lambda off, size: stage.at[slot, pl.ds(off, size), :],
                    lambda off, size: xs_ref.at[pl.ds(off, size), :], sem.at[slot], to_sorted=True)

    @pl.when(i == n_steps - 1)
    def _():
        drain(slot)

        @pl.when(n_steps >= 2)
        def _():
            drain(1 - slot)


def _segment_copies(seg_start_ref, seg_cnt_ref, tm, staged, sorted_rows, sem, *, to_sorted):
    def segment(e, stage_off):
        n, base = seg_cnt_ref[e], seg_start_ref[e]
        done = jnp.int32(0)
        for bit in range(_log2(tm) + 1):
            size = (1 << bit) * SUBLANES
            used = (n >> bit) & 1
            a = staged(pl.multiple_of(stage_off + done, SUBLANES), size)
            b = sorted_rows(pl.multiple_of(base + done, SUBLANES), size)

            src, dst = (a, b) if to_sorted else (b, a)

            @pl.when(used == 1)
            def _(src=src, dst=dst):
                pltpu.make_async_copy(src, dst, sem).start()
            done = done + used * size
        return stage_off + n * SUBLANES

    lax.fori_loop(0, N_EXPERTS, segment, jnp.int32(0))


def _dispatch(lpos, seg_start, seg_cnt, gap_start, gap_len, n_live, h2, n_sorted):
    rows, l = h2.shape
    t = rows // SUBLANES
    tm = min(TOKEN_TILE, t)
    smem = pl.BlockSpec(memory_space=pltpu.SMEM)
    per_tile = pl.BlockSpec((LANES,), lambda i: (i,), memory_space=pltpu.SMEM)
    return pl.pallas_call(
        _dispatch_kernel,
        out_shape=jax.ShapeDtypeStruct((n_sorted * SUBLANES, l), h2.dtype),
        grid=(t // tm,),
        in_specs=_slot_specs(tm, t // tm, lambda i: i) + [
            per_tile, per_tile, smem, smem, smem, pl.BlockSpec((tm * SUBLANES, l), lambda i: (i, 0))],
        out_specs=pl.BlockSpec(memory_space=pl.ANY),
        scratch_shapes=[pltpu.VMEM((MOE_BM * SUBLANES, l), h2.dtype),
                        pltpu.VMEM((2, TOP_K * tm * SUBLANES, l), h2.dtype),
                        pltpu.SemaphoreType.DMA((2,)), pltpu.SemaphoreType.DMA],
        compiler_params=pltpu.CompilerParams(
            dimension_semantics=("arbitrary",),
            vmem_limit_bytes=_vmem_limit((2 * TOP_K + 2) * tm * SUBLANES * l * 4 + MOE_BM * SUBLANES * l * 4)),
        name="dispatch",
    )(*([lpos] * TOP_K), seg_start, seg_cnt, gap_start, gap_len, n_live, h2)


def _moe_kernel(block_e_ref, next_e_ref, n_live_ref,
                x_ref, wg_ref, wu_ref, wd_ref, bg_ref, bu_ref, bd_ref, y_ref,
                wgs, wus, wds, wgb, wub, wdb, sem):
    i = pl.program_id(0)
    live = i < n_live_ref[0]
    e = block_e_ref[i]
    changed = jnp.logical_or(i == 0, e != block_e_ref[jnp.maximum(i - 1, 0)])

    def fetch(expert):
        return [pltpu.make_async_copy(w_ref.at[expert], stage, sem.at[n])
                for n, (w_ref, stage) in enumerate(((wg_ref, wgs), (wu_ref, wus), (wd_ref, wds)))]

    @pl.when(i == 0)
    def _():
        for cp in fetch(e):
            cp.start()

    @pl.when(jnp.logical_and(live, changed))
    def _():
        for cp in fetch(e):
            cp.wait()
        def cast(c, carry):
            rows = pl.ds(pl.multiple_of(c * LANES, LANES), LANES)
            for src, dst in ((wgs, wgb), (wus, wub), (wds, wdb)):
                dst[rows, :] = src[rows, :].astype(BF16)
            return carry

        lax.fori_loop(0, wgs.shape[0] // LANES, cast, 0)

        @pl.when(next_e_ref[i] != e)
        def _():
            for cp in fetch(next_e_ref[i]):
                cp.start()

    @pl.when(live)
    def _():
        x = _load_row_tiles(x_ref).astype(BF16)
        gate = jnp.minimum(_dot(x, wgb[...]) + bg_ref[...], SWIGLU_LIMIT)
        up = jnp.clip(_dot(x, wub[...]) + bu_ref[...], -SWIGLU_LIMIT, SWIGLU_LIMIT)
        hmid = (up + 1.0) * gate * _sigmoid(SWIGLU_ALPHA * gate)
        _store_row_tiles(y_ref, _dot(hmid.astype(BF16), wdb[...]) + bd_ref[...])

    @pl.when(jnp.logical_not(live))
    def _():
        y_ref[...] = jnp.zeros(y_ref.shape, y_ref.dtype)


def _moe(block_e, next_e, n_live, xs, wg, wu, wd, bg, bu, bd):
    d, f = wg.shape[-2:]
    n_blocks = block_e.shape[0]
    rows_in = pl.BlockSpec((MOE_BM * SUBLANES, LANES),
                           lambda i, be, ne, nl: (jnp.minimum(i, nl[0] - 1), 0))
    rows_out = pl.BlockSpec((MOE_BM * SUBLANES, LANES), lambda i, be, ne, nl: (i, 0))
    hbm = pl.BlockSpec(memory_space=pl.ANY)

    def bspec(n):
        return pl.BlockSpec((None, 1, n), lambda i, be, ne, nl: (be[i], 0, 0))

    est = 3 * d * f * 4 + 3 * d * f * 2 + 4 * MOE_BM * d * 4 + 4 * MOE_BM * f * 4
    return pl.pallas_call(
        _moe_kernel,
        out_shape=jax.ShapeDtypeStruct(xs.shape, F32),
        grid_spec=pltpu.PrefetchScalarGridSpec(
            num_scalar_prefetch=3,
            grid=(n_blocks,),
            in_specs=[rows_in, hbm, hbm, hbm, bspec(f), bspec(f), bspec(d)],
            out_specs=rows_out,
            scratch_shapes=[pltpu.VMEM((d, f), F32), pltpu.VMEM((d, f), F32), pltpu.VMEM((f, d), F32),
                            pltpu.VMEM((d, f), BF16), pltpu.VMEM((d, f), BF16), pltpu.VMEM((f, d), BF16),
                            pltpu.SemaphoreType.DMA((3,))]),
        compiler_params=pltpu.CompilerParams(
            dimension_semantics=("arbitrary",),
            vmem_limit_bytes=_vmem_limit(est)),
        name="moe",
    )(block_e, next_e, n_live, xs, wg, wu, wd, bg, bu, bd)


def _moe_layout(counts, n_rows):
    n_blocks = n_rows // MOE_BM + N_EXPERTS
    padded = (counts + MOE_BM - 1) // MOE_BM * MOE_BM
    ends = jnp.cumsum(padded)
    starts = ends - padded
    block_end = ends // MOE_BM
    i = jnp.arange(n_blocks, dtype=I32)
    ids = jnp.arange(N_EXPERTS, dtype=I32)
    last_e = jnp.max(jnp.where(counts > 0, ids, 0))
    block_e = jnp.minimum(jnp.sum(block_end[None, :] <= i[:, None], axis=1), last_e).astype(I32)
    later = jnp.logical_and(ids[None, :] > ids[:, None], counts[None, :] > 0)
    following = jnp.min(jnp.where(later, ids[None, :], N_EXPERTS), axis=1)
    following = jnp.where(following == N_EXPERTS, ids, following)
    next_e = jnp.sum(jnp.where(block_e[:, None] == ids[None, :], following[None, :], 0), axis=1)
    return (starts.astype(I32), (starts + counts).astype(I32), (padded - counts).astype(I32),
            block_e, next_e.astype(I32), block_end[-1:].astype(I32), n_blocks)


def _combine_kernel(*refs, tiles_per_batch):
    lpos_refs, tw_refs = refs[:TOP_K], refs[TOP_K:2 * TOP_K]
    (seg_start_ref, seg_cnt_ref, next_start_ref, next_cnt_ref, ys_ref, x1_ref, mod_ref, gpost_ref,
     o_ref, stage, acc, sem) = refs[2 * TOP_K:]
    tm, d = x1_ref.shape
    i = pl.program_id(0)
    slot = i % 2

    def fetch(start_ref, cnt_ref, s):
        _segment_copies(start_ref, cnt_ref, tm,
                        lambda off, size: stage.at[s, pl.ds(off, size), :],
                        lambda off, size: ys_ref.at[pl.ds(off, size), :], sem.at[s], to_sorted=False)

    @pl.when(i == 0)
    def _():
        fetch(seg_start_ref, seg_cnt_ref, 0)

    @pl.when(i + 1 < pl.num_programs(0))
    def _():
        fetch(next_start_ref, next_cnt_ref, 1 - slot)

    pltpu.make_async_copy(ys_ref.at[pl.ds(0, TOP_K * tm * SUBLANES), :], stage.at[slot],
                          sem.at[slot]).wait()

    def reduce(r, carry):
        row = tw_refs[0][r] * stage[slot, _tile_rows(lpos_refs[0][r]), :]
        for jj in range(1, TOP_K):
            row = row + tw_refs[jj][r] * stage[slot, _tile_rows(lpos_refs[jj][r]), :]
        acc[_tile_rows(r * SUBLANES), :] = row
        return carry

    lax.fori_loop(0, tm, reduce, 0, unroll=REDUCE_UNROLL)
    y = _load_row_tiles(acc)
    r = i // tiles_per_batch
    gt2 = mod_ref[pl.ds(r, 1), pl.ds(5 * d, d)]
    o_ref[...] = x1_ref[...] + gt2 * _rms(y, gpost_ref[...])


def _combine(lpos, tws, seg_start, seg_cnt, ys, x1, mod, gpost, tokens_per_batch):
    t, d = x1.shape
    tm = min(TOKEN_TILE, tokens_per_batch)
    n_tiles = t // tm
    kern = functools.partial(_combine_kernel, tiles_per_batch=tokens_per_batch // tm)
    est = (2 * TOP_K + 1) * tm * d * 4 + 2 * 2 * tm * d * 4 + 3 * tm * d * 4
    this_tile = pl.BlockSpec((LANES,), lambda i: (i,), memory_space=pltpu.SMEM)
    next_tile = pl.BlockSpec((LANES,), lambda i: (jnp.minimum(i + 1, n_tiles - 1),),
                             memory_space=pltpu.SMEM)
    return pl.pallas_call(
        kern,
        out_shape=jax.ShapeDtypeStruct((t, d), F32),
        grid=(n_tiles,),
        in_specs=_slot_specs(tm, n_tiles, lambda i: i) + _slot_specs(tm, n_tiles, lambda i: i) + [
                  this_tile, this_tile, next_tile, next_tile,
                  pl.BlockSpec(memory_space=pl.ANY),
                  pl.BlockSpec((tm, d), lambda i: (i, 0)),
                  pl.BlockSpec(mod.shape, lambda i: (0, 0)),
                  pl.BlockSpec((1, d), lambda i: (0, 0))],
        out_specs=pl.BlockSpec((tm, d), lambda i: (i, 0)),
        scratch_shapes=[pltpu.VMEM((2, TOP_K * tm * SUBLANES, LANES), F32),
                        pltpu.VMEM((tm * SUBLANES, LANES), F32),
                        pltpu.SemaphoreType.DMA((2,))],
        compiler_params=pltpu.CompilerParams(
            dimension_semantics=("arbitrary",),
            vmem_limit_bytes=_vmem_limit(est)),
        name="combine",
    )(*([lpos] * TOP_K), *([tws] * TOP_K), seg_start, seg_cnt, seg_start, seg_cnt, ys, x1, mod, gpost)


def kernel(x, c, ctx, c_ctx, w_ada, b_ada, g_pre_mix, g_post_mix, g_pre_ffn, g_post_ffn,
           w_in, w_out, hgrn_lb_logits, hgrn_norm, pool_w, pool_scale,
           w_router, b_router, w_gate, b_gate, w_up, b_up, w_down, b_down):
    b, s, d = x.shape
    layer = 0
    n_e = w_router.shape[-1]

    cc = jnp.concatenate([c, c_ctx[None, :]], axis=0)
    cc = jnp.pad(cc, ((0, -(b + 1) % SUBLANES), (0, 0)))
    mod = _ada(cc, w_ada[layer], b_ada[layer][None, :])

    w_in_b = w_in[layer].astype(BF16)
    lbl = hgrn_lb_logits[:2].reshape(4, HG_WIDTH)
    gn = hgrn_norm[layer][None, :]
    gpre = g_pre_mix[layer][None, :]

    lff_c, kf_c, lfb_c, kb_c, v_c = _inproj(ctx, mod, gpre, w_in_b, lbl, gn, mod_row=b, full=False)
    zeros = jnp.zeros((b, HG_HEADS, HEAD_DIM, HEAD_DIM), F32)
    (s_f,) = _hgrn(None, kf_c, v_c, lff_c, zeros, rev=False, need_o=False, want_state=True,
                   name="hgrn_ctx_fwd")
    (s_b,) = _hgrn(None, kb_c, v_c, lfb_c, zeros, rev=True, need_o=False, want_state=True,
                   name="hgrn_ctx_bwd")

    q, lff, kf, lfb, kb, v, gate, u = _inproj(x, mod, gpre, w_in_b, lbl, gn, mod_row=None, full=True)
    (o_f,) = _hgrn(q, kf, v, lff, s_f, rev=False, need_o=True, want_state=False, name="hgrn_fwd")
    (o_hg,) = _hgrn(q, kb, v, lfb, s_b, o_f, gate, rev=True, need_o=True, want_state=False,
                    name="hgrn_bwd")
    o_pool = _pool(u, pool_w[layer].astype(BF16), pool_scale[layer].reshape(len(POOL_WINDOWS), 1, POOL_GROUP))

    wr = jnp.pad(w_router[layer], ((0, 0), (0, LANES - n_e))).astype(BF16)
    br = jnp.pad(b_router[layer], (0, LANES - n_e))[None, :]
    x1, h2, lpos, tws, seg_rel, seg_cnt, cnt = _outproj(
        o_hg, o_pool, x, w_out[layer].astype(BF16), mod,
        g_post_mix[layer][None, :], g_pre_ffn[layer][None, :], wr, br)

    t = b * s
    starts, gap_start, gap_len, block_e, next_e, n_live, n_blocks = _moe_layout(
        cnt[0, :n_e].astype(I32), t * TOP_K)
    seg_start = ((seg_rel + jnp.pad(starts, (0, LANES - n_e))[None, None, :]) * SUBLANES).reshape(-1)
    lpos, tws, seg_cnt = lpos.reshape(TOP_K * t), tws.reshape(TOP_K * t), seg_cnt.reshape(-1)

    xs = _dispatch(lpos, seg_start, seg_cnt, gap_start, gap_len, n_live,
                   h2.reshape(t * d // LANES, LANES), n_blocks * MOE_BM)
    ys = _moe(block_e, next_e, n_live, xs, w_gate[layer], w_up[layer], w_down[layer],
              b_gate[layer][:, None, :], b_up[layer][:, None, :], b_down[layer][:, None, :])
    out = _combine(lpos, tws, seg_start, seg_cnt, ys, x1.reshape(t, d), mod,
                   g_post_ffn[layer][None, :], s)
    return out.reshape(b, s, d)
```

```python
import functools
import math

import numpy as np
import jax
import jax.numpy as jnp
from jax import lax
from jax.experimental import pallas as pl
from jax.experimental.pallas import tpu as pltpu

F32 = jnp.float32
BF16 = jnp.bfloat16
I32 = jnp.int32

GRID_W = 64
HG_HEADS = 4
HEAD_DIM = 128
HG_WIDTH = HG_HEADS * HEAD_DIM
POOL_WINDOWS = (2, 4, 8, 16)
POOL_GROUP = 128
N_EXPERTS = 32
TOP_K = 4
SWIGLU_LIMIT = 7.0
SWIGLU_ALPHA = 1.702
EPS = 1e-6

LANES = 128
SUBLANES = 8
V7X_VMEM_BYTES = 64 * 1024 * 1024

TOKEN_TILE = 512
HGRN_BLOCK = 1024
ADA_COLS = 1536
PLACE_UNROLL = 8
REDUCE_UNROLL = 32
VMEM_COMPILER_BYTES = 4 << 20
VMEM_RESERVED_BYTES = 6 << 20
CHUNK = 64
HGRN_GROUP = 4
LEVELS = (32, 16, 8, 4, 2, 1)
FINE_LEVELS = (2, 1)
MOE_BM = 256
OUTPROJ_SLICES = 4
INPROJ_SLICES = 2
LOG2_E = math.log2(math.e)


def _vmem_limit(nbytes):
    return int(min(nbytes * 3 // 2 + VMEM_COMPILER_BYTES, V7X_VMEM_BYTES - VMEM_RESERVED_BYTES))


def _log2(n):
    assert n & (n - 1) == 0
    return n.bit_length() - 1


def _sigmoid(x):
    return 1.0 / (1.0 + jnp.exp(-x))


def _rms(x, gain):
    return x * lax.rsqrt(jnp.mean(x * x, axis=-1, keepdims=True) + EPS) * gain


def _dot(a, b):
    return jnp.dot(a, b, preferred_element_type=F32)


def _store_row_tiles(ref, val):
    rows = val.shape[0]
    for c in range(SUBLANES):
        ref[pl.ds(c, rows, stride=SUBLANES), :] = val[:, c * LANES:(c + 1) * LANES]


def _load_row_tiles(ref):
    rows = ref.shape[0] // SUBLANES
    return jnp.concatenate([ref[pl.ds(c, rows, stride=SUBLANES), :] for c in range(SUBLANES)], axis=1)


def _dot_nt(a, b):
    return lax.dot_general(a, b, (((1,), (1,)), ((), ())), preferred_element_type=F32)


def _ada_kernel(c_ref, w_ref, b_ref, o_ref):
    c = c_ref[...]
    s = (c * _sigmoid(c)).astype(BF16)
    o_ref[...] = _dot(s, w_ref[...].astype(BF16)) + b_ref[...]


def _ada(cc, w, b):
    rows, d = cc.shape
    n = w.shape[1]
    tn = ADA_COLS if n % ADA_COLS == 0 else n
    return pl.pallas_call(
        _ada_kernel,
        out_shape=jax.ShapeDtypeStruct((rows, n), F32),
        grid=(n // tn,),
        in_specs=[pl.BlockSpec((rows, d), lambda j: (0, 0)),
                  pl.BlockSpec((d, tn), lambda j: (0, j)),
                  pl.BlockSpec((1, tn), lambda j: (0, j))],
        out_specs=pl.BlockSpec((rows, tn), lambda j: (0, j)),
        compiler_params=pltpu.CompilerParams(
            dimension_semantics=("arbitrary",),
            vmem_limit_bytes=_vmem_limit(2 * d * tn * 4 + d * tn * 2)),
        name="ada",
    )(cc, w, b)


def _inproj_kernel(x_ref, mod_ref, gpre_ref, w_ref, lbl_ref, gn_ref, *outs, mod_row, full):
    d = x_ref.shape[-1]
    r = pl.program_id(0) if mod_row is None else mod_row
    sh = mod_ref[pl.ds(r, 1), pl.ds(0, d)]
    sc = mod_ref[pl.ds(r, 1), pl.ds(d, d)]
    tm = x_ref.shape[1]
    n_slices = INPROJ_SLICES if tm % (INPROJ_SLICES * SUBLANES) == 0 else 1
    sub = tm // n_slices
    slices = [pl.ds(i * sub, sub) for i in range(n_slices)]
    hs = [(_rms(x_ref[0, rows, :], gpre_ref[...]) * (1.0 + sc) + sh).astype(BF16) for rows in slices]

    def proj(g):
        return [_dot(h, w_ref[:, g * HG_WIDTH:(g + 1) * HG_WIDTH]) for h in hs]

    def put(ref, vals):
        for rows, val in zip(slices, vals):
            for hh in range(HG_HEADS):
                ref[0, hh, rows, :] = val[:, hh * HEAD_DIM:(hh + 1) * HEAD_DIM]

    def lower_bound(direction):
        l0 = lbl_ref[pl.ds(direction, 1), :]
        l1 = lbl_ref[pl.ds(2 + direction, 1), :]
        m = jnp.maximum(l0, l1)
        e0 = jnp.exp(l0 - m)
        return e0 / (e0 + jnp.exp(l1 - m))

    if full:
        q_o, lff_o, kf_o, lfb_o, kb_o, v_o, gate_o, u_o = outs
        put(q_o, [q * _sigmoid(q) for q in proj(0)])
    else:
        lff_o, kf_o, lfb_o, kb_o, v_o = outs
    for direction, (lf_o, k_o) in enumerate(((lff_o, kf_o), (lfb_o, kb_o))):
        lb = lower_bound(direction)
        sgs = [_sigmoid(z) for z in proj(1 + direction)]
        put(lf_o, [jnp.log(lb + (1.0 - lb) * sg) for sg in sgs])
        put(k_o, [(1.0 - lb) * (1.0 - sg) for sg in sgs])
    put(v_o, proj(3))
    if full:
        put(gate_o, [gn_ref[...] * _sigmoid(z) for z in proj(4)])
        put(u_o, proj(5))


def _inproj(x, mod, gpre, w_in, lbl, gn, *, mod_row, full):
    b, t, d = x.shape
    tm = min(TOKEN_TILE, t)
    n_out = 8 if full else 5
    n_cols = w_in.shape[1]
    head = pl.BlockSpec((1, HG_HEADS, tm, HEAD_DIM), lambda i, j: (i, 0, j, 0))
    kern = functools.partial(_inproj_kernel, mod_row=mod_row, full=full)
    est = 2 * tm * d * 4 + 2 * d * n_cols * 2 + n_out * 2 * tm * HG_WIDTH * 4 + 4 * tm * HG_WIDTH * 4
    return pl.pallas_call(
        kern,
        out_shape=[jax.ShapeDtypeStruct((b, HG_HEADS, t, HEAD_DIM), F32)] * n_out,
        grid=(b, t // tm),
        in_specs=[pl.BlockSpec((1, tm, d), lambda i, j: (i, j, 0)),
                  pl.BlockSpec(mod.shape, lambda i, j: (0, 0)),
                  pl.BlockSpec((1, d), lambda i, j: (0, 0)),
                  pl.BlockSpec(w_in.shape, lambda i, j: (0, 0)),
                  pl.BlockSpec(lbl.shape, lambda i, j: (0, 0)),
                  pl.BlockSpec((1, HG_WIDTH), lambda i, j: (0, 0))],
        out_specs=[head] * n_out,
        compiler_params=pltpu.CompilerParams(
            dimension_semantics=("arbitrary", "arbitrary"),
            vmem_limit_bytes=_vmem_limit(est)),
        name="inproj_full" if full else "inproj_ctx",
    )(x, mod, gpre, w_in, lbl, gn)


def _hgrn_tables(rev):
    row = np.arange(CHUNK)[:, None]
    col = np.arange(CHUNK)[None, :]
    tri = ((col >= row) if rev else (col <= row)).astype(np.float32)
    blocks, masks = [tri], []
    for h in LEVELS:
        if h in FINE_LEVELS:
            blocks.append(tri - tri[_ref_row(np.arange(CHUNK), h, rev)])
        is_q = ((row // h) % 2) == (0 if rev else 1)
        key_half = ((col // h) % 2) == (1 if rev else 0)
        masks.append(((row // (2 * h)) == (col // (2 * h))) & is_q & key_half)
    pair = np.stack([np.concatenate(masks[i:i + 2], axis=1) for i in range(0, len(LEVELS), 2)])
    cums = np.concatenate(blocks, axis=0)
    return jnp.asarray(np.concatenate([cums, cums], axis=1), BF16), jnp.asarray(pair, F32)


def _ref_row(t, h, rev):
    return (t // (2 * h)) * (2 * h) + (h if rev else h - 1)


def _hgrn_chunks(qs, ks, vs, lfs, sts, tabs, rev, need_o):
    cums_ref, pair_ref = tabs
    n_heads = len(sts)
    heads = range(len(ks))
    n_pairs = len(LEVELS) // 2
    n_cum = CHUNK * (1 + len(FINE_LEVELS)) if need_o else CHUNK
    cums = cums_ref[0:n_cum, :]

    his, los = [], []
    for h in heads:
        lf2 = lfs[h] * LOG2_E
        hi = lf2.astype(BF16)
        his.append(hi)
        los.append((lf2 - hi.astype(F32)).astype(BF16))
    both = _dot(cums, jnp.concatenate([jnp.concatenate(his, axis=1), jnp.concatenate(los, axis=1)], axis=0))
    bd = [both[:, h * LANES:(h + 1) * LANES] for h in heads]

    k16 = [ks[h].astype(BF16) for h in heads]
    v16 = [vs[h].astype(BF16) for h in heads]
    b = [bd[h][0:CHUNK] for h in heads]

    def level_diff(h, level):
        if level in FINE_LEVELS:
            i = FINE_LEVELS.index(level)
            return bd[h][CHUNK * (1 + i):CHUNK * (2 + i)]
        refs = [_ref_row(m * 2 * level, level, rev) for m in range(CHUNK // (2 * level))]
        pieces = [jnp.broadcast_to(b[h][r:r + 1, :], (2 * level, LANES)) for r in refs]
        return b[h] - (pieces[0] if len(pieces) == 1 else jnp.concatenate(pieces, axis=0))
    edge = [b[h][0:1, :] if rev else b[h][CHUNK - 1:CHUNK, :] for h in heads]
    kd = [k16[h] * jnp.exp2(edge[h] - b[h]).astype(BF16) for h in heads]
    grow = [lax.dot_general(v16[h], kd[h], (((0,), (0,)), ((), ())), preferred_element_type=F32)
            for h in heads]
    q16 = [qs[h].astype(BF16) for h in heads] if need_o else None
    st = list(sts)
    carried = []
    for h in heads:
        if need_o:
            carried.append(_dot_nt(q16[h] * jnp.exp2(b[h]).astype(BF16), st[h % n_heads].astype(BF16)))
        st[h % n_heads] = st[h % n_heads] * jnp.exp2(edge[h]) + grow[h]
    if not need_o:
        return None, st

    zero = jnp.zeros((CHUNK, LANES), BF16)
    att = [[] for _ in heads]
    for p in range(n_pairs):
        lhs, rhs = [], []
        for h in heads:
            qe, ke = [], []
            for i in (2 * p, 2 * p + 1):
                e = jnp.exp2(-jnp.abs(level_diff(h, LEVELS[i]))).astype(BF16)
                qe.append(q16[h] * e)
                ke.append(k16[h] * e)
            lhs.append(jnp.concatenate(qe, axis=1))
            rhs.append(jnp.concatenate([jnp.concatenate([ke[0], zero], axis=1),
                                        jnp.concatenate([zero, ke[1]], axis=1)], axis=0))
        scores = [_dot_nt(lhs[h], rhs[h]) for h in heads]
        for h in heads:
            att[h].append((scores[h] * pair_ref[p]).astype(BF16))
    o = []
    for h in heads:
        vals = jnp.concatenate([v16[h]] * (2 * n_pairs), axis=0)
        same_row = jnp.sum(qs[h] * ks[h], axis=1, keepdims=True) * vs[h]
        o.append(carried[h] + _dot(jnp.concatenate(att[h], axis=1), vals) + same_row)
    return o, st


def _hgrn_kernel(*refs, rev, need_o, final, want_state, n_chunks):
    it = iter(refs)
    tabs = tuple(next(it) for _ in range(2))
    q_ref = next(it) if need_o else None
    k_ref, v_ref, lf_ref, s0_ref = next(it), next(it), next(it), next(it)
    prev_ref = next(it) if final else None
    gate_ref = next(it) if final else None
    o_ref = next(it) if need_o else None
    sout_ref = next(it) if want_state else None
    st_ref = next(it)

    j = pl.program_id(1)
    nb = pl.num_programs(1)

    @pl.when(j == 0)
    def _():
        st_ref[...] = s0_ref[0]

    group = HGRN_GROUP if n_chunks % HGRN_GROUP == 0 else 1

    def body(gi, carry):
        rows = []
        for s in range(group):
            c = gi * group + s
            c = (n_chunks - 1 - c) if rev else c
            rows.append(pl.ds(pl.multiple_of(c * CHUNK, CHUNK), CHUNK))
        pairs = [(r, hh) for r in rows for hh in range(HG_HEADS)]
        o, st_new = _hgrn_chunks([q_ref[0, hh, r, :] for r, hh in pairs] if need_o else None,
                                 [k_ref[0, hh, r, :] for r, hh in pairs],
                                 [v_ref[0, hh, r, :] for r, hh in pairs],
                                 [lf_ref[0, hh, r, :] for r, hh in pairs],
                                 [st_ref[hh] for hh in range(HG_HEADS)], tabs, rev, need_o)
        for hh in range(HG_HEADS):
            st_ref[hh] = st_new[hh]
        if need_o:
            for (r, hh), out in zip(pairs, o):
                if final:
                    tot = out + prev_ref[0, hh, r, :]
                    out = tot * lax.rsqrt(jnp.mean(tot * tot, axis=-1, keepdims=True) + EPS)
                    out = out * gate_ref[0, hh, r, :]
                o_ref[0, hh, r, :] = out
        return carry

    lax.fori_loop(0, n_chunks // group, body, 0)

    if want_state:
        @pl.when(j == nb - 1)
        def _():
            sout_ref[0] = st_ref[...]


def _hgrn(q, k, v, lf, s0, prev=None, gate=None, *, rev, need_o, want_state, name):
    b, hh, t, _ = k.shape
    tb = min(HGRN_BLOCK, t)
    nb = t // tb
    final = prev is not None
    tabs = _hgrn_tables(rev)

    seq = pl.BlockSpec((1, hh, tb, HEAD_DIM), lambda i, j: (i, 0, (nb - 1 - j) if rev else j, 0))
    state = pl.BlockSpec((1, hh, HEAD_DIM, HEAD_DIM), lambda i, j: (i, 0, 0, 0))
    args = list(tabs)
    in_specs = [pl.BlockSpec(a.shape, functools.partial(lambda nd, i, j: (0,) * nd, a.ndim)) for a in tabs]
    for a in ((q,) if need_o else ()) + (k, v, lf):
        args.append(a)
        in_specs.append(seq)
    args.append(s0)
    in_specs.append(state)
    if final:
        args += [prev, gate]
        in_specs += [seq, seq]
    out_shape, out_specs = [], []
    if need_o:
        out_shape.append(jax.ShapeDtypeStruct((b, hh, t, HEAD_DIM), F32))
        out_specs.append(seq)
    if want_state:
        out_shape.append(jax.ShapeDtypeStruct((b, hh, HEAD_DIM, HEAD_DIM), F32))
        out_specs.append(state)
    kern = functools.partial(_hgrn_kernel, rev=rev, need_o=need_o, final=final,
                             want_state=want_state, n_chunks=tb // CHUNK)
    est = 2 * (len(args) + len(out_shape)) * hh * tb * HEAD_DIM * 4
    return pl.pallas_call(
        kern,
        out_shape=out_shape,
        grid=(b, nb),
        in_specs=in_specs,
        out_specs=out_specs,
        scratch_shapes=[pltpu.VMEM((hh, HEAD_DIM, HEAD_DIM), F32)],
        compiler_params=pltpu.CompilerParams(
            dimension_semantics=("arbitrary", "arbitrary"),
            vmem_limit_bytes=_vmem_limit(est)),
        name=name,
    )(*args)


def _window_sum(x, w, stride, pos, extent):
    n = x.shape[0]
    whole_tiles = stride % SUBLANES == 0 and stride * extent == n

    def ahead(y, dist):
        if whole_tiles:
            k = min(dist, extent) * stride
            return jnp.concatenate([y[k:], jnp.zeros((k, y.shape[1]), y.dtype)], axis=0)
        return jnp.where(pos + dist < extent, pltpu.roll(y, (n - dist * stride) % n, 0), 0.0)

    def behind(y, dist):
        if whole_tiles:
            k = min(dist, extent) * stride
            return jnp.concatenate([jnp.zeros((k, y.shape[1]), y.dtype), y[:n - k]], axis=0)
        return jnp.where(pos >= dist, pltpu.roll(y, dist * stride, 0), 0.0)

    fwd, bwd, h = x, x, 1
    while h < w // 2:
        fwd = fwd + ahead(fwd, h)
        bwd = bwd + behind(bwd, h)
        h *= 2
    return fwd + behind(bwd, 1)


def _window_count(pos, w, extent):
    return (jnp.minimum(pos + w // 2, extent) - jnp.maximum(pos - w // 2, 0)).astype(F32)


def _pool_kernel(u_ref, pw_ref, ps_ref, o_ref):
    g = pl.program_id(1)
    n = u_ref.shape[2]
    tok = lax.broadcasted_iota(I32, (n, LANES), 0)
    col = tok & (GRID_W - 1)
    row = tok >> _log2(GRID_W)
    for gi, w in enumerate(POOL_WINDOWS):
        @pl.when(g == gi)
        def _(w=w):
            u = u_ref[0, 0]
            rows = n // GRID_W
            s = _window_sum(_window_sum(u, w, GRID_W, row, rows), w, 1, col, GRID_W)
            m = s / (_window_count(row, w, rows) * _window_count(col, w, GRID_W))
            o_ref[0, 0] = _dot((m - u).astype(BF16), pw_ref[0]) * ps_ref[0]


def _pool(u, pw, ps):
    b, g, t, c = u.shape
    blk = pl.BlockSpec((1, 1, t, c), lambda i, j: (i, j, 0, 0))
    return pl.pallas_call(
        _pool_kernel,
        out_shape=jax.ShapeDtypeStruct(u.shape, F32),
        grid=(b, g),
        in_specs=[blk,
                  pl.BlockSpec((1, c, c), lambda i, j: (j, 0, 0)),
                  pl.BlockSpec((1, 1, c), lambda i, j: (j, 0, 0))],
        out_specs=blk,
        compiler_params=pltpu.CompilerParams(
            dimension_semantics=("arbitrary", "arbitrary"),
            vmem_limit_bytes=_vmem_limit(12 * t * c * 4)),
        name="pool",
    )(u, pw, ps)


def _outproj_kernel(hg_ref, pool_ref, x_ref, w_ref, mod_ref, gpost_ref, gpre_ref, wr_ref, br_ref,
                    x1_ref, h2_ref, lpos_ref, tws_ref, seg_rel_ref, seg_cnt_ref, cnt_ref):
    d = x_ref.shape[-1]
    r = pl.program_id(0)

    @pl.when(jnp.logical_and(r == 0, pl.program_id(1) == 0))
    def _():
        cnt_ref[...] = jnp.zeros(cnt_ref.shape, F32)

    gt1 = mod_ref[pl.ds(r, 1), pl.ds(2 * d, d)]
    sh2 = mod_ref[pl.ds(r, 1), pl.ds(3 * d, d)]
    sc2 = mod_ref[pl.ds(r, 1), pl.ds(4 * d, d)]
    tm = x_ref.shape[1]
    sub = tm // OUTPROJ_SLICES
    slices = [pl.ds(i * sub, sub) for i in range(OUTPROJ_SLICES)]
    ys = []
    for rows in slices:
        cat = jnp.concatenate([hg_ref[0, hh, rows, :] for hh in range(HG_HEADS)]
                              + [pool_ref[0, g, rows, :] for g in range(len(POOL_WINDOWS))], axis=1)
        ys.append(_dot(cat.astype(BF16), w_ref[...]))
    logits = []
    for i, rows in enumerate(slices):
        x1 = x_ref[0, rows, :] + gt1 * _rms(ys[i], gpost_ref[...])
        x1_ref[0, rows, :] = x1
        h2 = _rms(x1, gpre_ref[...]) * (1.0 + sc2) + sh2
        _store_row_tiles(h2_ref.at[0, pl.ds(i * sub * SUBLANES, sub * SUBLANES), :], h2)
        logits.append(_dot(h2.astype(BF16), wr_ref[...]) + br_ref[...])

    lane = lax.broadcasted_iota(I32, (sub, LANES), 1).astype(F32)
    tis, tws, chosens = [], [], []
    for i, rows in enumerate(slices):
        lg = jnp.where(lane < N_EXPERTS, logits[i], -jnp.inf)
        ti = jnp.zeros(lg.shape, F32)
        tw = jnp.zeros(lg.shape, F32)
        chosen = jnp.zeros(lg.shape, F32)
        top = None
        for jj in range(TOP_K):
            m = jnp.max(lg, axis=1, keepdims=True)
            idx = jnp.min(jnp.where(lg == m, lane, float(LANES)), axis=1, keepdims=True)
            lg = jnp.where(lane == idx, -jnp.inf, lg)
            chosen = jnp.where(lane == idx, 1.0, chosen)
            top = m if top is None else top
            ti = jnp.where(lane == jj, idx, ti)
            tw = jnp.where(lane == jj, jnp.exp(m - top), tw)
        tis.append(ti.astype(I32))
        tws.append(tw / jnp.sum(tw, axis=1, keepdims=True))
        chosens.append(chosen)

    ti_all, tw_all, chosen = (jnp.concatenate(z, axis=0) for z in (tis, tws, chosens))
    row = lax.broadcasted_iota(I32, (tm, tm), 0)
    col = lax.broadcasted_iota(I32, (tm, tm), 1)
    before = jnp.where(col < row, 1.0, 0.0).astype(BF16)
    local = _dot(before, chosen.astype(BF16))
    seg_cnt = jnp.sum(chosen, axis=0, keepdims=True)
    lane8 = lax.broadcasted_iota(I32, (SUBLANES, LANES), 1)
    run = jnp.broadcast_to(seg_cnt, (SUBLANES, LANES))
    shift = 1
    while shift < LANES:
        run = run + jnp.where(lane8 >= shift, pltpu.roll(run, shift, 1), 0.0)
        shift *= 2
    seg_off = run[0:1, :] - seg_cnt
    lpos = jnp.take_along_axis(local + seg_off, ti_all, axis=1)
    lpos_ref[...] = (lpos.T[:TOP_K, :] * SUBLANES).astype(I32)
    tws_ref[...] = tw_all.T[:TOP_K, :]
    seg_rel_ref[0] = cnt_ref[...].astype(I32)
    seg_cnt_ref[0] = seg_cnt.astype(I32)
    cnt_ref[...] += seg_cnt


def _outproj(hg, pool, x, w_out, mod, gpost, gpre, wr, br):
    b, t, d = x.shape
    tm = min(TOKEN_TILE, t)
    head = pl.BlockSpec((1, HG_HEADS, tm, HEAD_DIM), lambda i, j: (i, 0, j, 0))
    tok = pl.BlockSpec((1, tm, d), lambda i, j: (i, j, 0))
    n_t = t // tm
    slots = pl.BlockSpec((TOP_K, tm), lambda i, j: (0, i * n_t + j))
    per_tile = pl.BlockSpec((1, 1, LANES), lambda i, j: (i * n_t + j, 0, 0))
    vec = pl.BlockSpec((1, d), lambda i, j: (0, 0))
    est = 2 * (2 * tm * HG_WIDTH * 4 + 3 * tm * d * 4 + 2 * tm * LANES * 4) + 2 * d * d * 2 + 6 * tm * d * 4
    return pl.pallas_call(
        _outproj_kernel,
        out_shape=[jax.ShapeDtypeStruct((b, t, d), F32),
                   jax.ShapeDtypeStruct((b, t * d // LANES, LANES), F32),
                   jax.ShapeDtypeStruct((TOP_K, b * t), I32), jax.ShapeDtypeStruct((TOP_K, b * t), F32),
                   jax.ShapeDtypeStruct((b * n_t, 1, LANES), I32),
                   jax.ShapeDtypeStruct((b * n_t, 1, LANES), I32),
                   jax.ShapeDtypeStruct((1, LANES), F32)],
        grid=(b, t // tm),
        in_specs=[head, head, tok,
                  pl.BlockSpec(w_out.shape, lambda i, j: (0, 0)),
                  pl.BlockSpec(mod.shape, lambda i, j: (0, 0)),
                  vec, vec,
                  pl.BlockSpec(wr.shape, lambda i, j: (0, 0)),
                  pl.BlockSpec((1, LANES), lambda i, j: (0, 0))],
        out_specs=[tok, pl.BlockSpec((1, tm * d // LANES, LANES), lambda i, j: (i, j, 0)),
                   slots, slots, per_tile, per_tile, pl.BlockSpec((1, LANES), lambda i, j: (0, 0))],
        compiler_params=pltpu.CompilerParams(
            dimension_semantics=("arbitrary", "arbitrary"),
            vmem_limit_bytes=_vmem_limit(est)),
        name="outproj",
    )(hg, pool, x, w_out, mod, gpost, gpre, wr, br)


def _slot_specs(tm, n_tiles, tile_of_step):
    return [pl.BlockSpec((tm,), functools.partial(lambda jj, i: (jj * n_tiles + tile_of_step(i),), jj),
                         memory_space=pltpu.SMEM) for jj in range(TOP_K)]


def _tile_rows(start):
    return pl.ds(pl.multiple_of(start, SUBLANES), SUBLANES)


def _gap_copies(e, gap_start_ref, gap_len_ref, zeros, xs_ref, sem):
    off, n = gap_start_ref[e], gap_len_ref[e]
    out = []
    for bit in range(_log2(MOE_BM)):
        size = (1 << bit) * SUBLANES
        used = (n >> bit) & 1
        out.append((used == 1,
                    pltpu.make_async_copy(zeros.at[pl.ds(0, size), :],
                                          xs_ref.at[pl.ds(pl.multiple_of(off * SUBLANES, SUBLANES), size), :],
                                          sem)))
        off = off + used * (1 << bit)
    return out


def _dispatch_kernel(*refs):
    lpos_refs = refs[:TOP_K]
    (seg_start_ref, seg_cnt_ref, gap_start_ref, gap_len_ref, n_live_ref, h_ref, xs_ref,
     zeros, stage, sem, zsem) = refs[TOP_K:]
    tm = h_ref.shape[0] // SUBLANES
    block_rows = MOE_BM * SUBLANES
    n_blocks = xs_ref.shape[0] // block_rows

    @pl.when(pl.program_id(0) == 0)
    def _():
        zeros[...] = jnp.zeros(zeros.shape, zeros.dtype)

        def tail_copy(blk):
            return pltpu.make_async_copy(
                zeros, xs_ref.at[pl.ds(pl.multiple_of(blk * block_rows, block_rows), block_rows), :], zsem)

        def tail_start(blk, carry):
            tail_copy(blk).start()
            return carry

        def tail_wait(blk, carry):
            tail_copy(blk).wait()
            return carry

        lax.fori_loop(n_live_ref[0], n_blocks, tail_start, 0)
        lax.fori_loop(n_live_ref[0], n_blocks, tail_wait, 0)

        def fill(e, carry):
            for used, cp in _gap_copies(e, gap_start_ref, gap_len_ref, zeros, xs_ref, zsem):
                @pl.when(used)
                def _(cp=cp):
                    cp.start()
            return carry

        def drain(e, carry):
            for used, cp in _gap_copies(e, gap_start_ref, gap_len_ref, zeros, xs_ref, zsem):
                @pl.when(used)
                def _(cp=cp):
                    cp.wait()
            return carry

        lax.fori_loop(0, N_EXPERTS, fill, 0)
        lax.fori_loop(0, N_EXPERTS, drain, 0)

    i = pl.program_id(0)
    n_steps = pl.num_programs(0)
    slot = i % 2

    def drain(s):
        pltpu.make_async_copy(stage.at[s], xs_ref.at[pl.ds(0, TOP_K * tm * SUBLANES), :], sem.at[s]).wait()

    @pl.when(i >= 2)
    def _():
        drain(slot)

    def place(r, carry):
        row = h_ref[_tile_rows(r * SUBLANES), :]
        for jj in range(TOP_K):
            stage[slot, _tile_rows(lpos_refs[jj][r]), :] = row
        return carry

    lax.fori_loop(0, tm, place, 0, unroll=PLACE_UNROLL)
    _segment_copies(seg_start_ref, seg_cnt_ref, tm,
                    lambda off, size: stage.at[slot, pl.ds(off, size), :],
                    lambda off, size: xs_ref.at[pl.ds(off, size), :], sem.at[slot], to_sorted=True)

    @pl.when(i == n_steps - 1)
    def _():
        drain(slot)

        @pl.when(n_steps >= 2)
        def _():
            drain(1 - slot)


def _segment_copies(seg_start_ref, seg_cnt_ref, tm, staged, sorted_rows, sem, *, to_sorted):
    def segment(e, stage_off):
        n, base = seg_cnt_ref[e], seg_start_ref[e]
        done = jnp.int32(0)
        for bit in range(_log2(tm) + 1):
            size = (1 << bit) * SUBLANES
            used = (n >> bit) & 1
            a = staged(pl.multiple_of(stage_off + done, SUBLANES), size)
            b = sorted_rows(pl.multiple_of(base + done, SUBLANES), size)

            src, dst = (a, b) if to_sorted else (b, a)

            @pl.when(used == 1)
            def _(src=src, dst=dst):
                pltpu.make_async_copy(src, dst, sem).start()
            done = done + used * size
        return stage_off + n * SUBLANES

    lax.fori_loop(0, N_EXPERTS, segment, jnp.int32(0))


def _dispatch(lpos, seg_start, seg_cnt, gap_start, gap_len, n_live, h2, n_sorted):
    rows, l = h2.shape
    t = rows // SUBLANES
    tm = min(TOKEN_TILE, t)
    smem = pl.BlockSpec(memory_space=pltpu.SMEM)
    per_tile = pl.BlockSpec((LANES,), lambda i: (i,), memory_space=pltpu.SMEM)
    return pl.pallas_call(
        _dispatch_kernel,
        out_shape=jax.ShapeDtypeStruct((n_sorted * SUBLANES, l), h2.dtype),
        grid=(t // tm,),
        in_specs=_slot_specs(tm, t // tm, lambda i: i) + [
            per_tile, per_tile, smem, smem, smem, pl.BlockSpec((tm * SUBLANES, l), lambda i: (i, 0))],
        out_specs=pl.BlockSpec(memory_space=pl.ANY),
        scratch_shapes=[pltpu.VMEM((MOE_BM * SUBLANES, l), h2.dtype),
                        pltpu.VMEM((2, TOP_K * tm * SUBLANES, l), h2.dtype),
                        pltpu.SemaphoreType.DMA((2,)), pltpu.SemaphoreType.DMA],
        compiler_params=pltpu.CompilerParams(
            dimension_semantics=("arbitrary",),
            vmem_limit_bytes=_vmem_limit((2 * TOP_K + 2) * tm * SUBLANES * l * 4 + MOE_BM * SUBLANES * l * 4)),
        name="dispatch",
    )(*([lpos] * TOP_K), seg_start, seg_cnt, gap_start, gap_len, n_live, h2)


def _moe_kernel(block_e_ref, next_e_ref, n_live_ref,
                x_ref, wg_ref, wu_ref, wd_ref, bg_ref, bu_ref, bd_ref, y_ref,
                wgs, wus, wds, wgb, wub, wdb, sem):
    i = pl.program_id(0)
    live = i < n_live_ref[0]
    e = block_e_ref[i]
    changed = jnp.logical_or(i == 0, e != block_e_ref[jnp.maximum(i - 1, 0)])

    def fetch(expert):
        return [pltpu.make_async_copy(w_ref.at[expert], stage, sem.at[n])
                for n, (w_ref, stage) in enumerate(((wg_ref, wgs), (wu_ref, wus), (wd_ref, wds)))]

    @pl.when(i == 0)
    def _():
        for cp in fetch(e):
            cp.start()

    @pl.when(jnp.logical_and(live, changed))
    def _():
        for cp in fetch(e):
            cp.wait()
        def cast(c, carry):
            rows = pl.ds(pl.multiple_of(c * LANES, LANES), LANES)
            for src, dst in ((wgs, wgb), (wus, wub), (wds, wdb)):
                dst[rows, :] = src[rows, :].astype(BF16)
            return carry

        lax.fori_loop(0, wgs.shape[0] // LANES, cast, 0)

        @pl.when(next_e_ref[i] != e)
        def _():
            for cp in fetch(next_e_ref[i]):
                cp.start()

    @pl.when(live)
    def _():
        x = _load_row_tiles(x_ref).astype(BF16)
        gate = jnp.minimum(_dot(x, wgb[...]) + bg_ref[...], SWIGLU_LIMIT)
        up = jnp.clip(_dot(x, wub[...]) + bu_ref[...], -SWIGLU_LIMIT, SWIGLU_LIMIT)
        hmid = (up + 1.0) * gate * _sigmoid(SWIGLU_ALPHA * gate)
        _store_row_tiles(y_ref, _dot(hmid.astype(BF16), wdb[...]) + bd_ref[...])

    @pl.when(jnp.logical_not(live))
    def _():
        y_ref[...] = jnp.zeros(y_ref.shape, y_ref.dtype)


def _moe(block_e, next_e, n_live, xs, wg, wu, wd, bg, bu, bd):
    d, f = wg.shape[-2:]
    n_blocks = block_e.shape[0]
    rows_in = pl.BlockSpec((MOE_BM * SUBLANES, LANES),
                           lambda i, be, ne, nl: (jnp.minimum(i, nl[0] - 1), 0))
    rows_out = pl.BlockSpec((MOE_BM * SUBLANES, LANES), lambda i, be, ne, nl: (i, 0))
    hbm = pl.BlockSpec(memory_space=pl.ANY)

    def bspec(n):
        return pl.BlockSpec((None, 1, n), lambda i, be, ne, nl: (be[i], 0, 0))

    est = 3 * d * f * 4 + 3 * d * f * 2 + 4 * MOE_BM * d * 4 + 4 * MOE_BM * f * 4
    return pl.pallas_call(
        _moe_kernel,
        out_shape=jax.ShapeDtypeStruct(xs.shape, F32),
        grid_spec=pltpu.PrefetchScalarGridSpec(
            num_scalar_prefetch=3,
            grid=(n_blocks,),
            in_specs=[rows_in, hbm, hbm, hbm, bspec(f), bspec(f), bspec(d)],
            out_specs=rows_out,
            scratch_shapes=[pltpu.VMEM((d, f), F32), pltpu.VMEM((d, f), F32), pltpu.VMEM((f, d), F32),
                            pltpu.VMEM((d, f), BF16), pltpu.VMEM((d, f), BF16), pltpu.VMEM((f, d), BF16),
                            pltpu.SemaphoreType.DMA((3,))]),
        compiler_params=pltpu.CompilerParams(
            dimension_semantics=("arbitrary",),
            vmem_limit_bytes=_vmem_limit(est)),
        name="moe",
    )(block_e, next_e, n_live, xs, wg, wu, wd, bg, bu, bd)


def _moe_layout(counts, n_rows):
    n_blocks = n_rows // MOE_BM + N_EXPERTS
    padded = (counts + MOE_BM - 1) // MOE_BM * MOE_BM
    ends = jnp.cumsum(padded)
    starts = ends - padded
    block_end = ends // MOE_BM
    i = jnp.arange(n_blocks, dtype=I32)
    ids = jnp.arange(N_EXPERTS, dtype=I32)
    last_e = jnp.max(jnp.where(counts > 0, ids, 0))
    block_e = jnp.minimum(jnp.sum(block_end[None, :] <= i[:, None], axis=1), last_e).astype(I32)
    later = jnp.logical_and(ids[None, :] > ids[:, None], counts[None, :] > 0)
    following = jnp.min(jnp.where(later, ids[None, :], N_EXPERTS), axis=1)
    following = jnp.where(following == N_EXPERTS, ids, following)
    next_e = jnp.sum(jnp.where(block_e[:, None] == ids[None, :], following[None, :], 0), axis=1)
    return (starts.astype(I32), (starts + counts).astype(I32), (padded - counts).astype(I32),
            block_e, next_e.astype(I32), block_end[-1:].astype(I32), n_blocks)


def _combine_kernel(*refs, tiles_per_batch):
    lpos_refs, tw_refs = refs[:TOP_K], refs[TOP_K:2 * TOP_K]
    (seg_start_ref, seg_cnt_ref, next_start_ref, next_cnt_ref, ys_ref, x1_ref, mod_ref, gpost_ref,
     o_ref, stage, acc, sem) = refs[2 * TOP_K:]
    tm, d = x1_ref.shape
    i = pl.program_id(0)
    slot = i % 2

    def fetch(start_ref, cnt_ref, s):
        _segment_copies(start_ref, cnt_ref, tm,
                        lambda off, size: stage.at[s, pl.ds(off, size), :],
                        lambda off, size: ys_ref.at[pl.ds(off, size), :], sem.at[s], to_sorted=False)

    @pl.when(i == 0)
    def _():
        fetch(seg_start_ref, seg_cnt_ref, 0)

    @pl.when(i + 1 < pl.num_programs(0))
    def _():
        fetch(next_start_ref, next_cnt_ref, 1 - slot)

    pltpu.make_async_copy(ys_ref.at[pl.ds(0, TOP_K * tm * SUBLANES), :], stage.at[slot],
                          sem.at[slot]).wait()

    def reduce(r, carry):
        row = tw_refs[0][r] * stage[slot, _tile_rows(lpos_refs[0][r]), :]
        for jj in range(1, TOP_K):
            row = row + tw_refs[jj][r] * stage[slot, _tile_rows(lpos_refs[jj][r]), :]
        acc[_tile_rows(r * SUBLANES), :] = row
        return carry

    lax.fori_loop(0, tm, reduce, 0, unroll=REDUCE_UNROLL)
    y = _load_row_tiles(acc)
    r = i // tiles_per_batch
    gt2 = mod_ref[pl.ds(r, 1), pl.ds(5 * d, d)]
    o_ref[...] = x1_ref[...] + gt2 * _rms(y, gpost_ref[...])


def _combine(lpos, tws, seg_start, seg_cnt, ys, x1, mod, gpost, tokens_per_batch):
    t, d = x1.shape
    tm = min(TOKEN_TILE, tokens_per_batch)
    n_tiles = t // tm
    kern = functools.partial(_combine_kernel, tiles_per_batch=tokens_per_batch // tm)
    est = (2 * TOP_K + 1) * tm * d * 4 + 2 * 2 * tm * d * 4 + 3 * tm * d * 4
    this_tile = pl.BlockSpec((LANES,), lambda i: (i,), memory_space=pltpu.SMEM)
    next_tile = pl.BlockSpec((LANES,), lambda i: (jnp.minimum(i + 1, n_tiles - 1),),
                             memory_space=pltpu.SMEM)
    return pl.pallas_call(
        kern,
        out_shape=jax.ShapeDtypeStruct((t, d), F32),
        grid=(n_tiles,),
        in_specs=_slot_specs(tm, n_tiles, lambda i: i) + _slot_specs(tm, n_tiles, lambda i: i) + [
                  this_tile, this_tile, next_tile, next_tile,
                  pl.BlockSpec(memory_space=pl.ANY),
                  pl.BlockSpec((tm, d), lambda i: (i, 0)),
                  pl.BlockSpec(mod.shape, lambda i: (0, 0)),
                  pl.BlockSpec((1, d), lambda i: (0, 0))],
        out_specs=pl.BlockSpec((tm, d), lambda i: (i, 0)),
        scratch_shapes=[pltpu.VMEM((2, TOP_K * tm * SUBLANES, LANES), F32),
                        pltpu.VMEM((tm * SUBLANES, LANES), F32),
                        pltpu.SemaphoreType.DMA((2,))],
        compiler_params=pltpu.CompilerParams(
            dimension_semantics=("arbitrary",),
            vmem_limit_bytes=_vmem_limit(est)),
        name="combine",
    )(*([lpos] * TOP_K), *([tws] * TOP_K), seg_start, seg_cnt, seg_start, seg_cnt, ys, x1, mod, gpost)


def kernel(x, c, ctx, c_ctx, w_ada, b_ada, g_pre_mix, g_post_mix, g_pre_ffn, g_post_ffn,
           w_in, w_out, hgrn_lb_logits, hgrn_norm, pool_w, pool_scale,
           w_router, b_router, w_gate, b_gate, w_up, b_up, w_down, b_down):
    b, s, d = x.shape
    layer = 0
    n_e = w_router.shape[-1]

    cc = jnp.concatenate([c, c_ctx[None, :]], axis=0)
    cc = jnp.pad(cc, ((0, -(b + 1) % SUBLANES), (0, 0)))
    mod = _ada(cc, w_ada[layer], b_ada[layer][None, :])

    w_in_b = w_in[layer].astype(BF16)
    lbl = hgrn_lb_logits[:2].reshape(4, HG_WIDTH)
    gn = hgrn_norm[layer][None, :]
    gpre = g_pre_mix[layer][None, :]

    lff_c, kf_c, lfb_c, kb_c, v_c = _inproj(ctx, mod, gpre, w_in_b, lbl, gn, mod_row=b, full=False)
    zeros = jnp.zeros((b, HG_HEADS, HEAD_DIM, HEAD_DIM), F32)
    (s_f,) = _hgrn(None, kf_c, v_c, lff_c, zeros, rev=False, need_o=False, want_state=True,
                   name="hgrn_ctx_fwd")
    (s_b,) = _hgrn(None, kb_c, v_c, lfb_c, zeros, rev=True, need_o=False, want_state=True,
                   name="hgrn_ctx_bwd")

    q, lff, kf, lfb, kb, v, gate, u = _inproj(x, mod, gpre, w_in_b, lbl, gn, mod_row=None, full=True)
    (o_f,) = _hgrn(q, kf, v, lff, s_f, rev=False, need_o=True, want_state=False, name="hgrn_fwd")
    (o_hg,) = _hgrn(q, kb, v, lfb, s_b, o_f, gate, rev=True, need_o=True, want_state=False,
                    name="hgrn_bwd")
    o_pool = _pool(u, pool_w[layer].astype(BF16), pool_scale[layer].reshape(len(POOL_WINDOWS), 1, POOL_GROUP))

    wr = jnp.pad(w_router[layer], ((0, 0), (0, LANES - n_e))).astype(BF16)
    br = jnp.pad(b_router[layer], (0, LANES - n_e))[None, :]
    x1, h2, lpos, tws, seg_rel, seg_cnt, cnt = _outproj(
        o_hg, o_pool, x, w_out[layer].astype(BF16), mod,
        g_post_mix[layer][None, :], g_pre_ffn[layer][None, :], wr, br)

    t = b * s
    starts, gap_start, gap_len, block_e, next_e, n_live, n_blocks = _moe_layout(
        cnt[0, :n_e].astype(I32), t * TOP_K)
    seg_start = ((seg_rel + jnp.pad(starts, (0, LANES - n_e))[None, None, :]) * SUBLANES).reshape(-1)
    lpos, tws, seg_cnt = lpos.reshape(TOP_K * t), tws.reshape(TOP_K * t), seg_cnt.reshape(-1)

    xs = _dispatch(lpos, seg_start, seg_cnt, gap_start, gap_len, n_live,
                   h2.reshape(t * d // LANES, LANES), n_blocks * MOE_BM)
    ys = _moe(block_e, next_e, n_live, xs, w_gate[layer], w_up[layer], w_down[layer],
              b_gate[layer][:, None, :], b_up[layer][:, None, :], b_down[layer][:, None, :])
    out = _combine(lpos, tws, seg_start, seg_cnt, ys, x1.reshape(t, d), mod,
                   g_post_ffn[layer][None, :], s)
    return out.reshape(b, s, d)
```
